```python
import jax, jax.numpy as jnp
from jax import lax
import numpy as np

D_MODEL = 2048
BATCH = 8
SEQ = 4096
DEPTH = 2

N_META = 16
D_ATTN = D_MODEL // 2
N_HEADS_SB = 16
HEAD_DIM_SB = D_ATTN // N_HEADS_SB
D_POOL = D_MODEL // 4
POOL_WINDOWS = (2, 4, 8, 16)
N_POOL_GROUPS = len(POOL_WINDOWS)
POOL_GROUP_DIM = D_POOL // N_POOL_GROUPS
D_CONV = D_MODEL // 4
CONV_WIDTH = 31
D_MIX = D_ATTN + D_POOL + D_CONV
D_IN_PROJ = 3 * D_ATTN + D_POOL + 2 * D_CONV
D_FF = ((8 * D_MODEL + 3 * 256 - 1) // (3 * 256)) * 256
BLOCK_Q = 128
EPS = 1e-6

kernel_name = "hybrid_stickbreak_pool_conformer_trunk"


def rmsnorm(x, g):
    xf = x.astype(jnp.float32)
    y = xf * lax.rsqrt(jnp.mean(xf * xf, axis=-1, keepdims=True) + EPS)
    return (y * g.astype(jnp.float32)).astype(x.dtype)


def layernorm(x, g, b):
    xf = x.astype(jnp.float32)
    mu = jnp.mean(xf, axis=-1, keepdims=True)
    var = jnp.mean(jnp.square(xf - mu), axis=-1, keepdims=True)
    y = (xf - mu) * lax.rsqrt(var + EPS)
    return (y * g.astype(jnp.float32) + b.astype(jnp.float32)).astype(x.dtype)


def stick_breaking_attention(q, k, v):
    b, l, h, dh = q.shape
    pad = (-l) % BLOCK_Q
    padw = ((0, 0), (pad, 0), (0, 0), (0, 0))
    qf = jnp.pad(q.astype(jnp.float32) * (dh ** -0.5), padw)
    kf = jnp.pad(k.astype(jnp.float32), padw)
    vf = jnp.pad(v.astype(jnp.float32), padw)
    lp = l + pad
    n_blocks = lp // BLOCK_Q
    key_pos = jnp.arange(lp) - pad

    def one_block(i):
        q_blk = lax.dynamic_slice_in_dim(qf, i * BLOCK_Q, BLOCK_Q, axis=1)
        q_pos = i * BLOCK_Q + jnp.arange(BLOCK_Q) - pad
        z = jnp.einsum('bqhd,bkhd->bhqk', q_blk, kf)
        mask = (key_pos[None, :] < q_pos[:, None]) & (key_pos[None, :] >= 0)
        log_not_beta = jnp.where(mask, -jax.nn.softplus(z), 0.0)
        later = lax.cumsum(log_not_beta, axis=3, reverse=True) - log_not_beta
        weights = jnp.where(mask, jnp.exp(jax.nn.log_sigmoid(z) + later), 0.0)
        return jnp.einsum('bhqk,bkhd->bqhd', weights, vf)

    out = lax.map(one_block, jnp.arange(n_blocks))
    out = jnp.moveaxis(out, 0, 1).reshape(b, lp, h, dh)[:, pad:]
    return out.astype(v.dtype)


def multiscale_pool(u, w_pool, pool_scale):
    b, l, _ = u.shape
    uf = u.astype(jnp.float32)
    cs = jnp.pad(jnp.cumsum(uf, axis=1), ((0, 0), (1, 0), (0, 0)))
    t = jnp.arange(l)
    groups = []
    for g, w in enumerate(POOL_WINDOWS):
        lo, hi = g * POOL_GROUP_DIM, (g + 1) * POOL_GROUP_DIM
        start = jnp.maximum(t + 1 - w, 0)
        csg = cs[:, :, lo:hi]
        window_sum = csg[:, 1:] - csg[:, start]
        count = (t + 1 - start).astype(jnp.float32)[None, :, None]
        groups.append(window_sum / count - uf[:, :, lo:hi])
    pooled = jnp.stack(groups, axis=2)
    mixed = jnp.einsum('blgc,gcd->blgd', pooled, w_pool.astype(jnp.float32))
    mixed = mixed.reshape(b, l, D_POOL) * pool_scale.astype(jnp.float32)
    return mixed.astype(u.dtype)


def conformer_conv(u_glu, w_dw, b_dw, ln_g, ln_b, w_pw):
    a, gate = jnp.split(u_glu, 2, axis=-1)
    u = a * jax.nn.sigmoid(gate)
    y = lax.conv_general_dilated(
        u, w_dw[:, None, :].astype(u.dtype), window_strides=(1,),
        padding=[(CONV_WIDTH - 1, 0)],
        dimension_numbers=('NWC', 'WIO', 'NWC'),
        feature_group_count=D_CONV) + b_dw
    y = layernorm(y, ln_g, ln_b)
    y = jax.nn.silu(y)
    return y @ w_pw


def _fwd_setup_inputs(seed: int = 0) -> dict:
    key = jax.random.key(seed)
    ks = jax.random.split(key, 20)
    f32 = jnp.float32

    def nrm(k, shape, scale):
        return jax.random.normal(k, shape, f32) * scale

    def gain(k, shape):
        return 1.0 + 0.05 * jax.random.normal(k, shape, f32)

    return {
        "x": nrm(ks[0], (BATCH, SEQ, D_MODEL), 1.0),
        "meta_tokens": nrm(ks[1], (N_META, D_MODEL), 1.0),
        "pre_mix_g": gain(ks[2], (DEPTH, D_MODEL)),
        "w_in": nrm(ks[3], (DEPTH, D_MODEL, D_IN_PROJ), D_MODEL ** -0.5),
        "w_pool": nrm(ks[4], (DEPTH, N_POOL_GROUPS, POOL_GROUP_DIM, POOL_GROUP_DIM), POOL_GROUP_DIM ** -0.5),
        "pool_scale": gain(ks[5], (DEPTH, D_POOL)),
        "w_dw": nrm(ks[6], (DEPTH, CONV_WIDTH, D_CONV), CONV_WIDTH ** -0.5),
        "b_dw": nrm(ks[7], (DEPTH, D_CONV), 0.02),
        "conv_ln_g": gain(ks[8], (DEPTH, D_CONV)),
        "conv_ln_b": nrm(ks[9], (DEPTH, D_CONV), 0.02),
        "w_pw": nrm(ks[10], (DEPTH, D_CONV, D_CONV), D_CONV ** -0.5),
        "mix_out_g": gain(ks[11], (DEPTH, D_MIX)),
        "w_out": nrm(ks[12], (DEPTH, D_MIX, D_MODEL), D_MIX ** -0.5),
        "post_mix_g": gain(ks[13], (DEPTH, D_MODEL)),
        "pre_ffn_g": gain(ks[14], (DEPTH, D_MODEL)),
        "w_gate": nrm(ks[15], (DEPTH, D_MODEL, D_FF), D_MODEL ** -0.5),
        "w_up": nrm(ks[16], (DEPTH, D_MODEL, D_FF), D_MODEL ** -0.5),
        "w_down": nrm(ks[17], (DEPTH, D_FF, D_MODEL), D_FF ** -0.5),
        "post_ffn_g": gain(ks[18], (DEPTH, D_MODEL)),
    }


def _fwd_reference(x, meta_tokens, pre_mix_g, w_in, w_pool, pool_scale, w_dw, b_dw,
              conv_ln_g, conv_ln_b, w_pw, mix_out_g, w_out, post_mix_g,
              pre_ffn_g, w_gate, w_up, w_down, post_ffn_g):
    b = x.shape[0]
    meta = jnp.broadcast_to(meta_tokens[None].astype(x.dtype), (b, N_META, D_MODEL))
    h = jnp.concatenate([meta, x], axis=1)
    l = h.shape[1]
    splits = [D_ATTN, 2 * D_ATTN, 3 * D_ATTN, 3 * D_ATTN + D_POOL]

    for i in range(DEPTH):
        u = rmsnorm(h, pre_mix_g[i])
        proj = u @ w_in[i]
        q, k, v, u_pool, u_conv = jnp.split(proj, splits, axis=-1)
        heads = (b, l, N_HEADS_SB, HEAD_DIM_SB)
        o_attn = stick_breaking_attention(q.reshape(heads), k.reshape(heads), v.reshape(heads))
        o_attn = o_attn.reshape(b, l, D_ATTN)
        o_pool = multiscale_pool(u_pool, w_pool[i], pool_scale[i])
        o_conv = conformer_conv(u_conv, w_dw[i], b_dw[i], conv_ln_g[i], conv_ln_b[i], w_pw[i])
        g = mix_out_g[i]
        merged = jnp.concatenate([
            rmsnorm(o_attn, g[:D_ATTN]),
            rmsnorm(o_pool, g[D_ATTN:D_ATTN + D_POOL]),
            rmsnorm(o_conv, g[D_ATTN + D_POOL:]),
        ], axis=-1)
        h = h + rmsnorm(merged @ w_out[i], post_mix_g[i])

        u = rmsnorm(h, pre_ffn_g[i])
        ff = (jax.nn.silu(u @ w_gate[i]) * (u @ w_up[i])) @ w_down[i]
        h = h + rmsnorm(ff, post_ffn_g[i])

    return h[:, N_META:]


import jax as _jax
import jax.numpy as _jnp

TWIN_FORMAT = 'train_step'
FWD_PARAMS = ['x', 'meta_tokens', 'pre_mix_g', 'w_in', 'w_pool', 'pool_scale', 'w_dw', 'b_dw', 'conv_ln_g', 'conv_ln_b', 'w_pw', 'mix_out_g', 'w_out', 'post_mix_g', 'pre_ffn_g', 'w_gate', 'w_up', 'w_down', 'post_ffn_g']
TWIN_WEIGHTS = ['meta_tokens', 'pre_mix_g', 'w_in', 'w_pool', 'pool_scale', 'w_dw', 'b_dw', 'conv_ln_g', 'conv_ln_b', 'w_pw', 'mix_out_g', 'w_out', 'post_mix_g', 'pre_ffn_g', 'w_gate', 'w_up', 'w_down', 'post_ffn_g']
TWIN_DIFF_INPUT = 'x'
TWIN_INPUTS = ['x', 'meta_tokens', 'pre_mix_g', 'w_in', 'w_pool', 'pool_scale', 'w_dw', 'b_dw', 'conv_ln_g', 'conv_ln_b', 'w_pw', 'mix_out_g', 'w_out', 'post_mix_g', 'pre_ffn_g', 'w_gate', 'w_up', 'w_down', 'post_ffn_g', 'loss_target', 'm_meta_tokens', 'm_pre_mix_g', 'm_w_in', 'm_w_pool', 'm_pool_scale', 'm_w_dw', 'm_b_dw', 'm_conv_ln_g', 'm_conv_ln_b', 'm_w_pw', 'm_mix_out_g', 'm_w_out', 'm_post_mix_g', 'm_pre_ffn_g', 'm_w_gate', 'm_w_up', 'm_w_down', 'm_post_ffn_g', 'v_meta_tokens', 'v_pre_mix_g', 'v_w_in', 'v_w_pool', 'v_pool_scale', 'v_w_dw', 'v_b_dw', 'v_conv_ln_g', 'v_conv_ln_b', 'v_w_pw', 'v_mix_out_g', 'v_w_out', 'v_post_mix_g', 'v_pre_ffn_g', 'v_w_gate', 'v_w_up', 'v_w_down', 'v_post_ffn_g']
TWIN_OUTPUTS = ['loss', 'grad_x', 'grad_meta_tokens', 'grad_pre_mix_g', 'grad_w_in', 'grad_w_pool', 'grad_pool_scale', 'grad_w_dw', 'grad_b_dw', 'grad_conv_ln_g', 'grad_conv_ln_b', 'grad_w_pw', 'grad_mix_out_g', 'grad_w_out', 'grad_post_mix_g', 'grad_pre_ffn_g', 'grad_w_gate', 'grad_w_up', 'grad_w_down', 'grad_post_ffn_g', 'delta_meta_tokens', 'delta_pre_mix_g', 'delta_w_in', 'delta_w_pool', 'delta_pool_scale', 'delta_w_dw', 'delta_b_dw', 'delta_conv_ln_g', 'delta_conv_ln_b', 'delta_w_pw', 'delta_mix_out_g', 'delta_w_out', 'delta_post_mix_g', 'delta_pre_ffn_g', 'delta_w_gate', 'delta_w_up', 'delta_w_down', 'delta_post_ffn_g', 'new_m_meta_tokens', 'new_m_pre_mix_g', 'new_m_w_in', 'new_m_w_pool', 'new_m_pool_scale', 'new_m_w_dw', 'new_m_b_dw', 'new_m_conv_ln_g', 'new_m_conv_ln_b', 'new_m_w_pw', 'new_m_mix_out_g', 'new_m_w_out', 'new_m_post_mix_g', 'new_m_pre_ffn_g', 'new_m_w_gate', 'new_m_w_up', 'new_m_w_down', 'new_m_post_ffn_g', 'new_v_meta_tokens', 'new_v_pre_mix_g', 'new_v_w_in', 'new_v_w_pool', 'new_v_pool_scale', 'new_v_w_dw', 'new_v_b_dw', 'new_v_conv_ln_g', 'new_v_conv_ln_b', 'new_v_w_pw', 'new_v_mix_out_g', 'new_v_w_out', 'new_v_post_mix_g', 'new_v_pre_ffn_g', 'new_v_w_gate', 'new_v_w_up', 'new_v_w_down', 'new_v_post_ffn_g']
TWIN_LEAF_KINDS = {'loss': 'loss', 'grad_x': 'grad_x', 'grad_meta_tokens': 'grad_w', 'grad_pre_mix_g': 'grad_w', 'grad_w_in': 'grad_w', 'grad_w_pool': 'grad_w', 'grad_pool_scale': 'grad_w', 'grad_w_dw': 'grad_w', 'grad_b_dw': 'grad_w', 'grad_conv_ln_g': 'grad_w', 'grad_conv_ln_b': 'grad_w', 'grad_w_pw': 'grad_w', 'grad_mix_out_g': 'grad_w', 'grad_w_out': 'grad_w', 'grad_post_mix_g': 'grad_w', 'grad_pre_ffn_g': 'grad_w', 'grad_w_gate': 'grad_w', 'grad_w_up': 'grad_w', 'grad_w_down': 'grad_w', 'grad_post_ffn_g': 'grad_w', 'delta_meta_tokens': 'delta_w', 'delta_pre_mix_g': 'delta_w', 'delta_w_in': 'delta_w', 'delta_w_pool': 'delta_w', 'delta_pool_scale': 'delta_w', 'delta_w_dw': 'delta_w', 'delta_b_dw': 'delta_w', 'delta_conv_ln_g': 'delta_w', 'delta_conv_ln_b': 'delta_w', 'delta_w_pw': 'delta_w', 'delta_mix_out_g': 'delta_w', 'delta_w_out': 'delta_w', 'delta_post_mix_g': 'delta_w', 'delta_pre_ffn_g': 'delta_w', 'delta_w_gate': 'delta_w', 'delta_w_up': 'delta_w', 'delta_w_down': 'delta_w', 'delta_post_ffn_g': 'delta_w', 'new_m_meta_tokens': 'new_m', 'new_m_pre_mix_g': 'new_m', 'new_m_w_in': 'new_m', 'new_m_w_pool': 'new_m', 'new_m_pool_scale': 'new_m', 'new_m_w_dw': 'new_m', 'new_m_b_dw': 'new_m', 'new_m_conv_ln_g': 'new_m', 'new_m_conv_ln_b': 'new_m', 'new_m_w_pw': 'new_m', 'new_m_mix_out_g': 'new_m', 'new_m_w_out': 'new_m', 'new_m_post_mix_g': 'new_m', 'new_m_pre_ffn_g': 'new_m', 'new_m_w_gate': 'new_m', 'new_m_w_up': 'new_m', 'new_m_w_down': 'new_m', 'new_m_post_ffn_g': 'new_m', 'new_v_meta_tokens': 'new_v', 'new_v_pre_mix_g': 'new_v', 'new_v_w_in': 'new_v', 'new_v_w_pool': 'new_v', 'new_v_pool_scale': 'new_v', 'new_v_w_dw': 'new_v', 'new_v_b_dw': 'new_v', 'new_v_conv_ln_g': 'new_v', 'new_v_conv_ln_b': 'new_v', 'new_v_w_pw': 'new_v', 'new_v_mix_out_g': 'new_v', 'new_v_w_out': 'new_v', 'new_v_post_mix_g': 'new_v', 'new_v_pre_ffn_g': 'new_v', 'new_v_w_gate': 'new_v', 'new_v_w_up': 'new_v', 'new_v_w_down': 'new_v', 'new_v_post_ffn_g': 'new_v'}


def _forward(args):
    return _fwd_reference(*[args[k] for k in FWD_PARAMS])


def _output_shape():
    def fwd():
        inp = _fwd_setup_inputs(0)
        return _fwd_reference(*[inp[k] for k in FWD_PARAMS])
    out = _jax.eval_shape(fwd)
    return out.shape, out.dtype

N_MICROBATCH = 1
ADAM_LR = 0.001
ADAM_B1 = 0.9
ADAM_B2 = 0.999
ADAM_EPS = 1e-08
ADAM_WD = 0.01
ADAM_STEP = 10
PER_EXAMPLE_BATCH_AXIS = {'x': 0, 'loss_target': 0}
SHARED_INPUTS = []
_WEIGHT_DTYPES = {'meta_tokens': _jnp.float32, 'pre_mix_g': _jnp.float32, 'w_in': _jnp.float32, 'w_pool': _jnp.float32, 'pool_scale': _jnp.float32, 'w_dw': _jnp.float32, 'b_dw': _jnp.float32, 'conv_ln_g': _jnp.float32, 'conv_ln_b': _jnp.float32, 'w_pw': _jnp.float32, 'mix_out_g': _jnp.float32, 'w_out': _jnp.float32, 'post_mix_g': _jnp.float32, 'pre_ffn_g': _jnp.float32, 'w_gate': _jnp.float32, 'w_up': _jnp.float32, 'w_down': _jnp.float32, 'post_ffn_g': _jnp.float32}
MOMENT_SCALE = {'meta_tokens': 2.075225e-02, 'pre_mix_g': 4.959602e-01, 'w_in': 3.197163e-01, 'w_pool': 5.454954e-01, 'pool_scale': 5.748926e-01, 'w_dw': 4.448056e-01, 'b_dw': 5.104793e+00, 'conv_ln_g': 1.813864e+00, 'conv_ln_b': 2.763251e+00, 'w_pw': 1.149821e+00, 'mix_out_g': 6.916813e-01, 'w_out': 7.117798e-01, 'post_mix_g': 1.602871e+01, 'pre_ffn_g': 4.662409e-01, 'w_gate': 1.711437e-01, 'w_up': 2.232750e-01, 'w_down': 3.726047e-01, 'post_ffn_g': 1.603429e+01}


def _to_microbatches(a, axis):
    t = _jnp.moveaxis(a, axis, 0)
    t = t.reshape((N_MICROBATCH, t.shape[0] // N_MICROBATCH) + t.shape[1:])
    return _jnp.moveaxis(t, 1, axis + 1)


def setup_inputs(seed: int = 0) -> dict:
    inp = _fwd_setup_inputs(seed)
    key = _jax.random.fold_in(_jax.random.key(seed), 7919)
    shape, _ = _output_shape()
    out = dict(inp)
    out["loss_target"] = _jax.random.normal(_jax.random.fold_in(key, 0), shape, _jnp.float32)
    for i, name in enumerate(TWIN_WEIGHTS):
        w = inp[name].astype(_jnp.float32)
        if MOMENT_SCALE is None:
            s = _jnp.sqrt(_jnp.mean(_jnp.square(w)) + 1e-30)
        else:
            s = MOMENT_SCALE[name]
        km, kv = _jax.random.split(_jax.random.fold_in(key, i + 1))
        out[name] = w
        out["m_" + name] = s * _jax.random.normal(km, w.shape, _jnp.float32)
        out["v_" + name] = (s * s) * _jax.random.uniform(kv, w.shape, _jnp.float32, 0.5, 1.5)
    if N_MICROBATCH > 1:
        for name, axis in PER_EXAMPLE_BATCH_AXIS.items():
            out[name] = _to_microbatches(out[name], axis)
    return {'x': out['x'], 'meta_tokens': out['meta_tokens'], 'pre_mix_g': out['pre_mix_g'], 'w_in': out['w_in'], 'w_pool': out['w_pool'], 'pool_scale': out['pool_scale'], 'w_dw': out['w_dw'], 'b_dw': out['b_dw'], 'conv_ln_g': out['conv_ln_g'], 'conv_ln_b': out['conv_ln_b'], 'w_pw': out['w_pw'], 'mix_out_g': out['mix_out_g'], 'w_out': out['w_out'], 'post_mix_g': out['post_mix_g'], 'pre_ffn_g': out['pre_ffn_g'], 'w_gate': out['w_gate'], 'w_up': out['w_up'], 'w_down': out['w_down'], 'post_ffn_g': out['post_ffn_g'], 'loss_target': out['loss_target'], 'm_meta_tokens': out['m_meta_tokens'], 'm_pre_mix_g': out['m_pre_mix_g'], 'm_w_in': out['m_w_in'], 'm_w_pool': out['m_w_pool'], 'm_pool_scale': out['m_pool_scale'], 'm_w_dw': out['m_w_dw'], 'm_b_dw': out['m_b_dw'], 'm_conv_ln_g': out['m_conv_ln_g'], 'm_conv_ln_b': out['m_conv_ln_b'], 'm_w_pw': out['m_w_pw'], 'm_mix_out_g': out['m_mix_out_g'], 'm_w_out': out['m_w_out'], 'm_post_mix_g': out['m_post_mix_g'], 'm_pre_ffn_g': out['m_pre_ffn_g'], 'm_w_gate': out['m_w_gate'], 'm_w_up': out['m_w_up'], 'm_w_down': out['m_w_down'], 'm_post_ffn_g': out['m_post_ffn_g'], 'v_meta_tokens': out['v_meta_tokens'], 'v_pre_mix_g': out['v_pre_mix_g'], 'v_w_in': out['v_w_in'], 'v_w_pool': out['v_w_pool'], 'v_pool_scale': out['v_pool_scale'], 'v_w_dw': out['v_w_dw'], 'v_b_dw': out['v_b_dw'], 'v_conv_ln_g': out['v_conv_ln_g'], 'v_conv_ln_b': out['v_conv_ln_b'], 'v_w_pw': out['v_w_pw'], 'v_mix_out_g': out['v_mix_out_g'], 'v_w_out': out['v_w_out'], 'v_post_mix_g': out['v_post_mix_g'], 'v_pre_ffn_g': out['v_pre_ffn_g'], 'v_w_gate': out['v_w_gate'], 'v_w_up': out['v_w_up'], 'v_w_down': out['v_w_down'], 'v_post_ffn_g': out['v_post_ffn_g']}


def _loss(weights, diff, rest, loss_target):
    with _jax.named_scope("forward"):
        args = {**rest, TWIN_DIFF_INPUT: diff, **{k: w.astype(_WEIGHT_DTYPES[k]) for k, w in weights.items()}}
        y = _forward(args)
    with _jax.named_scope("loss_head"):
        err = _jnp.square(y.astype(_jnp.float32) - loss_target)
        return 0.5 * _jnp.sum(_jnp.mean(err, axis=-1)) if err.ndim else 0.5 * err


def _adamw(w, g, m, v):
    m = ADAM_B1 * m + (1.0 - ADAM_B1) * g
    v = ADAM_B2 * v + (1.0 - ADAM_B2) * _jnp.square(g)
    m_hat = m / (1.0 - ADAM_B1 ** ADAM_STEP)
    v_hat = v / (1.0 - ADAM_B2 ** ADAM_STEP)
    delta = -ADAM_LR * (m_hat / (_jnp.sqrt(v_hat) + ADAM_EPS) + ADAM_WD * w)
    return delta, m, v


def reference(x, meta_tokens, pre_mix_g, w_in, w_pool, pool_scale, w_dw, b_dw, conv_ln_g, conv_ln_b, w_pw, mix_out_g, w_out, post_mix_g, pre_ffn_g, w_gate, w_up, w_down, post_ffn_g, loss_target, m_meta_tokens, m_pre_mix_g, m_w_in, m_w_pool, m_pool_scale, m_w_dw, m_b_dw, m_conv_ln_g, m_conv_ln_b, m_w_pw, m_mix_out_g, m_w_out, m_post_mix_g, m_pre_ffn_g, m_w_gate, m_w_up, m_w_down, m_post_ffn_g, v_meta_tokens, v_pre_mix_g, v_w_in, v_w_pool, v_pool_scale, v_w_dw, v_b_dw, v_conv_ln_g, v_conv_ln_b, v_w_pw, v_mix_out_g, v_w_out, v_post_mix_g, v_pre_ffn_g, v_w_gate, v_w_up, v_w_down, v_post_ffn_g):
    given = dict(x=x, meta_tokens=meta_tokens, pre_mix_g=pre_mix_g, w_in=w_in, w_pool=w_pool, pool_scale=pool_scale, w_dw=w_dw, b_dw=b_dw, conv_ln_g=conv_ln_g, conv_ln_b=conv_ln_b, w_pw=w_pw, mix_out_g=mix_out_g, w_out=w_out, post_mix_g=post_mix_g, pre_ffn_g=pre_ffn_g, w_gate=w_gate, w_up=w_up, w_down=w_down, post_ffn_g=post_ffn_g, loss_target=loss_target, m_meta_tokens=m_meta_tokens, m_pre_mix_g=m_pre_mix_g, m_w_in=m_w_in, m_w_pool=m_w_pool, m_pool_scale=m_pool_scale, m_w_dw=m_w_dw, m_b_dw=m_b_dw, m_conv_ln_g=m_conv_ln_g, m_conv_ln_b=m_conv_ln_b, m_w_pw=m_w_pw, m_mix_out_g=m_mix_out_g, m_w_out=m_w_out, m_post_mix_g=m_post_mix_g, m_pre_ffn_g=m_pre_ffn_g, m_w_gate=m_w_gate, m_w_up=m_w_up, m_w_down=m_w_down, m_post_ffn_g=m_post_ffn_g, v_meta_tokens=v_meta_tokens, v_pre_mix_g=v_pre_mix_g, v_w_in=v_w_in, v_w_pool=v_w_pool, v_pool_scale=v_pool_scale, v_w_dw=v_w_dw, v_b_dw=v_b_dw, v_conv_ln_g=v_conv_ln_g, v_conv_ln_b=v_conv_ln_b, v_w_pw=v_w_pw, v_mix_out_g=v_mix_out_g, v_w_out=v_w_out, v_post_mix_g=v_post_mix_g, v_pre_ffn_g=v_pre_ffn_g, v_w_gate=v_w_gate, v_w_up=v_w_up, v_w_down=v_w_down, v_post_ffn_g=v_post_ffn_g)
    weights = {n: given[n] for n in TWIN_WEIGHTS}
    shared = {n: given[n] for n in SHARED_INPUTS}
    per_example = {n: given[n] for n in ['x']}
    grad_fn = _jax.value_and_grad(_loss, argnums=(0, 1))

    def one_microbatch(ex, loss_target):
        ex = dict(ex)
        diff = ex.pop(TWIN_DIFF_INPUT)
        return grad_fn(weights, diff, {**shared, **ex}, loss_target)

    if N_MICROBATCH == 1:
        loss, (grad_w, grad_x) = one_microbatch(per_example, given["loss_target"])
    else:
        def body(carry, xs):
            loss_sum, grad_sum = carry
            l_k, (gw_k, gx_k) = one_microbatch(xs[0], xs[1])
            with _jax.named_scope("update"):
                return (loss_sum + l_k, _jax.tree.map(_jnp.add, grad_sum, gw_k)), gx_k

        init = (_jnp.zeros((), _jnp.float32), _jax.tree.map(_jnp.zeros_like, weights))
        (loss, grad_w), grad_x = _jax.lax.scan(body, init, (per_example, given["loss_target"]))
    with _jax.named_scope("update"):
        delta_w, new_m, new_v = {}, {}, {}
        for n in TWIN_WEIGHTS:
            delta_w[n], new_m[n], new_v[n] = _adamw(weights[n], grad_w[n], given["m_" + n], given["v_" + n])
    return (loss, grad_x, *[grad_w[n] for n in TWIN_WEIGHTS], *[delta_w[n] for n in TWIN_WEIGHTS],
            *[new_m[n] for n in TWIN_WEIGHTS], *[new_v[n] for n in TWIN_WEIGHTS])
```

```python
import functools

import jax
import jax.numpy as jnp
from jax import lax
from jax.experimental import pallas as pl
from jax.experimental.pallas import tpu as pltpu

F32 = jnp.float32
BF16 = jnp.bfloat16
EPS = 1e-6
N_HEADS = 16
POOL_WINDOWS = (2, 4, 8, 16)
CONV_WIDTH = 31
HALO = 32
ATT_BLOCK = 128
N_DEV = 8
VMEM_LIMIT = 60 * 1024 * 1024

ADAM_LR = 0.001
ADAM_B1 = 0.9
ADAM_B2 = 0.999
ADAM_EPS = 1e-08
ADAM_WD = 0.01
ADAM_STEP = 10

NN = (((1,), (0,)), ((), ()))
NT = (((1,), (1,)), ((), ()))
TN = (((0,), (0,)), ((), ()))
MESH = pl.DeviceIdType.MESH
ANY = pl.BlockSpec(memory_space=pl.ANY)


def _pick(n, cands):
    for c in cands:
        if c <= n and n % c == 0:
            return c
    return n


def _params(grid):
    return pltpu.CompilerParams(dimension_semantics=("arbitrary",) * len(grid), vmem_limit_bytes=VMEM_LIMIT)


def _dot(a, b, dims=NN):
    return lax.dot_general(a, b, dims, preferred_element_type=F32)


def _matmul(name, pairs, specs, out_shape, out_spec, grid, dims, acc_shape):
    n = len(pairs)
    nk = grid[-1]
    kaxis = len(grid) - 1

    def body(*refs):
        o_ref, acc = refs[2 * n], refs[2 * n + 1]
        tot = None
        for p in range(n):
            d = _dot(refs[2 * p][...], refs[2 * p + 1][...], dims)
            tot = d if tot is None else tot + d
        if nk == 1:
            o_ref[...] = tot.astype(o_ref.dtype)
            return
        k = pl.program_id(kaxis)

        @pl.when(k == 0)
        def _():
            acc[...] = tot

        @pl.when(k > 0)
        def _():
            acc[...] += tot

        @pl.when(k == nk - 1)
        def _():
            o_ref[...] = acc[...].astype(o_ref.dtype)

    ops, in_specs = [], []
    for (a, b), (sa, sb) in zip(pairs, specs):
        ops += [a, b]
        in_specs += [sa, sb]
    return pl.pallas_call(
        body, name=name, out_shape=out_shape, grid=grid, in_specs=in_specs, out_specs=out_spec,
        scratch_shapes=[pltpu.VMEM(acc_shape if nk > 1 else (8, 128), F32)], compiler_params=_params(grid),
    )(*ops)


def _mm_nn(a, b, name, out_dtype=F32, layer=None):
    m, kk = a.shape
    n = b.shape[-1]
    tm = _pick(m, (1408, 1024, 512, 256, 128))
    tn = _pick(n, (1024, 512, 256, 128))
    grid = (m // tm, n // tn, 1)
    sa = pl.BlockSpec((tm, kk), lambda i, j, k: (i, 0))
    if layer is None:
        sb = pl.BlockSpec((kk, tn), lambda i, j, k: (0, j))
    else:
        sb = pl.BlockSpec((None, kk, tn), lambda i, j, k: (layer, 0, j))
    return _matmul(name, [(a, b)], [(sa, sb)], jax.ShapeDtypeStruct((m, n), out_dtype),
                   pl.BlockSpec((tm, tn), lambda i, j, k: (i, j)), grid, NN, (tm, tn))


def _mm_nt(a, b, name, layer, out_dtype=F32):
    m, kk = a.shape
    n = b.shape[1]
    tm = _pick(m, (1408, 1024, 512, 256, 128))
    tn = _pick(n, (1024, 512, 256, 128))
    grid = (m // tm, n // tn, 1)
    sa = pl.BlockSpec((tm, kk), lambda i, j, k: (i, 0))
    sb = pl.BlockSpec((None, tn, kk), lambda i, j, k: (layer, j, 0))
    return _matmul(name, [(a, b)], [(sa, sb)], jax.ShapeDtypeStruct((m, n), out_dtype),
                   pl.BlockSpec((tm, tn), lambda i, j, k: (i, j)), grid, NT, (tm, tn))


def _mm_tn(a, b, name, out_dtype=BF16):
    l, m = a.shape
    n = b.shape[1]
    tl = _pick(l, (1408, 1024, 512, 256, 128))
    tm = _pick(m, (1024, 512, 256, 128))
    tn = _pick(n, (1024, 512, 256, 128))
    grid = (m // tm, n // tn, l // tl)
    sa = pl.BlockSpec((tl, tm), lambda i, j, t: (t, i))
    sb = pl.BlockSpec((tl, tn), lambda i, j, t: (t, j))
    return _matmul(name, [(a, b)], [(sa, sb)], jax.ShapeDtypeStruct((m, n), out_dtype),
                   pl.BlockSpec((tm, tn), lambda i, j, t: (i, j)), grid, TN, (tm, tn))


def _mm_cols(a, w, layer, name):
    m, kk = a.shape
    nd, _, _, n = w.shape
    tm = _pick(m, (1408, 1024, 512, 256, 128))
    grid = (nd, m // tm, 1)
    sa = pl.BlockSpec((tm, kk), lambda d, i, k: (i, 0))
    sb = pl.BlockSpec((None, None, kk, n), lambda d, i, k: (d, layer, 0, 0))
    return _matmul(name, [(a, w)], [(sa, sb)], jax.ShapeDtypeStruct((nd, m, n), F32),
                   pl.BlockSpec((None, tm, n), lambda d, i, k: (d, i, 0)), grid, NN, (tm, n))


def _mm_rows(a_s, w, layer, name):
    nd, m, kk = a_s.shape
    n = w.shape[-1]
    tm = _pick(m, (1408, 1024, 512, 256, 128))
    tn = _pick(n, (1024, 512, 256, 128))
    grid = (m // tm, n // tn, nd)
    sa = pl.BlockSpec((None, tm, kk), lambda i, j, d: (d, i, 0))
    sb = pl.BlockSpec((None, None, kk, tn), lambda i, j, d: (d, layer, 0, j))
    return _matmul(name, [(a_s, w)], [(sa, sb)], jax.ShapeDtypeStruct((m, n), F32),
                   pl.BlockSpec((tm, tn), lambda i, j, d: (i, j)), grid, NN, (tm, tn))


def _mm_cols_t(das, ws, layer, name):
    nd, m, n = das[0].shape
    kk = ws[0].shape[2]
    tm = _pick(m, (1408, 1024, 512, 256, 128))
    tk = _pick(kk, (1024, 512, 256, 128))
    grid = (m // tm, kk // tk, nd)
    sa = pl.BlockSpec((None, tm, n), lambda i, j, d: (d, i, 0))
    sb = pl.BlockSpec((None, None, tk, n), lambda i, j, d: (d, layer, j, 0))
    return _matmul(name, list(zip(das, ws)), [(sa, sb)] * len(das), jax.ShapeDtypeStruct((m, kk), F32),
                   pl.BlockSpec((tm, tk), lambda i, j, d: (i, j)), grid, NT, (tm, tk))


def _mm_tn_cols(a, b_s, name):
    l, kk = a.shape
    nd, _, n = b_s.shape
    tl = _pick(l, (1408, 1024, 512, 256, 128))
    tk = _pick(kk, (1024, 512, 256, 128))
    grid = (nd, kk // tk, l // tl)
    sa = pl.BlockSpec((tl, tk), lambda d, j, t: (t, j))
    sb = pl.BlockSpec((None, tl, n), lambda d, j, t: (d, t, 0))
    return _matmul(name, [(a, b_s)], [(sa, sb)], jax.ShapeDtypeStruct((nd, kk, n), BF16),
                   pl.BlockSpec((None, tk, n), lambda d, j, t: (d, j, 0)), grid, TN, (tk, n))


def _mm_tn_rows(a_s, b, name):
    nd, l, kk = a_s.shape
    n = b.shape[1]
    tl = _pick(l, (1408, 1024, 512, 256, 128))
    tn = _pick(n, (1024, 512, 256, 128))
    grid = (nd, n // tn, l // tl)
    sa = pl.BlockSpec((None, tl, kk), lambda d, j, t: (d, t, 0))
    sb = pl.BlockSpec((tl, tn), lambda d, j, t: (t, j))
    return _matmul(name, [(a_s, b)], [(sa, sb)], jax.ShapeDtypeStruct((nd, kk, n), BF16),
                   pl.BlockSpec((None, kk, tn), lambda d, j, t: (d, 0, j)), grid, TN, (kk, tn))


def _ffn_up(u, wg, wu, layer):
    m, kk = u.shape
    nd, _, _, n = wg.shape
    tm = _pick(m, (704, 512, 256, 128))
    grid = (nd, m // tm)

    def body(u_ref, wg_ref, wu_ref, a_ref, b_ref, act_ref):
        uu = u_ref[...]
        a = _dot(uu, wg_ref[...])
        b = _dot(uu, wu_ref[...])
        a_ref[...] = a
        b_ref[...] = b
        act_ref[...] = (a * jax.nn.sigmoid(a) * b).astype(BF16)

    wspec = pl.BlockSpec((None, None, kk, n), lambda d, i: (d, layer, 0, 0))
    ospec = pl.BlockSpec((None, tm, n), lambda d, i: (d, i, 0))
    return pl.pallas_call(
        body, name="ffn_up", grid=grid,
        out_shape=(jax.ShapeDtypeStruct((nd, m, n), F32), jax.ShapeDtypeStruct((nd, m, n), F32),
                   jax.ShapeDtypeStruct((nd, m, n), BF16)),
        in_specs=[pl.BlockSpec((tm, kk), lambda d, i: (i, 0)), wspec, wspec],
        out_specs=(ospec, ospec, ospec), compiler_params=_params(grid),
    )(u, wg, wu)


def _ffn_bwd_act(dff, wd, layer, a_s, b_s):
    m, kk = dff.shape
    nd, _, n, _ = wd.shape
    tm = _pick(m, (704, 512, 256, 128))
    grid = (nd, m // tm)

    def body(d_ref, w_ref, a_ref, b_ref, da_ref, db_ref):
        dact = _dot(d_ref[...], w_ref[...], NT)
        a, b = a_ref[...], b_ref[...]
        sg = jax.nn.sigmoid(a)
        db_ref[...] = (dact * a * sg).astype(BF16)
        da_ref[...] = (dact * b * sg * (1.0 + a * (1.0 - sg))).astype(BF16)

    tspec = pl.BlockSpec((None, tm, n), lambda d, i: (d, i, 0))
    return pl.pallas_call(
        body, name="ffn_bwd_act", grid=grid,
        out_shape=(jax.ShapeDtypeStruct((nd, m, n), BF16), jax.ShapeDtypeStruct((nd, m, n), BF16)),
        in_specs=[pl.BlockSpec((tm, kk), lambda d, i: (i, 0)),
                  pl.BlockSpec((None, None, n, kk), lambda d, i: (d, layer, 0, 0)), tspec, tspec],
        out_specs=(tspec, tspec), compiler_params=_params(grid),
    )(dff, wd, a_s, b_s)


def _rstd(x):
    return lax.rsqrt(jnp.mean(x * x, axis=-1, keepdims=True) + EPS)


def _row_tile(m):
    return _pick(m, (384, 256, 128))


def _rms_fwd(x, g, name):
    m, d = x.shape
    tr = _row_tile(m)
    grid = (m // tr,)

    def body(x_ref, g_ref, o_ref):
        xx = x_ref[...]
        o_ref[...] = (xx * _rstd(xx) * g_ref[...]).astype(BF16)

    row = pl.BlockSpec((tr, d), lambda i: (i, 0))
    return pl.pallas_call(body, name=name, grid=grid, out_shape=jax.ShapeDtypeStruct((m, d), BF16),
                          in_specs=[row, pl.BlockSpec((1, d), lambda i: (0, 0))], out_specs=row,
                          compiler_params=_params(grid))(x, g)


def _resid_rms_fwd(h, z, g, name):
    m, d = h.shape
    tr = _row_tile(m)
    grid = (m // tr,)

    def body(h_ref, z_ref, g_ref, o_ref):
        zz = z_ref[...]
        o_ref[...] = h_ref[...] + zz * _rstd(zz) * g_ref[...]

    row = pl.BlockSpec((tr, d), lambda i: (i, 0))
    return pl.pallas_call(body, name=name, grid=grid, out_shape=jax.ShapeDtypeStruct((m, d), F32),
                          in_specs=[row, row, pl.BlockSpec((1, d), lambda i: (0, 0))], out_specs=row,
                          compiler_params=_params(grid))(h, z, g)


def _rms_bwd_math(x, g, dy):
    r = _rstd(x)
    dyg = dy * g
    dx = r * dyg - x * (r * r * r) * jnp.mean(x * dyg, axis=-1, keepdims=True)
    dg = jnp.sum(dy * x * r, axis=0, keepdims=True)
    return dx, dg


def _rms_bwd(x, g, dy, dres, out_dtype, name):
    m, d = x.shape
    tr = _row_tile(m)
    grid = (m // tr,)
    has_res = dres is not None

    def body(*refs):
        x_ref, g_ref, dy_ref = refs[:3]
        dx_ref, dg_ref = refs[-2:]
        dx, dg = _rms_bwd_math(x_ref[...], g_ref[...], dy_ref[...].astype(F32))
        if has_res:
            dx = dx + refs[3][...]
        dx_ref[...] = dx.astype(out_dtype)

        @pl.when(pl.program_id(0) == 0)
        def _():
            dg_ref[...] = jnp.zeros_like(dg_ref)

        dg_ref[...] += dg

    row = pl.BlockSpec((tr, d), lambda i: (i, 0))
    vec = pl.BlockSpec((1, d), lambda i: (0, 0))
    ops = [x, g, dy] + ([dres] if has_res else [])
    return pl.pallas_call(
        body, name=name, grid=grid,
        out_shape=(jax.ShapeDtypeStruct((m, d), out_dtype), jax.ShapeDtypeStruct((1, d), F32)),
        in_specs=[row, vec, row] + ([row] if has_res else []), out_specs=(row, vec),
        compiler_params=_params(grid))(*ops)


def _merge_fwd(oa, op, oc, g):
    m = oa.shape[0]
    da, dp, dc = oa.shape[1], op.shape[1], oc.shape[1]
    d = da + dp + dc
    tr = _row_tile(m)
    grid = (m // tr,)

    def body(a_ref, p_ref, c_ref, g_ref, o_ref):
        off = 0
        for ref, w in ((a_ref, da), (p_ref, dp), (c_ref, dc)):
            xx = ref[...]
            o_ref[:, off:off + w] = (xx * _rstd(xx) * g_ref[:, off:off + w]).astype(BF16)
            off += w

    specs = [pl.BlockSpec((tr, w), lambda i: (i, 0)) for w in (da, dp, dc)]
    return pl.pallas_call(body, name="merge_fwd", grid=grid, out_shape=jax.ShapeDtypeStruct((m, d), BF16),
                          in_specs=specs + [pl.BlockSpec((1, d), lambda i: (0, 0))],
                          out_specs=pl.BlockSpec((tr, d), lambda i: (i, 0)), compiler_params=_params(grid))(oa, op, oc, g)


def _merge_bwd(oa, op, oc, g, dmerged):
    m = oa.shape[0]
    da, dp, dc = oa.shape[1], op.shape[1], oc.shape[1]
    d = da + dp + dc
    tr = _row_tile(m)
    grid = (m // tr,)

    def body(a_ref, p_ref, c_ref, g_ref, dm_ref, da_ref, dp_ref, dc_ref, dg_ref):
        @pl.when(pl.program_id(0) == 0)
        def _():
            dg_ref[...] = jnp.zeros_like(dg_ref)

        off = 0
        for ref, oref, w in ((a_ref, da_ref, da), (p_ref, dp_ref, dp), (c_ref, dc_ref, dc)):
            dx, dg = _rms_bwd_math(ref[...], g_ref[:, off:off + w], dm_ref[:, off:off + w])
            oref[...] = dx.astype(oref.dtype)
            dg_ref[:, off:off + w] += dg
            off += w

    specs = [pl.BlockSpec((tr, w), lambda i: (i, 0)) for w in (da, dp, dc)]
    vec = pl.BlockSpec((1, d), lambda i: (0, 0))
    return pl.pallas_call(
        body, name="merge_bwd", grid=grid,
        out_shape=(jax.ShapeDtypeStruct((m, da), F32), jax.ShapeDtypeStruct((m, dp), F32),
                   jax.ShapeDtypeStruct((m, dc), BF16), jax.ShapeDtypeStruct((1, d), F32)),
        in_specs=specs + [vec, pl.BlockSpec((tr, d), lambda i: (i, 0))], out_specs=tuple(specs) + (vec,),
        compiler_params=_params(grid))(oa, op, oc, g, dmerged)


def _loss_grad(h, target, n_meta, n_tok):
    m, d = h.shape
    tr = _row_tile(m)
    grid = (m // tr,)

    def body(h_ref, t_ref, loss_ref, dh_ref):
        i = pl.program_id(0)
        row = i * tr + lax.broadcasted_iota(jnp.int32, (tr, 1), 0)
        live = jnp.logical_and(row >= n_meta, row < n_meta + n_tok)
        diff = jnp.where(live, h_ref[...] - t_ref[...], 0.0)
        dh_ref[...] = diff * (1.0 / d)

        @pl.when(i == 0)
        def _():
            loss_ref[...] = jnp.zeros_like(loss_ref)

        loss_ref[...] += jnp.sum(jnp.sum(diff * diff, axis=1, keepdims=True), axis=0, keepdims=True) * (0.5 / d)

    row = pl.BlockSpec((tr, d), lambda i: (i, 0))
    return pl.pallas_call(
        body, name="loss_grad", grid=grid,
        out_shape=(jax.ShapeDtypeStruct((1, 1), F32), jax.ShapeDtypeStruct((m, d), F32)),
        in_specs=[row, row], out_specs=(pl.BlockSpec((1, 1), lambda i: (0, 0)), row),
        compiler_params=_params(grid))(h, target)


def _split_dot(x, tmat):
    hi = x.astype(BF16)
    lo = (x - hi.astype(F32)).astype(BF16)
    return _dot(hi, tmat) + _dot(lo, tmat)


def _softplus(z):
    return jnp.maximum(z, 0.0) + jnp.log1p(jnp.exp(-jnp.abs(z)))


def _attn_consts(t, nbp):
    row = lax.broadcasted_iota(jnp.int32, (t, t), 0)
    col = lax.broadcasted_iota(jnp.int32, (t, t), 1)
    lane = lax.broadcasted_iota(jnp.int32, (t, nbp), 1)
    return row, col, lane


def _attn_fwd(q, k, v):
    nh, lp, dh = q.shape
    t = ATT_BLOCK
    nb = lp // t
    nbp = -(-nb // 128) * 128
    grid = (nh, nb)

    def body(q_ref, k_ref, v_ref, o_ref, c_ref, cs_ref):
        i = pl.program_id(1)
        qb = q_ref[...]
        row, col, lane = _attn_consts(t, nbp)
        tri = col < row
        after = (row > col).astype(BF16)
        o_ref[...] = jnp.zeros_like(o_ref)
        c_ref[...] = jnp.zeros_like(c_ref)
        cs_ref[...] = jnp.zeros_like(cs_ref)

        def step(n, carry):
            j = i - n
            start = pl.multiple_of(j * t, t)
            kb = k_ref[pl.ds(start, t), :]
            vb = v_ref[pl.ds(start, t), :]
            z = _dot(qb, kb, NT)
            mask = jnp.logical_or(j < i, tri)
            sp = _softplus(z)
            lnb = jnp.where(mask, -sp, 0.0)
            cs = cs_ref[...]
            later = _split_dot(lnb, after) + cs
            w = jnp.where(mask, jnp.exp(z - sp + later), 0.0)
            o_ref[...] += _dot(w.astype(BF16), vb)
            c_ref[...] += jnp.where(lane == j, cs, 0.0)
            cs_ref[...] = cs + jnp.sum(lnb, axis=1, keepdims=True)
            return carry

        lax.fori_loop(0, i + 1, step, 0)

    blk = pl.BlockSpec((None, t, dh), lambda h, i: (h, i, 0))
    full = pl.BlockSpec((None, lp, dh), lambda h, i: (h, 0, 0))
    return pl.pallas_call(
        body, name="attn_fwd", grid=grid,
        out_shape=(jax.ShapeDtypeStruct((nh, lp, dh), F32), jax.ShapeDtypeStruct((nh, nb, t, nbp), F32)),
        in_specs=[blk, full, full],
        out_specs=(blk, pl.BlockSpec((None, None, t, nbp), lambda h, i: (h, i, 0, 0))),
        scratch_shapes=[pltpu.VMEM((t, 1), F32)],
        compiler_params=_params(grid))(q, k, v)


def _attn_bwd(q, k, v, do, cmat, scale):
    nh, lp, dh = q.shape
    t = ATT_BLOCK
    nb = lp // t
    nbp = cmat.shape[-1]
    grid = (nh, nb)

    def body(q_ref, k_ref, v_ref, do_ref, c_ref, dq_ref, dk_ref, dv_ref, gp_ref):
        i = pl.program_id(1)

        @pl.when(i == 0)
        def _():
            dk_ref[...] = jnp.zeros_like(dk_ref)
            dv_ref[...] = jnp.zeros_like(dv_ref)

        dq_ref[...] = jnp.zeros_like(dq_ref)
        gp_ref[...] = jnp.zeros_like(gp_ref)
        qb = q_ref[...]
        dob = do_ref[...].astype(BF16)
        cm = c_ref[...]
        row, col, lane = _attn_consts(t, nbp)
        tri = col < row
        after = (row > col).astype(BF16)
        before = (row < col).astype(BF16)

        def step(j, carry):
            gp = gp_ref[...]
            start = pl.multiple_of(j * t, t)
            kb = k_ref[pl.ds(start, t), :]
            vb = v_ref[pl.ds(start, t), :]
            z = _dot(qb, kb, NT)
            mask = jnp.logical_or(j < i, tri)
            sp = _softplus(z)
            lnb = jnp.where(mask, -sp, 0.0)
            cs = jnp.sum(jnp.where(lane == j, cm, 0.0), axis=1, keepdims=True)
            later = _split_dot(lnb, after) + cs
            logsig = z - sp
            w = jnp.where(mask, jnp.exp(logsig + later), 0.0)
            dlogw = w * _dot(dob, vb, NT)
            gin = _split_dot(dlogw, before) + gp
            sig = jnp.exp(logsig)
            dz = jnp.where(mask, dlogw * (1.0 - sig) - gin * sig, 0.0).astype(BF16)
            dq_ref[...] += _dot(dz, kb)
            dk_ref[pl.ds(start, t), :] += _dot(dz, qb, TN)
            dv_ref[pl.ds(start, t), :] += _dot(w.astype(BF16), dob, TN)
            gp_ref[...] = gp + jnp.sum(dlogw, axis=1, keepdims=True)
            return carry

        lax.fori_loop(0, i + 1, step, 0)
        dq_ref[...] = dq_ref[...] * scale

    blk = pl.BlockSpec((None, t, dh), lambda h, i: (h, i, 0))
    full = pl.BlockSpec((None, lp, dh), lambda h, i: (h, 0, 0))
    shp = jax.ShapeDtypeStruct((nh, lp, dh), F32)
    return pl.pallas_call(
        body, name="attn_bwd", grid=grid, out_shape=(shp, shp, shp),
        in_specs=[blk, full, full, blk, pl.BlockSpec((None, None, t, nbp), lambda h, i: (h, i, 0, 0))],
        out_specs=(blk, full, full), scratch_shapes=[pltpu.VMEM((t, 1), F32)],
        compiler_params=_params(grid))(q, k, v, do, cmat)


def _halo_specs(tm, width, nt):
    per = tm // HALO
    prev = pl.BlockSpec((HALO, width), lambda i: (jnp.maximum(i * per - 1, 0), 0))
    nxt = pl.BlockSpec((HALO, width), lambda i: (jnp.minimum((i + 1) * per, nt * per - 1), 0))
    return prev, nxt


def _shift_down(x, k):
    return x if k == 0 else pltpu.roll(x, k, axis=0)


def _shift_up(x, k):
    return x if k == 0 else pltpu.roll(x, x.shape[0] - k, axis=0)


def _pool_fwd(u, wp, scale, layer):
    m, dp = u.shape
    g = dp // len(POOL_WINDOWS)
    tm = _row_tile(m)
    nt = m // tm
    grid = (nt,)
    prev, _ = _halo_specs(tm, dp, nt)

    def body(u_ref, h_ref, w_ref, s_ref, o_ref):
        i = pl.program_id(0)
        halo = jnp.where(i > 0, h_ref[...], 0.0)
        s = jnp.concatenate([halo, u_ref[...]], axis=0)
        tpos = i * tm + lax.broadcasted_iota(jnp.int32, (tm, 1), 0)
        for gi, win in enumerate(POOL_WINDOWS):
            s = s + _shift_down(s, win // 2)
            cols = slice(gi * g, (gi + 1) * g)
            cnt = jnp.minimum(tpos + 1, win).astype(F32)
            pooled = s[HALO:, cols] / cnt - u_ref[:, cols]
            o_ref[:, cols] = _dot(pooled.astype(BF16), w_ref[gi]) * s_ref[:, cols]

    row = pl.BlockSpec((tm, dp), lambda i: (i, 0))
    return pl.pallas_call(
        body, name="pool_fwd", grid=grid, out_shape=jax.ShapeDtypeStruct((m, dp), F32),
        in_specs=[row, prev, pl.BlockSpec((None, len(POOL_WINDOWS), g, g), lambda i: (layer, 0, 0, 0)),
                  pl.BlockSpec((1, dp), lambda i: (0, 0))],
        out_specs=row, compiler_params=_params(grid))(u, u, wp, scale)


def _pool_bwd(u, dmix, wp, scale, layer):
    m, dp = u.shape
    ng = len(POOL_WINDOWS)
    g = dp // ng
    tm = _row_tile(m)
    nt = m // tm
    grid = (nt,)
    prev, nxt = _halo_specs(tm, dp, nt)

    def body(u_ref, h_ref, dm_ref, dmn_ref, w_ref, s_ref, du_ref, dw_ref, ds_ref):
        i = pl.program_id(0)

        @pl.when(i == 0)
        def _():
            dw_ref[...] = jnp.zeros_like(dw_ref)
            ds_ref[...] = jnp.zeros_like(ds_ref)

        halo = jnp.where(i > 0, h_ref[...], 0.0)
        s = jnp.concatenate([halo, u_ref[...]], axis=0)
        dmn = jnp.where(i < nt - 1, dmn_ref[...], 0.0)
        dmc = jnp.concatenate([dm_ref[...], dmn], axis=0)
        tpos = i * tm + lax.broadcasted_iota(jnp.int32, (tm, 1), 0)
        tpos_x = i * tm + lax.broadcasted_iota(jnp.int32, (tm + HALO, 1), 0)
        for gi, win in enumerate(POOL_WINDOWS):
            s = s + _shift_down(s, win // 2)
            cols = slice(gi * g, (gi + 1) * g)
            cnt = jnp.minimum(tpos + 1, win).astype(F32)
            pooled = (s[HALO:, cols] / cnt - u_ref[:, cols]).astype(BF16)
            wg = w_ref[gi]
            pm = _dot(pooled, wg)
            ds_ref[:, cols] += jnp.sum(dm_ref[:, cols] * pm, axis=0, keepdims=True)
            dpm = (dmc[:, cols] * s_ref[:, cols]).astype(BF16)
            dw_ref[gi] += _dot(pooled, dpm[:tm], TN)
            dpool = _dot(dpm, wg, NT)
            f = dpool / jnp.minimum(tpos_x + 1, win).astype(F32)
            step = 1
            while step < win:
                f = f + _shift_up(f, step)
                step *= 2
            du_ref[:, cols] = f[:tm] - dpool[:tm]

    row = pl.BlockSpec((tm, dp), lambda i: (i, 0))
    return pl.pallas_call(
        body, name="pool_bwd", grid=grid,
        out_shape=(jax.ShapeDtypeStruct((m, dp), F32), jax.ShapeDtypeStruct((ng, g, g), F32),
                   jax.ShapeDtypeStruct((1, dp), F32)),
        in_specs=[row, prev, row, nxt, pl.BlockSpec((None, ng, g, g), lambda i: (layer, 0, 0, 0)),
                  pl.BlockSpec((1, dp), lambda i: (0, 0))],
        out_specs=(row, pl.BlockSpec((ng, g, g), lambda i: (0, 0, 0)), pl.BlockSpec((1, dp), lambda i: (0, 0))),
        compiler_params=_params(grid))(u, u, dmix, dmix, wp, scale)


def _conv_taps(u, w_ref, rows):
    y = None
    for k in range(CONV_WIDTH):
        term = _shift_down(u, CONV_WIDTH - 1 - k)[HALO:HALO + rows] * w_ref[k:k + 1, :]
        y = term if y is None else y + term
    return y


def _layernorm_stats(y):
    mu = jnp.mean(y, axis=-1, keepdims=True)
    yc = y - mu
    rstd = lax.rsqrt(jnp.mean(yc * yc, axis=-1, keepdims=True) + EPS)
    return yc * rstd, rstd


def _conv_fwd(uc, wdw, b, lg, lb):
    m, c2 = uc.shape
    c = c2 // 2
    tm = _row_tile(m)
    nt = m // tm
    grid = (nt,)
    prev, _ = _halo_specs(tm, c2, nt)

    def body(x_ref, h_ref, w_ref, b_ref, g_ref, bb_ref, o_ref):
        i = pl.program_id(0)
        halo = jnp.where(i > 0, h_ref[...], 0.0)
        xc = jnp.concatenate([halo, x_ref[...]], axis=0)
        u = xc[:, :c] * jax.nn.sigmoid(xc[:, c:])
        y = _conv_taps(u, w_ref, tm) + b_ref[...]
        xhat, _ = _layernorm_stats(y)
        ln = xhat * g_ref[...] + bb_ref[...]
        o_ref[...] = (ln * jax.nn.sigmoid(ln)).astype(BF16)

    vec = pl.BlockSpec((1, c), lambda i: (0, 0))
    return pl.pallas_call(
        body, name="conv_fwd", grid=grid, out_shape=jax.ShapeDtypeStruct((m, c), BF16),
        in_specs=[pl.BlockSpec((tm, c2), lambda i: (i, 0)), prev, pl.BlockSpec((HALO, c), lambda i: (0, 0)), vec, vec, vec],
        out_specs=pl.BlockSpec((tm, c), lambda i: (i, 0)), compiler_params=_params(grid))(uc, uc, wdw, b, lg, lb)


def _conv_bwd(uc, ds, wdw, b, lg, lb):
    m, c2 = uc.shape
    c = c2 // 2
    tm = _row_tile(m)
    nt = m // tm
    grid = (nt,)
    prev, nxt = _halo_specs(tm, c2, nt)
    _, nxt_c = _halo_specs(tm, c, nt)

    def body(x_ref, hp_ref, hn_ref, ds_ref, dsn_ref, w_ref, b_ref, g_ref, bb_ref,
             dx_ref, dw_ref, db_ref, dg_ref, dbb_ref):
        i = pl.program_id(0)

        @pl.when(i == 0)
        def _():
            for ref in (dw_ref, db_ref, dg_ref, dbb_ref):
                ref[...] = jnp.zeros_like(ref)

        hp = jnp.where(i > 0, hp_ref[...], 0.0)
        xc = jnp.concatenate([hp, x_ref[...], hn_ref[...]], axis=0)
        sg = jax.nn.sigmoid(xc[:, c:])
        u = xc[:, :c] * sg
        rows = tm + HALO
        y = _conv_taps(u, w_ref, rows) + b_ref[...]
        xhat, rstd = _layernorm_stats(y)
        ln = xhat * g_ref[...] + bb_ref[...]
        sl = jax.nn.sigmoid(ln)
        dsn = jnp.where(i < nt - 1, dsn_ref[...], 0.0)
        dsx = jnp.concatenate([ds_ref[...], dsn], axis=0)
        dln = dsx * sl * (1.0 + ln * (1.0 - sl))
        dxh = dln * g_ref[...]
        dy = rstd * (dxh - jnp.mean(dxh, axis=-1, keepdims=True)
                     - xhat * jnp.mean(dxh * xhat, axis=-1, keepdims=True))
        dyt = dy[:tm]
        dg_ref[...] += jnp.sum(dln[:tm] * xhat[:tm], axis=0, keepdims=True)
        dbb_ref[...] += jnp.sum(dln[:tm], axis=0, keepdims=True)
        db_ref[...] += jnp.sum(dyt, axis=0, keepdims=True)
        du = None
        for k in range(CONV_WIDTH):
            lag = CONV_WIDTH - 1 - k
            dw_ref[k:k + 1, :] += jnp.sum(dyt * _shift_down(u, lag)[HALO:HALO + tm], axis=0, keepdims=True)
            term = _shift_up(dy, lag)[:tm] * w_ref[k:k + 1, :]
            du = term if du is None else du + term
        a_t = xc[HALO:HALO + tm, :c]
        sg_t = sg[HALO:HALO + tm]
        dx_ref[:, :c] = du * sg_t
        dx_ref[:, c:] = du * a_t * sg_t * (1.0 - sg_t)

    vec = pl.BlockSpec((1, c), lambda i: (0, 0))
    wsp = pl.BlockSpec((HALO, c), lambda i: (0, 0))
    vshape = jax.ShapeDtypeStruct((1, c), F32)
    return pl.pallas_call(
        body, name="conv_bwd", grid=grid,
        out_shape=(jax.ShapeDtypeStruct((m, c2), F32), jax.ShapeDtypeStruct((HALO, c), F32), vshape, vshape, vshape),
        in_specs=[pl.BlockSpec((tm, c2), lambda i: (i, 0)), prev, nxt, pl.BlockSpec((tm, c), lambda i: (i, 0)), nxt_c,
                  wsp, vec, vec, vec],
        out_specs=(pl.BlockSpec((tm, c2), lambda i: (i, 0)), wsp, vec, vec, vec),
        compiler_params=_params(grid))(uc, uc, uc, ds, ds, wdw, b, lg, lb)


def _mesh_pos():
    return lax.axis_index("x"), lax.axis_index("y"), lax.axis_index("c")


def _other_chips(x, y):
    return [(1 - x, y), (x, 1 - y), (1 - x, 1 - y)]


def _comm_call(body, name, arrs, out_shapes, n_sems):
    n = len(arrs)
    return pl.pallas_call(
        body, name=name, out_shape=out_shapes, in_specs=[ANY] * n, out_specs=tuple([ANY] * len(out_shapes)),
        scratch_shapes=[pltpu.SemaphoreType.DMA((n, n_sems)), pltpu.SemaphoreType.DMA((n, n_sems)),
                        pltpu.SemaphoreType.DMA((n, 4))],
    )(*arrs)


def _all_gather(arrs, name):
    n = len(arrs)

    def body(*refs):
        ins, outs = refs[:n], refs[n:2 * n]
        send_sems, recv_sems, local_sems = refs[2 * n:]
        x, y, c = _mesh_pos()
        me, sib = (x, y, c), (x, y, 1 - c)
        chips = _other_chips(x, y)

        def slot(a, p):
            return outs[a].at[4 * p[0] + 2 * p[1] + p[2]]

        def copy(a, k, block, to, src=None):
            return pltpu.make_async_remote_copy(
                src_ref=slot(a, block) if src is None else src, dst_ref=slot(a, block),
                send_sem=send_sems.at[a, k], recv_sem=recv_sems.at[a, k], device_id=to, device_id_type=MESH)

        sends, locals_ = [], []
        for a in range(n):
            for j, chip in enumerate(chips):
                sends.append(copy(a, 1 + j, me, (*chip, c), src=ins[a]))
            sends.append(copy(a, 0, me, sib, src=ins[a]))
            locals_.append(pltpu.make_async_copy(ins[a], slot(a, me), local_sems.at[a, 0]))
        for cp in sends + locals_:
            cp.start()
        for a in range(n):
            for j, chip in enumerate(chips):
                copy(a, 1 + j, (*chip, c), me).wait_recv()
                fwd = copy(a, 4 + j, (*chip, c), sib)
                fwd.start()
                sends.append(fwd)
        for a in range(n):
            copy(a, 0, sib, me).wait_recv()
            for j, chip in enumerate(chips):
                copy(a, 4 + j, (*chip, 1 - c), me).wait_recv()
        for cp in sends:
            cp.wait_send()
        for cp in locals_:
            cp.wait()

    outs = tuple(jax.ShapeDtypeStruct((N_DEV,) + a.shape, a.dtype) for a in arrs)
    return _comm_call(body, name, arrs, outs, 7)


def _reduce_pair_exchange(parts, name):
    n = len(parts)

    def body(*refs):
        ins, owns, recvs = refs[:n], refs[n:2 * n], refs[2 * n:3 * n]
        send_sems, recv_sems, local_sems = refs[3 * n:]
        x, y, c = _mesh_pos()
        sib = (x, y, 1 - c)
        copies = []
        for a in range(n):
            for q in range(4):
                copies.append(pltpu.make_async_remote_copy(
                    src_ref=ins[a].at[2 * q + 1 - c], dst_ref=recvs[a].at[q], send_sem=send_sems.at[a, q],
                    recv_sem=recv_sems.at[a, q], device_id=sib, device_id_type=MESH))
                copies.append(pltpu.make_async_copy(ins[a].at[2 * q + c], owns[a].at[q], local_sems.at[a, q]))
        for cp in copies:
            cp.start()
        for cp in copies:
            cp.wait()

    outs = tuple(jax.ShapeDtypeStruct((4,) + p.shape[1:], p.dtype) for p in parts) * 2
    res = _comm_call(body, name, parts, outs, 4)
    return res[:n], res[n:]


def _reduce_chip_exchange(parts, name):
    n = len(parts)

    def body(*refs):
        ins, outs = refs[:n], refs[n:2 * n]
        send_sems, recv_sems, local_sems = refs[2 * n:]
        x, y, c = _mesh_pos()
        mine = 2 * x + y
        chips = _other_chips(x, y)
        sends, locals_ = [], []
        for a in range(n):
            for j, chip in enumerate(chips):
                sends.append(pltpu.make_async_remote_copy(
                    src_ref=ins[a].at[2 * chip[0] + chip[1]], dst_ref=outs[a].at[mine], send_sem=send_sems.at[a, j],
                    recv_sem=recv_sems.at[a, j], device_id=(*chip, c), device_id_type=MESH))
            locals_.append(pltpu.make_async_copy(ins[a].at[mine], outs[a].at[mine], local_sems.at[a, 0]))
        for cp in sends + locals_:
            cp.start()
        for a in range(n):
            for j, chip in enumerate(chips):
                landing = outs[a].at[2 * chip[0] + chip[1]]
                pltpu.make_async_remote_copy(
                    src_ref=landing, dst_ref=landing, send_sem=send_sems.at[a, j], recv_sem=recv_sems.at[a, j],
                    device_id=(x, y, c), device_id_type=MESH).wait_recv()
        for cp in sends:
            cp.wait_send()
        for cp in locals_:
            cp.wait()

    outs = tuple(jax.ShapeDtypeStruct(p.shape, p.dtype) for p in parts)
    return _comm_call(body, name, parts, outs, 3)


def _pair_add(own, recv, name):
    _, nl, kk, n = own.shape
    tr = _pick(kk, (512, 352, 256, 128, 64))
    grid = (4, nl, kk // tr)

    def body(a_ref, b_ref, o_ref):
        o_ref[...] = (a_ref[...].astype(F32) + b_ref[...].astype(F32)).astype(o_ref.dtype)

    spec = pl.BlockSpec((None, None, tr, n), lambda q, l, r: (q, l, r, 0))
    return pl.pallas_call(body, name=name, grid=grid, out_shape=jax.ShapeDtypeStruct(own.shape, own.dtype),
                          in_specs=[spec, spec], out_specs=spec, compiler_params=_params(grid))(own, recv)


def _adamw_math(w, g, m, v):
    m2 = ADAM_B1 * m + (1.0 - ADAM_B1) * g
    v2 = ADAM_B2 * v + (1.0 - ADAM_B2) * (g * g)
    m_hat = m2 / (1.0 - ADAM_B1 ** ADAM_STEP)
    v_hat = v2 / (1.0 - ADAM_B2 ** ADAM_STEP)
    delta = -ADAM_LR * (m_hat / (jnp.sqrt(v_hat) + ADAM_EPS) + ADAM_WD * w)
    return delta, m2, v2


def _adamw_sharded(parts, w, m, v, name):
    nl, kk, n = w.shape
    tr = _pick(kk, (256, 352, 128, 64))
    grid = (nl, kk // tr)

    def body(p_ref, w_ref, m_ref, v_ref, g_ref, d_ref, m2_ref, v2_ref):
        g = p_ref[0].astype(F32)
        for q in range(1, 4):
            g = g + p_ref[q].astype(F32)
        d, m2, v2 = _adamw_math(w_ref[...], g, m_ref[...], v_ref[...])
        g_ref[...] = g
        d_ref[...] = d
        m2_ref[...] = m2
        v2_ref[...] = v2

    spec = pl.BlockSpec((None, tr, n), lambda l, r: (l, r, 0))
    shp = jax.ShapeDtypeStruct(w.shape, F32)
    return pl.pallas_call(
        body, name=name, grid=grid, out_shape=(shp, shp, shp, shp),
        in_specs=[pl.BlockSpec((4, None, tr, n), lambda l, r: (0, l, r, 0)), spec, spec, spec],
        out_specs=(spec, spec, spec, spec), compiler_params=_params(grid))(parts, w, m, v)


def _sum_slots(g8):
    _, r, _ = g8.shape

    def body(g_ref, o_ref):
        s = g_ref[0]
        for d in range(1, N_DEV):
            s = s + g_ref[d]
        o_ref[...] = s

    return pl.pallas_call(body, name="sum_slots", out_shape=jax.ShapeDtypeStruct((r, 128), F32))(g8)


def _adamw_flat(w, g, m, v):
    def body(w_ref, g_ref, m_ref, v_ref, d_ref, m2_ref, v2_ref):
        d, m2, v2 = _adamw_math(w_ref[...], g_ref[...], m_ref[...], v_ref[...])
        d_ref[...] = d
        m2_ref[...] = m2
        v2_ref[...] = v2

    shp = jax.ShapeDtypeStruct(w.shape, F32)
    return pl.pallas_call(body, name="adamw_flat", out_shape=(shp, shp, shp))(w, g, m, v)


def _pack(arrs):
    flat = jnp.concatenate([a.reshape(-1) for a in arrs])
    pad = (-flat.shape[0]) % 1024
    return jnp.pad(flat, (0, pad)).reshape(-1, 128)


def _unpack(buf, like):
    flat = buf.reshape(-1)
    out, off = [], 0
    for a in like:
        out.append(flat[off:off + a.size].reshape(a.shape))
        off += a.size
    return out


def _heads(x2d, scale=None):
    lp, da = x2d.shape
    xh = x2d.reshape(lp, N_HEADS, da // N_HEADS).transpose(1, 0, 2)
    if scale is not None:
        xh = xh * scale
    return xh


def _unheads(xh):
    nh, lp, dh = xh.shape
    return xh.transpose(1, 0, 2).reshape(lp, nh * dh)


def _layer_fwd(h, l, W, P):
    d = h.shape[1]
    da, dp = d // 2, d // 4
    dh = da // N_HEADS
    sv = {"h": h}
    u1 = _rms_fwd(h, P["pre_mix_g"][l:l + 1], "rms_pre_mix")
    proj_s = _mm_cols(u1, W["in"], l, "mm_in")
    lp = h.shape[0]
    proj = proj_s.transpose(1, 0, 2).reshape(lp, -1)
    qh = _heads(proj[:, :da], dh ** -0.5).astype(BF16)
    kh = _heads(proj[:, da:2 * da]).astype(BF16)
    vh = _heads(proj[:, 2 * da:3 * da]).astype(BF16)
    u_pool = proj[:, 3 * da:3 * da + dp]
    u_conv = proj[:, 3 * da + dp:]
    oh, cmat = _attn_fwd(qh, kh, vh)
    o_attn = _unheads(oh)
    o_pool = _pool_fwd(u_pool, W["pool"], P["pool_scale"][l:l + 1], l)
    s_conv = _conv_fwd(u_conv, W["dw"][l], P["b_dw"][l:l + 1], P["conv_ln_g"][l:l + 1], P["conv_ln_b"][l:l + 1])
    o_conv = _mm_nn(s_conv, W["pw"], "mm_pw", layer=l)
    merged = _merge_fwd(o_attn, o_pool, o_conv, P["mix_out_g"][l:l + 1])
    z1 = _mm_nn(merged, W["out"], "mm_out", layer=l)
    h1 = _resid_rms_fwd(h, z1, P["post_mix_g"][l:l + 1], "resid_post_mix")
    u2 = _rms_fwd(h1, P["pre_ffn_g"][l:l + 1], "rms_pre_ffn")
    a_s, b_s, act_s = _ffn_up(u2, W["gate"], W["up"], l)
    ff = _mm_rows(act_s, W["down"], l, "mm_down")
    h2 = _resid_rms_fwd(h1, ff, P["post_ffn_g"][l:l + 1], "resid_post_ffn")
    sv.update(u1=u1, qh=qh, kh=kh, vh=vh, cmat=cmat, u_pool=u_pool, u_conv=u_conv, o_attn=o_attn, o_pool=o_pool,
              o_conv=o_conv, s_conv=s_conv, merged=merged, z1=z1, h1=h1, u2=u2, a_s=a_s, b_s=b_s, act_s=act_s, ff=ff)
    return h2, sv


def _layer_bwd(dh_out, l, W, P, sv):
    d = dh_out.shape[1]
    da = d // 2
    dhd = da // N_HEADS
    lp = dh_out.shape[0]
    g = {}
    dff, g["post_ffn_g"] = _rms_bwd(sv["ff"], P["post_ffn_g"][l:l + 1], dh_out, None, BF16, "rms_bwd_post_ffn")
    da_s, db_s = _ffn_bwd_act(dff, W["down"], l, sv["a_s"], sv["b_s"])
    g["w_down"] = _mm_tn_rows(sv["act_s"], dff, "mm_dw_down")
    du2 = _mm_cols_t([da_s, db_s], [W["gate"], W["up"]], l, "mm_d_u2")
    g["w_gate"] = _mm_tn_cols(sv["u2"], da_s, "mm_dw_gate")
    g["w_up"] = _mm_tn_cols(sv["u2"], db_s, "mm_dw_up")
    dh1, g["pre_ffn_g"] = _rms_bwd(sv["h1"], P["pre_ffn_g"][l:l + 1], du2, dh_out, F32, "rms_bwd_pre_ffn")
    dz1, g["post_mix_g"] = _rms_bwd(sv["z1"], P["post_mix_g"][l:l + 1], dh1, None, BF16, "rms_bwd_post_mix")
    dmerged = _mm_nt(dz1, W["out"], "mm_d_merged", l)
    g["w_out"] = _mm_tn(sv["merged"], dz1, "mm_dw_out")
    d_oa, d_op, d_oc, g["mix_out_g"] = _merge_bwd(sv["o_attn"], sv["o_pool"], sv["o_conv"], P["mix_out_g"][l:l + 1], dmerged)
    d_s = _mm_nt(d_oc, W["pw"], "mm_d_sconv", l)
    g["w_pw"] = _mm_tn(sv["s_conv"], d_oc, "mm_dw_pw")
    d_uc, dwdw, g["b_dw"], g["conv_ln_g"], g["conv_ln_b"] = _conv_bwd(
        sv["u_conv"], d_s, W["dw"][l], P["b_dw"][l:l + 1], P["conv_ln_g"][l:l + 1], P["conv_ln_b"][l:l + 1])
    g["w_dw"] = dwdw[:CONV_WIDTH]
    d_up, g["w_pool"], g["pool_scale"] = _pool_bwd(sv["u_pool"], d_op, W["pool"], P["pool_scale"][l:l + 1], l)
    dqh, dkh, dvh = _attn_bwd(sv["qh"], sv["kh"], sv["vh"], _heads(d_oa), sv["cmat"], dhd ** -0.5)
    dproj = jnp.concatenate([_unheads(dqh), _unheads(dkh), _unheads(dvh), d_up, d_uc], axis=1).astype(BF16)
    dproj_s = dproj.reshape(lp, N_DEV, -1).transpose(1, 0, 2)
    du1 = _mm_cols_t([dproj_s], [W["in"]], l, "mm_d_u1")
    g["w_in"] = _mm_tn_cols(sv["u1"], dproj_s, "mm_dw_in")
    dh0, g["pre_mix_g"] = _rms_bwd(sv["h"], P["pre_mix_g"][l:l + 1], du1, dh1, F32, "rms_bwd_pre_mix")
    return dh0, g


SHARDED = ("w_in", "w_gate", "w_up", "w_down", "w_out", "w_pw")
SMALL = ("pre_mix_g", "w_pool", "pool_scale", "b_dw", "conv_ln_g", "conv_ln_b", "mix_out_g", "post_mix_g",
         "pre_ffn_g", "post_ffn_g", "w_dw", "meta_tokens")
ORDER = ("meta_tokens", "pre_mix_g", "w_in", "w_pool", "pool_scale", "w_dw", "b_dw", "conv_ln_g", "conv_ln_b", "w_pw",
         "mix_out_g", "w_out", "post_mix_g", "pre_ffn_g", "w_gate", "w_up", "w_down", "post_ffn_g")


def kernel(x, meta_tokens, pre_mix_g, w_in, w_pool, pool_scale, w_dw, b_dw, conv_ln_g, conv_ln_b, w_pw, mix_out_g, w_out, post_mix_g, pre_ffn_g, w_gate, w_up, w_down, post_ffn_g, loss_target, m_meta_tokens, m_pre_mix_g, m_w_in, m_w_pool, m_pool_scale, m_w_dw, m_b_dw, m_conv_ln_g, m_conv_ln_b, m_w_pw, m_mix_out_g, m_w_out, m_post_mix_g, m_pre_ffn_g, m_w_gate, m_w_up, m_w_down, m_post_ffn_g, v_meta_tokens, v_pre_mix_g, v_w_in, v_w_pool, v_pool_scale, v_w_dw, v_b_dw, v_conv_ln_g, v_conv_ln_b, v_w_pw, v_mix_out_g, v_w_out, v_post_mix_g, v_pre_ffn_g, v_w_gate, v_w_up, v_w_down, v_post_ffn_g):
    P = dict(meta_tokens=meta_tokens, pre_mix_g=pre_mix_g, w_in=w_in, w_pool=w_pool, pool_scale=pool_scale, w_dw=w_dw,
             b_dw=b_dw, conv_ln_g=conv_ln_g, conv_ln_b=conv_ln_b, w_pw=w_pw, mix_out_g=mix_out_g, w_out=w_out,
             post_mix_g=post_mix_g, pre_ffn_g=pre_ffn_g, w_gate=w_gate, w_up=w_up, w_down=w_down, post_ffn_g=post_ffn_g)
    M = dict(meta_tokens=m_meta_tokens, pre_mix_g=m_pre_mix_g, w_in=m_w_in, w_pool=m_w_pool, pool_scale=m_pool_scale,
             w_dw=m_w_dw, b_dw=m_b_dw, conv_ln_g=m_conv_ln_g, conv_ln_b=m_conv_ln_b, w_pw=m_w_pw, mix_out_g=m_mix_out_g,
             w_out=m_w_out, post_mix_g=m_post_mix_g, pre_ffn_g=m_pre_ffn_g, w_gate=m_w_gate, w_up=m_w_up,
             w_down=m_w_down, post_ffn_g=m_post_ffn_g)
    V = dict(meta_tokens=v_meta_tokens, pre_mix_g=v_pre_mix_g, w_in=v_w_in, w_pool=v_w_pool, pool_scale=v_pool_scale,
             w_dw=v_w_dw, b_dw=v_b_dw, conv_ln_g=v_conv_ln_g, conv_ln_b=v_conv_ln_b, w_pw=v_w_pw, mix_out_g=v_mix_out_g,
             w_out=v_w_out, post_mix_g=v_post_mix_g, pre_ffn_g=v_pre_ffn_g, w_gate=v_w_gate, w_up=v_w_up,
             w_down=v_w_down, post_ffn_g=v_post_ffn_g)
    xi, yi, ci = _mesh_pos()
    dev = 4 * xi + 2 * yi + ci
    n_tok, d = x.shape[1], x.shape[2]
    n_meta = meta_tokens.shape[0]
    n_layers = w_in.shape[0]
    c = d // 4
    l_real = n_meta + n_tok
    lp = -(-l_real // ATT_BLOCK) * ATT_BLOCK

    gathered = _all_gather([P[k].astype(BF16) for k in SHARDED], "gather_weights")
    g_dw, g_meta = _all_gather([w_dw, meta_tokens], "gather_small")
    W = dict(zip(("in", "gate", "up", "down", "out", "pw"), gathered))
    W["out"] = W["out"].transpose(1, 0, 2, 3).reshape(n_layers, d, d)
    W["pw"] = W["pw"].transpose(1, 0, 2, 3).reshape(n_layers, c, c)
    W["pool"] = w_pool.astype(BF16)
    wdw_full = g_dw.transpose(1, 2, 0, 3).reshape(n_layers, CONV_WIDTH, c)
    W["dw"] = jnp.pad(wdw_full, ((0, 0), (0, HALO - CONV_WIDTH), (0, 0)))
    meta_full = g_meta.transpose(1, 0, 2).reshape(n_meta, d)

    h = jnp.concatenate([meta_full, x[0], jnp.zeros((lp - l_real, d), F32)], axis=0)
    saved = []
    for l in range(n_layers):
        h, sv = _layer_fwd(h, l, W, P)
        saved.append(sv)
    target = jnp.pad(loss_target[0], ((n_meta, lp - l_real), (0, 0)))
    loss_part, dh = _loss_grad(h, target, n_meta, n_tok)
    loss = lax.psum(loss_part[0, 0], ("x", "y", "c"))

    grads = [None] * n_layers
    for l in reversed(range(n_layers)):
        dh, grads[l] = _layer_bwd(dh, l, W, P, saved[l])
    grad_x = dh[n_meta:l_real][None]

    parts = []
    for k in SHARDED:
        per_layer = [grads[l][k] for l in range(n_layers)]
        per_layer = [p if p.ndim == 3 else p.reshape(N_DEV, p.shape[0] // N_DEV, p.shape[1]) for p in per_layer]
        parts.append(jnp.stack(per_layer, axis=1))
    own, recv = _reduce_pair_exchange(parts, "reduce_pairs")
    pair = [_pair_add(o, r, "pair_add_" + k) for k, o, r in zip(SHARDED, own, recv)]
    by_chip = _reduce_chip_exchange(pair, "reduce_chips")
    out = {}
    for k, p in zip(SHARDED, by_chip):
        out[k] = _adamw_sharded(p, P[k], M[k], V[k], "adamw_" + k)

    small_parts = []
    for k in SMALL:
        if k == "meta_tokens":
            small_parts.append(dh[:n_meta])
        else:
            small_parts.append(jnp.stack([grads[l][k].reshape(P[k].shape[1:] if k != "w_dw" else (CONV_WIDTH, c))
                                          for l in range(n_layers)], axis=0))
    (g8,) = _all_gather([_pack(small_parts)], "gather_small_grads")
    g_small = dict(zip(SMALL, _unpack(_sum_slots(g8), small_parts)))
    g_small["w_dw"] = lax.dynamic_slice_in_dim(g_small["w_dw"], dev * w_dw.shape[2], w_dw.shape[2], axis=2)
    g_small["meta_tokens"] = lax.dynamic_slice_in_dim(g_small["meta_tokens"], dev * meta_tokens.shape[1],
                                                      meta_tokens.shape[1], axis=1)
    like = [P[k] for k in SMALL]
    res = _adamw_flat(_pack(like), _pack([g_small[k] for k in SMALL]), _pack([M[k] for k in SMALL]),
                      _pack([V[k] for k in SMALL]))
    res = [_unpack(r, like) for r in res]
    for i, k in enumerate(SMALL):
        out[k] = (g_small[k], res[0][i], res[1][i], res[2][i])

    return (loss, grad_x, *[out[k][0] for k in ORDER], *[out[k][1] for k in ORDER],
            *[out[k][2] for k in ORDER], *[out[k][3] for k in ORDER])
```

```python
import functools

import jax
import jax.numpy as jnp
from jax import lax
from jax.experimental import pallas as pl
from jax.experimental.pallas import tpu as pltpu

F32 = jnp.float32
BF16 = jnp.bfloat16
EPS = 1e-6
N_HEADS = 16
POOL_WINDOWS = (2, 4, 8, 16)
CONV_WIDTH = 31
HALO = 32
ATT_BLOCK = 128
N_DEV = 8
VMEM_LIMIT = 60 * 1024 * 1024

ADAM_LR = 0.001
ADAM_B1 = 0.9
ADAM_B2 = 0.999
ADAM_EPS = 1e-08
ADAM_WD = 0.01
ADAM_STEP = 10

NN = (((1,), (0,)), ((), ()))
NT = (((1,), (1,)), ((), ()))
TN = (((0,), (0,)), ((), ()))
MESH = pl.DeviceIdType.MESH
ANY = pl.BlockSpec(memory_space=pl.ANY)


def _pick(n, cands):
    for c in cands:
        if c <= n and n % c == 0:
            return c
    return n


def _params(grid):
    return pltpu.CompilerParams(dimension_semantics=("arbitrary",) * len(grid), vmem_limit_bytes=VMEM_LIMIT)


def _dot(a, b, dims=NN):
    return lax.dot_general(a, b, dims, preferred_element_type=F32)


def _matmul(name, pairs, specs, out_shape, out_spec, grid, dims, acc_shape):
    n = len(pairs)
    nk = grid[-1]
    kaxis = len(grid) - 1

    def body(*refs):
        o_ref, acc = refs[2 * n], refs[2 * n + 1]
        tot = None
        for p in range(n):
            d = _dot(refs[2 * p][...], refs[2 * p + 1][...], dims)
            tot = d if tot is None else tot + d
        if nk == 1:
            o_ref[...] = tot.astype(o_ref.dtype)
            return
        k = pl.program_id(kaxis)

        @pl.when(k == 0)
        def _():
            acc[...] = tot

        @pl.when(k > 0)
        def _():
            acc[...] += tot

        @pl.when(k == nk - 1)
        def _():
            o_ref[...] = acc[...].astype(o_ref.dtype)

    ops, in_specs = [], []
    for (a, b), (sa, sb) in zip(pairs, specs):
        ops += [a, b]
        in_specs += [sa, sb]
    return pl.pallas_call(
        body, name=name, out_shape=out_shape, grid=grid, in_specs=in_specs, out_specs=out_spec,
        scratch_shapes=[pltpu.VMEM(acc_shape if nk > 1 else (8, 128), F32)], compiler_params=_params(grid),
    )(*ops)


def _mm_nn(a, b, name, out_dtype=F32, layer=None):
    m, kk = a.shape
    n = b.shape[-1]
    tm = _pick(m, (1408, 1024, 512, 256, 128))
    tn = _pick(n, (1024, 512, 256, 128))
    grid = (m // tm, n // tn, 1)
    sa = pl.BlockSpec((tm, kk), lambda i, j, k: (i, 0))
    if layer is None:
        sb = pl.BlockSpec((kk, tn), lambda i, j, k: (0, j))
    else:
        sb = pl.BlockSpec((None, kk, tn), lambda i, j, k: (layer, 0, j))
    return _matmul(name, [(a, b)], [(sa, sb)], jax.ShapeDtypeStruct((m, n), out_dtype),
                   pl.BlockSpec((tm, tn), lambda i, j, k: (i, j)), grid, NN, (tm, tn))


def _mm_nt(a, b, name, layer, out_dtype=F32):
    m, kk = a.shape
    n = b.shape[1]
    tm = _pick(m, (1408, 1024, 512, 256, 128))
    tn = _pick(n, (1024, 512, 256, 128))
    grid = (m // tm, n // tn, 1)
    sa = pl.BlockSpec((tm, kk), lambda i, j, k: (i, 0))
    sb = pl.BlockSpec((None, tn, kk), lambda i, j, k: (layer, j, 0))
    return _matmul(name, [(a, b)], [(sa, sb)], jax.ShapeDtypeStruct((m, n), out_dtype),
                   pl.BlockSpec((tm, tn), lambda i, j, k: (i, j)), grid, NT, (tm, tn))


def _mm_tn(a, b, name, out_dtype=BF16):
    l, m = a.shape
    n = b.shape[1]
    tl = _pick(l, (1408, 1024, 512, 256, 128))
    tm = _pick(m, (1024, 512, 256, 128))
    tn = _pick(n, (1024, 512, 256, 128))
    grid = (m // tm, n // tn, l // tl)
    sa = pl.BlockSpec((tl, tm), lambda i, j, t: (t, i))
    sb = pl.BlockSpec((tl, tn), lambda i, j, t: (t, j))
    return _matmul(name, [(a, b)], [(sa, sb)], jax.ShapeDtypeStruct((m, n), out_dtype),
                   pl.BlockSpec((tm, tn), lambda i, j, t: (i, j)), grid, TN, (tm, tn))


def _mm_cols(a, w, layer, name):
    m, kk = a.shape
    nd, _, _, n = w.shape
    tm = _pick(m, (1408, 1024, 512, 256, 128))
    grid = (nd, m // tm, 1)
    sa = pl.BlockSpec((tm, kk), lambda d, i, k: (i, 0))
    sb = pl.BlockSpec((None, None, kk, n), lambda d, i, k: (d, layer, 0, 0))
    return _matmul(name, [(a, w)], [(sa, sb)], jax.ShapeDtypeStruct((nd, m, n), F32),
                   pl.BlockSpec((None, tm, n), lambda d, i, k: (d, i, 0)), grid, NN, (tm, n))


def _mm_rows(a_s, w, layer, name):
    nd, m, kk = a_s.shape
    n = w.shape[-1]
    tm = _pick(m, (1408, 1024, 512, 256, 128))
    tn = _pick(n, (1024, 512, 256, 128))
    grid = (m // tm, n // tn, nd)
    sa = pl.BlockSpec((None, tm, kk), lambda i, j, d: (d, i, 0))
    sb = pl.BlockSpec((None, None, kk, tn), lambda i, j, d: (d, layer, 0, j))
    return _matmul(name, [(a_s, w)], [(sa, sb)], jax.ShapeDtypeStruct((m, n), F32),
                   pl.BlockSpec((tm, tn), lambda i, j, d: (i, j)), grid, NN, (tm, tn))


def _mm_cols_t(das, ws, layer, name):
    nd, m, n = das[0].shape
    kk = ws[0].shape[2]
    tm = _pick(m, (1408, 1024, 512, 256, 128))
    tk = _pick(kk, (1024, 512, 256, 128))
    grid = (m // tm, kk // tk, nd)
    sa = pl.BlockSpec((None, tm, n), lambda i, j, d: (d, i, 0))
    sb = pl.BlockSpec((None, None, tk, n), lambda i, j, d: (d, layer, j, 0))
    return _matmul(name, list(zip(das, ws)), [(sa, sb)] * len(das), jax.ShapeDtypeStruct((m, kk), F32),
                   pl.BlockSpec((tm, tk), lambda i, j, d: (i, j)), grid, NT, (tm, tk))


def _mm_tn_cols(a, b_s, name):
    l, kk = a.shape
    nd, _, n = b_s.shape
    tl = _pick(l, (1408, 1024, 512, 256, 128))
    tk = _pick(kk, (1024, 512, 256, 128))
    grid = (nd, kk // tk, l // tl)
    sa = pl.BlockSpec((tl, tk), lambda d, j, t: (t, j))
    sb = pl.BlockSpec((None, tl, n), lambda d, j, t: (d, t, 0))
    return _matmul(name, [(a, b_s)], [(sa, sb)], jax.ShapeDtypeStruct((nd, kk, n), BF16),
                   pl.BlockSpec((None, tk, n), lambda d, j, t: (d, j, 0)), grid, TN, (tk, n))


def _mm_tn_rows(a_s, b, name):
    nd, l, kk = a_s.shape
    n = b.shape[1]
    tl = _pick(l, (1408, 1024, 512, 256, 128))
    tn = _pick(n, (1024, 512, 256, 128))
    grid = (nd, n // tn, l // tl)
    sa = pl.BlockSpec((None, tl, kk), lambda d, j, t: (d, t, 0))
    sb = pl.BlockSpec((tl, tn), lambda d, j, t: (t, j))
    return _matmul(name, [(a_s, b)], [(sa, sb)], jax.ShapeDtypeStruct((nd, kk, n), BF16),
                   pl.BlockSpec((None, kk, tn), lambda d, j, t: (d, 0, j)), grid, TN, (kk, tn))


def _ffn_up(u, wg, wu, layer):
    m, kk = u.shape
    nd, _, _, n = wg.shape
    tm = _pick(m, (704, 512, 256, 128))
    grid = (nd, m // tm)

    def body(u_ref, wg_ref, wu_ref, a_ref, b_ref, act_ref):
        uu = u_ref[...]
        a = _dot(uu, wg_ref[...])
        b = _dot(uu, wu_ref[...])
        a_ref[...] = a
        b_ref[...] = b
        act_ref[...] = (a * jax.nn.sigmoid(a) * b).astype(BF16)

    wspec = pl.BlockSpec((None, None, kk, n), lambda d, i: (d, layer, 0, 0))
    ospec = pl.BlockSpec((None, tm, n), lambda d, i: (d, i, 0))
    return pl.pallas_call(
        body, name="ffn_up", grid=grid,
        out_shape=(jax.ShapeDtypeStruct((nd, m, n), F32), jax.ShapeDtypeStruct((nd, m, n), F32),
                   jax.ShapeDtypeStruct((nd, m, n), BF16)),
        in_specs=[pl.BlockSpec((tm, kk), lambda d, i: (i, 0)), wspec, wspec],
        out_specs=(ospec, ospec, ospec), compiler_params=_params(grid),
    )(u, wg, wu)


def _ffn_bwd_act(dff, wd, layer, a_s, b_s):
    m, kk = dff.shape
    nd, _, n, _ = wd.shape
    tm = _pick(m, (704, 512, 256, 128))
    grid = (nd, m // tm)

    def body(d_ref, w_ref, a_ref, b_ref, da_ref, db_ref):
        dact = _dot(d_ref[...], w_ref[...], NT)
        a, b = a_ref[...], b_ref[...]
        sg = jax.nn.sigmoid(a)
        db_ref[...] = (dact * a * sg).astype(BF16)
        da_ref[...] = (dact * b * sg * (1.0 + a * (1.0 - sg))).astype(BF16)

    tspec = pl.BlockSpec((None, tm, n), lambda d, i: (d, i, 0))
    return pl.pallas_call(
        body, name="ffn_bwd_act", grid=grid,
        out_shape=(jax.ShapeDtypeStruct((nd, m, n), BF16), jax.ShapeDtypeStruct((nd, m, n), BF16)),
        in_specs=[pl.BlockSpec((tm, kk), lambda d, i: (i, 0)),
                  pl.BlockSpec((None, None, n, kk), lambda d, i: (d, layer, 0, 0)), tspec, tspec],
        out_specs=(tspec, tspec), compiler_params=_params(grid),
    )(dff, wd, a_s, b_s)


def _rstd(x):
    return lax.rsqrt(jnp.mean(x * x, axis=-1, keepdims=True) + EPS)


def _row_tile(m):
    return _pick(m, (384, 256, 128))


def _rms_fwd(x, g, name):
    m, d = x.shape
    tr = _row_tile(m)
    grid = (m // tr,)

    def body(x_ref, g_ref, o_ref):
        xx = x_ref[...]
        o_ref[...] = (xx * _rstd(xx) * g_ref[...]).astype(BF16)

    row = pl.BlockSpec((tr, d), lambda i: (i, 0))
    return pl.pallas_call(body, name=name, grid=grid, out_shape=jax.ShapeDtypeStruct((m, d), BF16),
                          in_specs=[row, pl.BlockSpec((1, d), lambda i: (0, 0))], out_specs=row,
                          compiler_params=_params(grid))(x, g)


def _resid_rms_fwd(h, z, g, name):
    m, d = h.shape
    tr = _row_tile(m)
    grid = (m // tr,)

    def body(h_ref, z_ref, g_ref, o_ref):
        zz = z_ref[...]
        o_ref[...] = h_ref[...] + zz * _rstd(zz) * g_ref[...]

    row = pl.BlockSpec((tr, d), lambda i: (i, 0))
    return pl.pallas_call(body, name=name, grid=grid, out_shape=jax.ShapeDtypeStruct((m, d), F32),
                          in_specs=[row, row, pl.BlockSpec((1, d), lambda i: (0, 0))], out_specs=row,
                          compiler_params=_params(grid))(h, z, g)


def _rms_bwd_math(x, g, dy):
    r = _rstd(x)
    dyg = dy * g
    dx = r * dyg - x * (r * r * r) * jnp.mean(x * dyg, axis=-1, keepdims=True)
    dg = jnp.sum(dy * x * r, axis=0, keepdims=True)
    return dx, dg


def _rms_bwd(x, g, dy, dres, out_dtype, name):
    m, d = x.shape
    tr = _row_tile(m)
    grid = (m // tr,)
    has_res = dres is not None

    def body(*refs):
        x_ref, g_ref, dy_ref = refs[:3]
        dx_ref, dg_ref = refs[-2:]
        dx, dg = _rms_bwd_math(x_ref[...], g_ref[...], dy_ref[...].astype(F32))
        if has_res:
            dx = dx + refs[3][...]
        dx_ref[...] = dx.astype(out_dtype)

        @pl.when(pl.program_id(0) == 0)
        def _():
            dg_ref[...] = jnp.zeros_like(dg_ref)

        dg_ref[...] += dg

    row = pl.BlockSpec((tr, d), lambda i: (i, 0))
    vec = pl.BlockSpec((1, d), lambda i: (0, 0))
    ops = [x, g, dy] + ([dres] if has_res else [])
    return pl.pallas_call(
        body, name=name, grid=grid,
        out_shape=(jax.ShapeDtypeStruct((m, d), out_dtype), jax.ShapeDtypeStruct((1, d), F32)),
        in_specs=[row, vec, row] + ([row] if has_res else []), out_specs=(row, vec),
        compiler_params=_params(grid))(*ops)


def _merge_fwd(oa, op, oc, g):
    m = oa.shape[0]
    da, dp, dc = oa.shape[1], op.shape[1], oc.shape[1]
    d = da + dp + dc
    tr = _row_tile(m)
    grid = (m // tr,)

    def body(a_ref, p_ref, c_ref, g_ref, o_ref):
        off = 0
        for ref, w in ((a_ref, da), (p_ref, dp), (c_ref, dc)):
            xx = ref[...]
            o_ref[:, off:off + w] = (xx * _rstd(xx) * g_ref[:, off:off + w]).astype(BF16)
            off += w

    specs = [pl.BlockSpec((tr, w), lambda i: (i, 0)) for w in (da, dp, dc)]
    return pl.pallas_call(body, name="merge_fwd", grid=grid, out_shape=jax.ShapeDtypeStruct((m, d), BF16),
                          in_specs=specs + [pl.BlockSpec((1, d), lambda i: (0, 0))],
                          out_specs=pl.BlockSpec((tr, d), lambda i: (i, 0)), compiler_params=_params(grid))(oa, op, oc, g)


def _merge_bwd(oa, op, oc, g, dmerged):
    m = oa.shape[0]
    da, dp, dc = oa.shape[1], op.shape[1], oc.shape[1]
    d = da + dp + dc
    tr = _row_tile(m)
    grid = (m // tr,)

    def body(a_ref, p_ref, c_ref, g_ref, dm_ref, da_ref, dp_ref, dc_ref, dg_ref):
        @pl.when(pl.program_id(0) == 0)
        def _():
            dg_ref[...] = jnp.zeros_like(dg_ref)

        off = 0
        for ref, oref, w in ((a_ref, da_ref, da), (p_ref, dp_ref, dp), (c_ref, dc_ref, dc)):
            dx, dg = _rms_bwd_math(ref[...], g_ref[:, off:off + w], dm_ref[:, off:off + w])
            oref[...] = dx.astype(oref.dtype)
            dg_ref[:, off:off + w] += dg
            off += w

    specs = [pl.BlockSpec((tr, w), lambda i: (i, 0)) for w in (da, dp, dc)]
    vec = pl.BlockSpec((1, d), lambda i: (0, 0))
    return pl.pallas_call(
        body, name="merge_bwd", grid=grid,
        out_shape=(jax.ShapeDtypeStruct((m, da), F32), jax.ShapeDtypeStruct((m, dp), F32),
                   jax.ShapeDtypeStruct((m, dc), BF16), jax.ShapeDtypeStruct((1, d), F32)),
        in_specs=specs + [vec, pl.BlockSpec((tr, d), lambda i: (i, 0))], out_specs=tuple(specs) + (vec,),
        compiler_params=_params(grid))(oa, op, oc, g, dmerged)


def _loss_grad(h, target, n_meta, n_tok):
    m, d = h.shape
    tr = _row_tile(m)
    grid = (m // tr,)

    def body(h_ref, t_ref, loss_ref, dh_ref):
        i = pl.program_id(0)
        row = i * tr + lax.broadcasted_iota(jnp.int32, (tr, 1), 0)
        live = jnp.logical_and(row >= n_meta, row < n_meta + n_tok)
        diff = jnp.where(live, h_ref[...] - t_ref[...], 0.0)
        dh_ref[...] = diff * (1.0 / d)

        @pl.when(i == 0)
        def _():
            loss_ref[...] = jnp.zeros_like(loss_ref)

        loss_ref[...] += jnp.sum(jnp.sum(diff * diff, axis=1, keepdims=True), axis=0, keepdims=True) * (0.5 / d)

    row = pl.BlockSpec((tr, d), lambda i: (i, 0))
    return pl.pallas_call(
        body, name="loss_grad", grid=grid,
        out_shape=(jax.ShapeDtypeStruct((1, 1), F32), jax.ShapeDtypeStruct((m, d), F32)),
        in_specs=[row, row], out_specs=(pl.BlockSpec((1, 1), lambda i: (0, 0)), row),
        compiler_params=_params(grid))(h, target)


def _split_dot(x, tmat):
    hi = x.astype(BF16)
    lo = (x - hi.astype(F32)).astype(BF16)
    return _dot(hi, tmat) + _dot(lo, tmat)


def _softplus(z):
    return jnp.maximum(z, 0.0) + jnp.log(1.0 + jnp.exp(-jnp.abs(z)))


def _attn_tiles(lp):
    t = ATT_BLOCK
    nb = lp // t
    u = 3 if nb % 3 == 0 else (2 if nb % 2 == 0 else 1)
    return t, nb, u


def _attn_masks(tq, t, u):
    row = lax.broadcasted_iota(jnp.int32, (tq, t), 0)
    col = lax.broadcasted_iota(jnp.int32, (tq, t), 1)
    masks = [col + r * t < row for r in range(u)]
    a = lax.broadcasted_iota(jnp.int32, (t, 2 * t), 0)
    s = lax.broadcasted_iota(jnp.int32, (t, 2 * t), 1)
    after = jnp.logical_or(a > s, s >= t).astype(BF16)
    before = jnp.logical_or(a < s, s >= t).astype(BF16)
    return masks, after, before


def _attn_fwd(q, k, v):
    nh, lp, dh = q.shape
    t, nb, u = _attn_tiles(lp)
    tq = u * t
    grid = (nh, nb // u)

    def body(q_ref, k_ref, v_ref, o_ref, tot_ref):
        i = pl.program_id(1)
        qb = q_ref[...]
        masks, after, _ = _attn_masks(tq, t, u)
        o_ref[...] = jnp.zeros_like(o_ref)
        tot_ref[...] = jnp.zeros_like(tot_ref)

        def blocks(base, masked):
            outs, sums = [], []
            for r in reversed(range(u)):
                start = pl.multiple_of((base + r) * t, t)
                kb = k_ref[pl.ds(start, t), :]
                z = _dot(qb, kb, NT)
                sp = _softplus(z)
                lnb = jnp.where(masks[r], -sp, 0.0) if masked else -sp
                res = _split_dot(lnb, after)
                outs.append((start, z - sp + res[:, :t], r))
                sums.append(res[:, t:])
            cs = tot_ref[...]
            for (start, logw, r), rs in zip(outs, sums):
                w = jnp.exp(logw + cs)
                if masked:
                    w = jnp.where(masks[r], w, 0.0)
                o_ref[...] += _dot(w.astype(BF16), v_ref[pl.ds(start, t), :])
                cs = cs + rs
            tot_ref[...] = cs

        blocks(i * u, True)

        def step(n, carry):
            blocks((i - 1 - n) * u, False)
            return carry

        lax.fori_loop(0, i, step, 0)

    blk = pl.BlockSpec((None, tq, dh), lambda h, i: (h, i, 0))
    full = pl.BlockSpec((None, lp, dh), lambda h, i: (h, 0, 0))
    return pl.pallas_call(
        body, name="attn_fwd", grid=grid,
        out_shape=(jax.ShapeDtypeStruct((nh, lp, dh), F32), jax.ShapeDtypeStruct((nh, lp, t), F32)),
        in_specs=[blk, full, full],
        out_specs=(blk, pl.BlockSpec((None, tq, t), lambda h, i: (h, i, 0))),
        compiler_params=_params(grid))(q, k, v)


def _attn_bwd(q, qt, k, v, do, dot_, tot, scale):
    nh, lp, dh = q.shape
    t, nb, u = _attn_tiles(lp)
    tq = u * t
    grid = (nh, nb // u)

    def body(q_ref, qt_ref, k_ref, v_ref, do_ref, dot_ref, tot_ref, dq_ref, dk_ref, dv_ref, p_ref, g_ref):
        i = pl.program_id(1)

        @pl.when(i == 0)
        def _():
            dk_ref[...] = jnp.zeros_like(dk_ref)
            dv_ref[...] = jnp.zeros_like(dv_ref)

        dq_ref[...] = jnp.zeros_like(dq_ref)
        p_ref[...] = jnp.zeros_like(p_ref)
        g_ref[...] = jnp.zeros_like(g_ref)
        qb, qtb, dob, dotb = q_ref[...], qt_ref[...], do_ref[...], dot_ref[...]
        total = tot_ref[...]
        masks, after, before = _attn_masks(tq, t, u)

        def blocks(base, masked):
            part = []
            for r in range(u):
                j = base + r
                start = pl.multiple_of(j * t, t)
                kb = k_ref[pl.ds(start, t), :]
                vb = v_ref[pl.ds(start, t), :]
                z = _dot(qb, kb, NT)
                sp = _softplus(z)
                lnb = jnp.where(masks[r], -sp, 0.0) if masked else -sp
                res = _split_dot(lnb, after)
                part.append((j, kb, z - sp, res[:, :t], res[:, t:], _dot(dob, vb, NT), r))
            pfx = p_ref[...]
            gp = g_ref[...]
            for j, kb, logsig, later_in, rs, dw, r in part:
                pfx = pfx + rs
                w = jnp.exp(logsig + later_in + (total - pfx))
                if masked:
                    w = jnp.where(masks[r], w, 0.0)
                dlogw = w * dw
                res2 = _split_dot(dlogw, before)
                dz = dlogw - jnp.exp(logsig) * (dlogw + res2[:, :t] + gp)
                if masked:
                    dz = jnp.where(masks[r], dz, 0.0)
                gp = gp + res2[:, t:]
                dzb = dz.astype(BF16)
                dq_ref[...] += _dot(dzb, kb)
                dk_ref[j] += _dot(qtb, dzb)
                dv_ref[j] += _dot(dotb, w.astype(BF16))
            p_ref[...] = pfx
            g_ref[...] = gp

        def step(n, carry):
            blocks(n * u, False)
            return carry

        lax.fori_loop(0, i, step, 0)
        blocks(i * u, True)
        dq_ref[...] = dq_ref[...] * scale

    blk = pl.BlockSpec((None, tq, dh), lambda h, i: (h, i, 0))
    blk_t = pl.BlockSpec((None, dh, tq), lambda h, i: (h, 0, i))
    full = pl.BlockSpec((None, lp, dh), lambda h, i: (h, 0, 0))
    acc = pl.BlockSpec((None, nb, dh, t), lambda h, i: (h, 0, 0, 0))
    acc_shape = jax.ShapeDtypeStruct((nh, nb, dh, t), F32)
    return pl.pallas_call(
        body, name="attn_bwd", grid=grid,
        out_shape=(jax.ShapeDtypeStruct((nh, lp, dh), F32), acc_shape, acc_shape),
        in_specs=[blk, blk_t, full, full, blk, blk_t, pl.BlockSpec((None, tq, t), lambda h, i: (h, i, 0))],
        out_specs=(blk, acc, acc),
        scratch_shapes=[pltpu.VMEM((tq, t), F32), pltpu.VMEM((tq, t), F32)],
        compiler_params=_params(grid))(q, qt, k, v, do, dot_, tot)


def _halo_specs(tm, width, nt):
    per = tm // HALO
    prev = pl.BlockSpec((HALO, width), lambda i: (jnp.maximum(i * per - 1, 0), 0))
    nxt = pl.BlockSpec((HALO, width), lambda i: (jnp.minimum((i + 1) * per, nt * per - 1), 0))
    return prev, nxt


def _shift_down(x, k):
    return x if k == 0 else pltpu.roll(x, k, axis=0)


def _shift_up(x, k):
    return x if k == 0 else pltpu.roll(x, x.shape[0] - k, axis=0)


def _pool_fwd(u, wp, scale, layer):
    m, dp = u.shape
    g = dp // len(POOL_WINDOWS)
    tm = _row_tile(m)
    nt = m // tm
    grid = (nt,)
    prev, _ = _halo_specs(tm, dp, nt)

    def body(u_ref, h_ref, w_ref, s_ref, o_ref):
        i = pl.program_id(0)
        halo = jnp.where(i > 0, h_ref[...], 0.0)
        s = jnp.concatenate([halo, u_ref[...]], axis=0)
        tpos = i * tm + lax.broadcasted_iota(jnp.int32, (tm, 1), 0)
        for gi, win in enumerate(POOL_WINDOWS):
            s = s + _shift_down(s, win // 2)
            cols = slice(gi * g, (gi + 1) * g)
            cnt = jnp.minimum(tpos + 1, win).astype(F32)
            pooled = s[HALO:, cols] / cnt - u_ref[:, cols]
            o_ref[:, cols] = _dot(pooled.astype(BF16), w_ref[gi]) * s_ref[:, cols]

    row = pl.BlockSpec((tm, dp), lambda i: (i, 0))
    return pl.pallas_call(
        body, name="pool_fwd", grid=grid, out_shape=jax.ShapeDtypeStruct((m, dp), F32),
        in_specs=[row, prev, pl.BlockSpec((None, len(POOL_WINDOWS), g, g), lambda i: (layer, 0, 0, 0)),
                  pl.BlockSpec((1, dp), lambda i: (0, 0))],
        out_specs=row, compiler_params=_params(grid))(u, u, wp, scale)


def _pool_bwd(u, dmix, wp, scale, layer):
    m, dp = u.shape
    ng = len(POOL_WINDOWS)
    g = dp // ng
    tm = _row_tile(m)
    nt = m // tm
    grid = (nt,)
    prev, nxt = _halo_specs(tm, dp, nt)

    def body(u_ref, h_ref, dm_ref, dmn_ref, w_ref, s_ref, du_ref, dw_ref, ds_ref):
        i = pl.program_id(0)

        @pl.when(i == 0)
        def _():
            dw_ref[...] = jnp.zeros_like(dw_ref)
            ds_ref[...] = jnp.zeros_like(ds_ref)

        halo = jnp.where(i > 0, h_ref[...], 0.0)
        s = jnp.concatenate([halo, u_ref[...]], axis=0)
        dmn = jnp.where(i < nt - 1, dmn_ref[...], 0.0)
        dmc = jnp.concatenate([dm_ref[...], dmn], axis=0)
        tpos = i * tm + lax.broadcasted_iota(jnp.int32, (tm, 1), 0)
        tpos_x = i * tm + lax.broadcasted_iota(jnp.int32, (tm + HALO, 1), 0)
        for gi, win in enumerate(POOL_WINDOWS):
            s = s + _shift_down(s, win // 2)
            cols = slice(gi * g, (gi + 1) * g)
            cnt = jnp.minimum(tpos + 1, win).astype(F32)
            pooled = (s[HALO:, cols] / cnt - u_ref[:, cols]).astype(BF16)
            wg = w_ref[gi]
            pm = _dot(pooled, wg)
            ds_ref[:, cols] += jnp.sum(dm_ref[:, cols] * pm, axis=0, keepdims=True)
            dpm = (dmc[:, cols] * s_ref[:, cols]).astype(BF16)
            dw_ref[gi] += _dot(pooled, dpm[:tm], TN)
            dpool = _dot(dpm, wg, NT)
            f = dpool / jnp.minimum(tpos_x + 1, win).astype(F32)
            step = 1
            while step < win:
                f = f + _shift_up(f, step)
                step *= 2
            du_ref[:, cols] = f[:tm] - dpool[:tm]

    row = pl.BlockSpec((tm, dp), lambda i: (i, 0))
    return pl.pallas_call(
        body, name="pool_bwd", grid=grid,
        out_shape=(jax.ShapeDtypeStruct((m, dp), F32), jax.ShapeDtypeStruct((ng, g, g), F32),
                   jax.ShapeDtypeStruct((1, dp), F32)),
        in_specs=[row, prev, row, nxt, pl.BlockSpec((None, ng, g, g), lambda i: (layer, 0, 0, 0)),
                  pl.BlockSpec((1, dp), lambda i: (0, 0))],
        out_specs=(row, pl.BlockSpec((ng, g, g), lambda i: (0, 0, 0)), pl.BlockSpec((1, dp), lambda i: (0, 0))),
        compiler_params=_params(grid))(u, u, dmix, dmix, wp, scale)


def _conv_taps(u, w_ref, rows):
    y = None
    for k in range(CONV_WIDTH):
        term = _shift_down(u, CONV_WIDTH - 1 - k)[HALO:HALO + rows] * w_ref[k:k + 1, :]
        y = term if y is None else y + term
    return y


def _layernorm_stats(y):
    mu = jnp.mean(y, axis=-1, keepdims=True)
    yc = y - mu
    rstd = lax.rsqrt(jnp.mean(yc * yc, axis=-1, keepdims=True) + EPS)
    return yc * rstd, rstd


def _conv_fwd(uc, wdw, b, lg, lb):
    m, c2 = uc.shape
    c = c2 // 2
    tm = _row_tile(m)
    nt = m // tm
    grid = (nt,)
    prev, _ = _halo_specs(tm, c2, nt)

    def body(x_ref, h_ref, w_ref, b_ref, g_ref, bb_ref, o_ref):
        i = pl.program_id(0)
        halo = jnp.where(i > 0, h_ref[...], 0.0)
        xc = jnp.concatenate([halo, x_ref[...]], axis=0)
        u = xc[:, :c] * jax.nn.sigmoid(xc[:, c:])
        y = _conv_taps(u, w_ref, tm) + b_ref[...]
        xhat, _ = _layernorm_stats(y)
        ln = xhat * g_ref[...] + bb_ref[...]
        o_ref[...] = (ln * jax.nn.sigmoid(ln)).astype(BF16)

    vec = pl.BlockSpec((1, c), lambda i: (0, 0))
    return pl.pallas_call(
        body, name="conv_fwd", grid=grid, out_shape=jax.ShapeDtypeStruct((m, c), BF16),
        in_specs=[pl.BlockSpec((tm, c2), lambda i: (i, 0)), prev, pl.BlockSpec((HALO, c), lambda i: (0, 0)), vec, vec, vec],
        out_specs=pl.BlockSpec((tm, c), lambda i: (i, 0)), compiler_params=_params(grid))(uc, uc, wdw, b, lg, lb)


def _conv_bwd(uc, ds, wdw, b, lg, lb):
    m, c2 = uc.shape
    c = c2 // 2
    tm = _row_tile(m)
    nt = m // tm
    grid = (nt,)
    prev, nxt = _halo_specs(tm, c2, nt)
    _, nxt_c = _halo_specs(tm, c, nt)

    def body(x_ref, hp_ref, hn_ref, ds_ref, dsn_ref, w_ref, b_ref, g_ref, bb_ref,
             dx_ref, dw_ref, db_ref, dg_ref, dbb_ref):
        i = pl.program_id(0)

        @pl.when(i == 0)
        def _():
            for ref in (dw_ref, db_ref, dg_ref, dbb_ref):
                ref[...] = jnp.zeros_like(ref)

        hp = jnp.where(i > 0, hp_ref[...], 0.0)
        xc = jnp.concatenate([hp, x_ref[...], hn_ref[...]], axis=0)
        sg = jax.nn.sigmoid(xc[:, c:])
        u = xc[:, :c] * sg
        rows = tm + HALO
        y = _conv_taps(u, w_ref, rows) + b_ref[...]
        xhat, rstd = _layernorm_stats(y)
        ln = xhat * g_ref[...] + bb_ref[...]
        sl = jax.nn.sigmoid(ln)
        dsn = jnp.where(i < nt - 1, dsn_ref[...], 0.0)
        dsx = jnp.concatenate([ds_ref[...], dsn], axis=0)
        dln = dsx * sl * (1.0 + ln * (1.0 - sl))
        dxh = dln * g_ref[...]
        dy = rstd * (dxh - jnp.mean(dxh, axis=-1, keepdims=True)
                     - xhat * jnp.mean(dxh * xhat, axis=-1, keepdims=True))
        dyt = dy[:tm]
        dg_ref[...] += jnp.sum(dln[:tm] * xhat[:tm], axis=0, keepdims=True)
        dbb_ref[...] += jnp.sum(dln[:tm], axis=0, keepdims=True)
        db_ref[...] += jnp.sum(dyt, axis=0, keepdims=True)
        du = None
        for k in range(CONV_WIDTH):
            lag = CONV_WIDTH - 1 - k
            dw_ref[k:k + 1, :] += jnp.sum(dyt * _shift_down(u, lag)[HALO:HALO + tm], axis=0, keepdims=True)
            term = _shift_up(dy, lag)[:tm] * w_ref[k:k + 1, :]
            du = term if du is None else du + term
        a_t = xc[HALO:HALO + tm, :c]
        sg_t = sg[HALO:HALO + tm]
        dx_ref[:, :c] = du * sg_t
        dx_ref[:, c:] = du * a_t * sg_t * (1.0 - sg_t)

    vec = pl.BlockSpec((1, c), lambda i: (0, 0))
    wsp = pl.BlockSpec((HALO, c), lambda i: (0, 0))
    vshape = jax.ShapeDtypeStruct((1, c), F32)
    return pl.pallas_call(
        body, name="conv_bwd", grid=grid,
        out_shape=(jax.ShapeDtypeStruct((m, c2), F32), jax.ShapeDtypeStruct((HALO, c), F32), vshape, vshape, vshape),
        in_specs=[pl.BlockSpec((tm, c2), lambda i: (i, 0)), prev, nxt, pl.BlockSpec((tm, c), lambda i: (i, 0)), nxt_c,
                  wsp, vec, vec, vec],
        out_specs=(pl.BlockSpec((tm, c2), lambda i: (i, 0)), wsp, vec, vec, vec),
        compiler_params=_params(grid))(uc, uc, uc, ds, ds, wdw, b, lg, lb)


def _mesh_pos():
    return lax.axis_index("x"), lax.axis_index("y"), lax.axis_index("c")


def _other_chips(x, y):
    return [(1 - x, y), (x, 1 - y), (1 - x, 1 - y)]


def _comm_call(body, name, arrs, out_shapes, n_sems):
    n = len(arrs)
    return pl.pallas_call(
        body, name=name, out_shape=out_shapes, in_specs=[ANY] * n, out_specs=tuple([ANY] * len(out_shapes)),
        scratch_shapes=[pltpu.SemaphoreType.DMA((n, n_sems)), pltpu.SemaphoreType.DMA((n, n_sems)),
                        pltpu.SemaphoreType.DMA((n, 4))],
    )(*arrs)


def _all_gather(arrs, name):
    n = len(arrs)

    def body(*refs):
        ins, outs = refs[:n], refs[n:2 * n]
        send_sems, recv_sems, local_sems = refs[2 * n:]
        x, y, c = _mesh_pos()
        me, sib = (x, y, c), (x, y, 1 - c)
        chips = _other_chips(x, y)

        def slot(a, p):
            return outs[a].at[4 * p[0] + 2 * p[1] + p[2]]

        def copy(a, k, block, to, src=None):
            return pltpu.make_async_remote_copy(
                src_ref=slot(a, block) if src is None else src, dst_ref=slot(a, block),
                send_sem=send_sems.at[a, k], recv_sem=recv_sems.at[a, k], device_id=to, device_id_type=MESH)

        sends, locals_ = [], []
        for a in range(n):
            for j, chip in enumerate(chips):
                sends.append(copy(a, 1 + j, me, (*chip, c), src=ins[a]))
            sends.append(copy(a, 0, me, sib, src=ins[a]))
            locals_.append(pltpu.make_async_copy(ins[a], slot(a, me), local_sems.at[a, 0]))
        for cp in sends + locals_:
            cp.start()
        for a in range(n):
            for j, chip in enumerate(chips):
                copy(a, 1 + j, (*chip, c), me).wait_recv()
                fwd = copy(a, 4 + j, (*chip, c), sib)
                fwd.start()
                sends.append(fwd)
        for a in range(n):
            copy(a, 0, sib, me).wait_recv()
            for j, chip in enumerate(chips):
                copy(a, 4 + j, (*chip, 1 - c), me).wait_recv()
        for cp in sends:
            cp.wait_send()
        for cp in locals_:
            cp.wait()

    outs = tuple(jax.ShapeDtypeStruct((N_DEV,) + a.shape, a.dtype) for a in arrs)
    return _comm_call(body, name, arrs, outs, 7)


def _pair_tile(kk):
    return _pick(kk, (1024, 704, 512, 256, 128, 64))


def _pair_exchange(parts, core, name):
    _, nl, kk, n = parts.shape
    tr = _pair_tile(kk)
    nr = kk // tr
    grid = (4, nl, nr)

    def body(c_ref, src_ref, recv_ref, send_sems, recv_sems):
        q, l, r = pl.program_id(0), pl.program_id(1), pl.program_id(2)
        x, y, c = _mesh_pos()
        step = (q * nl + l) * nr + r
        slot = step % 2
        dst = recv_ref.at[pl.ds(pl.multiple_of(step * tr, tr), tr), :]
        cp = pltpu.make_async_remote_copy(src_ref=src_ref, dst_ref=dst, send_sem=send_sems.at[slot],
                                          recv_sem=recv_sems.at[slot], device_id=(x, y, 1 - c), device_id_type=MESH)
        cp.start()
        cp.wait_send()
        cp.wait_recv()

    spec = pltpu.PrefetchScalarGridSpec(
        num_scalar_prefetch=1, grid=grid,
        in_specs=[pl.BlockSpec((tr, n), lambda q, l, r, c: (((2 * q + 1 - c[0]) * nl + l) * nr + r, 0))],
        out_specs=ANY, scratch_shapes=[pltpu.SemaphoreType.DMA((2,)), pltpu.SemaphoreType.DMA((2,))])
    recv = pl.pallas_call(body, name=name, grid_spec=spec, out_shape=jax.ShapeDtypeStruct((4 * nl * kk, n), parts.dtype),
                          compiler_params=_params(grid))(core, parts.reshape(N_DEV * nl * kk, n))
    return recv.reshape(4, nl, kk, n)


def _reduce_chip_exchange(parts, name):
    n = len(parts)

    def body(*refs):
        ins, outs = refs[:n], refs[n:2 * n]
        send_sems, recv_sems, local_sems = refs[2 * n:]
        x, y, c = _mesh_pos()
        mine = 2 * x + y
        chips = _other_chips(x, y)
        sends, locals_ = [], []
        for a in range(n):
            for j, chip in enumerate(chips):
                sends.append(pltpu.make_async_remote_copy(
                    src_ref=ins[a].at[2 * chip[0] + chip[1]], dst_ref=outs[a].at[mine], send_sem=send_sems.at[a, j],
                    recv_sem=recv_sems.at[a, j], device_id=(*chip, c), device_id_type=MESH))
            locals_.append(pltpu.make_async_copy(ins[a].at[mine], outs[a].at[mine], local_sems.at[a, 0]))
        for cp in sends + locals_:
            cp.start()
        for a in range(n):
            for j, chip in enumerate(chips):
                landing = outs[a].at[2 * chip[0] + chip[1]]
                pltpu.make_async_remote_copy(
                    src_ref=landing, dst_ref=landing, send_sem=send_sems.at[a, j], recv_sem=recv_sems.at[a, j],
                    device_id=(x, y, c), device_id_type=MESH).wait_recv()
        for cp in sends:
            cp.wait_send()
        for cp in locals_:
            cp.wait()

    outs = tuple(jax.ShapeDtypeStruct(p.shape, p.dtype) for p in parts)
    return _comm_call(body, name, parts, outs, 3)


def _pair_add(parts, recv, core, name):
    _, nl, kk, n = recv.shape
    tr = _pair_tile(kk)
    grid = (4, nl, kk // tr)

    def body(c_ref, a_ref, b_ref, o_ref):
        o_ref[...] = (a_ref[...].astype(F32) + b_ref[...].astype(F32)).astype(o_ref.dtype)

    out = pl.BlockSpec((None, None, tr, n), lambda q, l, r, c: (q, l, r, 0))
    spec = pltpu.PrefetchScalarGridSpec(
        num_scalar_prefetch=1, grid=grid,
        in_specs=[pl.BlockSpec((None, None, None, tr, n), lambda q, l, r, c: (q, c[0], l, r, 0)), out], out_specs=out)
    return pl.pallas_call(body, name=name, grid_spec=spec, out_shape=jax.ShapeDtypeStruct(recv.shape, recv.dtype),
                          compiler_params=_params(grid))(core, parts.reshape(4, 2, nl, kk, n), recv)


def _adamw_math(w, g, m, v):
    m2 = ADAM_B1 * m + (1.0 - ADAM_B1) * g
    v2 = ADAM_B2 * v + (1.0 - ADAM_B2) * (g * g)
    m_hat = m2 / (1.0 - ADAM_B1 ** ADAM_STEP)
    v_hat = v2 / (1.0 - ADAM_B2 ** ADAM_STEP)
    delta = -ADAM_LR * (m_hat / (jnp.sqrt(v_hat) + ADAM_EPS) + ADAM_WD * w)
    return delta, m2, v2


def _adamw_sharded(parts, w, m, v, name):
    nl, kk, n = w.shape
    tr = _pick(kk, (256, 352, 128, 64))
    grid = (nl, kk // tr)

    def body(p_ref, w_ref, m_ref, v_ref, g_ref, d_ref, m2_ref, v2_ref):
        g = p_ref[0].astype(F32)
        for q in range(1, 4):
            g = g + p_ref[q].astype(F32)
        d, m2, v2 = _adamw_math(w_ref[...], g, m_ref[...], v_ref[...])
        g_ref[...] = g
        d_ref[...] = d
        m2_ref[...] = m2
        v2_ref[...] = v2

    spec = pl.BlockSpec((None, tr, n), lambda l, r: (l, r, 0))
    shp = jax.ShapeDtypeStruct(w.shape, F32)
    return pl.pallas_call(
        body, name=name, grid=grid, out_shape=(shp, shp, shp, shp),
        in_specs=[pl.BlockSpec((4, None, tr, n), lambda l, r: (0, l, r, 0)), spec, spec, spec],
        out_specs=(spec, spec, spec, spec), compiler_params=_params(grid))(parts, w, m, v)


def _sum_slots(g8):
    _, r, _ = g8.shape

    def body(g_ref, o_ref):
        s = g_ref[0]
        for d in range(1, N_DEV):
            s = s + g_ref[d]
        o_ref[...] = s

    return pl.pallas_call(body, name="sum_slots", out_shape=jax.ShapeDtypeStruct((r, 128), F32))(g8)


def _adamw_flat(w, g, m, v):
    def body(w_ref, g_ref, m_ref, v_ref, d_ref, m2_ref, v2_ref):
        d, m2, v2 = _adamw_math(w_ref[...], g_ref[...], m_ref[...], v_ref[...])
        d_ref[...] = d
        m2_ref[...] = m2
        v2_ref[...] = v2

    shp = jax.ShapeDtypeStruct(w.shape, F32)
    return pl.pallas_call(body, name="adamw_flat", out_shape=(shp, shp, shp))(w, g, m, v)


def _pack(arrs):
    flat = jnp.concatenate([a.reshape(-1) for a in arrs])
    pad = (-flat.shape[0]) % 1024
    return jnp.pad(flat, (0, pad)).reshape(-1, 128)


def _unpack(buf, like):
    flat = buf.reshape(-1)
    out, off = [], 0
    for a in like:
        out.append(flat[off:off + a.size].reshape(a.shape))
        off += a.size
    return out


def _heads(x2d, scale=None):
    lp, da = x2d.shape
    xh = x2d.reshape(lp, N_HEADS, da // N_HEADS).transpose(1, 0, 2)
    if scale is not None:
        xh = xh * scale
    return xh


def _unheads(xh):
    nh, lp, dh = xh.shape
    return xh.transpose(1, 0, 2).reshape(lp, nh * dh)


def _layer_fwd(h, l, W, P):
    d = h.shape[1]
    da, dp = d // 2, d // 4
    dh = da // N_HEADS
    sv = {"h": h}
    u1 = _rms_fwd(h, P["pre_mix_g"][l:l + 1], "rms_pre_mix")
    proj_s = _mm_cols(u1, W["in"], l, "mm_in")
    lp = h.shape[0]
    proj = proj_s.transpose(1, 0, 2).reshape(lp, -1)
    qh = _heads(proj[:, :da], dh ** -0.5).astype(BF16)
    kh = _heads(proj[:, da:2 * da]).astype(BF16)
    vh = _heads(proj[:, 2 * da:3 * da]).astype(BF16)
    u_pool = proj[:, 3 * da:3 * da + dp]
    u_conv = proj[:, 3 * da + dp:]
    oh, tot = _attn_fwd(qh, kh, vh)
    o_attn = _unheads(oh)
    o_pool = _pool_fwd(u_pool, W["pool"], P["pool_scale"][l:l + 1], l)
    s_conv = _conv_fwd(u_conv, W["dw"][l], P["b_dw"][l:l + 1], P["conv_ln_g"][l:l + 1], P["conv_ln_b"][l:l + 1])
    o_conv = _mm_nn(s_conv, W["pw"], "mm_pw", layer=l)
    merged = _merge_fwd(o_attn, o_pool, o_conv, P["mix_out_g"][l:l + 1])
    z1 = _mm_nn(merged, W["out"], "mm_out", layer=l)
    h1 = _resid_rms_fwd(h, z1, P["post_mix_g"][l:l + 1], "resid_post_mix")
    u2 = _rms_fwd(h1, P["pre_ffn_g"][l:l + 1], "rms_pre_ffn")
    a_s, b_s, act_s = _ffn_up(u2, W["gate"], W["up"], l)
    ff = _mm_rows(act_s, W["down"], l, "mm_down")
    h2 = _resid_rms_fwd(h1, ff, P["post_ffn_g"][l:l + 1], "resid_post_ffn")
    sv.update(u1=u1, qh=qh, kh=kh, vh=vh, tot=tot, u_pool=u_pool, u_conv=u_conv, o_attn=o_attn, o_pool=o_pool,
              o_conv=o_conv, s_conv=s_conv, merged=merged, z1=z1, h1=h1, u2=u2, a_s=a_s, b_s=b_s, act_s=act_s, ff=ff)
    return h2, sv


def _layer_bwd(dh_out, l, W, P, sv):
    d = dh_out.shape[1]
    da = d // 2
    dhd = da // N_HEADS
    lp = dh_out.shape[0]
    g = {}
    dff, g["post_ffn_g"] = _rms_bwd(sv["ff"], P["post_ffn_g"][l:l + 1], dh_out, None, BF16, "rms_bwd_post_ffn")
    da_s, db_s = _ffn_bwd_act(dff, W["down"], l, sv["a_s"], sv["b_s"])
    g["w_down"] = _mm_tn_rows(sv["act_s"], dff, "mm_dw_down")
    du2 = _mm_cols_t([da_s, db_s], [W["gate"], W["up"]], l, "mm_d_u2")
    g["w_gate"] = _mm_tn_cols(sv["u2"], da_s, "mm_dw_gate")
    g["w_up"] = _mm_tn_cols(sv["u2"], db_s, "mm_dw_up")
    dh1, g["pre_ffn_g"] = _rms_bwd(sv["h1"], P["pre_ffn_g"][l:l + 1], du2, dh_out, F32, "rms_bwd_pre_ffn")
    dz1, g["post_mix_g"] = _rms_bwd(sv["z1"], P["post_mix_g"][l:l + 1], dh1, None, BF16, "rms_bwd_post_mix")
    dmerged = _mm_nt(dz1, W["out"], "mm_d_merged", l)
    g["w_out"] = _mm_tn(sv["merged"], dz1, "mm_dw_out")
    d_oa, d_op, d_oc, g["mix_out_g"] = _merge_bwd(sv["o_attn"], sv["o_pool"], sv["o_conv"], P["mix_out_g"][l:l + 1], dmerged)
    d_s = _mm_nt(d_oc, W["pw"], "mm_d_sconv", l)
    g["w_pw"] = _mm_tn(sv["s_conv"], d_oc, "mm_dw_pw")
    d_uc, dwdw, g["b_dw"], g["conv_ln_g"], g["conv_ln_b"] = _conv_bwd(
        sv["u_conv"], d_s, W["dw"][l], P["b_dw"][l:l + 1], P["conv_ln_g"][l:l + 1], P["conv_ln_b"][l:l + 1])
    g["w_dw"] = dwdw[:CONV_WIDTH]
    d_up, g["w_pool"], g["pool_scale"] = _pool_bwd(sv["u_pool"], d_op, W["pool"], P["pool_scale"][l:l + 1], l)
    doh = _heads(d_oa).astype(BF16)
    dqh, dkt, dvt = _attn_bwd(sv["qh"], sv["qh"].transpose(0, 2, 1), sv["kh"], sv["vh"], doh, doh.transpose(0, 2, 1),
                              sv["tot"], dhd ** -0.5)
    dk2, dv2 = (a.transpose(1, 3, 0, 2).reshape(lp, da) for a in (dkt, dvt))
    dproj = jnp.concatenate([_unheads(dqh), dk2, dv2, d_up, d_uc], axis=1).astype(BF16)
    dproj_s = dproj.reshape(lp, N_DEV, -1).transpose(1, 0, 2)
    du1 = _mm_cols_t([dproj_s], [W["in"]], l, "mm_d_u1")
    g["w_in"] = _mm_tn_cols(sv["u1"], dproj_s, "mm_dw_in")
    dh0, g["pre_mix_g"] = _rms_bwd(sv["h"], P["pre_mix_g"][l:l + 1], du1, dh1, F32, "rms_bwd_pre_mix")
    return dh0, g


SHARDED = ("w_in", "w_gate", "w_up", "w_down", "w_out", "w_pw")
SMALL = ("pre_mix_g", "w_pool", "pool_scale", "b_dw", "conv_ln_g", "conv_ln_b", "mix_out_g", "post_mix_g",
         "pre_ffn_g", "post_ffn_g", "w_dw", "meta_tokens")
ORDER = ("meta_tokens", "pre_mix_g", "w_in", "w_pool", "pool_scale", "w_dw", "b_dw", "conv_ln_g", "conv_ln_b", "w_pw",
         "mix_out_g", "w_out", "post_mix_g", "pre_ffn_g", "w_gate", "w_up", "w_down", "post_ffn_g")


def kernel(x, meta_tokens, pre_mix_g, w_in, w_pool, pool_scale, w_dw, b_dw, conv_ln_g, conv_ln_b, w_pw, mix_out_g, w_out, post_mix_g, pre_ffn_g, w_gate, w_up, w_down, post_ffn_g, loss_target, m_meta_tokens, m_pre_mix_g, m_w_in, m_w_pool, m_pool_scale, m_w_dw, m_b_dw, m_conv_ln_g, m_conv_ln_b, m_w_pw, m_mix_out_g, m_w_out, m_post_mix_g, m_pre_ffn_g, m_w_gate, m_w_up, m_w_down, m_post_ffn_g, v_meta_tokens, v_pre_mix_g, v_w_in, v_w_pool, v_pool_scale, v_w_dw, v_b_dw, v_conv_ln_g, v_conv_ln_b, v_w_pw, v_mix_out_g, v_w_out, v_post_mix_g, v_pre_ffn_g, v_w_gate, v_w_up, v_w_down, v_post_ffn_g):
    P = dict(meta_tokens=meta_tokens, pre_mix_g=pre_mix_g, w_in=w_in, w_pool=w_pool, pool_scale=pool_scale, w_dw=w_dw,
             b_dw=b_dw, conv_ln_g=conv_ln_g, conv_ln_b=conv_ln_b, w_pw=w_pw, mix_out_g=mix_out_g, w_out=w_out,
             post_mix_g=post_mix_g, pre_ffn_g=pre_ffn_g, w_gate=w_gate, w_up=w_up, w_down=w_down, post_ffn_g=post_ffn_g)
    M = dict(meta_tokens=m_meta_tokens, pre_mix_g=m_pre_mix_g, w_in=m_w_in, w_pool=m_w_pool, pool_scale=m_pool_scale,
             w_dw=m_w_dw, b_dw=m_b_dw, conv_ln_g=m_conv_ln_g, conv_ln_b=m_conv_ln_b, w_pw=m_w_pw, mix_out_g=m_mix_out_g,
             w_out=m_w_out, post_mix_g=m_post_mix_g, pre_ffn_g=m_pre_ffn_g, w_gate=m_w_gate, w_up=m_w_up,
             w_down=m_w_down, post_ffn_g=m_post_ffn_g)
    V = dict(meta_tokens=v_meta_tokens, pre_mix_g=v_pre_mix_g, w_in=v_w_in, w_pool=v_w_pool, pool_scale=v_pool_scale,
             w_dw=v_w_dw, b_dw=v_b_dw, conv_ln_g=v_conv_ln_g, conv_ln_b=v_conv_ln_b, w_pw=v_w_pw, mix_out_g=v_mix_out_g,
             w_out=v_w_out, post_mix_g=v_post_mix_g, pre_ffn_g=v_pre_ffn_g, w_gate=v_w_gate, w_up=v_w_up,
             w_down=v_w_down, post_ffn_g=v_post_ffn_g)
    xi, yi, ci = _mesh_pos()
    dev = 4 * xi + 2 * yi + ci
    n_tok, d = x.shape[1], x.shape[2]
    n_meta = meta_tokens.shape[0]
    n_layers = w_in.shape[0]
    c = d // 4
    l_real = n_meta + n_tok
    lp = -(-l_real // ATT_BLOCK) * ATT_BLOCK

    gathered = _all_gather([P[k].astype(BF16) for k in SHARDED], "gather_weights")
    g_dw, g_meta = _all_gather([w_dw, meta_tokens], "gather_small")
    W = dict(zip(("in", "gate", "up", "down", "out", "pw"), gathered))
    W["out"] = W["out"].transpose(1, 0, 2, 3).reshape(n_layers, d, d)
    W["pw"] = W["pw"].transpose(1, 0, 2, 3).reshape(n_layers, c, c)
    W["pool"] = w_pool.astype(BF16)
    wdw_full = g_dw.transpose(1, 2, 0, 3).reshape(n_layers, CONV_WIDTH, c)
    W["dw"] = jnp.pad(wdw_full, ((0, 0), (0, HALO - CONV_WIDTH), (0, 0)))
    meta_full = g_meta.transpose(1, 0, 2).reshape(n_meta, d)

    h = jnp.concatenate([meta_full, x[0], jnp.zeros((lp - l_real, d), F32)], axis=0)
    saved = []
    for l in range(n_layers):
        h, sv = _layer_fwd(h, l, W, P)
        saved.append(sv)
    target = jnp.pad(loss_target[0], ((n_meta, lp - l_real), (0, 0)))
    loss_part, dh = _loss_grad(h, target, n_meta, n_tok)
    loss = lax.psum(loss_part[0, 0], ("x", "y", "c"))

    grads = [None] * n_layers
    for l in reversed(range(n_layers)):
        dh, grads[l] = _layer_bwd(dh, l, W, P, saved[l])
    grad_x = dh[n_meta:l_real][None]

    parts = []
    for k in SHARDED:
        per_layer = [grads[l][k] for l in range(n_layers)]
        per_layer = [p if p.ndim == 3 else p.reshape(N_DEV, p.shape[0] // N_DEV, p.shape[1]) for p in per_layer]
        parts.append(jnp.stack(per_layer, axis=1))
    core = jnp.reshape(ci, (1,)).astype(jnp.int32)
    pair = [_pair_add(p, _pair_exchange(p, core, "pair_exchange_" + k), core, "pair_add_" + k)
            for k, p in zip(SHARDED, parts)]
    by_chip = _reduce_chip_exchange(pair, "reduce_chips")
    out = {}
    for k, p in zip(SHARDED, by_chip):
        out[k] = _adamw_sharded(p, P[k], M[k], V[k], "adamw_" + k)

    small_parts = []
    for k in SMALL:
        if k == "meta_tokens":
            small_parts.append(dh[:n_meta])
        else:
            small_parts.append(jnp.stack([grads[l][k].reshape(P[k].shape[1:] if k != "w_dw" else (CONV_WIDTH, c))
                                          for l in range(n_layers)], axis=0))
    (g8,) = _all_gather([_pack(small_parts)], "gather_small_grads")
    g_small = dict(zip(SMALL, _unpack(_sum_slots(g8), small_parts)))
    g_small["w_dw"] = lax.dynamic_slice_in_dim(g_small["w_dw"], dev * w_dw.shape[2], w_dw.shape[2], axis=2)
    g_small["meta_tokens"] = lax.dynamic_slice_in_dim(g_small["meta_tokens"], dev * meta_tokens.shape[1],
                                                      meta_tokens.shape[1], axis=1)
    like = [P[k] for k in SMALL]
    res = _adamw_flat(_pack(like), _pack([g_small[k] for k in SMALL]), _pack([M[k] for k in SMALL]),
                      _pack([V[k] for k in SMALL]))
    res = [_unpack(r, like) for r in res]
    for i, k in enumerate(SMALL):
        out[k] = (g_small[k], res[0][i], res[1][i], res[2][i])

    return (loss, grad_x, *[out[k][0] for k in ORDER], *[out[k][1] for k in ORDER],
            *[out[k][2] for k in ORDER], *[out[k][3] for k in ORDER])
```

```python
import functools

import jax
import jax.numpy as jnp
from jax import lax
from jax.experimental import pallas as pl
from jax.experimental.pallas import tpu as pltpu

F32 = jnp.float32
BF16 = jnp.bfloat16
EPS = 1e-6
N_HEADS = 16
POOL_WINDOWS = (2, 4, 8, 16)
CONV_WIDTH = 31
HALO = 32
ATT_BLOCK = 128
N_DEV = 8
VMEM_LIMIT = 60 * 1024 * 1024

ADAM_LR = 0.001
ADAM_B1 = 0.9
ADAM_B2 = 0.999
ADAM_EPS = 1e-08
ADAM_WD = 0.01
ADAM_STEP = 10

NN = (((1,), (0,)), ((), ()))
NT = (((1,), (1,)), ((), ()))
TN = (((0,), (0,)), ((), ()))
MESH = pl.DeviceIdType.MESH
ANY = pl.BlockSpec(memory_space=pl.ANY)


def _pick(n, cands):
    for c in cands:
        if c <= n and n % c == 0:
            return c
    return n


def _params(grid):
    return pltpu.CompilerParams(dimension_semantics=("arbitrary",) * len(grid), vmem_limit_bytes=VMEM_LIMIT)


def _dot(a, b, dims=NN):
    return lax.dot_general(a, b, dims, preferred_element_type=F32)


def _matmul(name, pairs, specs, out_shape, out_spec, grid, dims, acc_shape):
    n = len(pairs)
    nk = grid[-1]
    kaxis = len(grid) - 1

    def body(*refs):
        o_ref, acc = refs[2 * n], refs[2 * n + 1]
        tot = None
        for p in range(n):
            d = _dot(refs[2 * p][...], refs[2 * p + 1][...], dims)
            tot = d if tot is None else tot + d
        if nk == 1:
            o_ref[...] = tot.astype(o_ref.dtype)
            return
        k = pl.program_id(kaxis)

        @pl.when(k == 0)
        def _():
            acc[...] = tot

        @pl.when(k > 0)
        def _():
            acc[...] += tot

        @pl.when(k == nk - 1)
        def _():
            o_ref[...] = acc[...].astype(o_ref.dtype)

    ops, in_specs = [], []
    for (a, b), (sa, sb) in zip(pairs, specs):
        ops += [a, b]
        in_specs += [sa, sb]
    return pl.pallas_call(
        body, name=name, out_shape=out_shape, grid=grid, in_specs=in_specs, out_specs=out_spec,
        scratch_shapes=[pltpu.VMEM(acc_shape if nk > 1 else (8, 128), F32)], compiler_params=_params(grid),
    )(*ops)


def _mm_nn(a, b, name, out_dtype=F32, layer=None):
    m, kk = a.shape
    n = b.shape[-1]
    tm = _pick(m, (1408, 1024, 512, 256, 128))
    tn = _pick(n, (1024, 512, 256, 128))
    grid = (m // tm, n // tn, 1)
    sa = pl.BlockSpec((tm, kk), lambda i, j, k: (i, 0))
    if layer is None:
        sb = pl.BlockSpec((kk, tn), lambda i, j, k: (0, j))
    else:
        sb = pl.BlockSpec((None, kk, tn), lambda i, j, k: (layer, 0, j))
    return _matmul(name, [(a, b)], [(sa, sb)], jax.ShapeDtypeStruct((m, n), out_dtype),
                   pl.BlockSpec((tm, tn), lambda i, j, k: (i, j)), grid, NN, (tm, tn))


def _mm_nt(a, b, name, layer, out_dtype=F32):
    m, kk = a.shape
    n = b.shape[1]
    tm = _pick(m, (1408, 1024, 512, 256, 128))
    tn = _pick(n, (1024, 512, 256, 128))
    grid = (m // tm, n // tn, 1)
    sa = pl.BlockSpec((tm, kk), lambda i, j, k: (i, 0))
    sb = pl.BlockSpec((None, tn, kk), lambda i, j, k: (layer, j, 0))
    return _matmul(name, [(a, b)], [(sa, sb)], jax.ShapeDtypeStruct((m, n), out_dtype),
                   pl.BlockSpec((tm, tn), lambda i, j, k: (i, j)), grid, NT, (tm, tn))


def _mm_tn(a, b, name, out_dtype=BF16):
    l, m = a.shape
    n = b.shape[1]
    tl = _pick(l, (1408, 1024, 512, 256, 128))
    tm = _pick(m, (1024, 512, 256, 128))
    tn = _pick(n, (1024, 512, 256, 128))
    grid = (m // tm, n // tn, l // tl)
    sa = pl.BlockSpec((tl, tm), lambda i, j, t: (t, i))
    sb = pl.BlockSpec((tl, tn), lambda i, j, t: (t, j))
    return _matmul(name, [(a, b)], [(sa, sb)], jax.ShapeDtypeStruct((m, n), out_dtype),
                   pl.BlockSpec((tm, tn), lambda i, j, t: (i, j)), grid, TN, (tm, tn))


def _mm_cols(a, w, layer, name):
    m, kk = a.shape
    nd, _, _, n = w.shape
    tm = _pick(m, (1408, 1024, 512, 256, 128))
    grid = (nd, m // tm, 1)
    sa = pl.BlockSpec((tm, kk), lambda d, i, k: (i, 0))
    sb = pl.BlockSpec((None, None, kk, n), lambda d, i, k: (d, layer, 0, 0))
    return _matmul(name, [(a, w)], [(sa, sb)], jax.ShapeDtypeStruct((nd, m, n), F32),
                   pl.BlockSpec((None, tm, n), lambda d, i, k: (d, i, 0)), grid, NN, (tm, n))


def _mm_rows(a_s, w, layer, name):
    nd, m, kk = a_s.shape
    n = w.shape[-1]
    tm = _pick(m, (1408, 1024, 512, 256, 128))
    tn = _pick(n, (1024, 512, 256, 128))
    grid = (m // tm, n // tn, nd)
    sa = pl.BlockSpec((None, tm, kk), lambda i, j, d: (d, i, 0))
    sb = pl.BlockSpec((None, None, kk, tn), lambda i, j, d: (d, layer, 0, j))
    return _matmul(name, [(a_s, w)], [(sa, sb)], jax.ShapeDtypeStruct((m, n), F32),
                   pl.BlockSpec((tm, tn), lambda i, j, d: (i, j)), grid, NN, (tm, tn))


def _mm_cols_t(das, ws, layer, name):
    nd, m, n = das[0].shape
    kk = ws[0].shape[2]
    tm = _pick(m, (1408, 1024, 512, 256, 128))
    tk = _pick(kk, (1024, 512, 256, 128))
    grid = (m // tm, kk // tk, nd)
    sa = pl.BlockSpec((None, tm, n), lambda i, j, d: (d, i, 0))
    sb = pl.BlockSpec((None, None, tk, n), lambda i, j, d: (d, layer, j, 0))
    return _matmul(name, list(zip(das, ws)), [(sa, sb)] * len(das), jax.ShapeDtypeStruct((m, kk), F32),
                   pl.BlockSpec((tm, tk), lambda i, j, d: (i, j)), grid, NT, (tm, tk))


def _mm_tn_cols(a, b_s, name):
    l, kk = a.shape
    nd, _, n = b_s.shape
    tl = _pick(l, (1408, 1024, 512, 256, 128))
    tk = _pick(kk, (1024, 512, 256, 128))
    grid = (nd, kk // tk, l // tl)
    sa = pl.BlockSpec((tl, tk), lambda d, j, t: (t, j))
    sb = pl.BlockSpec((None, tl, n), lambda d, j, t: (d, t, 0))
    return _matmul(name, [(a, b_s)], [(sa, sb)], jax.ShapeDtypeStruct((nd, kk, n), BF16),
                   pl.BlockSpec((None, tk, n), lambda d, j, t: (d, j, 0)), grid, TN, (tk, n))


def _mm_tn_rows(a_s, b, name):
    nd, l, kk = a_s.shape
    n = b.shape[1]
    tl = _pick(l, (1408, 1024, 512, 256, 128))
    tn = _pick(n, (1024, 512, 256, 128))
    grid = (nd, n // tn, l // tl)
    sa = pl.BlockSpec((None, tl, kk), lambda d, j, t: (d, t, 0))
    sb = pl.BlockSpec((tl, tn), lambda d, j, t: (t, j))
    return _matmul(name, [(a_s, b)], [(sa, sb)], jax.ShapeDtypeStruct((nd, kk, n), BF16),
                   pl.BlockSpec((None, kk, tn), lambda d, j, t: (d, 0, j)), grid, TN, (kk, tn))


def _ffn_up(u, wg, wu, layer):
    m, kk = u.shape
    nd, _, _, n = wg.shape
    tm = _pick(m, (704, 512, 256, 128))
    grid = (nd, m // tm)

    def body(u_ref, wg_ref, wu_ref, a_ref, b_ref, act_ref):
        uu = u_ref[...]
        a = _dot(uu, wg_ref[...])
        b = _dot(uu, wu_ref[...])
        a_ref[...] = a
        b_ref[...] = b
        act_ref[...] = (a * jax.nn.sigmoid(a) * b).astype(BF16)

    wspec = pl.BlockSpec((None, None, kk, n), lambda d, i: (d, layer, 0, 0))
    ospec = pl.BlockSpec((None, tm, n), lambda d, i: (d, i, 0))
    return pl.pallas_call(
        body, name="ffn_up", grid=grid,
        out_shape=(jax.ShapeDtypeStruct((nd, m, n), F32), jax.ShapeDtypeStruct((nd, m, n), F32),
                   jax.ShapeDtypeStruct((nd, m, n), BF16)),
        in_specs=[pl.BlockSpec((tm, kk), lambda d, i: (i, 0)), wspec, wspec],
        out_specs=(ospec, ospec, ospec), compiler_params=_params(grid),
    )(u, wg, wu)


def _ffn_bwd_act(dff, wd, layer, a_s, b_s):
    m, kk = dff.shape
    nd, _, n, _ = wd.shape
    tm = _pick(m, (704, 512, 256, 128))
    grid = (nd, m // tm)

    def body(d_ref, w_ref, a_ref, b_ref, da_ref, db_ref):
        dact = _dot(d_ref[...], w_ref[...], NT)
        a, b = a_ref[...], b_ref[...]
        sg = jax.nn.sigmoid(a)
        db_ref[...] = (dact * a * sg).astype(BF16)
        da_ref[...] = (dact * b * sg * (1.0 + a * (1.0 - sg))).astype(BF16)

    tspec = pl.BlockSpec((None, tm, n), lambda d, i: (d, i, 0))
    return pl.pallas_call(
        body, name="ffn_bwd_act", grid=grid,
        out_shape=(jax.ShapeDtypeStruct((nd, m, n), BF16), jax.ShapeDtypeStruct((nd, m, n), BF16)),
        in_specs=[pl.BlockSpec((tm, kk), lambda d, i: (i, 0)),
                  pl.BlockSpec((None, None, n, kk), lambda d, i: (d, layer, 0, 0)), tspec, tspec],
        out_specs=(tspec, tspec), compiler_params=_params(grid),
    )(dff, wd, a_s, b_s)


def _rstd(x):
    return lax.rsqrt(jnp.mean(x * x, axis=-1, keepdims=True) + EPS)


def _row_tile(m):
    return _pick(m, (384, 256, 128))


def _rms_fwd(x, g, name):
    m, d = x.shape
    tr = _row_tile(m)
    grid = (m // tr,)

    def body(x_ref, g_ref, o_ref):
        xx = x_ref[...]
        o_ref[...] = (xx * _rstd(xx) * g_ref[...]).astype(BF16)

    row = pl.BlockSpec((tr, d), lambda i: (i, 0))
    return pl.pallas_call(body, name=name, grid=grid, out_shape=jax.ShapeDtypeStruct((m, d), BF16),
                          in_specs=[row, pl.BlockSpec((1, d), lambda i: (0, 0))], out_specs=row,
                          compiler_params=_params(grid))(x, g)


def _resid_rms_fwd(h, z, g, name):
    m, d = h.shape
    tr = _row_tile(m)
    grid = (m // tr,)

    def body(h_ref, z_ref, g_ref, o_ref):
        zz = z_ref[...]
        o_ref[...] = h_ref[...] + zz * _rstd(zz) * g_ref[...]

    row = pl.BlockSpec((tr, d), lambda i: (i, 0))
    return pl.pallas_call(body, name=name, grid=grid, out_shape=jax.ShapeDtypeStruct((m, d), F32),
                          in_specs=[row, row, pl.BlockSpec((1, d), lambda i: (0, 0))], out_specs=row,
                          compiler_params=_params(grid))(h, z, g)


def _rms_bwd_math(x, g, dy):
    r = _rstd(x)
    dyg = dy * g
    dx = r * dyg - x * (r * r * r) * jnp.mean(x * dyg, axis=-1, keepdims=True)
    dg = jnp.sum(dy * x * r, axis=0, keepdims=True)
    return dx, dg


def _rms_bwd(x, g, dy, dres, out_dtype, name):
    m, d = x.shape
    tr = _row_tile(m)
    grid = (m // tr,)
    has_res = dres is not None

    def body(*refs):
        x_ref, g_ref, dy_ref = refs[:3]
        dx_ref, dg_ref = refs[-2:]
        dx, dg = _rms_bwd_math(x_ref[...], g_ref[...], dy_ref[...].astype(F32))
        if has_res:
            dx = dx + refs[3][...]
        dx_ref[...] = dx.astype(out_dtype)

        @pl.when(pl.program_id(0) == 0)
        def _():
            dg_ref[...] = jnp.zeros_like(dg_ref)

        dg_ref[...] += dg

    row = pl.BlockSpec((tr, d), lambda i: (i, 0))
    vec = pl.BlockSpec((1, d), lambda i: (0, 0))
    ops = [x, g, dy] + ([dres] if has_res else [])
    return pl.pallas_call(
        body, name=name, grid=grid,
        out_shape=(jax.ShapeDtypeStruct((m, d), out_dtype), jax.ShapeDtypeStruct((1, d), F32)),
        in_specs=[row, vec, row] + ([row] if has_res else []), out_specs=(row, vec),
        compiler_params=_params(grid))(*ops)


def _merge_fwd(oa, op, oc, g):
    m = oa.shape[0]
    da, dp, dc = oa.shape[1], op.shape[1], oc.shape[1]
    d = da + dp + dc
    tr = _row_tile(m)
    grid = (m // tr,)

    def body(a_ref, p_ref, c_ref, g_ref, o_ref):
        off = 0
        for ref, w in ((a_ref, da), (p_ref, dp), (c_ref, dc)):
            xx = ref[...]
            o_ref[:, off:off + w] = (xx * _rstd(xx) * g_ref[:, off:off + w]).astype(BF16)
            off += w

    specs = [pl.BlockSpec((tr, w), lambda i: (i, 0)) for w in (da, dp, dc)]
    return pl.pallas_call(body, name="merge_fwd", grid=grid, out_shape=jax.ShapeDtypeStruct((m, d), BF16),
                          in_specs=specs + [pl.BlockSpec((1, d), lambda i: (0, 0))],
                          out_specs=pl.BlockSpec((tr, d), lambda i: (i, 0)), compiler_params=_params(grid))(oa, op, oc, g)


def _merge_bwd(oa, op, oc, g, dmerged):
    m = oa.shape[0]
    da, dp, dc = oa.shape[1], op.shape[1], oc.shape[1]
    d = da + dp + dc
    tr = _row_tile(m)
    grid = (m // tr,)

    def body(a_ref, p_ref, c_ref, g_ref, dm_ref, da_ref, dp_ref, dc_ref, dg_ref):
        @pl.when(pl.program_id(0) == 0)
        def _():
            dg_ref[...] = jnp.zeros_like(dg_ref)

        off = 0
        for ref, oref, w in ((a_ref, da_ref, da), (p_ref, dp_ref, dp), (c_ref, dc_ref, dc)):
            dx, dg = _rms_bwd_math(ref[...], g_ref[:, off:off + w], dm_ref[:, off:off + w])
            oref[...] = dx.astype(oref.dtype)
            dg_ref[:, off:off + w] += dg
            off += w

    specs = [pl.BlockSpec((tr, w), lambda i: (i, 0)) for w in (da, dp, dc)]
    vec = pl.BlockSpec((1, d), lambda i: (0, 0))
    return pl.pallas_call(
        body, name="merge_bwd", grid=grid,
        out_shape=(jax.ShapeDtypeStruct((m, da), F32), jax.ShapeDtypeStruct((m, dp), F32),
                   jax.ShapeDtypeStruct((m, dc), BF16), jax.ShapeDtypeStruct((1, d), F32)),
        in_specs=specs + [vec, pl.BlockSpec((tr, d), lambda i: (i, 0))], out_specs=tuple(specs) + (vec,),
        compiler_params=_params(grid))(oa, op, oc, g, dmerged)


def _loss_grad(h, target, n_meta, n_tok):
    m, d = h.shape
    tr = _row_tile(m)
    grid = (m // tr,)

    def body(h_ref, t_ref, loss_ref, dh_ref):
        i = pl.program_id(0)
        row = i * tr + lax.broadcasted_iota(jnp.int32, (tr, 1), 0)
        live = jnp.logical_and(row >= n_meta, row < n_meta + n_tok)
        diff = jnp.where(live, h_ref[...] - t_ref[...], 0.0)
        dh_ref[...] = diff * (1.0 / d)

        @pl.when(i == 0)
        def _():
            loss_ref[...] = jnp.zeros_like(loss_ref)

        loss_ref[...] += jnp.sum(jnp.sum(diff * diff, axis=1, keepdims=True), axis=0, keepdims=True) * (0.5 / d)

    row = pl.BlockSpec((tr, d), lambda i: (i, 0))
    return pl.pallas_call(
        body, name="loss_grad", grid=grid,
        out_shape=(jax.ShapeDtypeStruct((1, 1), F32), jax.ShapeDtypeStruct((m, d), F32)),
        in_specs=[row, row], out_specs=(pl.BlockSpec((1, 1), lambda i: (0, 0)), row),
        compiler_params=_params(grid))(h, target)


def _split_dot(x, tmat):
    hi = x.astype(BF16)
    lo = (x - hi.astype(F32)).astype(BF16)
    return _dot(hi, tmat) + _dot(lo, tmat)


def _softplus(z):
    return jnp.maximum(z, 0.0) + jnp.log(1.0 + jnp.exp(-jnp.abs(z)))


def _attn_tiles(lp):
    t = ATT_BLOCK
    nb = lp // t
    u = 3 if nb % 3 == 0 else (2 if nb % 2 == 0 else 1)
    return t, nb, u


def _attn_masks(tq, t, u):
    row = lax.broadcasted_iota(jnp.int32, (tq, t), 0)
    col = lax.broadcasted_iota(jnp.int32, (tq, t), 1)
    masks = [col + r * t < row for r in range(u)]
    a = lax.broadcasted_iota(jnp.int32, (t, 2 * t), 0)
    s = lax.broadcasted_iota(jnp.int32, (t, 2 * t), 1)
    after = jnp.logical_or(a > s, s >= t).astype(BF16)
    before = jnp.logical_or(a < s, s >= t).astype(BF16)
    return masks, after, before


def _service_heads(arrs, nh):
    sizes = [a.size * a.dtype.itemsize for a in arrs]
    heads, done = [], 0
    for sz in sizes:
        done += sz
        heads.append(min(nh - 1, int(1.1 * nh * done / sum(sizes)) + 1))
    return heads


def _attn_fwd(q, k, v, gather=()):
    nh, lp, dh = q.shape
    t, nb, u = _attn_tiles(lp)
    tq = u * t
    nq = nb // u
    grid = (nh, nq)
    ng = len(gather)
    service = _service_heads(gather, nh)

    def body(q_ref, k_ref, v_ref, *rest):
        o_ref, tot_ref = rest[ng:ng + 2]
        h, i = pl.program_id(0), pl.program_id(1)
        if ng:
            plan = _GatherPlan(rest[:ng], rest[ng + 2:2 * ng + 2], *rest[2 * ng + 2:])
            pl.when(jnp.logical_and(h == 0, i == 0))(plan.start)
        qb = q_ref[...]
        masks, after, _ = _attn_masks(tq, t, u)
        o_ref[...] = jnp.zeros_like(o_ref)
        tot_ref[...] = jnp.zeros_like(tot_ref)

        def blocks(base, masked):
            outs, sums = [], []
            for r in reversed(range(u)):
                start = pl.multiple_of((base + r) * t, t)
                kb = k_ref[pl.ds(start, t), :]
                z = _dot(qb, kb, NT)
                sp = _softplus(z)
                lnb = jnp.where(masks[r], -sp, 0.0) if masked else -sp
                res = _split_dot(lnb, after)
                outs.append((start, z - sp + res[:, :t], r))
                sums.append(res[:, t:])
            cs = tot_ref[...]
            for (start, logw, r), rs in zip(outs, sums):
                w = jnp.exp(logw + cs)
                if masked:
                    w = jnp.where(masks[r], w, 0.0)
                o_ref[...] += _dot(w.astype(BF16), v_ref[pl.ds(start, t), :])
                cs = cs + rs
            tot_ref[...] = cs

        blocks(i * u, True)

        def step(n, carry):
            blocks((i - 1 - n) * u, False)
            return carry

        lax.fori_loop(0, i, step, 0)

        if ng:
            for a, head in enumerate(service):
                pl.when(jnp.logical_and(h == head, i == nq - 1))(functools.partial(plan.forward, a))
            pl.when(jnp.logical_and(h == nh - 1, i == nq - 1))(plan.finish)

    blk = pl.BlockSpec((None, tq, dh), lambda h, i: (h, i, 0))
    full = pl.BlockSpec((None, lp, dh), lambda h, i: (h, 0, 0))
    return pl.pallas_call(
        body, name="attn_fwd_gather" if ng else "attn_fwd", grid=grid,
        out_shape=(jax.ShapeDtypeStruct((nh, lp, dh), F32), jax.ShapeDtypeStruct((nh, lp, t), F32)) + _gather_shapes(gather),
        in_specs=[blk, full, full] + [ANY] * ng,
        out_specs=(blk, pl.BlockSpec((None, tq, t), lambda h, i: (h, i, 0))) + (ANY,) * ng,
        scratch_shapes=_comm_sems(ng, 7) if ng else [],
        compiler_params=_params(grid))(q, k, v, *gather)


def _attn_bwd(q, qt, k, v, do, dot_, tot, scale, exchange=()):
    nh, lp, dh = q.shape
    t, nb, u = _attn_tiles(lp)
    tq = u * t
    nq = nb // u
    grid = (nh, nq)
    ne = len(exchange)

    def body(q_ref, qt_ref, k_ref, v_ref, do_ref, dot_ref, tot_ref, *rest):
        dq_ref, dk_ref, dv_ref = rest[ne:ne + 3]
        p_ref, g_ref = rest[2 * ne + 3:2 * ne + 5]
        h, i = pl.program_id(0), pl.program_id(1)
        if ne:
            plan = _ChipExchangePlan(rest[:ne], rest[ne + 3:2 * ne + 3], *rest[2 * ne + 5:])
            pl.when(jnp.logical_and(h == 0, i == 0))(plan.start)

        @pl.when(i == 0)
        def _():
            dk_ref[...] = jnp.zeros_like(dk_ref)
            dv_ref[...] = jnp.zeros_like(dv_ref)

        dq_ref[...] = jnp.zeros_like(dq_ref)
        p_ref[...] = jnp.zeros_like(p_ref)
        g_ref[...] = jnp.zeros_like(g_ref)
        qb, qtb, dob, dotb = q_ref[...], qt_ref[...], do_ref[...], dot_ref[...]
        total = tot_ref[...]
        masks, after, before = _attn_masks(tq, t, u)

        def blocks(base, masked):
            part = []
            for r in range(u):
                j = base + r
                start = pl.multiple_of(j * t, t)
                kb = k_ref[pl.ds(start, t), :]
                vb = v_ref[pl.ds(start, t), :]
                z = _dot(qb, kb, NT)
                sp = _softplus(z)
                lnb = jnp.where(masks[r], -sp, 0.0) if masked else -sp
                res = _split_dot(lnb, after)
                part.append((j, kb, z - sp, res[:, :t], res[:, t:], _dot(dob, vb, NT), r))
            pfx = p_ref[...]
            gp = g_ref[...]
            for j, kb, logsig, later_in, rs, dw, r in part:
                pfx = pfx + rs
                w = jnp.exp(logsig + later_in + (total - pfx))
                if masked:
                    w = jnp.where(masks[r], w, 0.0)
                dlogw = w * dw
                res2 = _split_dot(dlogw, before)
                dz = dlogw - jnp.exp(logsig) * (dlogw + res2[:, :t] + gp)
                if masked:
                    dz = jnp.where(masks[r], dz, 0.0)
                gp = gp + res2[:, t:]
                dzb = dz.astype(BF16)
                dq_ref[...] += _dot(dzb, kb)
                dk_ref[j] += _dot(qtb, dzb)
                dv_ref[j] += _dot(dotb, w.astype(BF16))
            p_ref[...] = pfx
            g_ref[...] = gp

        def step(n, carry):
            blocks(n * u, False)
            return carry

        lax.fori_loop(0, i, step, 0)
        blocks(i * u, True)
        dq_ref[...] = dq_ref[...] * scale
        if ne:
            pl.when(jnp.logical_and(h == nh - 1, i == nq - 1))(plan.finish)

    blk = pl.BlockSpec((None, tq, dh), lambda h, i: (h, i, 0))
    blk_t = pl.BlockSpec((None, dh, tq), lambda h, i: (h, 0, i))
    full = pl.BlockSpec((None, lp, dh), lambda h, i: (h, 0, 0))
    acc = pl.BlockSpec((None, nb, dh, t), lambda h, i: (h, 0, 0, 0))
    acc_shape = jax.ShapeDtypeStruct((nh, nb, dh, t), F32)
    return pl.pallas_call(
        body, name="attn_bwd_exchange" if ne else "attn_bwd", grid=grid,
        out_shape=(jax.ShapeDtypeStruct((nh, lp, dh), F32), acc_shape, acc_shape) + _exchange_shapes(exchange),
        in_specs=[blk, blk_t, full, full, blk, blk_t, pl.BlockSpec((None, tq, t), lambda h, i: (h, i, 0))] + [ANY] * ne,
        out_specs=(blk, acc, acc) + (ANY,) * ne,
        scratch_shapes=[pltpu.VMEM((tq, t), F32), pltpu.VMEM((tq, t), F32)] + (_comm_sems(ne, 3) if ne else []),
        compiler_params=_params(grid))(q, qt, k, v, do, dot_, tot, *exchange)


def _halo_specs(tm, width, nt):
    per = tm // HALO
    prev = pl.BlockSpec((HALO, width), lambda i: (jnp.maximum(i * per - 1, 0), 0))
    nxt = pl.BlockSpec((HALO, width), lambda i: (jnp.minimum((i + 1) * per, nt * per - 1), 0))
    return prev, nxt


def _shift_down(x, k):
    return x if k == 0 else pltpu.roll(x, k, axis=0)


def _shift_up(x, k):
    return x if k == 0 else pltpu.roll(x, x.shape[0] - k, axis=0)


def _pool_fwd(u, wp, scale, layer):
    m, dp = u.shape
    g = dp // len(POOL_WINDOWS)
    tm = _row_tile(m)
    nt = m // tm
    grid = (nt,)
    prev, _ = _halo_specs(tm, dp, nt)

    def body(u_ref, h_ref, w_ref, s_ref, o_ref):
        i = pl.program_id(0)
        halo = jnp.where(i > 0, h_ref[...], 0.0)
        s = jnp.concatenate([halo, u_ref[...]], axis=0)
        tpos = i * tm + lax.broadcasted_iota(jnp.int32, (tm, 1), 0)
        for gi, win in enumerate(POOL_WINDOWS):
            s = s + _shift_down(s, win // 2)
            cols = slice(gi * g, (gi + 1) * g)
            cnt = jnp.minimum(tpos + 1, win).astype(F32)
            pooled = s[HALO:, cols] / cnt - u_ref[:, cols]
            o_ref[:, cols] = _dot(pooled.astype(BF16), w_ref[gi]) * s_ref[:, cols]

    row = pl.BlockSpec((tm, dp), lambda i: (i, 0))
    return pl.pallas_call(
        body, name="pool_fwd", grid=grid, out_shape=jax.ShapeDtypeStruct((m, dp), F32),
        in_specs=[row, prev, pl.BlockSpec((None, len(POOL_WINDOWS), g, g), lambda i: (layer, 0, 0, 0)),
                  pl.BlockSpec((1, dp), lambda i: (0, 0))],
        out_specs=row, compiler_params=_params(grid))(u, u, wp, scale)


def _pool_bwd(u, dmix, wp, scale, layer):
    m, dp = u.shape
    ng = len(POOL_WINDOWS)
    g = dp // ng
    tm = _row_tile(m)
    nt = m // tm
    grid = (nt,)
    prev, nxt = _halo_specs(tm, dp, nt)

    def body(u_ref, h_ref, dm_ref, dmn_ref, w_ref, s_ref, du_ref, dw_ref, ds_ref):
        i = pl.program_id(0)

        @pl.when(i == 0)
        def _():
            dw_ref[...] = jnp.zeros_like(dw_ref)
            ds_ref[...] = jnp.zeros_like(ds_ref)

        halo = jnp.where(i > 0, h_ref[...], 0.0)
        s = jnp.concatenate([halo, u_ref[...]], axis=0)
        dmn = jnp.where(i < nt - 1, dmn_ref[...], 0.0)
        dmc = jnp.concatenate([dm_ref[...], dmn], axis=0)
        tpos = i * tm + lax.broadcasted_iota(jnp.int32, (tm, 1), 0)
        tpos_x = i * tm + lax.broadcasted_iota(jnp.int32, (tm + HALO, 1), 0)
        for gi, win in enumerate(POOL_WINDOWS):
            s = s + _shift_down(s, win // 2)
            cols = slice(gi * g, (gi + 1) * g)
            cnt = jnp.minimum(tpos + 1, win).astype(F32)
            pooled = (s[HALO:, cols] / cnt - u_ref[:, cols]).astype(BF16)
            wg = w_ref[gi]
            pm = _dot(pooled, wg)
            ds_ref[:, cols] += jnp.sum(dm_ref[:, cols] * pm, axis=0, keepdims=True)
            dpm = (dmc[:, cols] * s_ref[:, cols]).astype(BF16)
            dw_ref[gi] += _dot(pooled, dpm[:tm], TN)
            dpool = _dot(dpm, wg, NT)
            f = dpool / jnp.minimum(tpos_x + 1, win).astype(F32)
            step = 1
            while step < win:
                f = f + _shift_up(f, step)
                step *= 2
            du_ref[:, cols] = f[:tm] - dpool[:tm]

    row = pl.BlockSpec((tm, dp), lambda i: (i, 0))
    return pl.pallas_call(
        body, name="pool_bwd", grid=grid,
        out_shape=(jax.ShapeDtypeStruct((m, dp), F32), jax.ShapeDtypeStruct((ng, g, g), F32),
                   jax.ShapeDtypeStruct((1, dp), F32)),
        in_specs=[row, prev, row, nxt, pl.BlockSpec((None, ng, g, g), lambda i: (layer, 0, 0, 0)),
                  pl.BlockSpec((1, dp), lambda i: (0, 0))],
        out_specs=(row, pl.BlockSpec((ng, g, g), lambda i: (0, 0, 0)), pl.BlockSpec((1, dp), lambda i: (0, 0))),
        compiler_params=_params(grid))(u, u, dmix, dmix, wp, scale)


def _conv_taps(u, w_ref, rows):
    y = None
    for k in range(CONV_WIDTH):
        term = _shift_down(u, CONV_WIDTH - 1 - k)[HALO:HALO + rows] * w_ref[k:k + 1, :]
        y = term if y is None else y + term
    return y


def _layernorm_stats(y):
    mu = jnp.mean(y, axis=-1, keepdims=True)
    yc = y - mu
    rstd = lax.rsqrt(jnp.mean(yc * yc, axis=-1, keepdims=True) + EPS)
    return yc * rstd, rstd


def _conv_fwd(uc, wdw, b, lg, lb):
    m, c2 = uc.shape
    c = c2 // 2
    tm = _row_tile(m)
    nt = m // tm
    grid = (nt,)
    prev, _ = _halo_specs(tm, c2, nt)

    def body(x_ref, h_ref, w_ref, b_ref, g_ref, bb_ref, o_ref):
        i = pl.program_id(0)
        halo = jnp.where(i > 0, h_ref[...], 0.0)
        xc = jnp.concatenate([halo, x_ref[...]], axis=0)
        u = xc[:, :c] * jax.nn.sigmoid(xc[:, c:])
        y = _conv_taps(u, w_ref, tm) + b_ref[...]
        xhat, _ = _layernorm_stats(y)
        ln = xhat * g_ref[...] + bb_ref[...]
        o_ref[...] = (ln * jax.nn.sigmoid(ln)).astype(BF16)

    vec = pl.BlockSpec((1, c), lambda i: (0, 0))
    return pl.pallas_call(
        body, name="conv_fwd", grid=grid, out_shape=jax.ShapeDtypeStruct((m, c), BF16),
        in_specs=[pl.BlockSpec((tm, c2), lambda i: (i, 0)), prev, pl.BlockSpec((HALO, c), lambda i: (0, 0)), vec, vec, vec],
        out_specs=pl.BlockSpec((tm, c), lambda i: (i, 0)), compiler_params=_params(grid))(uc, uc, wdw, b, lg, lb)


def _conv_bwd(uc, ds, wdw, b, lg, lb):
    m, c2 = uc.shape
    c = c2 // 2
    tm = _row_tile(m)
    nt = m // tm
    grid = (nt,)
    prev, nxt = _halo_specs(tm, c2, nt)
    _, nxt_c = _halo_specs(tm, c, nt)

    def body(x_ref, hp_ref, hn_ref, ds_ref, dsn_ref, w_ref, b_ref, g_ref, bb_ref,
             dx_ref, dw_ref, db_ref, dg_ref, dbb_ref):
        i = pl.program_id(0)

        @pl.when(i == 0)
        def _():
            for ref in (dw_ref, db_ref, dg_ref, dbb_ref):
                ref[...] = jnp.zeros_like(ref)

        hp = jnp.where(i > 0, hp_ref[...], 0.0)
        xc = jnp.concatenate([hp, x_ref[...], hn_ref[...]], axis=0)
        sg = jax.nn.sigmoid(xc[:, c:])
        u = xc[:, :c] * sg
        rows = tm + HALO
        y = _conv_taps(u, w_ref, rows) + b_ref[...]
        xhat, rstd = _layernorm_stats(y)
        ln = xhat * g_ref[...] + bb_ref[...]
        sl = jax.nn.sigmoid(ln)
        dsn = jnp.where(i < nt - 1, dsn_ref[...], 0.0)
        dsx = jnp.concatenate([ds_ref[...], dsn], axis=0)
        dln = dsx * sl * (1.0 + ln * (1.0 - sl))
        dxh = dln * g_ref[...]
        dy = rstd * (dxh - jnp.mean(dxh, axis=-1, keepdims=True)
                     - xhat * jnp.mean(dxh * xhat, axis=-1, keepdims=True))
        dyt = dy[:tm]
        dg_ref[...] += jnp.sum(dln[:tm] * xhat[:tm], axis=0, keepdims=True)
        dbb_ref[...] += jnp.sum(dln[:tm], axis=0, keepdims=True)
        db_ref[...] += jnp.sum(dyt, axis=0, keepdims=True)
        du = None
        for k in range(CONV_WIDTH):
            lag = CONV_WIDTH - 1 - k
            dw_ref[k:k + 1, :] += jnp.sum(dyt * _shift_down(u, lag)[HALO:HALO + tm], axis=0, keepdims=True)
            term = _shift_up(dy, lag)[:tm] * w_ref[k:k + 1, :]
            du = term if du is None else du + term
        a_t = xc[HALO:HALO + tm, :c]
        sg_t = sg[HALO:HALO + tm]
        dx_ref[:, :c] = du * sg_t
        dx_ref[:, c:] = du * a_t * sg_t * (1.0 - sg_t)

    vec = pl.BlockSpec((1, c), lambda i: (0, 0))
    wsp = pl.BlockSpec((HALO, c), lambda i: (0, 0))
    vshape = jax.ShapeDtypeStruct((1, c), F32)
    return pl.pallas_call(
        body, name="conv_bwd", grid=grid,
        out_shape=(jax.ShapeDtypeStruct((m, c2), F32), jax.ShapeDtypeStruct((HALO, c), F32), vshape, vshape, vshape),
        in_specs=[pl.BlockSpec((tm, c2), lambda i: (i, 0)), prev, nxt, pl.BlockSpec((tm, c), lambda i: (i, 0)), nxt_c,
                  wsp, vec, vec, vec],
        out_specs=(pl.BlockSpec((tm, c2), lambda i: (i, 0)), wsp, vec, vec, vec),
        compiler_params=_params(grid))(uc, uc, uc, ds, ds, wdw, b, lg, lb)


def _mesh_pos():
    return lax.axis_index("x"), lax.axis_index("y"), lax.axis_index("c")


def _other_chips(x, y):
    return [(1 - x, y), (x, 1 - y), (1 - x, 1 - y)]


def _comm_sems(n, per):
    return [pltpu.SemaphoreType.DMA((n, per)), pltpu.SemaphoreType.DMA((n, per)), pltpu.SemaphoreType.DMA((n,))]


def _comm_call(body, name, arrs, out_shapes, n_sems):
    n = len(arrs)
    return pl.pallas_call(
        body, name=name, out_shape=out_shapes, in_specs=[ANY] * n, out_specs=tuple([ANY] * len(out_shapes)),
        scratch_shapes=_comm_sems(n, n_sems),
    )(*arrs)


class _GatherPlan:
    def __init__(self, ins, outs, send_sems, recv_sems, local_sems):
        self.ins, self.outs = ins, outs
        self.send_sems, self.recv_sems, self.local_sems = send_sems, recv_sems, local_sems
        x, y, c = _mesh_pos()
        self.c, self.me, self.sib = c, (x, y, c), (x, y, 1 - c)
        self.chips = _other_chips(x, y)

    def _slot(self, a, p):
        return self.outs[a].at[4 * p[0] + 2 * p[1] + p[2]]

    def _copy(self, a, k, block, to, own=False):
        return pltpu.make_async_remote_copy(
            src_ref=self.ins[a] if own else self._slot(a, block), dst_ref=self._slot(a, block),
            send_sem=self.send_sems.at[a, k], recv_sem=self.recv_sems.at[a, k], device_id=to, device_id_type=MESH)

    def _local(self, a):
        return pltpu.make_async_copy(self.ins[a], self._slot(a, self.me), self.local_sems.at[a])

    def start(self):
        for a in range(len(self.ins)):
            for j, chip in enumerate(self.chips):
                self._copy(a, 1 + j, self.me, (*chip, self.c), own=True).start()
            self._copy(a, 0, self.me, self.sib, own=True).start()
            self._local(a).start()

    def forward(self, a):
        for j, chip in enumerate(self.chips):
            self._copy(a, 1 + j, (*chip, self.c), self.me).wait_recv()
            self._copy(a, 4 + j, (*chip, self.c), self.sib).start()

    def finish(self):
        for a in range(len(self.ins)):
            self._copy(a, 0, self.sib, self.me).wait_recv()
            for j, chip in enumerate(self.chips):
                self._copy(a, 4 + j, (*chip, 1 - self.c), self.me).wait_recv()
        for a in range(len(self.ins)):
            self._copy(a, 0, self.me, self.sib, own=True).wait_send()
            for j, chip in enumerate(self.chips):
                self._copy(a, 1 + j, self.me, (*chip, self.c), own=True).wait_send()
                self._copy(a, 4 + j, (*chip, self.c), self.sib).wait_send()
            self._local(a).wait()


def _gather_shapes(arrs):
    return tuple(jax.ShapeDtypeStruct((N_DEV,) + a.shape, a.dtype) for a in arrs)


def _all_gather(arrs, name):
    n = len(arrs)

    def body(*refs):
        plan = _GatherPlan(refs[:n], refs[n:2 * n], *refs[2 * n:])
        plan.start()
        for a in range(n):
            plan.forward(a)
        plan.finish()

    return _comm_call(body, name, arrs, _gather_shapes(arrs), 7)


class _ChipExchangePlan:
    def __init__(self, ins, outs, send_sems, recv_sems, local_sems):
        self.ins, self.outs = ins, outs
        self.send_sems, self.recv_sems, self.local_sems = send_sems, recv_sems, local_sems
        x, y, c = _mesh_pos()
        self.pos, self.mine = (x, y, c), 2 * x + y
        self.chips = _other_chips(x, y)

    def _send(self, a, j):
        chip = self.chips[j]
        return pltpu.make_async_remote_copy(
            src_ref=self.ins[a].at[2 * chip[0] + chip[1]], dst_ref=self.outs[a].at[self.mine],
            send_sem=self.send_sems.at[a, j], recv_sem=self.recv_sems.at[a, j], device_id=(*chip, self.pos[2]),
            device_id_type=MESH)

    def _landing(self, a, j):
        chip = self.chips[j]
        ref = self.outs[a].at[2 * chip[0] + chip[1]]
        return pltpu.make_async_remote_copy(src_ref=ref, dst_ref=ref, send_sem=self.send_sems.at[a, j],
                                            recv_sem=self.recv_sems.at[a, j], device_id=self.pos, device_id_type=MESH)

    def _local(self, a):
        return pltpu.make_async_copy(self.ins[a].at[self.mine], self.outs[a].at[self.mine], self.local_sems.at[a])

    def start(self):
        for a in range(len(self.ins)):
            for j in range(3):
                self._send(a, j).start()
            self._local(a).start()

    def finish(self):
        for a in range(len(self.ins)):
            for j in range(3):
                self._landing(a, j).wait_recv()
        for a in range(len(self.ins)):
            for j in range(3):
                self._send(a, j).wait_send()
            self._local(a).wait()


def _pair_tile(kk):
    return _pick(kk, (1024, 704, 512, 256, 128, 64))


def _pair_exchange(parts, core, name):
    _, kk, n = parts.shape
    tr = _pair_tile(kk)
    nr = kk // tr
    grid = (4, nr)

    def body(c_ref, src_ref, recv_ref, send_sems, recv_sems):
        q, r = pl.program_id(0), pl.program_id(1)
        x, y, c = _mesh_pos()
        step = q * nr + r
        slot = step % 2
        dst = recv_ref.at[pl.ds(pl.multiple_of(step * tr, tr), tr), :]
        cp = pltpu.make_async_remote_copy(src_ref=src_ref, dst_ref=dst, send_sem=send_sems.at[slot],
                                          recv_sem=recv_sems.at[slot], device_id=(x, y, 1 - c), device_id_type=MESH)
        cp.start()
        cp.wait_send()
        cp.wait_recv()

    spec = pltpu.PrefetchScalarGridSpec(
        num_scalar_prefetch=1, grid=grid,
        in_specs=[pl.BlockSpec((tr, n), lambda q, r, c: ((2 * q + 1 - c[0]) * nr + r, 0))],
        out_specs=ANY, scratch_shapes=[pltpu.SemaphoreType.DMA((2,)), pltpu.SemaphoreType.DMA((2,))])
    recv = pl.pallas_call(body, name=name, grid_spec=spec, out_shape=jax.ShapeDtypeStruct((4 * kk, n), parts.dtype),
                          compiler_params=_params(grid))(core, parts.reshape(N_DEV * kk, n))
    return recv.reshape(4, kk, n)


def _exchange_shapes(parts):
    return tuple(jax.ShapeDtypeStruct(p.shape, p.dtype) for p in parts)


def _reduce_chip_exchange(parts, name):
    n = len(parts)

    def body(*refs):
        plan = _ChipExchangePlan(refs[:n], refs[n:2 * n], *refs[2 * n:])
        plan.start()
        plan.finish()

    return _comm_call(body, name, parts, _exchange_shapes(parts), 3)


def _pair_add(parts, recv, core, name):
    _, kk, n = recv.shape
    tr = _pair_tile(kk)
    grid = (4, kk // tr)

    def body(c_ref, a_ref, b_ref, o_ref):
        o_ref[...] = (a_ref[...].astype(F32) + b_ref[...].astype(F32)).astype(o_ref.dtype)

    out = pl.BlockSpec((None, tr, n), lambda q, r, c: (q, r, 0))
    spec = pltpu.PrefetchScalarGridSpec(
        num_scalar_prefetch=1, grid=grid,
        in_specs=[pl.BlockSpec((None, None, tr, n), lambda q, r, c: (q, c[0], r, 0)), out], out_specs=out)
    return pl.pallas_call(body, name=name, grid_spec=spec, out_shape=jax.ShapeDtypeStruct(recv.shape, recv.dtype),
                          compiler_params=_params(grid))(core, parts.reshape(4, 2, kk, n), recv)


def _adamw_math(w, g, m, v):
    m2 = ADAM_B1 * m + (1.0 - ADAM_B1) * g
    v2 = ADAM_B2 * v + (1.0 - ADAM_B2) * (g * g)
    m_hat = m2 / (1.0 - ADAM_B1 ** ADAM_STEP)
    v_hat = v2 / (1.0 - ADAM_B2 ** ADAM_STEP)
    delta = -ADAM_LR * (m_hat / (jnp.sqrt(v_hat) + ADAM_EPS) + ADAM_WD * w)
    return delta, m2, v2


def _adamw_sharded(parts, w, m, v, name):
    nl, kk, n = w.shape
    tr = _pick(kk, (256, 128, 64)) if n <= 1024 else _pick(kk, (64, 32))
    grid = (nl, kk // tr)

    def body(*refs):
        p_refs = refs[:nl]
        w_ref, m_ref, v_ref, g_ref, d_ref, m2_ref, v2_ref = refs[nl:]
        layer = pl.program_id(0)
        g = None
        for l, p_ref in enumerate(p_refs):
            s = p_ref[0].astype(F32)
            for q in range(1, 4):
                s = s + p_ref[q].astype(F32)
            g = s if g is None else jnp.where(layer == l, s, g)
        d, m2, v2 = _adamw_math(w_ref[...], g, m_ref[...], v_ref[...])
        g_ref[...] = g
        d_ref[...] = d
        m2_ref[...] = m2
        v2_ref[...] = v2

    spec = pl.BlockSpec((None, tr, n), lambda l, r: (l, r, 0))
    shp = jax.ShapeDtypeStruct(w.shape, F32)
    p_specs = [pl.BlockSpec((4, tr, n), lambda l, r, k=k: (0, jnp.where(l == k, r, 0), 0)) for k in range(nl)]
    return pl.pallas_call(
        body, name=name, grid=grid, out_shape=(shp, shp, shp, shp), in_specs=p_specs + [spec, spec, spec],
        out_specs=(spec, spec, spec, spec), compiler_params=_params(grid))(*parts, w, m, v)


def _sum_slots(g8):
    _, r, _ = g8.shape

    def body(g_ref, o_ref):
        s = g_ref[0]
        for d in range(1, N_DEV):
            s = s + g_ref[d]
        o_ref[...] = s

    return pl.pallas_call(body, name="sum_slots", out_shape=jax.ShapeDtypeStruct((r, 128), F32))(g8)


def _adamw_flat(w, g, m, v):
    def body(w_ref, g_ref, m_ref, v_ref, d_ref, m2_ref, v2_ref):
        d, m2, v2 = _adamw_math(w_ref[...], g_ref[...], m_ref[...], v_ref[...])
        d_ref[...] = d
        m2_ref[...] = m2
        v2_ref[...] = v2

    shp = jax.ShapeDtypeStruct(w.shape, F32)
    return pl.pallas_call(body, name="adamw_flat", out_shape=(shp, shp, shp))(w, g, m, v)


def _pack(arrs):
    flat = jnp.concatenate([a.reshape(-1) for a in arrs])
    pad = (-flat.shape[0]) % 1024
    return jnp.pad(flat, (0, pad)).reshape(-1, 128)


def _unpack(buf, like):
    flat = buf.reshape(-1)
    out, off = [], 0
    for a in like:
        out.append(flat[off:off + a.size].reshape(a.shape))
        off += a.size
    return out


def _heads(x2d, scale=None):
    lp, da = x2d.shape
    xh = x2d.reshape(lp, N_HEADS, da // N_HEADS).transpose(1, 0, 2)
    if scale is not None:
        xh = xh * scale
    return xh


def _unheads(xh):
    nh, lp, dh = xh.shape
    return xh.transpose(1, 0, 2).reshape(lp, nh * dh)


def _layer_fwd(h, l, W, P, gather_after_proj=None):
    d = h.shape[1]
    da, dp = d // 2, d // 4
    dh = da // N_HEADS
    sv = {"h": h}
    u1 = _rms_fwd(h, P["pre_mix_g"][l:l + 1], "rms_pre_mix")
    proj_s = _mm_cols(u1, *W["in"], "mm_in")
    lp = h.shape[0]
    proj = proj_s.transpose(1, 0, 2).reshape(lp, -1)
    qh = _heads(proj[:, :da], dh ** -0.5).astype(BF16)
    kh = _heads(proj[:, da:2 * da]).astype(BF16)
    vh = _heads(proj[:, 2 * da:3 * da]).astype(BF16)
    u_pool = proj[:, 3 * da:3 * da + dp]
    u_conv = proj[:, 3 * da + dp:]
    if gather_after_proj is None:
        oh, tot = _attn_fwd(qh, kh, vh)
    else:
        arrs, complete = gather_after_proj
        oh, tot, *gathered = _attn_fwd(qh, kh, vh, arrs)
        complete(gathered)
    o_attn = _unheads(oh)
    o_pool = _pool_fwd(u_pool, W["pool"], P["pool_scale"][l:l + 1], l)
    s_conv = _conv_fwd(u_conv, W["dw"][l], P["b_dw"][l:l + 1], P["conv_ln_g"][l:l + 1], P["conv_ln_b"][l:l + 1])
    o_conv = _mm_nn(s_conv, W["pw"][0], "mm_pw", layer=W["pw"][1])
    merged = _merge_fwd(o_attn, o_pool, o_conv, P["mix_out_g"][l:l + 1])
    z1 = _mm_nn(merged, W["out"][0], "mm_out", layer=W["out"][1])
    h1 = _resid_rms_fwd(h, z1, P["post_mix_g"][l:l + 1], "resid_post_mix")
    u2 = _rms_fwd(h1, P["pre_ffn_g"][l:l + 1], "rms_pre_ffn")
    a_s, b_s, act_s = _ffn_up(u2, W["gate"][0], W["up"][0], W["gate"][1])
    ff = _mm_rows(act_s, *W["down"], "mm_down")
    h2 = _resid_rms_fwd(h1, ff, P["post_ffn_g"][l:l + 1], "resid_post_ffn")
    sv.update(u1=u1, qh=qh, kh=kh, vh=vh, tot=tot, u_pool=u_pool, u_conv=u_conv, o_attn=o_attn, o_pool=o_pool,
              o_conv=o_conv, s_conv=s_conv, merged=merged, z1=z1, h1=h1, u2=u2, a_s=a_s, b_s=b_s, act_s=act_s, ff=ff)
    return h2, sv


EARLY = ("w_down", "w_gate", "w_up", "w_out", "w_pw")


def _layer_bwd(dh_out, l, W, P, sv, exchange_with_attn=None):
    d = dh_out.shape[1]
    da = d // 2
    dhd = da // N_HEADS
    lp = dh_out.shape[0]
    g = {}
    dff, g["post_ffn_g"] = _rms_bwd(sv["ff"], P["post_ffn_g"][l:l + 1], dh_out, None, BF16, "rms_bwd_post_ffn")
    da_s, db_s = _ffn_bwd_act(dff, *W["down"], sv["a_s"], sv["b_s"])
    g["w_down"] = _mm_tn_rows(sv["act_s"], dff, "mm_dw_down")
    du2 = _mm_cols_t([da_s, db_s], [W["gate"][0], W["up"][0]], W["gate"][1], "mm_d_u2")
    g["w_gate"] = _mm_tn_cols(sv["u2"], da_s, "mm_dw_gate")
    g["w_up"] = _mm_tn_cols(sv["u2"], db_s, "mm_dw_up")
    dh1, g["pre_ffn_g"] = _rms_bwd(sv["h1"], P["pre_ffn_g"][l:l + 1], du2, dh_out, F32, "rms_bwd_pre_ffn")
    dz1, g["post_mix_g"] = _rms_bwd(sv["z1"], P["post_mix_g"][l:l + 1], dh1, None, BF16, "rms_bwd_post_mix")
    dmerged = _mm_nt(dz1, W["out"][0], "mm_d_merged", W["out"][1])
    g["w_out"] = _mm_tn(sv["merged"], dz1, "mm_dw_out")
    d_oa, d_op, d_oc, g["mix_out_g"] = _merge_bwd(sv["o_attn"], sv["o_pool"], sv["o_conv"], P["mix_out_g"][l:l + 1], dmerged)
    d_s = _mm_nt(d_oc, W["pw"][0], "mm_d_sconv", W["pw"][1])
    g["w_pw"] = _mm_tn(sv["s_conv"], d_oc, "mm_dw_pw")
    d_uc, dwdw, g["b_dw"], g["conv_ln_g"], g["conv_ln_b"] = _conv_bwd(
        sv["u_conv"], d_s, W["dw"][l], P["b_dw"][l:l + 1], P["conv_ln_g"][l:l + 1], P["conv_ln_b"][l:l + 1])
    g["w_dw"] = dwdw[:CONV_WIDTH]
    d_up, g["w_pool"], g["pool_scale"] = _pool_bwd(sv["u_pool"], d_op, W["pool"], P["pool_scale"][l:l + 1], l)
    doh = _heads(d_oa).astype(BF16)
    riders = exchange_with_attn(g) if exchange_with_attn is not None else []
    dqh, dkt, dvt, *exchanged = _attn_bwd(sv["qh"], sv["qh"].transpose(0, 2, 1), sv["kh"], sv["vh"], doh,
                                          doh.transpose(0, 2, 1), sv["tot"], dhd ** -0.5, riders)
    dk2, dv2 = (a.transpose(1, 3, 0, 2).reshape(lp, da) for a in (dkt, dvt))
    dproj = jnp.concatenate([_unheads(dqh), dk2, dv2, d_up, d_uc], axis=1).astype(BF16)
    dproj_s = dproj.reshape(lp, N_DEV, -1).transpose(1, 0, 2)
    du1 = _mm_cols_t([dproj_s], [W["in"][0]], W["in"][1], "mm_d_u1")
    g["w_in"] = _mm_tn_cols(sv["u1"], dproj_s, "mm_dw_in")
    dh0, g["pre_mix_g"] = _rms_bwd(sv["h"], P["pre_mix_g"][l:l + 1], du1, dh1, F32, "rms_bwd_pre_mix")
    return dh0, g, exchanged


SHARDED = ("w_in", "w_gate", "w_up", "w_down", "w_out", "w_pw")
SMALL = ("pre_mix_g", "w_pool", "pool_scale", "b_dw", "conv_ln_g", "conv_ln_b", "mix_out_g", "post_mix_g",
         "pre_ffn_g", "post_ffn_g", "w_dw", "meta_tokens")
ORDER = ("meta_tokens", "pre_mix_g", "w_in", "w_pool", "pool_scale", "w_dw", "b_dw", "conv_ln_g", "conv_ln_b", "w_pw",
         "mix_out_g", "w_out", "post_mix_g", "pre_ffn_g", "w_gate", "w_up", "w_down", "post_ffn_g")


def kernel(x, meta_tokens, pre_mix_g, w_in, w_pool, pool_scale, w_dw, b_dw, conv_ln_g, conv_ln_b, w_pw, mix_out_g, w_out, post_mix_g, pre_ffn_g, w_gate, w_up, w_down, post_ffn_g, loss_target, m_meta_tokens, m_pre_mix_g, m_w_in, m_w_pool, m_pool_scale, m_w_dw, m_b_dw, m_conv_ln_g, m_conv_ln_b, m_w_pw, m_mix_out_g, m_w_out, m_post_mix_g, m_pre_ffn_g, m_w_gate, m_w_up, m_w_down, m_post_ffn_g, v_meta_tokens, v_pre_mix_g, v_w_in, v_w_pool, v_pool_scale, v_w_dw, v_b_dw, v_conv_ln_g, v_conv_ln_b, v_w_pw, v_mix_out_g, v_w_out, v_post_mix_g, v_pre_ffn_g, v_w_gate, v_w_up, v_w_down, v_post_ffn_g):
    P = dict(meta_tokens=meta_tokens, pre_mix_g=pre_mix_g, w_in=w_in, w_pool=w_pool, pool_scale=pool_scale, w_dw=w_dw,
             b_dw=b_dw, conv_ln_g=conv_ln_g, conv_ln_b=conv_ln_b, w_pw=w_pw, mix_out_g=mix_out_g, w_out=w_out,
             post_mix_g=post_mix_g, pre_ffn_g=pre_ffn_g, w_gate=w_gate, w_up=w_up, w_down=w_down, post_ffn_g=post_ffn_g)
    M = dict(meta_tokens=m_meta_tokens, pre_mix_g=m_pre_mix_g, w_in=m_w_in, w_pool=m_w_pool, pool_scale=m_pool_scale,
             w_dw=m_w_dw, b_dw=m_b_dw, conv_ln_g=m_conv_ln_g, conv_ln_b=m_conv_ln_b, w_pw=m_w_pw, mix_out_g=m_mix_out_g,
             w_out=m_w_out, post_mix_g=m_post_mix_g, pre_ffn_g=m_pre_ffn_g, w_gate=m_w_gate, w_up=m_w_up,
             w_down=m_w_down, post_ffn_g=m_post_ffn_g)
    V = dict(meta_tokens=v_meta_tokens, pre_mix_g=v_pre_mix_g, w_in=v_w_in, w_pool=v_w_pool, pool_scale=v_pool_scale,
             w_dw=v_w_dw, b_dw=v_b_dw, conv_ln_g=v_conv_ln_g, conv_ln_b=v_conv_ln_b, w_pw=v_w_pw, mix_out_g=v_mix_out_g,
             w_out=v_w_out, post_mix_g=v_post_mix_g, pre_ffn_g=v_pre_ffn_g, w_gate=v_w_gate, w_up=v_w_up,
             w_down=v_w_down, post_ffn_g=v_post_ffn_g)
    xi, yi, ci = _mesh_pos()
    dev = 4 * xi + 2 * yi + ci
    n_tok, d = x.shape[1], x.shape[2]
    n_meta = meta_tokens.shape[0]
    n_layers = w_in.shape[0]
    c = d // 4
    l_real = n_meta + n_tok
    lp = -(-l_real // ATT_BLOCK) * ATT_BLOCK

    g_in0, g_dw, g_meta = _all_gather([w_in[:1].astype(BF16), w_dw, meta_tokens], "gather_first")
    wdw_full = g_dw.transpose(1, 2, 0, 3).reshape(n_layers, CONV_WIDTH, c)
    common = {"pool": w_pool.astype(BF16), "dw": jnp.pad(wdw_full, ((0, 0), (0, HALO - CONV_WIDTH), (0, 0)))}
    W = [dict(common) for _ in range(n_layers)]
    W[0]["in"] = (g_in0, 0)
    meta_full = g_meta.transpose(1, 0, 2).reshape(n_meta, d)
    later = [w_pw.astype(BF16), w_out.astype(BF16), w_in[1:].astype(BF16), w_gate.astype(BF16), w_up.astype(BF16),
             w_down.astype(BF16)]

    def complete(gathered):
        g_pw, g_out, g_in, g_gate, g_up, g_down = gathered
        full_out = g_out.transpose(1, 0, 2, 3).reshape(n_layers, d, d)
        full_pw = g_pw.transpose(1, 0, 2, 3).reshape(n_layers, c, c)
        for l in range(n_layers):
            W[l].update(out=(full_out, l), pw=(full_pw, l), gate=(g_gate, l), up=(g_up, l), down=(g_down, l))
            if l > 0:
                W[l]["in"] = (g_in, l - 1)

    h = jnp.concatenate([meta_full, x[0], jnp.zeros((lp - l_real, d), F32)], axis=0)
    saved = []
    for l in range(n_layers):
        h, sv = _layer_fwd(h, l, W[l], P, (later, complete) if l == 0 else None)
        saved.append(sv)
    target = jnp.pad(loss_target[0], ((n_meta, lp - l_real), (0, 0)))
    loss_part, dh = _loss_grad(h, target, n_meta, n_tok)
    loss = lax.psum(loss_part[0, 0], ("x", "y", "c"))

    core = jnp.reshape(ci, (1,)).astype(jnp.int32)

    def pair_reduce(g, keys, l):
        res = []
        for k in keys:
            p = g[k] if g[k].ndim == 3 else g[k].reshape(N_DEV, g[k].shape[0] // N_DEV, g[k].shape[1])
            recv = _pair_exchange(p, core, f"pair_exchange_{k}_{l}")
            res.append(_pair_add(p, recv, core, f"pair_add_{k}_{l}"))
        return res

    grads = [None] * n_layers
    riders = []
    pair_sums = []
    for l in reversed(range(1, n_layers)):
        dh, grads[l], _ = _layer_bwd(dh, l, W[l], P, saved[l])
        riders += [(l, k) for k in SHARDED]
        pair_sums += pair_reduce(grads[l], SHARDED, l)

    def ride(g):
        riders.extend((0, k) for k in EARLY)
        return pair_sums + pair_reduce(g, EARLY, 0)

    dh, grads[0], exchanged = _layer_bwd(dh, 0, W[0], P, saved[0], ride)
    by_chip = dict(zip(riders, exchanged))
    (by_chip[(0, "w_in")],) = _reduce_chip_exchange(pair_reduce(grads[0], ("w_in",), 0), "reduce_chips_w_in")
    grad_x = dh[n_meta:l_real][None]
    out = {}
    for k in SHARDED:
        out[k] = _adamw_sharded([by_chip[(l, k)] for l in range(n_layers)], P[k], M[k], V[k], "adamw_" + k)

    small_parts = []
    for k in SMALL:
        if k == "meta_tokens":
            small_parts.append(dh[:n_meta])
        else:
            small_parts.append(jnp.stack([grads[l][k].reshape(P[k].shape[1:] if k != "w_dw" else (CONV_WIDTH, c))
                                          for l in range(n_layers)], axis=0))
    (g8,) = _all_gather([_pack(small_parts)], "gather_small_grads")
    g_small = dict(zip(SMALL, _unpack(_sum_slots(g8), small_parts)))
    g_small["w_dw"] = lax.dynamic_slice_in_dim(g_small["w_dw"], dev * w_dw.shape[2], w_dw.shape[2], axis=2)
    g_small["meta_tokens"] = lax.dynamic_slice_in_dim(g_small["meta_tokens"], dev * meta_tokens.shape[1],
                                                      meta_tokens.shape[1], axis=1)
    like = [P[k] for k in SMALL]
    res = _adamw_flat(_pack(like), _pack([g_small[k] for k in SMALL]), _pack([M[k] for k in SMALL]),
                      _pack([V[k] for k in SMALL]))
    res = [_unpack(r, like) for r in res]
    for i, k in enumerate(SMALL):
        out[k] = (g_small[k], res[0][i], res[1][i], res[2][i])

    return (loss, grad_x, *[out[k][0] for k in ORDER], *[out[k][1] for k in ORDER],
            *[out[k][2] for k in ORDER], *[out[k][3] for k in ORDER])
```

```python
import functools

import jax
import jax.numpy as jnp
from jax import lax
from jax.experimental import pallas as pl
from jax.experimental.pallas import tpu as pltpu

F32 = jnp.float32
BF16 = jnp.bfloat16
EPS = 1e-6
N_HEADS = 16
POOL_WINDOWS = (2, 4, 8, 16)
CONV_WIDTH = 31
HALO = 32
ATT_BLOCK = 128
DEAD_LOG_WEIGHT = 110.0
N_DEV = 8
VMEM_LIMIT = 60 * 1024 * 1024

ADAM_LR = 0.001
ADAM_B1 = 0.9
ADAM_B2 = 0.999
ADAM_EPS = 1e-08
ADAM_WD = 0.01
ADAM_STEP = 10

NN = (((1,), (0,)), ((), ()))
NT = (((1,), (1,)), ((), ()))
TN = (((0,), (0,)), ((), ()))
MESH = pl.DeviceIdType.MESH
ANY = pl.BlockSpec(memory_space=pl.ANY)


def _pick(n, cands):
    for c in cands:
        if c <= n and n % c == 0:
            return c
    return n


def _params(grid):
    return pltpu.CompilerParams(dimension_semantics=("arbitrary",) * len(grid), vmem_limit_bytes=VMEM_LIMIT)


def _dot(a, b, dims=NN):
    return lax.dot_general(a, b, dims, preferred_element_type=F32)


def _matmul(name, pairs, specs, out_shape, out_spec, grid, dims, acc_shape):
    n = len(pairs)
    nk = grid[-1]
    kaxis = len(grid) - 1

    def body(*refs):
        o_ref, acc = refs[2 * n], refs[2 * n + 1]
        tot = None
        for p in range(n):
            d = _dot(refs[2 * p][...], refs[2 * p + 1][...], dims)
            tot = d if tot is None else tot + d
        if nk == 1:
            o_ref[...] = tot.astype(o_ref.dtype)
            return
        k = pl.program_id(kaxis)

        @pl.when(k == 0)
        def _():
            acc[...] = tot

        @pl.when(k > 0)
        def _():
            acc[...] += tot

        @pl.when(k == nk - 1)
        def _():
            o_ref[...] = acc[...].astype(o_ref.dtype)

    ops, in_specs = [], []
    for (a, b), (sa, sb) in zip(pairs, specs):
        ops += [a, b]
        in_specs += [sa, sb]
    return pl.pallas_call(
        body, name=name, out_shape=out_shape, grid=grid, in_specs=in_specs, out_specs=out_spec,
        scratch_shapes=[pltpu.VMEM(acc_shape if nk > 1 else (8, 128), F32)], compiler_params=_params(grid),
    )(*ops)


def _mm_nn(a, b, name, out_dtype=F32, layer=None):
    m, kk = a.shape
    n = b.shape[-1]
    tm = _pick(m, (1408, 1024, 512, 256, 128))
    tn = _pick(n, (1024, 512, 256, 128))
    grid = (m // tm, n // tn, 1)
    sa = pl.BlockSpec((tm, kk), lambda i, j, k: (i, 0))
    if layer is None:
        sb = pl.BlockSpec((kk, tn), lambda i, j, k: (0, j))
    else:
        sb = pl.BlockSpec((None, kk, tn), lambda i, j, k: (layer, 0, j))
    return _matmul(name, [(a, b)], [(sa, sb)], jax.ShapeDtypeStruct((m, n), out_dtype),
                   pl.BlockSpec((tm, tn), lambda i, j, k: (i, j)), grid, NN, (tm, tn))


def _mm_nt(a, b, name, layer, out_dtype=F32):
    m, kk = a.shape
    n = b.shape[1]
    tm = _pick(m, (1408, 1024, 512, 256, 128))
    tn = _pick(n, (1024, 512, 256, 128))
    grid = (m // tm, n // tn, 1)
    sa = pl.BlockSpec((tm, kk), lambda i, j, k: (i, 0))
    sb = pl.BlockSpec((None, tn, kk), lambda i, j, k: (layer, j, 0))
    return _matmul(name, [(a, b)], [(sa, sb)], jax.ShapeDtypeStruct((m, n), out_dtype),
                   pl.BlockSpec((tm, tn), lambda i, j, k: (i, j)), grid, NT, (tm, tn))


def _mm_tn(a, b, name, out_dtype=BF16):
    l, m = a.shape
    n = b.shape[1]
    tl = _pick(l, (1408, 1024, 512, 256, 128))
    tm = _pick(m, (1024, 512, 256, 128))
    tn = _pick(n, (1024, 512, 256, 128))
    grid = (m // tm, n // tn, l // tl)
    sa = pl.BlockSpec((tl, tm), lambda i, j, t: (t, i))
    sb = pl.BlockSpec((tl, tn), lambda i, j, t: (t, j))
    return _matmul(name, [(a, b)], [(sa, sb)], jax.ShapeDtypeStruct((m, n), out_dtype),
                   pl.BlockSpec((tm, tn), lambda i, j, t: (i, j)), grid, TN, (tm, tn))


def _mm_cols(a, w, layer, name):
    m, kk = a.shape
    nd, _, _, n = w.shape
    tm = _pick(m, (1408, 1024, 512, 256, 128))
    grid = (nd, m // tm, 1)
    sa = pl.BlockSpec((tm, kk), lambda d, i, k: (i, 0))
    sb = pl.BlockSpec((None, None, kk, n), lambda d, i, k: (d, layer, 0, 0))
    return _matmul(name, [(a, w)], [(sa, sb)], jax.ShapeDtypeStruct((nd, m, n), F32),
                   pl.BlockSpec((None, tm, n), lambda d, i, k: (d, i, 0)), grid, NN, (tm, n))


def _mm_rows(a_s, w, layer, name):
    nd, m, kk = a_s.shape
    n = w.shape[-1]
    tm = _pick(m, (1408, 1024, 512, 256, 128))
    tn = _pick(n, (1024, 512, 256, 128))
    grid = (m // tm, n // tn, nd)
    sa = pl.BlockSpec((None, tm, kk), lambda i, j, d: (d, i, 0))
    sb = pl.BlockSpec((None, None, kk, tn), lambda i, j, d: (d, layer, 0, j))
    return _matmul(name, [(a_s, w)], [(sa, sb)], jax.ShapeDtypeStruct((m, n), F32),
                   pl.BlockSpec((tm, tn), lambda i, j, d: (i, j)), grid, NN, (tm, tn))


def _mm_cols_t(das, ws, layer, name):
    nd, m, n = das[0].shape
    kk = ws[0].shape[2]
    tm = _pick(m, (1408, 1024, 512, 256, 128))
    tk = _pick(kk, (1024, 512, 256, 128))
    grid = (m // tm, kk // tk, nd)
    sa = pl.BlockSpec((None, tm, n), lambda i, j, d: (d, i, 0))
    sb = pl.BlockSpec((None, None, tk, n), lambda i, j, d: (d, layer, j, 0))
    return _matmul(name, list(zip(das, ws)), [(sa, sb)] * len(das), jax.ShapeDtypeStruct((m, kk), F32),
                   pl.BlockSpec((tm, tk), lambda i, j, d: (i, j)), grid, NT, (tm, tk))


def _mm_tn_cols(a, b_s, name):
    l, kk = a.shape
    nd, _, n = b_s.shape
    tl = _pick(l, (1408, 1024, 512, 256, 128))
    tk = _pick(kk, (1024, 512, 256, 128))
    grid = (nd, kk // tk, l // tl)
    sa = pl.BlockSpec((tl, tk), lambda d, j, t: (t, j))
    sb = pl.BlockSpec((None, tl, n), lambda d, j, t: (d, t, 0))
    return _matmul(name, [(a, b_s)], [(sa, sb)], jax.ShapeDtypeStruct((nd, kk, n), BF16),
                   pl.BlockSpec((None, tk, n), lambda d, j, t: (d, j, 0)), grid, TN, (tk, n))


def _mm_tn_rows(a_s, b, name):
    nd, l, kk = a_s.shape
    n = b.shape[1]
    tl = _pick(l, (1408, 1024, 512, 256, 128))
    tn = _pick(n, (1024, 512, 256, 128))
    grid = (nd, n // tn, l // tl)
    sa = pl.BlockSpec((None, tl, kk), lambda d, j, t: (d, t, 0))
    sb = pl.BlockSpec((tl, tn), lambda d, j, t: (t, j))
    return _matmul(name, [(a_s, b)], [(sa, sb)], jax.ShapeDtypeStruct((nd, kk, n), BF16),
                   pl.BlockSpec((None, kk, tn), lambda d, j, t: (d, 0, j)), grid, TN, (kk, tn))


def _ffn_up(u, wg, wu, layer):
    m, kk = u.shape
    nd, _, _, n = wg.shape
    tm = _pick(m, (704, 512, 256, 128))
    grid = (nd, m // tm)

    def body(u_ref, wg_ref, wu_ref, a_ref, b_ref, act_ref):
        uu = u_ref[...]
        a = _dot(uu, wg_ref[...])
        b = _dot(uu, wu_ref[...])
        a_ref[...] = a
        b_ref[...] = b
        act_ref[...] = (a * jax.nn.sigmoid(a) * b).astype(BF16)

    wspec = pl.BlockSpec((None, None, kk, n), lambda d, i: (d, layer, 0, 0))
    ospec = pl.BlockSpec((None, tm, n), lambda d, i: (d, i, 0))
    return pl.pallas_call(
        body, name="ffn_up", grid=grid,
        out_shape=(jax.ShapeDtypeStruct((nd, m, n), F32), jax.ShapeDtypeStruct((nd, m, n), F32),
                   jax.ShapeDtypeStruct((nd, m, n), BF16)),
        in_specs=[pl.BlockSpec((tm, kk), lambda d, i: (i, 0)), wspec, wspec],
        out_specs=(ospec, ospec, ospec), compiler_params=_params(grid),
    )(u, wg, wu)


def _ffn_bwd_act(dff, wd, layer, a_s, b_s):
    m, kk = dff.shape
    nd, _, n, _ = wd.shape
    tm = _pick(m, (704, 512, 256, 128))
    grid = (nd, m // tm)

    def body(d_ref, w_ref, a_ref, b_ref, da_ref, db_ref):
        dact = _dot(d_ref[...], w_ref[...], NT)
        a, b = a_ref[...], b_ref[...]
        sg = jax.nn.sigmoid(a)
        db_ref[...] = (dact * a * sg).astype(BF16)
        da_ref[...] = (dact * b * sg * (1.0 + a * (1.0 - sg))).astype(BF16)

    tspec = pl.BlockSpec((None, tm, n), lambda d, i: (d, i, 0))
    return pl.pallas_call(
        body, name="ffn_bwd_act", grid=grid,
        out_shape=(jax.ShapeDtypeStruct((nd, m, n), BF16), jax.ShapeDtypeStruct((nd, m, n), BF16)),
        in_specs=[pl.BlockSpec((tm, kk), lambda d, i: (i, 0)),
                  pl.BlockSpec((None, None, n, kk), lambda d, i: (d, layer, 0, 0)), tspec, tspec],
        out_specs=(tspec, tspec), compiler_params=_params(grid),
    )(dff, wd, a_s, b_s)


def _rstd(x):
    return lax.rsqrt(jnp.mean(x * x, axis=-1, keepdims=True) + EPS)


def _row_tile(m):
    return _pick(m, (384, 256, 128))


def _rms_fwd(x, g, name):
    m, d = x.shape
    tr = _row_tile(m)
    grid = (m // tr,)

    def body(x_ref, g_ref, o_ref):
        xx = x_ref[...]
        o_ref[...] = (xx * _rstd(xx) * g_ref[...]).astype(BF16)

    row = pl.BlockSpec((tr, d), lambda i: (i, 0))
    return pl.pallas_call(body, name=name, grid=grid, out_shape=jax.ShapeDtypeStruct((m, d), BF16),
                          in_specs=[row, pl.BlockSpec((1, d), lambda i: (0, 0))], out_specs=row,
                          compiler_params=_params(grid))(x, g)


def _resid_rms_fwd(h, z, g, name):
    m, d = h.shape
    tr = _row_tile(m)
    grid = (m // tr,)

    def body(h_ref, z_ref, g_ref, o_ref):
        zz = z_ref[...]
        o_ref[...] = h_ref[...] + zz * _rstd(zz) * g_ref[...]

    row = pl.BlockSpec((tr, d), lambda i: (i, 0))
    return pl.pallas_call(body, name=name, grid=grid, out_shape=jax.ShapeDtypeStruct((m, d), F32),
                          in_specs=[row, row, pl.BlockSpec((1, d), lambda i: (0, 0))], out_specs=row,
                          compiler_params=_params(grid))(h, z, g)


def _rms_bwd_math(x, g, dy):
    r = _rstd(x)
    dyg = dy * g
    dx = r * dyg - x * (r * r * r) * jnp.mean(x * dyg, axis=-1, keepdims=True)
    dg = jnp.sum(dy * x * r, axis=0, keepdims=True)
    return dx, dg


def _rms_bwd(x, g, dy, dres, out_dtype, name):
    m, d = x.shape
    tr = _row_tile(m)
    grid = (m // tr,)
    has_res = dres is not None

    def body(*refs):
        x_ref, g_ref, dy_ref = refs[:3]
        dx_ref, dg_ref = refs[-2:]
        dx, dg = _rms_bwd_math(x_ref[...], g_ref[...], dy_ref[...].astype(F32))
        if has_res:
            dx = dx + refs[3][...]
        dx_ref[...] = dx.astype(out_dtype)

        @pl.when(pl.program_id(0) == 0)
        def _():
            dg_ref[...] = jnp.zeros_like(dg_ref)

        dg_ref[...] += dg

    row = pl.BlockSpec((tr, d), lambda i: (i, 0))
    vec = pl.BlockSpec((1, d), lambda i: (0, 0))
    ops = [x, g, dy] + ([dres] if has_res else [])
    return pl.pallas_call(
        body, name=name, grid=grid,
        out_shape=(jax.ShapeDtypeStruct((m, d), out_dtype), jax.ShapeDtypeStruct((1, d), F32)),
        in_specs=[row, vec, row] + ([row] if has_res else []), out_specs=(row, vec),
        compiler_params=_params(grid))(*ops)


def _merge_fwd(oa, op, oc, g):
    m = oa.shape[0]
    da, dp, dc = oa.shape[1], op.shape[1], oc.shape[1]
    d = da + dp + dc
    tr = _row_tile(m)
    grid = (m // tr,)

    def body(a_ref, p_ref, c_ref, g_ref, o_ref):
        off = 0
        for ref, w in ((a_ref, da), (p_ref, dp), (c_ref, dc)):
            xx = ref[...]
            o_ref[:, off:off + w] = (xx * _rstd(xx) * g_ref[:, off:off + w]).astype(BF16)
            off += w

    specs = [pl.BlockSpec((tr, w), lambda i: (i, 0)) for w in (da, dp, dc)]
    return pl.pallas_call(body, name="merge_fwd", grid=grid, out_shape=jax.ShapeDtypeStruct((m, d), BF16),
                          in_specs=specs + [pl.BlockSpec((1, d), lambda i: (0, 0))],
                          out_specs=pl.BlockSpec((tr, d), lambda i: (i, 0)), compiler_params=_params(grid))(oa, op, oc, g)


def _merge_bwd(oa, op, oc, g, dmerged):
    m = oa.shape[0]
    da, dp, dc = oa.shape[1], op.shape[1], oc.shape[1]
    d = da + dp + dc
    tr = _row_tile(m)
    grid = (m // tr,)

    def body(a_ref, p_ref, c_ref, g_ref, dm_ref, da_ref, dp_ref, dc_ref, dg_ref):
        @pl.when(pl.program_id(0) == 0)
        def _():
            dg_ref[...] = jnp.zeros_like(dg_ref)

        off = 0
        for ref, oref, w in ((a_ref, da_ref, da), (p_ref, dp_ref, dp), (c_ref, dc_ref, dc)):
            dx, dg = _rms_bwd_math(ref[...], g_ref[:, off:off + w], dm_ref[:, off:off + w])
            oref[...] = dx.astype(oref.dtype)
            dg_ref[:, off:off + w] += dg
            off += w

    specs = [pl.BlockSpec((tr, w), lambda i: (i, 0)) for w in (da, dp, dc)]
    vec = pl.BlockSpec((1, d), lambda i: (0, 0))
    return pl.pallas_call(
        body, name="merge_bwd", grid=grid,
        out_shape=(jax.ShapeDtypeStruct((m, da), F32), jax.ShapeDtypeStruct((m, dp), F32),
                   jax.ShapeDtypeStruct((m, dc), BF16), jax.ShapeDtypeStruct((1, d), F32)),
        in_specs=specs + [vec, pl.BlockSpec((tr, d), lambda i: (i, 0))], out_specs=tuple(specs) + (vec,),
        compiler_params=_params(grid))(oa, op, oc, g, dmerged)


def _loss_grad(h, target, n_meta, n_tok):
    m, d = h.shape
    tr = _row_tile(m)
    grid = (m // tr,)

    def body(h_ref, t_ref, loss_ref, dh_ref):
        i = pl.program_id(0)
        row = i * tr + lax.broadcasted_iota(jnp.int32, (tr, 1), 0)
        live = jnp.logical_and(row >= n_meta, row < n_meta + n_tok)
        diff = jnp.where(live, h_ref[...] - t_ref[...], 0.0)
        dh_ref[...] = diff * (1.0 / d)

        @pl.when(i == 0)
        def _():
            loss_ref[...] = jnp.zeros_like(loss_ref)

        loss_ref[...] += jnp.sum(jnp.sum(diff * diff, axis=1, keepdims=True), axis=0, keepdims=True) * (0.5 / d)

    row = pl.BlockSpec((tr, d), lambda i: (i, 0))
    return pl.pallas_call(
        body, name="loss_grad", grid=grid,
        out_shape=(jax.ShapeDtypeStruct((1, 1), F32), jax.ShapeDtypeStruct((m, d), F32)),
        in_specs=[row, row], out_specs=(pl.BlockSpec((1, 1), lambda i: (0, 0)), row),
        compiler_params=_params(grid))(h, target)


def _split_dot(x, tmat):
    hi = x.astype(BF16)
    lo = (x - hi.astype(F32)).astype(BF16)
    return _dot(hi, tmat) + _dot(lo, tmat)


def _softplus(z):
    return jnp.maximum(z, 0.0) + jnp.log(1.0 + jnp.exp(-jnp.abs(z)))


def _attn_tiles(lp):
    t = ATT_BLOCK
    nb = lp // t
    u = 3 if nb % 3 == 0 else (2 if nb % 2 == 0 else 1)
    return t, nb, u


def _attn_masks(tq, t, u):
    row = lax.broadcasted_iota(jnp.int32, (tq, t), 0)
    col = lax.broadcasted_iota(jnp.int32, (tq, t), 1)
    masks = [col + r * t < row for r in range(u)]
    a = lax.broadcasted_iota(jnp.int32, (t, 2 * t), 0)
    s = lax.broadcasted_iota(jnp.int32, (t, 2 * t), 1)
    after = jnp.logical_or(a > s, s >= t).astype(BF16)
    before = jnp.logical_or(a < s, s >= t).astype(BF16)
    return masks, after, before


def _service_heads(arrs, nh):
    sizes = [a.size * a.dtype.itemsize for a in arrs]
    heads, done = [], 0
    for sz in sizes:
        done += sz
        heads.append(min(nh - 1, int(1.1 * nh * done / sum(sizes)) + 1))
    return heads


def _attn_fwd(q, k, v, gather=()):
    nh, lp, dh = q.shape
    t, nb, u = _attn_tiles(lp)
    tq = u * t
    nq = nb // u
    grid = (nh, nq)
    ng = len(gather)
    service = _service_heads(gather, nh)

    def body(q_ref, k_ref, v_ref, *rest):
        o_ref, tot_ref, cnt_ref = rest[ng:ng + 3]
        h, i = pl.program_id(0), pl.program_id(1)
        if ng:
            plan = _GatherPlan(rest[:ng], rest[ng + 3:2 * ng + 3], *rest[2 * ng + 3:])
            pl.when(jnp.logical_and(h == 0, i == 0))(plan.start)
        qb = q_ref[...]
        masks, after, _ = _attn_masks(tq, t, u)
        o_ref[...] = jnp.zeros_like(o_ref)
        tot_ref[...] = jnp.zeros_like(tot_ref)

        def blocks(base, masked):
            outs, sums = [], []
            for r in reversed(range(u)):
                start = pl.multiple_of((base + r) * t, t)
                kb = k_ref[pl.ds(start, t), :]
                z = _dot(qb, kb, NT)
                sp = _softplus(z)
                lnb = jnp.where(masks[r], -sp, 0.0) if masked else -sp
                res = _split_dot(lnb, after)
                outs.append((start, z - sp + res[:, :t], r))
                sums.append(res[:, t:])
            cs = tot_ref[...]
            for (start, logw, r), rs in zip(outs, sums):
                w = jnp.exp(logw + cs)
                if masked:
                    w = jnp.where(masks[r], w, 0.0)
                o_ref[...] += _dot(w.astype(BF16), v_ref[pl.ds(start, t), :])
                cs = cs + rs
            tot_ref[...] = cs

        blocks(i * u, True)

        def live():
            return jnp.max(tot_ref[...]) > -DEAD_LOG_WEIGHT

        def step(state):
            n, _ = state
            blocks((i - 1 - n) * u, False)
            return n + 1, live()

        n_done, _ = lax.while_loop(lambda s: jnp.logical_and(s[0] < i, s[1]), step, (jnp.int32(0), live()))
        cnt_ref[h, i] = n_done.astype(F32)

        if ng:
            for a, head in enumerate(service):
                pl.when(jnp.logical_and(h == head, i == nq - 1))(functools.partial(plan.forward, a))
            pl.when(jnp.logical_and(h == nh - 1, i == nq - 1))(plan.finish)

    blk = pl.BlockSpec((None, tq, dh), lambda h, i: (h, i, 0))
    full = pl.BlockSpec((None, lp, dh), lambda h, i: (h, 0, 0))
    return pl.pallas_call(
        body, name="attn_fwd_gather" if ng else "attn_fwd", grid=grid,
        out_shape=(jax.ShapeDtypeStruct((nh, lp, dh), F32), jax.ShapeDtypeStruct((nh, lp, t), F32),
                   jax.ShapeDtypeStruct((nh, nq), F32)) + _gather_shapes(gather),
        in_specs=[blk, full, full] + [ANY] * ng,
        out_specs=(blk, pl.BlockSpec((None, tq, t), lambda h, i: (h, i, 0)), pl.BlockSpec(memory_space=pltpu.SMEM))
        + (ANY,) * ng,
        scratch_shapes=_comm_sems(ng, 7) if ng else [],
        compiler_params=_params(grid))(q, k, v, *gather)


def _attn_bwd(q, qt, k, v, do, dot_, tot, cnt, scale, exchange=()):
    nh, lp, dh = q.shape
    t, nb, u = _attn_tiles(lp)
    tq = u * t
    nq = nb // u
    grid = (nh, nq)
    ne = len(exchange)

    def body(q_ref, qt_ref, k_ref, v_ref, do_ref, dot_ref, tot_ref, cnt_ref, *rest):
        dq_ref, dk_ref, dv_ref = rest[ne:ne + 3]
        p_ref, g_ref = rest[2 * ne + 3:2 * ne + 5]
        h, i = pl.program_id(0), pl.program_id(1)
        if ne:
            plan = _ChipExchangePlan(rest[:ne], rest[ne + 3:2 * ne + 3], *rest[2 * ne + 5:])
            pl.when(jnp.logical_and(h == 0, i == 0))(plan.start)

        @pl.when(i == 0)
        def _():
            dk_ref[...] = jnp.zeros_like(dk_ref)
            dv_ref[...] = jnp.zeros_like(dv_ref)

        dq_ref[...] = jnp.zeros_like(dq_ref)
        p_ref[...] = jnp.zeros_like(p_ref)
        g_ref[...] = jnp.zeros_like(g_ref)
        qb, qtb, dob, dotb = q_ref[...], qt_ref[...], do_ref[...], dot_ref[...]
        total = tot_ref[...]
        masks, after, before = _attn_masks(tq, t, u)

        def blocks(base, masked):
            part = []
            for r in range(u):
                j = base + r
                start = pl.multiple_of(j * t, t)
                kb = k_ref[pl.ds(start, t), :]
                vb = v_ref[pl.ds(start, t), :]
                z = _dot(qb, kb, NT)
                sp = _softplus(z)
                lnb = jnp.where(masks[r], -sp, 0.0) if masked else -sp
                res = _split_dot(lnb, after)
                part.append((j, kb, z - sp, res[:, :t], res[:, t:], _dot(dob, vb, NT), r))
            pfx = p_ref[...]
            gp = g_ref[...]
            for j, kb, logsig, later_in, rs, dw, r in part:
                pfx = pfx + rs
                w = jnp.exp(logsig + later_in + (total - pfx))
                if masked:
                    w = jnp.where(masks[r], w, 0.0)
                dlogw = w * dw
                res2 = _split_dot(dlogw, before)
                dz = dlogw - jnp.exp(logsig) * (dlogw + res2[:, :t] + gp)
                if masked:
                    dz = jnp.where(masks[r], dz, 0.0)
                gp = gp + res2[:, t:]
                dzb = dz.astype(BF16)
                dq_ref[...] += _dot(dzb, kb)
                dk_ref[j] += _dot(qtb, dzb)
                dv_ref[j] += _dot(dotb, w.astype(BF16))
            p_ref[...] = pfx
            g_ref[...] = gp

        def step(n, carry):
            blocks(n * u, False)
            return carry

        walked = jnp.clip(cnt_ref[h, i].astype(jnp.int32), 0, i)
        lax.fori_loop(i - walked, i, step, 0)
        blocks(i * u, True)
        dq_ref[...] = dq_ref[...] * scale
        if ne:
            pl.when(jnp.logical_and(h == nh - 1, i == nq - 1))(plan.finish)

    blk = pl.BlockSpec((None, tq, dh), lambda h, i: (h, i, 0))
    blk_t = pl.BlockSpec((None, dh, tq), lambda h, i: (h, 0, i))
    full = pl.BlockSpec((None, lp, dh), lambda h, i: (h, 0, 0))
    acc = pl.BlockSpec((None, nb, dh, t), lambda h, i: (h, 0, 0, 0))
    acc_shape = jax.ShapeDtypeStruct((nh, nb, dh, t), F32)
    return pl.pallas_call(
        body, name="attn_bwd_exchange" if ne else "attn_bwd", grid=grid,
        out_shape=(jax.ShapeDtypeStruct((nh, lp, dh), F32), acc_shape, acc_shape) + _exchange_shapes(exchange),
        in_specs=[blk, blk_t, full, full, blk, blk_t, pl.BlockSpec((None, tq, t), lambda h, i: (h, i, 0)),
                  pl.BlockSpec(memory_space=pltpu.SMEM)] + [ANY] * ne,
        out_specs=(blk, acc, acc) + (ANY,) * ne,
        scratch_shapes=[pltpu.VMEM((tq, t), F32), pltpu.VMEM((tq, t), F32)] + (_comm_sems(ne, 3) if ne else []),
        compiler_params=_params(grid))(q, qt, k, v, do, dot_, tot, cnt, *exchange)


def _halo_specs(tm, width, nt):
    per = tm // HALO
    prev = pl.BlockSpec((HALO, width), lambda i: (jnp.maximum(i * per - 1, 0), 0))
    nxt = pl.BlockSpec((HALO, width), lambda i: (jnp.minimum((i + 1) * per, nt * per - 1), 0))
    return prev, nxt


def _shift_down(x, k):
    return x if k == 0 else pltpu.roll(x, k, axis=0)


def _shift_up(x, k):
    return x if k == 0 else pltpu.roll(x, x.shape[0] - k, axis=0)


def _pool_fwd(u, wp, scale, layer):
    m, dp = u.shape
    g = dp // len(POOL_WINDOWS)
    tm = _row_tile(m)
    nt = m // tm
    grid = (nt,)
    prev, _ = _halo_specs(tm, dp, nt)

    def body(u_ref, h_ref, w_ref, s_ref, o_ref):
        i = pl.program_id(0)
        halo = jnp.where(i > 0, h_ref[...], 0.0)
        s = jnp.concatenate([halo, u_ref[...]], axis=0)
        tpos = i * tm + lax.broadcasted_iota(jnp.int32, (tm, 1), 0)
        for gi, win in enumerate(POOL_WINDOWS):
            s = s + _shift_down(s, win // 2)
            cols = slice(gi * g, (gi + 1) * g)
            cnt = jnp.minimum(tpos + 1, win).astype(F32)
            pooled = s[HALO:, cols] / cnt - u_ref[:, cols]
            o_ref[:, cols] = _dot(pooled.astype(BF16), w_ref[gi]) * s_ref[:, cols]

    row = pl.BlockSpec((tm, dp), lambda i: (i, 0))
    return pl.pallas_call(
        body, name="pool_fwd", grid=grid, out_shape=jax.ShapeDtypeStruct((m, dp), F32),
        in_specs=[row, prev, pl.BlockSpec((None, len(POOL_WINDOWS), g, g), lambda i: (layer, 0, 0, 0)),
                  pl.BlockSpec((1, dp), lambda i: (0, 0))],
        out_specs=row, compiler_params=_params(grid))(u, u, wp, scale)


def _pool_bwd(u, dmix, wp, scale, layer):
    m, dp = u.shape
    ng = len(POOL_WINDOWS)
    g = dp // ng
    tm = _row_tile(m)
    nt = m // tm
    grid = (nt,)
    prev, nxt = _halo_specs(tm, dp, nt)

    def body(u_ref, h_ref, dm_ref, dmn_ref, w_ref, s_ref, du_ref, dw_ref, ds_ref):
        i = pl.program_id(0)

        @pl.when(i == 0)
        def _():
            dw_ref[...] = jnp.zeros_like(dw_ref)
            ds_ref[...] = jnp.zeros_like(ds_ref)

        halo = jnp.where(i > 0, h_ref[...], 0.0)
        s = jnp.concatenate([halo, u_ref[...]], axis=0)
        dmn = jnp.where(i < nt - 1, dmn_ref[...], 0.0)
        dmc = jnp.concatenate([dm_ref[...], dmn], axis=0)
        tpos = i * tm + lax.broadcasted_iota(jnp.int32, (tm, 1), 0)
        tpos_x = i * tm + lax.broadcasted_iota(jnp.int32, (tm + HALO, 1), 0)
        for gi, win in enumerate(POOL_WINDOWS):
            s = s + _shift_down(s, win // 2)
            cols = slice(gi * g, (gi + 1) * g)
            cnt = jnp.minimum(tpos + 1, win).astype(F32)
            pooled = (s[HALO:, cols] / cnt - u_ref[:, cols]).astype(BF16)
            wg = w_ref[gi]
            pm = _dot(pooled, wg)
            ds_ref[:, cols] += jnp.sum(dm_ref[:, cols] * pm, axis=0, keepdims=True)
            dpm = (dmc[:, cols] * s_ref[:, cols]).astype(BF16)
            dw_ref[gi] += _dot(pooled, dpm[:tm], TN)
            dpool = _dot(dpm, wg, NT)
            f = dpool / jnp.minimum(tpos_x + 1, win).astype(F32)
            step = 1
            while step < win:
                f = f + _shift_up(f, step)
                step *= 2
            du_ref[:, cols] = f[:tm] - dpool[:tm]

    row = pl.BlockSpec((tm, dp), lambda i: (i, 0))
    return pl.pallas_call(
        body, name="pool_bwd", grid=grid,
        out_shape=(jax.ShapeDtypeStruct((m, dp), F32), jax.ShapeDtypeStruct((ng, g, g), F32),
                   jax.ShapeDtypeStruct((1, dp), F32)),
        in_specs=[row, prev, row, nxt, pl.BlockSpec((None, ng, g, g), lambda i: (layer, 0, 0, 0)),
                  pl.BlockSpec((1, dp), lambda i: (0, 0))],
        out_specs=(row, pl.BlockSpec((ng, g, g), lambda i: (0, 0, 0)), pl.BlockSpec((1, dp), lambda i: (0, 0))),
        compiler_params=_params(grid))(u, u, dmix, dmix, wp, scale)


def _conv_taps(u, w_ref, rows):
    y = None
    for k in range(CONV_WIDTH):
        term = _shift_down(u, CONV_WIDTH - 1 - k)[HALO:HALO + rows] * w_ref[k:k + 1, :]
        y = term if y is None else y + term
    return y


def _layernorm_stats(y):
    mu = jnp.mean(y, axis=-1, keepdims=True)
    yc = y - mu
    rstd = lax.rsqrt(jnp.mean(yc * yc, axis=-1, keepdims=True) + EPS)
    return yc * rstd, rstd


def _conv_fwd(uc, wdw, b, lg, lb):
    m, c2 = uc.shape
    c = c2 // 2
    tm = _row_tile(m)
    nt = m // tm
    grid = (nt,)
    prev, _ = _halo_specs(tm, c2, nt)

    def body(x_ref, h_ref, w_ref, b_ref, g_ref, bb_ref, o_ref):
        i = pl.program_id(0)
        halo = jnp.where(i > 0, h_ref[...], 0.0)
        xc = jnp.concatenate([halo, x_ref[...]], axis=0)
        u = xc[:, :c] * jax.nn.sigmoid(xc[:, c:])
        y = _conv_taps(u, w_ref, tm) + b_ref[...]
        xhat, _ = _layernorm_stats(y)
        ln = xhat * g_ref[...] + bb_ref[...]
        o_ref[...] = (ln * jax.nn.sigmoid(ln)).astype(BF16)

    vec = pl.BlockSpec((1, c), lambda i: (0, 0))
    return pl.pallas_call(
        body, name="conv_fwd", grid=grid, out_shape=jax.ShapeDtypeStruct((m, c), BF16),
        in_specs=[pl.BlockSpec((tm, c2), lambda i: (i, 0)), prev, pl.BlockSpec((HALO, c), lambda i: (0, 0)), vec, vec, vec],
        out_specs=pl.BlockSpec((tm, c), lambda i: (i, 0)), compiler_params=_params(grid))(uc, uc, wdw, b, lg, lb)


def _conv_bwd(uc, ds, wdw, b, lg, lb):
    m, c2 = uc.shape
    c = c2 // 2
    tm = _row_tile(m)
    nt = m // tm
    grid = (nt,)
    prev, nxt = _halo_specs(tm, c2, nt)
    _, nxt_c = _halo_specs(tm, c, nt)

    def body(x_ref, hp_ref, hn_ref, ds_ref, dsn_ref, w_ref, b_ref, g_ref, bb_ref,
             dx_ref, dw_ref, db_ref, dg_ref, dbb_ref):
        i = pl.program_id(0)

        @pl.when(i == 0)
        def _():
            for ref in (dw_ref, db_ref, dg_ref, dbb_ref):
                ref[...] = jnp.zeros_like(ref)

        hp = jnp.where(i > 0, hp_ref[...], 0.0)
        xc = jnp.concatenate([hp, x_ref[...], hn_ref[...]], axis=0)
        sg = jax.nn.sigmoid(xc[:, c:])
        u = xc[:, :c] * sg
        rows = tm + HALO
        y = _conv_taps(u, w_ref, rows) + b_ref[...]
        xhat, rstd = _layernorm_stats(y)
        ln = xhat * g_ref[...] + bb_ref[...]
        sl = jax.nn.sigmoid(ln)
        dsn = jnp.where(i < nt - 1, dsn_ref[...], 0.0)
        dsx = jnp.concatenate([ds_ref[...], dsn], axis=0)
        dln = dsx * sl * (1.0 + ln * (1.0 - sl))
        dxh = dln * g_ref[...]
        dy = rstd * (dxh - jnp.mean(dxh, axis=-1, keepdims=True)
                     - xhat * jnp.mean(dxh * xhat, axis=-1, keepdims=True))
        dyt = dy[:tm]
        dg_ref[...] += jnp.sum(dln[:tm] * xhat[:tm], axis=0, keepdims=True)
        dbb_ref[...] += jnp.sum(dln[:tm], axis=0, keepdims=True)
        db_ref[...] += jnp.sum(dyt, axis=0, keepdims=True)
        du = None
        for k in range(CONV_WIDTH):
            lag = CONV_WIDTH - 1 - k
            dw_ref[k:k + 1, :] += jnp.sum(dyt * _shift_down(u, lag)[HALO:HALO + tm], axis=0, keepdims=True)
            term = _shift_up(dy, lag)[:tm] * w_ref[k:k + 1, :]
            du = term if du is None else du + term
        a_t = xc[HALO:HALO + tm, :c]
        sg_t = sg[HALO:HALO + tm]
        dx_ref[:, :c] = du * sg_t
        dx_ref[:, c:] = du * a_t * sg_t * (1.0 - sg_t)

    vec = pl.BlockSpec((1, c), lambda i: (0, 0))
    wsp = pl.BlockSpec((HALO, c), lambda i: (0, 0))
    vshape = jax.ShapeDtypeStruct((1, c), F32)
    return pl.pallas_call(
        body, name="conv_bwd", grid=grid,
        out_shape=(jax.ShapeDtypeStruct((m, c2), F32), jax.ShapeDtypeStruct((HALO, c), F32), vshape, vshape, vshape),
        in_specs=[pl.BlockSpec((tm, c2), lambda i: (i, 0)), prev, nxt, pl.BlockSpec((tm, c), lambda i: (i, 0)), nxt_c,
                  wsp, vec, vec, vec],
        out_specs=(pl.BlockSpec((tm, c2), lambda i: (i, 0)), wsp, vec, vec, vec),
        compiler_params=_params(grid))(uc, uc, uc, ds, ds, wdw, b, lg, lb)


def _mesh_pos():
    return lax.axis_index("x"), lax.axis_index("y"), lax.axis_index("c")


def _other_chips(x, y):
    return [(1 - x, y), (x, 1 - y), (1 - x, 1 - y)]


def _comm_sems(n, per):
    return [pltpu.SemaphoreType.DMA((n, per)), pltpu.SemaphoreType.DMA((n, per)), pltpu.SemaphoreType.DMA((n,))]


def _comm_call(body, name, arrs, out_shapes, n_sems):
    n = len(arrs)
    return pl.pallas_call(
        body, name=name, out_shape=out_shapes, in_specs=[ANY] * n, out_specs=tuple([ANY] * len(out_shapes)),
        scratch_shapes=_comm_sems(n, n_sems),
    )(*arrs)


class _GatherPlan:
    def __init__(self, ins, outs, send_sems, recv_sems, local_sems):
        self.ins, self.outs = ins, outs
        self.send_sems, self.recv_sems, self.local_sems = send_sems, recv_sems, local_sems
        x, y, c = _mesh_pos()
        self.c, self.me, self.sib = c, (x, y, c), (x, y, 1 - c)
        self.chips = _other_chips(x, y)

    def _slot(self, a, p):
        return self.outs[a].at[4 * p[0] + 2 * p[1] + p[2]]

    def _copy(self, a, k, block, to, own=False):
        return pltpu.make_async_remote_copy(
            src_ref=self.ins[a] if own else self._slot(a, block), dst_ref=self._slot(a, block),
            send_sem=self.send_sems.at[a, k], recv_sem=self.recv_sems.at[a, k], device_id=to, device_id_type=MESH)

    def _local(self, a):
        return pltpu.make_async_copy(self.ins[a], self._slot(a, self.me), self.local_sems.at[a])

    def start(self):
        for a in range(len(self.ins)):
            for j, chip in enumerate(self.chips):
                self._copy(a, 1 + j, self.me, (*chip, self.c), own=True).start()
            self._copy(a, 0, self.me, self.sib, own=True).start()
            self._local(a).start()

    def forward(self, a):
        for j, chip in enumerate(self.chips):
            self._copy(a, 1 + j, (*chip, self.c), self.me).wait_recv()
            self._copy(a, 4 + j, (*chip, self.c), self.sib).start()

    def finish(self):
        for a in range(len(self.ins)):
            self._copy(a, 0, self.sib, self.me).wait_recv()
            for j, chip in enumerate(self.chips):
                self._copy(a, 4 + j, (*chip, 1 - self.c), self.me).wait_recv()
        for a in range(len(self.ins)):
            self._copy(a, 0, self.me, self.sib, own=True).wait_send()
            for j, chip in enumerate(self.chips):
                self._copy(a, 1 + j, self.me, (*chip, self.c), own=True).wait_send()
                self._copy(a, 4 + j, (*chip, self.c), self.sib).wait_send()
            self._local(a).wait()


def _gather_shapes(arrs):
    return tuple(jax.ShapeDtypeStruct((N_DEV,) + a.shape, a.dtype) for a in arrs)


def _all_gather(arrs, name):
    n = len(arrs)

    def body(*refs):
        plan = _GatherPlan(refs[:n], refs[n:2 * n], *refs[2 * n:])
        plan.start()
        for a in range(n):
            plan.forward(a)
        plan.finish()

    return _comm_call(body, name, arrs, _gather_shapes(arrs), 7)


class _ChipExchangePlan:
    def __init__(self, ins, outs, send_sems, recv_sems, local_sems):
        self.ins, self.outs = ins, outs
        self.send_sems, self.recv_sems, self.local_sems = send_sems, recv_sems, local_sems
        x, y, c = _mesh_pos()
        self.pos, self.mine = (x, y, c), 2 * x + y
        self.chips = _other_chips(x, y)

    def _send(self, a, j):
        chip = self.chips[j]
        return pltpu.make_async_remote_copy(
            src_ref=self.ins[a].at[2 * chip[0] + chip[1]], dst_ref=self.outs[a].at[self.mine],
            send_sem=self.send_sems.at[a, j], recv_sem=self.recv_sems.at[a, j], device_id=(*chip, self.pos[2]),
            device_id_type=MESH)

    def _landing(self, a, j):
        chip = self.chips[j]
        ref = self.outs[a].at[2 * chip[0] + chip[1]]
        return pltpu.make_async_remote_copy(src_ref=ref, dst_ref=ref, send_sem=self.send_sems.at[a, j],
                                            recv_sem=self.recv_sems.at[a, j], device_id=self.pos, device_id_type=MESH)

    def _local(self, a):
        return pltpu.make_async_copy(self.ins[a].at[self.mine], self.outs[a].at[self.mine], self.local_sems.at[a])

    def start(self):
        for a in range(len(self.ins)):
            for j in range(3):
                self._send(a, j).start()
            self._local(a).start()

    def finish(self):
        for a in range(len(self.ins)):
            for j in range(3):
                self._landing(a, j).wait_recv()
        for a in range(len(self.ins)):
            for j in range(3):
                self._send(a, j).wait_send()
            self._local(a).wait()


def _pair_tile(kk):
    return _pick(kk, (1024, 704, 512, 256, 128, 64))


def _pair_exchange(parts, core, name):
    _, kk, n = parts.shape
    tr = _pair_tile(kk)
    nr = kk // tr
    grid = (4, nr)

    def body(c_ref, src_ref, recv_ref, send_sems, recv_sems):
        q, r = pl.program_id(0), pl.program_id(1)
        x, y, c = _mesh_pos()
        step = q * nr + r
        slot = step % 2
        dst = recv_ref.at[pl.ds(pl.multiple_of(step * tr, tr), tr), :]
        cp = pltpu.make_async_remote_copy(src_ref=src_ref, dst_ref=dst, send_sem=send_sems.at[slot],
                                          recv_sem=recv_sems.at[slot], device_id=(x, y, 1 - c), device_id_type=MESH)
        cp.start()
        cp.wait_send()
        cp.wait_recv()

    spec = pltpu.PrefetchScalarGridSpec(
        num_scalar_prefetch=1, grid=grid,
        in_specs=[pl.BlockSpec((tr, n), lambda q, r, c: ((2 * q + 1 - c[0]) * nr + r, 0))],
        out_specs=ANY, scratch_shapes=[pltpu.SemaphoreType.DMA((2,)), pltpu.SemaphoreType.DMA((2,))])
    recv = pl.pallas_call(body, name=name, grid_spec=spec, out_shape=jax.ShapeDtypeStruct((4 * kk, n), parts.dtype),
                          compiler_params=_params(grid))(core, parts.reshape(N_DEV * kk, n))
    return recv.reshape(4, kk, n)


def _exchange_shapes(parts):
    return tuple(jax.ShapeDtypeStruct(p.shape, p.dtype) for p in parts)


def _reduce_chip_exchange(parts, name):
    n = len(parts)

    def body(*refs):
        plan = _ChipExchangePlan(refs[:n], refs[n:2 * n], *refs[2 * n:])
        plan.start()
        plan.finish()

    return _comm_call(body, name, parts, _exchange_shapes(parts), 3)


def _pair_add(parts, recv, core, name):
    _, kk, n = recv.shape
    tr = _pair_tile(kk)
    grid = (4, kk // tr)

    def body(c_ref, a_ref, b_ref, o_ref):
        o_ref[...] = (a_ref[...].astype(F32) + b_ref[...].astype(F32)).astype(o_ref.dtype)

    out = pl.BlockSpec((None, tr, n), lambda q, r, c: (q, r, 0))
    spec = pltpu.PrefetchScalarGridSpec(
        num_scalar_prefetch=1, grid=grid,
        in_specs=[pl.BlockSpec((None, None, tr, n), lambda q, r, c: (q, c[0], r, 0)), out], out_specs=out)
    return pl.pallas_call(body, name=name, grid_spec=spec, out_shape=jax.ShapeDtypeStruct(recv.shape, recv.dtype),
                          compiler_params=_params(grid))(core, parts.reshape(4, 2, kk, n), recv)


def _adamw_math(w, g, m, v):
    m2 = ADAM_B1 * m + (1.0 - ADAM_B1) * g
    v2 = ADAM_B2 * v + (1.0 - ADAM_B2) * (g * g)
    m_hat = m2 / (1.0 - ADAM_B1 ** ADAM_STEP)
    v_hat = v2 / (1.0 - ADAM_B2 ** ADAM_STEP)
    delta = -ADAM_LR * (m_hat / (jnp.sqrt(v_hat) + ADAM_EPS) + ADAM_WD * w)
    return delta, m2, v2


def _adamw_sharded(parts, w, m, v, name):
    nl, kk, n = w.shape
    tr = _pick(kk, (256, 128, 64)) if n <= 1024 else _pick(kk, (64, 32))
    grid = (nl, kk // tr)

    def body(*refs):
        p_refs = refs[:nl]
        w_ref, m_ref, v_ref, g_ref, d_ref, m2_ref, v2_ref = refs[nl:]
        layer = pl.program_id(0)
        g = None
        for l, p_ref in enumerate(p_refs):
            s = p_ref[0].astype(F32)
            for q in range(1, 4):
                s = s + p_ref[q].astype(F32)
            g = s if g is None else jnp.where(layer == l, s, g)
        d, m2, v2 = _adamw_math(w_ref[...], g, m_ref[...], v_ref[...])
        g_ref[...] = g
        d_ref[...] = d
        m2_ref[...] = m2
        v2_ref[...] = v2

    spec = pl.BlockSpec((None, tr, n), lambda l, r: (l, r, 0))
    shp = jax.ShapeDtypeStruct(w.shape, F32)
    p_specs = [pl.BlockSpec((4, tr, n), lambda l, r, k=k: (0, jnp.where(l == k, r, 0), 0)) for k in range(nl)]
    return pl.pallas_call(
        body, name=name, grid=grid, out_shape=(shp, shp, shp, shp), in_specs=p_specs + [spec, spec, spec],
        out_specs=(spec, spec, spec, spec), compiler_params=_params(grid))(*parts, w, m, v)


def _sum_slots(g8):
    _, r, _ = g8.shape

    def body(g_ref, o_ref):
        s = g_ref[0]
        for d in range(1, N_DEV):
            s = s + g_ref[d]
        o_ref[...] = s

    return pl.pallas_call(body, name="sum_slots", out_shape=jax.ShapeDtypeStruct((r, 128), F32))(g8)


def _adamw_flat(w, g, m, v):
    def body(w_ref, g_ref, m_ref, v_ref, d_ref, m2_ref, v2_ref):
        d, m2, v2 = _adamw_math(w_ref[...], g_ref[...], m_ref[...], v_ref[...])
        d_ref[...] = d
        m2_ref[...] = m2
        v2_ref[...] = v2

    shp = jax.ShapeDtypeStruct(w.shape, F32)
    return pl.pallas_call(body, name="adamw_flat", out_shape=(shp, shp, shp))(w, g, m, v)


def _pack(arrs):
    flat = jnp.concatenate([a.reshape(-1) for a in arrs])
    pad = (-flat.shape[0]) % 1024
    return jnp.pad(flat, (0, pad)).reshape(-1, 128)


def _unpack(buf, like):
    flat = buf.reshape(-1)
    out, off = [], 0
    for a in like:
        out.append(flat[off:off + a.size].reshape(a.shape))
        off += a.size
    return out


def _heads(x2d, scale=None):
    lp, da = x2d.shape
    xh = x2d.reshape(lp, N_HEADS, da // N_HEADS).transpose(1, 0, 2)
    if scale is not None:
        xh = xh * scale
    return xh


def _unheads(xh):
    nh, lp, dh = xh.shape
    return xh.transpose(1, 0, 2).reshape(lp, nh * dh)


def _layer_fwd(h, l, W, P, gather_after_proj=None):
    d = h.shape[1]
    da, dp = d // 2, d // 4
    dh = da // N_HEADS
    sv = {"h": h}
    u1 = _rms_fwd(h, P["pre_mix_g"][l:l + 1], "rms_pre_mix")
    proj_s = _mm_cols(u1, *W["in"], "mm_in")
    lp = h.shape[0]
    proj = proj_s.transpose(1, 0, 2).reshape(lp, -1)
    qh = _heads(proj[:, :da], dh ** -0.5).astype(BF16)
    kh = _heads(proj[:, da:2 * da]).astype(BF16)
    vh = _heads(proj[:, 2 * da:3 * da]).astype(BF16)
    u_pool = proj[:, 3 * da:3 * da + dp]
    u_conv = proj[:, 3 * da + dp:]
    if gather_after_proj is None:
        oh, tot, cnt = _attn_fwd(qh, kh, vh)
    else:
        arrs, complete = gather_after_proj
        oh, tot, cnt, *gathered = _attn_fwd(qh, kh, vh, arrs)
        complete(gathered)
    o_attn = _unheads(oh)
    o_pool = _pool_fwd(u_pool, W["pool"], P["pool_scale"][l:l + 1], l)
    s_conv = _conv_fwd(u_conv, W["dw"][l], P["b_dw"][l:l + 1], P["conv_ln_g"][l:l + 1], P["conv_ln_b"][l:l + 1])
    o_conv = _mm_nn(s_conv, W["pw"][0], "mm_pw", layer=W["pw"][1])
    merged = _merge_fwd(o_attn, o_pool, o_conv, P["mix_out_g"][l:l + 1])
    z1 = _mm_nn(merged, W["out"][0], "mm_out", layer=W["out"][1])
    h1 = _resid_rms_fwd(h, z1, P["post_mix_g"][l:l + 1], "resid_post_mix")
    u2 = _rms_fwd(h1, P["pre_ffn_g"][l:l + 1], "rms_pre_ffn")
    a_s, b_s, act_s = _ffn_up(u2, W["gate"][0], W["up"][0], W["gate"][1])
    ff = _mm_rows(act_s, *W["down"], "mm_down")
    h2 = _resid_rms_fwd(h1, ff, P["post_ffn_g"][l:l + 1], "resid_post_ffn")
    sv.update(u1=u1, qh=qh, kh=kh, vh=vh, tot=tot, cnt=cnt, u_pool=u_pool, u_conv=u_conv, o_attn=o_attn, o_pool=o_pool,
              o_conv=o_conv, s_conv=s_conv, merged=merged, z1=z1, h1=h1, u2=u2, a_s=a_s, b_s=b_s, act_s=act_s, ff=ff)
    return h2, sv


EARLY = ("w_down", "w_gate", "w_up", "w_out", "w_pw")


def _layer_bwd(dh_out, l, W, P, sv, exchange_with_attn=None):
    d = dh_out.shape[1]
    da = d // 2
    dhd = da // N_HEADS
    lp = dh_out.shape[0]
    g = {}
    dff, g["post_ffn_g"] = _rms_bwd(sv["ff"], P["post_ffn_g"][l:l + 1], dh_out, None, BF16, "rms_bwd_post_ffn")
    da_s, db_s = _ffn_bwd_act(dff, *W["down"], sv["a_s"], sv["b_s"])
    g["w_down"] = _mm_tn_rows(sv["act_s"], dff, "mm_dw_down")
    du2 = _mm_cols_t([da_s, db_s], [W["gate"][0], W["up"][0]], W["gate"][1], "mm_d_u2")
    g["w_gate"] = _mm_tn_cols(sv["u2"], da_s, "mm_dw_gate")
    g["w_up"] = _mm_tn_cols(sv["u2"], db_s, "mm_dw_up")
    dh1, g["pre_ffn_g"] = _rms_bwd(sv["h1"], P["pre_ffn_g"][l:l + 1], du2, dh_out, F32, "rms_bwd_pre_ffn")
    dz1, g["post_mix_g"] = _rms_bwd(sv["z1"], P["post_mix_g"][l:l + 1], dh1, None, BF16, "rms_bwd_post_mix")
    dmerged = _mm_nt(dz1, W["out"][0], "mm_d_merged", W["out"][1])
    g["w_out"] = _mm_tn(sv["merged"], dz1, "mm_dw_out")
    d_oa, d_op, d_oc, g["mix_out_g"] = _merge_bwd(sv["o_attn"], sv["o_pool"], sv["o_conv"], P["mix_out_g"][l:l + 1], dmerged)
    d_s = _mm_nt(d_oc, W["pw"][0], "mm_d_sconv", W["pw"][1])
    g["w_pw"] = _mm_tn(sv["s_conv"], d_oc, "mm_dw_pw")
    d_uc, dwdw, g["b_dw"], g["conv_ln_g"], g["conv_ln_b"] = _conv_bwd(
        sv["u_conv"], d_s, W["dw"][l], P["b_dw"][l:l + 1], P["conv_ln_g"][l:l + 1], P["conv_ln_b"][l:l + 1])
    g["w_dw"] = dwdw[:CONV_WIDTH]
    d_up, g["w_pool"], g["pool_scale"] = _pool_bwd(sv["u_pool"], d_op, W["pool"], P["pool_scale"][l:l + 1], l)
    doh = _heads(d_oa).astype(BF16)
    riders = exchange_with_attn(g) if exchange_with_attn is not None else []
    dqh, dkt, dvt, *exchanged = _attn_bwd(sv["qh"], sv["qh"].transpose(0, 2, 1), sv["kh"], sv["vh"], doh,
                                          doh.transpose(0, 2, 1), sv["tot"], sv["cnt"], dhd ** -0.5, riders)
    dk2, dv2 = (a.transpose(1, 3, 0, 2).reshape(lp, da) for a in (dkt, dvt))
    dproj = jnp.concatenate([_unheads(dqh), dk2, dv2, d_up, d_uc], axis=1).astype(BF16)
    dproj_s = dproj.reshape(lp, N_DEV, -1).transpose(1, 0, 2)
    du1 = _mm_cols_t([dproj_s], [W["in"][0]], W["in"][1], "mm_d_u1")
    g["w_in"] = _mm_tn_cols(sv["u1"], dproj_s, "mm_dw_in")
    dh0, g["pre_mix_g"] = _rms_bwd(sv["h"], P["pre_mix_g"][l:l + 1], du1, dh1, F32, "rms_bwd_pre_mix")
    return dh0, g, exchanged


SHARDED = ("w_in", "w_gate", "w_up", "w_down", "w_out", "w_pw")
SMALL = ("pre_mix_g", "w_pool", "pool_scale", "b_dw", "conv_ln_g", "conv_ln_b", "mix_out_g", "post_mix_g",
         "pre_ffn_g", "post_ffn_g", "w_dw", "meta_tokens")
ORDER = ("meta_tokens", "pre_mix_g", "w_in", "w_pool", "pool_scale", "w_dw", "b_dw", "conv_ln_g", "conv_ln_b", "w_pw",
         "mix_out_g", "w_out", "post_mix_g", "pre_ffn_g", "w_gate", "w_up", "w_down", "post_ffn_g")


def kernel(x, meta_tokens, pre_mix_g, w_in, w_pool, pool_scale, w_dw, b_dw, conv_ln_g, conv_ln_b, w_pw, mix_out_g, w_out, post_mix_g, pre_ffn_g, w_gate, w_up, w_down, post_ffn_g, loss_target, m_meta_tokens, m_pre_mix_g, m_w_in, m_w_pool, m_pool_scale, m_w_dw, m_b_dw, m_conv_ln_g, m_conv_ln_b, m_w_pw, m_mix_out_g, m_w_out, m_post_mix_g, m_pre_ffn_g, m_w_gate, m_w_up, m_w_down, m_post_ffn_g, v_meta_tokens, v_pre_mix_g, v_w_in, v_w_pool, v_pool_scale, v_w_dw, v_b_dw, v_conv_ln_g, v_conv_ln_b, v_w_pw, v_mix_out_g, v_w_out, v_post_mix_g, v_pre_ffn_g, v_w_gate, v_w_up, v_w_down, v_post_ffn_g):
    P = dict(meta_tokens=meta_tokens, pre_mix_g=pre_mix_g, w_in=w_in, w_pool=w_pool, pool_scale=pool_scale, w_dw=w_dw,
             b_dw=b_dw, conv_ln_g=conv_ln_g, conv_ln_b=conv_ln_b, w_pw=w_pw, mix_out_g=mix_out_g, w_out=w_out,
             post_mix_g=post_mix_g, pre_ffn_g=pre_ffn_g, w_gate=w_gate, w_up=w_up, w_down=w_down, post_ffn_g=post_ffn_g)
    M = dict(meta_tokens=m_meta_tokens, pre_mix_g=m_pre_mix_g, w_in=m_w_in, w_pool=m_w_pool, pool_scale=m_pool_scale,
             w_dw=m_w_dw, b_dw=m_b_dw, conv_ln_g=m_conv_ln_g, conv_ln_b=m_conv_ln_b, w_pw=m_w_pw, mix_out_g=m_mix_out_g,
             w_out=m_w_out, post_mix_g=m_post_mix_g, pre_ffn_g=m_pre_ffn_g, w_gate=m_w_gate, w_up=m_w_up,
             w_down=m_w_down, post_ffn_g=m_post_ffn_g)
    V = dict(meta_tokens=v_meta_tokens, pre_mix_g=v_pre_mix_g, w_in=v_w_in, w_pool=v_w_pool, pool_scale=v_pool_scale,
             w_dw=v_w_dw, b_dw=v_b_dw, conv_ln_g=v_conv_ln_g, conv_ln_b=v_conv_ln_b, w_pw=v_w_pw, mix_out_g=v_mix_out_g,
             w_out=v_w_out, post_mix_g=v_post_mix_g, pre_ffn_g=v_pre_ffn_g, w_gate=v_w_gate, w_up=v_w_up,
             w_down=v_w_down, post_ffn_g=v_post_ffn_g)
    xi, yi, ci = _mesh_pos()
    dev = 4 * xi + 2 * yi + ci
    n_tok, d = x.shape[1], x.shape[2]
    n_meta = meta_tokens.shape[0]
    n_layers = w_in.shape[0]
    c = d // 4
    l_real = n_meta + n_tok
    lp = -(-l_real // ATT_BLOCK) * ATT_BLOCK

    g_in0, g_dw, g_meta = _all_gather([w_in[:1].astype(BF16), w_dw, meta_tokens], "gather_first")
    wdw_full = g_dw.transpose(1, 2, 0, 3).reshape(n_layers, CONV_WIDTH, c)
    common = {"pool": w_pool.astype(BF16), "dw": jnp.pad(wdw_full, ((0, 0), (0, HALO - CONV_WIDTH), (0, 0)))}
    W = [dict(common) for _ in range(n_layers)]
    W[0]["in"] = (g_in0, 0)
    meta_full = g_meta.transpose(1, 0, 2).reshape(n_meta, d)
    later = [w_pw.astype(BF16), w_out.astype(BF16), w_in[1:].astype(BF16), w_gate.astype(BF16), w_up.astype(BF16),
             w_down.astype(BF16)]

    def complete(gathered):
        g_pw, g_out, g_in, g_gate, g_up, g_down = gathered
        full_out = g_out.transpose(1, 0, 2, 3).reshape(n_layers, d, d)
        full_pw = g_pw.transpose(1, 0, 2, 3).reshape(n_layers, c, c)
        for l in range(n_layers):
            W[l].update(out=(full_out, l), pw=(full_pw, l), gate=(g_gate, l), up=(g_up, l), down=(g_down, l))
            if l > 0:
                W[l]["in"] = (g_in, l - 1)

    h = jnp.concatenate([meta_full, x[0], jnp.zeros((lp - l_real, d), F32)], axis=0)
    saved = []
    for l in range(n_layers):
        h, sv = _layer_fwd(h, l, W[l], P, (later, complete) if l == 0 else None)
        saved.append(sv)
    target = jnp.pad(loss_target[0], ((n_meta, lp - l_real), (0, 0)))
    loss_part, dh = _loss_grad(h, target, n_meta, n_tok)
    loss = lax.psum(loss_part[0, 0], ("x", "y", "c"))

    core = jnp.reshape(ci, (1,)).astype(jnp.int32)

    def pair_reduce(g, keys, l):
        res = []
        for k in keys:
            p = g[k] if g[k].ndim == 3 else g[k].reshape(N_DEV, g[k].shape[0] // N_DEV, g[k].shape[1])
            recv = _pair_exchange(p, core, f"pair_exchange_{k}_{l}")
            res.append(_pair_add(p, recv, core, f"pair_add_{k}_{l}"))
        return res

    grads = [None] * n_layers
    riders = []
    pair_sums = []
    for l in reversed(range(1, n_layers)):
        dh, grads[l], _ = _layer_bwd(dh, l, W[l], P, saved[l])
        riders += [(l, k) for k in SHARDED]
        pair_sums += pair_reduce(grads[l], SHARDED, l)

    def ride(g):
        riders.extend((0, k) for k in EARLY)
        return pair_sums + pair_reduce(g, EARLY, 0)

    dh, grads[0], exchanged = _layer_bwd(dh, 0, W[0], P, saved[0], ride)
    by_chip = dict(zip(riders, exchanged))
    (by_chip[(0, "w_in")],) = _reduce_chip_exchange(pair_reduce(grads[0], ("w_in",), 0), "reduce_chips_w_in")
    grad_x = dh[n_meta:l_real][None]
    out = {}
    for k in SHARDED:
        out[k] = _adamw_sharded([by_chip[(l, k)] for l in range(n_layers)], P[k], M[k], V[k], "adamw_" + k)

    small_parts = []
    for k in SMALL:
        if k == "meta_tokens":
            small_parts.append(dh[:n_meta])
        else:
            small_parts.append(jnp.stack([grads[l][k].reshape(P[k].shape[1:] if k != "w_dw" else (CONV_WIDTH, c))
                                          for l in range(n_layers)], axis=0))
    (g8,) = _all_gather([_pack(small_parts)], "gather_small_grads")
    g_small = dict(zip(SMALL, _unpack(_sum_slots(g8), small_parts)))
    g_small["w_dw"] = lax.dynamic_slice_in_dim(g_small["w_dw"], dev * w_dw.shape[2], w_dw.shape[2], axis=2)
    g_small["meta_tokens"] = lax.dynamic_slice_in_dim(g_small["meta_tokens"], dev * meta_tokens.shape[1],
                                                      meta_tokens.shape[1], axis=1)
    like = [P[k] for k in SMALL]
    res = _adamw_flat(_pack(like), _pack([g_small[k] for k in SMALL]), _pack([M[k] for k in SMALL]),
                      _pack([V[k] for k in SMALL]))
    res = [_unpack(r, like) for r in res]
    for i, k in enumerate(SMALL):
        out[k] = (g_small[k], res[0][i], res[1][i], res[2][i])

    return (loss, grad_x, *[out[k][0] for k in ORDER], *[out[k][1] for k in ORDER],
            *[out[k][2] for k in ORDER], *[out[k][3] for k in ORDER])
```

```python
import functools

import jax
import jax.numpy as jnp
from jax import lax
from jax.experimental import pallas as pl
from jax.experimental.pallas import tpu as pltpu

F32 = jnp.float32
BF16 = jnp.bfloat16
EPS = 1e-6
N_HEADS = 16
POOL_WINDOWS = (2, 4, 8, 16)
CONV_WIDTH = 31
HALO = 32
ATT_BLOCK = 128
DEAD_LOG_WEIGHT = 110.0
N_DEV = 8
VMEM_LIMIT = 60 * 1024 * 1024

ADAM_LR = 0.001
ADAM_B1 = 0.9
ADAM_B2 = 0.999
ADAM_EPS = 1e-08
ADAM_WD = 0.01
ADAM_STEP = 10

NN = (((1,), (0,)), ((), ()))
NT = (((1,), (1,)), ((), ()))
TN = (((0,), (0,)), ((), ()))
MESH = pl.DeviceIdType.MESH
ANY = pl.BlockSpec(memory_space=pl.ANY)


def _pick(n, cands):
    for c in cands:
        if c <= n and n % c == 0:
            return c
    return n


def _params(grid):
    return pltpu.CompilerParams(dimension_semantics=("arbitrary",) * len(grid), vmem_limit_bytes=VMEM_LIMIT)


def _dot(a, b, dims=NN):
    return lax.dot_general(a, b, dims, preferred_element_type=F32)


def _matmul(name, pairs, specs, out_shape, out_spec, grid, dims, acc_shape):
    n = len(pairs)
    nk = grid[-1]
    kaxis = len(grid) - 1

    def body(*refs):
        o_ref, acc = refs[2 * n], refs[2 * n + 1]
        tot = None
        for p in range(n):
            d = _dot(refs[2 * p][...], refs[2 * p + 1][...], dims)
            tot = d if tot is None else tot + d
        if nk == 1:
            o_ref[...] = tot.astype(o_ref.dtype)
            return
        k = pl.program_id(kaxis)

        @pl.when(k == 0)
        def _():
            acc[...] = tot

        @pl.when(k > 0)
        def _():
            acc[...] += tot

        @pl.when(k == nk - 1)
        def _():
            o_ref[...] = acc[...].astype(o_ref.dtype)

    ops, in_specs = [], []
    for (a, b), (sa, sb) in zip(pairs, specs):
        ops += [a, b]
        in_specs += [sa, sb]
    return pl.pallas_call(
        body, name=name, out_shape=out_shape, grid=grid, in_specs=in_specs, out_specs=out_spec,
        scratch_shapes=[pltpu.VMEM(acc_shape if nk > 1 else (8, 128), F32)], compiler_params=_params(grid),
    )(*ops)


ROW_TILES = (1408, 1024, 512, 256, 128)


def _mm_nt(a, w, name, out_dtype=F32):
    m, kk = a.shape
    n = w.shape[0]
    tm = _pick(m, ROW_TILES)
    tn = _pick(n, (1024, 768, 512, 256, 128))
    grid = (m // tm, n // tn, 1)
    sa = pl.BlockSpec((tm, kk), lambda i, j, k: (i, 0))
    sb = pl.BlockSpec((tn, kk), lambda i, j, k: (j, 0))
    return _matmul(name, [(a, w)], [(sa, sb)], jax.ShapeDtypeStruct((m, n), out_dtype),
                   pl.BlockSpec((tm, tn), lambda i, j, k: (i, j)), grid, NT, (tm, tn))


def _mm_nn(pairs, name, out_dtype=F32):
    m, kc = pairs[0][0].shape
    n = pairs[0][1].shape[1]
    tm = _pick(m, ROW_TILES)
    tn = _pick(n, (1024, 512, 256, 128))
    tk = _pick(kc, (2048, 1024, 768, 512, 256, 128))
    grid = (m // tm, n // tn, kc // tk)
    sa = pl.BlockSpec((tm, tk), lambda i, j, k: (i, k))
    sb = pl.BlockSpec((tk, tn), lambda i, j, k: (k, j))
    return _matmul(name, pairs, [(sa, sb)] * len(pairs), jax.ShapeDtypeStruct((m, n), out_dtype),
                   pl.BlockSpec((tm, tn), lambda i, j, k: (i, j)), grid, NN, (tm, tn))


def _mm_tn(a, b, name, out_dtype=BF16):
    l, m = a.shape
    n = b.shape[1]
    tl = _pick(l, ROW_TILES)
    tm = _pick(m, (1024, 768, 512, 256, 128))
    tn = _pick(n, (1024, 512, 256, 128))
    grid = (m // tm, n // tn, l // tl)
    sa = pl.BlockSpec((tl, tm), lambda i, j, t: (t, i))
    sb = pl.BlockSpec((tl, tn), lambda i, j, t: (t, j))
    return _matmul(name, [(a, b)], [(sa, sb)], jax.ShapeDtypeStruct((m, n), out_dtype),
                   pl.BlockSpec((tm, tn), lambda i, j, t: (i, j)), grid, TN, (tm, tn))


def _ffn_up(u, wg, wu):
    m, kk = u.shape
    n = wg.shape[0]
    tm = _pick(m, ROW_TILES)
    tn = _pick(n, (512, 256, 128))
    grid = (m // tm, n // tn)

    def body(u_ref, wg_ref, wu_ref, a_ref, b_ref, act_ref):
        uu = u_ref[...]
        a = _dot(uu, wg_ref[...], NT)
        b = _dot(uu, wu_ref[...], NT)
        a_ref[...] = a
        b_ref[...] = b
        act_ref[...] = (a * jax.nn.sigmoid(a) * b).astype(BF16)

    wspec = pl.BlockSpec((tn, kk), lambda i, j: (j, 0))
    ospec = pl.BlockSpec((tm, tn), lambda i, j: (i, j))
    return pl.pallas_call(
        body, name="ffn_up", grid=grid,
        out_shape=(jax.ShapeDtypeStruct((m, n), F32), jax.ShapeDtypeStruct((m, n), F32), jax.ShapeDtypeStruct((m, n), BF16)),
        in_specs=[pl.BlockSpec((tm, kk), lambda i, j: (i, 0)), wspec, wspec],
        out_specs=(ospec, ospec, ospec), compiler_params=_params(grid),
    )(u, wg, wu)


def _ffn_bwd_act(dff, wd, a, b):
    m, kk = dff.shape
    n = wd.shape[0]
    tm = _pick(m, ROW_TILES)
    tn = _pick(n, (512, 256, 128))
    grid = (m // tm, n // tn)

    def body(d_ref, w_ref, a_ref, b_ref, da_ref, db_ref):
        dact = _dot(d_ref[...], w_ref[...], NT)
        aa, bb = a_ref[...], b_ref[...]
        sg = jax.nn.sigmoid(aa)
        db_ref[...] = (dact * aa * sg).astype(BF16)
        da_ref[...] = (dact * bb * sg * (1.0 + aa * (1.0 - sg))).astype(BF16)

    tspec = pl.BlockSpec((tm, tn), lambda i, j: (i, j))
    return pl.pallas_call(
        body, name="ffn_bwd_act", grid=grid,
        out_shape=(jax.ShapeDtypeStruct((m, n), BF16), jax.ShapeDtypeStruct((m, n), BF16)),
        in_specs=[pl.BlockSpec((tm, kk), lambda i, j: (i, 0)), pl.BlockSpec((tn, kk), lambda i, j: (j, 0)), tspec, tspec],
        out_specs=(tspec, tspec), compiler_params=_params(grid),
    )(dff, wd, a, b)


def _rstd(x):
    return lax.rsqrt(jnp.mean(x * x, axis=-1, keepdims=True) + EPS)


def _row_tile(m):
    return _pick(m, (384, 256, 128))


def _rms_fwd(x, g, name):
    m, d = x.shape
    tr = _row_tile(m)
    grid = (m // tr,)

    def body(x_ref, g_ref, o_ref):
        xx = x_ref[...]
        o_ref[...] = (xx * _rstd(xx) * g_ref[...]).astype(BF16)

    row = pl.BlockSpec((tr, d), lambda i: (i, 0))
    return pl.pallas_call(body, name=name, grid=grid, out_shape=jax.ShapeDtypeStruct((m, d), BF16),
                          in_specs=[row, pl.BlockSpec((1, d), lambda i: (0, 0))], out_specs=row,
                          compiler_params=_params(grid))(x, g)


def _resid_rms_fwd(h, z, g, name):
    m, d = h.shape
    tr = _row_tile(m)
    grid = (m // tr,)

    def body(h_ref, z_ref, g_ref, o_ref):
        zz = z_ref[...]
        o_ref[...] = h_ref[...] + zz * _rstd(zz) * g_ref[...]

    row = pl.BlockSpec((tr, d), lambda i: (i, 0))
    return pl.pallas_call(body, name=name, grid=grid, out_shape=jax.ShapeDtypeStruct((m, d), F32),
                          in_specs=[row, row, pl.BlockSpec((1, d), lambda i: (0, 0))], out_specs=row,
                          compiler_params=_params(grid))(h, z, g)


def _rms_bwd_math(x, g, dy):
    r = _rstd(x)
    dyg = dy * g
    dx = r * dyg - x * (r * r * r) * jnp.mean(x * dyg, axis=-1, keepdims=True)
    dg = jnp.sum(dy * x * r, axis=0, keepdims=True)
    return dx, dg


def _rms_bwd(x, g, dy, dres, out_dtype, name):
    m, d = x.shape
    tr = _row_tile(m)
    grid = (m // tr,)
    has_res = dres is not None

    def body(*refs):
        x_ref, g_ref, dy_ref = refs[:3]
        dx_ref, dg_ref = refs[-2:]
        dx, dg = _rms_bwd_math(x_ref[...], g_ref[...], dy_ref[...].astype(F32))
        if has_res:
            dx = dx + refs[3][...]
        dx_ref[...] = dx.astype(out_dtype)

        @pl.when(pl.program_id(0) == 0)
        def _():
            dg_ref[...] = jnp.zeros_like(dg_ref)

        dg_ref[...] += dg

    row = pl.BlockSpec((tr, d), lambda i: (i, 0))
    vec = pl.BlockSpec((1, d), lambda i: (0, 0))
    ops = [x, g, dy] + ([dres] if has_res else [])
    return pl.pallas_call(
        body, name=name, grid=grid,
        out_shape=(jax.ShapeDtypeStruct((m, d), out_dtype), jax.ShapeDtypeStruct((1, d), F32)),
        in_specs=[row, vec, row] + ([row] if has_res else []), out_specs=(row, vec),
        compiler_params=_params(grid))(*ops)


def _merge_fwd(oa, op, oc, g):
    m = oa.shape[0]
    da, dp, dc = oa.shape[1], op.shape[1], oc.shape[1]
    d = da + dp + dc
    tr = _row_tile(m)
    grid = (m // tr,)

    def body(a_ref, p_ref, c_ref, g_ref, o_ref):
        off = 0
        for ref, w in ((a_ref, da), (p_ref, dp), (c_ref, dc)):
            xx = ref[...]
            o_ref[:, off:off + w] = (xx * _rstd(xx) * g_ref[:, off:off + w]).astype(BF16)
            off += w

    specs = [pl.BlockSpec((tr, w), lambda i: (i, 0)) for w in (da, dp, dc)]
    return pl.pallas_call(body, name="merge_fwd", grid=grid, out_shape=jax.ShapeDtypeStruct((m, d), BF16),
                          in_specs=specs + [pl.BlockSpec((1, d), lambda i: (0, 0))],
                          out_specs=pl.BlockSpec((tr, d), lambda i: (i, 0)), compiler_params=_params(grid))(oa, op, oc, g)


def _merge_bwd(oa, op, oc, g, dmerged):
    m = oa.shape[0]
    da, dp, dc = oa.shape[1], op.shape[1], oc.shape[1]
    d = da + dp + dc
    tr = _row_tile(m)
    grid = (m // tr,)

    def body(a_ref, p_ref, c_ref, g_ref, dm_ref, da_ref, dp_ref, dc_ref, dg_ref):
        @pl.when(pl.program_id(0) == 0)
        def _():
            dg_ref[...] = jnp.zeros_like(dg_ref)

        off = 0
        for ref, oref, w in ((a_ref, da_ref, da), (p_ref, dp_ref, dp), (c_ref, dc_ref, dc)):
            dx, dg = _rms_bwd_math(ref[...], g_ref[:, off:off + w], dm_ref[:, off:off + w])
            oref[...] = dx.astype(oref.dtype)
            dg_ref[:, off:off + w] += dg
            off += w

    specs = [pl.BlockSpec((tr, w), lambda i: (i, 0)) for w in (da, dp, dc)]
    vec = pl.BlockSpec((1, d), lambda i: (0, 0))
    return pl.pallas_call(
        body, name="merge_bwd", grid=grid,
        out_shape=(jax.ShapeDtypeStruct((m, da), F32), jax.ShapeDtypeStruct((m, dp), F32),
                   jax.ShapeDtypeStruct((m, dc), BF16), jax.ShapeDtypeStruct((1, d), F32)),
        in_specs=specs + [vec, pl.BlockSpec((tr, d), lambda i: (i, 0))], out_specs=tuple(specs) + (vec,),
        compiler_params=_params(grid))(oa, op, oc, g, dmerged)


def _loss_grad(h, target, n_meta, n_tok):
    m, d = h.shape
    tr = _row_tile(m)
    grid = (m // tr,)

    def body(h_ref, t_ref, loss_ref, dh_ref):
        i = pl.program_id(0)
        row = i * tr + lax.broadcasted_iota(jnp.int32, (tr, 1), 0)
        live = jnp.logical_and(row >= n_meta, row < n_meta + n_tok)
        diff = jnp.where(live, h_ref[...] - t_ref[...], 0.0)
        dh_ref[...] = diff * (1.0 / d)

        @pl.when(i == 0)
        def _():
            loss_ref[...] = jnp.zeros_like(loss_ref)

        loss_ref[...] += jnp.sum(jnp.sum(diff * diff, axis=1, keepdims=True), axis=0, keepdims=True) * (0.5 / d)

    row = pl.BlockSpec((tr, d), lambda i: (i, 0))
    return pl.pallas_call(
        body, name="loss_grad", grid=grid,
        out_shape=(jax.ShapeDtypeStruct((1, 1), F32), jax.ShapeDtypeStruct((m, d), F32)),
        in_specs=[row, row], out_specs=(pl.BlockSpec((1, 1), lambda i: (0, 0)), row),
        compiler_params=_params(grid))(h, target)


def _split_dot(x, tmat):
    hi = x.astype(BF16)
    lo = (x - hi.astype(F32)).astype(BF16)
    return _dot(hi, tmat) + _dot(lo, tmat)


def _softplus(z):
    return jnp.maximum(z, 0.0) + jnp.log(1.0 + jnp.exp(-jnp.abs(z)))


def _attn_tiles(lp):
    t = ATT_BLOCK
    nb = lp // t
    u = 3 if nb % 3 == 0 else (2 if nb % 2 == 0 else 1)
    return t, nb, u


def _attn_masks(tq, t, u):
    row = lax.broadcasted_iota(jnp.int32, (tq, t), 0)
    col = lax.broadcasted_iota(jnp.int32, (tq, t), 1)
    masks = [col + r * t < row for r in range(u)]
    a = lax.broadcasted_iota(jnp.int32, (t, 2 * t), 0)
    s = lax.broadcasted_iota(jnp.int32, (t, 2 * t), 1)
    after = jnp.logical_or(a > s, s >= t).astype(BF16)
    before = jnp.logical_or(a < s, s >= t).astype(BF16)
    return masks, after, before


def _service_heads(arrs, nh):
    sizes = [a.size * a.dtype.itemsize for a in arrs]
    heads, done = [], 0
    for sz in sizes:
        done += sz
        heads.append(min(nh - 1, int(1.1 * nh * done / sum(sizes)) + 1))
    return heads


def _attn_fwd(q, k, v, gather=()):
    nh, lp, dh = q.shape
    t, nb, u = _attn_tiles(lp)
    tq = u * t
    nq = nb // u
    grid = (nh, nq)
    ng = len(gather)
    service = _service_heads(gather, nh)

    def body(q_ref, k_ref, v_ref, *rest):
        o_ref, tot_ref, cnt_ref = rest[ng:ng + 3]
        h, i = pl.program_id(0), pl.program_id(1)
        if ng:
            plan = _GatherPlan(rest[:ng], rest[ng + 3:2 * ng + 3], *rest[2 * ng + 3:])
            pl.when(jnp.logical_and(h == 0, i == 0))(plan.start)
        qb = q_ref[...]
        masks, after, _ = _attn_masks(tq, t, u)
        o_ref[...] = jnp.zeros_like(o_ref)
        tot_ref[...] = jnp.zeros_like(tot_ref)

        def blocks(base, masked):
            outs, sums = [], []
            for r in reversed(range(u)):
                start = pl.multiple_of((base + r) * t, t)
                kb = k_ref[pl.ds(start, t), :]
                z = _dot(qb, kb, NT)
                sp = _softplus(z)
                lnb = jnp.where(masks[r], -sp, 0.0) if masked else -sp
                res = _split_dot(lnb, after)
                outs.append((start, z - sp + res[:, :t], r))
                sums.append(res[:, t:])
            cs = tot_ref[...]
            for (start, logw, r), rs in zip(outs, sums):
                w = jnp.exp(logw + cs)
                if masked:
                    w = jnp.where(masks[r], w, 0.0)
                o_ref[...] += _dot(w.astype(BF16), v_ref[pl.ds(start, t), :])
                cs = cs + rs
            tot_ref[...] = cs

        blocks(i * u, True)

        def live():
            return jnp.max(tot_ref[...]) > -DEAD_LOG_WEIGHT

        def step(state):
            n, _ = state
            blocks((i - 1 - n) * u, False)
            return n + 1, live()

        n_done, _ = lax.while_loop(lambda s: jnp.logical_and(s[0] < i, s[1]), step, (jnp.int32(0), live()))
        cnt_ref[h, i] = n_done.astype(F32)

        if ng:
            for a, head in enumerate(service):
                pl.when(jnp.logical_and(h == head, i == nq - 1))(functools.partial(plan.forward, a))
            pl.when(jnp.logical_and(h == nh - 1, i == nq - 1))(plan.finish)

    blk = pl.BlockSpec((None, tq, dh), lambda h, i: (h, i, 0))
    full = pl.BlockSpec((None, lp, dh), lambda h, i: (h, 0, 0))
    return pl.pallas_call(
        body, name="attn_fwd_gather" if ng else "attn_fwd", grid=grid,
        out_shape=(jax.ShapeDtypeStruct((nh, lp, dh), F32), jax.ShapeDtypeStruct((nh, lp, t), F32),
                   jax.ShapeDtypeStruct((nh, nq), F32)) + _gather_shapes(gather),
        in_specs=[blk, full, full] + [ANY] * ng,
        out_specs=(blk, pl.BlockSpec((None, tq, t), lambda h, i: (h, i, 0)), pl.BlockSpec(memory_space=pltpu.SMEM))
        + (ANY,) * ng,
        scratch_shapes=_comm_sems(ng, 7) if ng else [],
        compiler_params=_params(grid))(q, k, v, *gather)


def _attn_bwd(q, qt, k, v, do, dot_, tot, cnt, scale, exchange=()):
    nh, lp, dh = q.shape
    t, nb, u = _attn_tiles(lp)
    tq = u * t
    nq = nb // u
    grid = (nh, nq)
    ne = len(exchange)

    def body(q_ref, qt_ref, k_ref, v_ref, do_ref, dot_ref, tot_ref, cnt_ref, *rest):
        dq_ref, dk_ref, dv_ref = rest[ne:ne + 3]
        p_ref, g_ref = rest[2 * ne + 3:2 * ne + 5]
        h, i = pl.program_id(0), pl.program_id(1)
        if ne:
            plan = _ChipExchangePlan(rest[:ne], rest[ne + 3:2 * ne + 3], *rest[2 * ne + 5:])
            pl.when(jnp.logical_and(h == 0, i == 0))(plan.start)

        @pl.when(i == 0)
        def _():
            dk_ref[...] = jnp.zeros_like(dk_ref)
            dv_ref[...] = jnp.zeros_like(dv_ref)

        dq_ref[...] = jnp.zeros_like(dq_ref)
        p_ref[...] = jnp.zeros_like(p_ref)
        g_ref[...] = jnp.zeros_like(g_ref)
        qb, qtb, dob, dotb = q_ref[...], qt_ref[...], do_ref[...], dot_ref[...]
        total = tot_ref[...]
        masks, after, before = _attn_masks(tq, t, u)

        def blocks(base, masked):
            part = []
            for r in range(u):
                j = base + r
                start = pl.multiple_of(j * t, t)
                kb = k_ref[pl.ds(start, t), :]
                vb = v_ref[pl.ds(start, t), :]
                z = _dot(qb, kb, NT)
                sp = _softplus(z)
                lnb = jnp.where(masks[r], -sp, 0.0) if masked else -sp
                res = _split_dot(lnb, after)
                part.append((j, kb, z - sp, res[:, :t], res[:, t:], _dot(dob, vb, NT), r))
            pfx = p_ref[...]
            gp = g_ref[...]
            for j, kb, logsig, later_in, rs, dw, r in part:
                pfx = pfx + rs
                w = jnp.exp(logsig + later_in + (total - pfx))
                if masked:
                    w = jnp.where(masks[r], w, 0.0)
                dlogw = w * dw
                res2 = _split_dot(dlogw, before)
                dz = dlogw - jnp.exp(logsig) * (dlogw + res2[:, :t] + gp)
                if masked:
                    dz = jnp.where(masks[r], dz, 0.0)
                gp = gp + res2[:, t:]
                dzb = dz.astype(BF16)
                dq_ref[...] += _dot(dzb, kb)
                dk_ref[j] += _dot(qtb, dzb)
                dv_ref[j] += _dot(dotb, w.astype(BF16))
            p_ref[...] = pfx
            g_ref[...] = gp

        def step(n, carry):
            blocks(n * u, False)
            return carry

        walked = jnp.clip(cnt_ref[h, i].astype(jnp.int32), 0, i)
        lax.fori_loop(i - walked, i, step, 0)
        blocks(i * u, True)
        dq_ref[...] = dq_ref[...] * scale
        if ne:
            pl.when(jnp.logical_and(h == nh - 1, i == nq - 1))(plan.finish)

    blk = pl.BlockSpec((None, tq, dh), lambda h, i: (h, i, 0))
    blk_t = pl.BlockSpec((None, dh, tq), lambda h, i: (h, 0, i))
    full = pl.BlockSpec((None, lp, dh), lambda h, i: (h, 0, 0))
    acc = pl.BlockSpec((None, nb, dh, t), lambda h, i: (h, 0, 0, 0))
    acc_shape = jax.ShapeDtypeStruct((nh, nb, dh, t), F32)
    return pl.pallas_call(
        body, name="attn_bwd_exchange" if ne else "attn_bwd", grid=grid,
        out_shape=(jax.ShapeDtypeStruct((nh, lp, dh), F32), acc_shape, acc_shape) + _exchange_shapes(exchange),
        in_specs=[blk, blk_t, full, full, blk, blk_t, pl.BlockSpec((None, tq, t), lambda h, i: (h, i, 0)),
                  pl.BlockSpec(memory_space=pltpu.SMEM)] + [ANY] * ne,
        out_specs=(blk, acc, acc) + (ANY,) * ne,
        scratch_shapes=[pltpu.VMEM((tq, t), F32), pltpu.VMEM((tq, t), F32)] + (_comm_sems(ne, 3) if ne else []),
        compiler_params=_params(grid))(q, qt, k, v, do, dot_, tot, cnt, *exchange)


def _halo_specs(tm, width, nt):
    per = tm // HALO
    prev = pl.BlockSpec((HALO, width), lambda i: (jnp.maximum(i * per - 1, 0), 0))
    nxt = pl.BlockSpec((HALO, width), lambda i: (jnp.minimum((i + 1) * per, nt * per - 1), 0))
    return prev, nxt


def _shift_down(x, k):
    return x if k == 0 else pltpu.roll(x, k, axis=0)


def _shift_up(x, k):
    return x if k == 0 else pltpu.roll(x, x.shape[0] - k, axis=0)


def _pool_fwd(u, wp, scale, layer):
    m, dp = u.shape
    g = dp // len(POOL_WINDOWS)
    tm = _row_tile(m)
    nt = m // tm
    grid = (nt,)
    prev, _ = _halo_specs(tm, dp, nt)

    def body(u_ref, h_ref, w_ref, s_ref, o_ref):
        i = pl.program_id(0)
        halo = jnp.where(i > 0, h_ref[...], 0.0)
        s = jnp.concatenate([halo, u_ref[...]], axis=0)
        tpos = i * tm + lax.broadcasted_iota(jnp.int32, (tm, 1), 0)
        for gi, win in enumerate(POOL_WINDOWS):
            s = s + _shift_down(s, win // 2)
            cols = slice(gi * g, (gi + 1) * g)
            cnt = jnp.minimum(tpos + 1, win).astype(F32)
            pooled = s[HALO:, cols] / cnt - u_ref[:, cols]
            o_ref[:, cols] = _dot(pooled.astype(BF16), w_ref[gi]) * s_ref[:, cols]

    row = pl.BlockSpec((tm, dp), lambda i: (i, 0))
    return pl.pallas_call(
        body, name="pool_fwd", grid=grid, out_shape=jax.ShapeDtypeStruct((m, dp), F32),
        in_specs=[row, prev, pl.BlockSpec((None, len(POOL_WINDOWS), g, g), lambda i: (layer, 0, 0, 0)),
                  pl.BlockSpec((1, dp), lambda i: (0, 0))],
        out_specs=row, compiler_params=_params(grid))(u, u, wp, scale)


def _pool_bwd(u, dmix, wp, scale, layer):
    m, dp = u.shape
    ng = len(POOL_WINDOWS)
    g = dp // ng
    tm = _row_tile(m)
    nt = m // tm
    grid = (nt,)
    prev, nxt = _halo_specs(tm, dp, nt)

    def body(u_ref, h_ref, dm_ref, dmn_ref, w_ref, s_ref, du_ref, dw_ref, ds_ref):
        i = pl.program_id(0)

        @pl.when(i == 0)
        def _():
            dw_ref[...] = jnp.zeros_like(dw_ref)
            ds_ref[...] = jnp.zeros_like(ds_ref)

        halo = jnp.where(i > 0, h_ref[...], 0.0)
        s = jnp.concatenate([halo, u_ref[...]], axis=0)
        dmn = jnp.where(i < nt - 1, dmn_ref[...], 0.0)
        dmc = jnp.concatenate([dm_ref[...], dmn], axis=0)
        tpos = i * tm + lax.broadcasted_iota(jnp.int32, (tm, 1), 0)
        tpos_x = i * tm + lax.broadcasted_iota(jnp.int32, (tm + HALO, 1), 0)
        for gi, win in enumerate(POOL_WINDOWS):
            s = s + _shift_down(s, win // 2)
            cols = slice(gi * g, (gi + 1) * g)
            cnt = jnp.minimum(tpos + 1, win).astype(F32)
            pooled = (s[HALO:, cols] / cnt - u_ref[:, cols]).astype(BF16)
            wg = w_ref[gi]
            pm = _dot(pooled, wg)
            ds_ref[:, cols] += jnp.sum(dm_ref[:, cols] * pm, axis=0, keepdims=True)
            dpm = (dmc[:, cols] * s_ref[:, cols]).astype(BF16)
            dw_ref[gi] += _dot(pooled, dpm[:tm], TN)
            dpool = _dot(dpm, wg, NT)
            f = dpool / jnp.minimum(tpos_x + 1, win).astype(F32)
            step = 1
            while step < win:
                f = f + _shift_up(f, step)
                step *= 2
            du_ref[:, cols] = f[:tm] - dpool[:tm]

    row = pl.BlockSpec((tm, dp), lambda i: (i, 0))
    return pl.pallas_call(
        body, name="pool_bwd", grid=grid,
        out_shape=(jax.ShapeDtypeStruct((m, dp), F32), jax.ShapeDtypeStruct((ng, g, g), F32),
                   jax.ShapeDtypeStruct((1, dp), F32)),
        in_specs=[row, prev, row, nxt, pl.BlockSpec((None, ng, g, g), lambda i: (layer, 0, 0, 0)),
                  pl.BlockSpec((1, dp), lambda i: (0, 0))],
        out_specs=(row, pl.BlockSpec((ng, g, g), lambda i: (0, 0, 0)), pl.BlockSpec((1, dp), lambda i: (0, 0))),
        compiler_params=_params(grid))(u, u, dmix, dmix, wp, scale)


def _conv_taps(u, w_ref, rows):
    y = None
    for k in range(CONV_WIDTH):
        term = _shift_down(u, CONV_WIDTH - 1 - k)[HALO:HALO + rows] * w_ref[k:k + 1, :]
        y = term if y is None else y + term
    return y


def _layernorm_stats(y):
    mu = jnp.mean(y, axis=-1, keepdims=True)
    yc = y - mu
    rstd = lax.rsqrt(jnp.mean(yc * yc, axis=-1, keepdims=True) + EPS)
    return yc * rstd, rstd


def _conv_fwd(uc, wdw, b, lg, lb):
    m, c2 = uc.shape
    c = c2 // 2
    tm = _row_tile(m)
    nt = m // tm
    grid = (nt,)
    prev, _ = _halo_specs(tm, c2, nt)

    def body(x_ref, h_ref, w_ref, b_ref, g_ref, bb_ref, o_ref):
        i = pl.program_id(0)
        halo = jnp.where(i > 0, h_ref[...], 0.0)
        xc = jnp.concatenate([halo, x_ref[...]], axis=0)
        u = xc[:, :c] * jax.nn.sigmoid(xc[:, c:])
        y = _conv_taps(u, w_ref, tm) + b_ref[...]
        xhat, _ = _layernorm_stats(y)
        ln = xhat * g_ref[...] + bb_ref[...]
        o_ref[...] = (ln * jax.nn.sigmoid(ln)).astype(BF16)

    vec = pl.BlockSpec((1, c), lambda i: (0, 0))
    return pl.pallas_call(
        body, name="conv_fwd", grid=grid, out_shape=jax.ShapeDtypeStruct((m, c), BF16),
        in_specs=[pl.BlockSpec((tm, c2), lambda i: (i, 0)), prev, pl.BlockSpec((HALO, c), lambda i: (0, 0)), vec, vec, vec],
        out_specs=pl.BlockSpec((tm, c), lambda i: (i, 0)), compiler_params=_params(grid))(uc, uc, wdw, b, lg, lb)


def _conv_bwd(uc, ds, wdw, b, lg, lb):
    m, c2 = uc.shape
    c = c2 // 2
    tm = _row_tile(m)
    nt = m // tm
    grid = (nt,)
    prev, nxt = _halo_specs(tm, c2, nt)
    _, nxt_c = _halo_specs(tm, c, nt)

    def body(x_ref, hp_ref, hn_ref, ds_ref, dsn_ref, w_ref, b_ref, g_ref, bb_ref,
             dx_ref, dw_ref, db_ref, dg_ref, dbb_ref):
        i = pl.program_id(0)

        @pl.when(i == 0)
        def _():
            for ref in (dw_ref, db_ref, dg_ref, dbb_ref):
                ref[...] = jnp.zeros_like(ref)

        hp = jnp.where(i > 0, hp_ref[...], 0.0)
        xc = jnp.concatenate([hp, x_ref[...], hn_ref[...]], axis=0)
        sg = jax.nn.sigmoid(xc[:, c:])
        u = xc[:, :c] * sg
        rows = tm + HALO
        y = _conv_taps(u, w_ref, rows) + b_ref[...]
        xhat, rstd = _layernorm_stats(y)
        ln = xhat * g_ref[...] + bb_ref[...]
        sl = jax.nn.sigmoid(ln)
        dsn = jnp.where(i < nt - 1, dsn_ref[...], 0.0)
        dsx = jnp.concatenate([ds_ref[...], dsn], axis=0)
        dln = dsx * sl * (1.0 + ln * (1.0 - sl))
        dxh = dln * g_ref[...]
        dy = rstd * (dxh - jnp.mean(dxh, axis=-1, keepdims=True)
                     - xhat * jnp.mean(dxh * xhat, axis=-1, keepdims=True))
        dyt = dy[:tm]
        dg_ref[...] += jnp.sum(dln[:tm] * xhat[:tm], axis=0, keepdims=True)
        dbb_ref[...] += jnp.sum(dln[:tm], axis=0, keepdims=True)
        db_ref[...] += jnp.sum(dyt, axis=0, keepdims=True)
        du = None
        for k in range(CONV_WIDTH):
            lag = CONV_WIDTH - 1 - k
            dw_ref[k:k + 1, :] += jnp.sum(dyt * _shift_down(u, lag)[HALO:HALO + tm], axis=0, keepdims=True)
            term = _shift_up(dy, lag)[:tm] * w_ref[k:k + 1, :]
            du = term if du is None else du + term
        a_t = xc[HALO:HALO + tm, :c]
        sg_t = sg[HALO:HALO + tm]
        dx_ref[:, :c] = du * sg_t
        dx_ref[:, c:] = du * a_t * sg_t * (1.0 - sg_t)

    vec = pl.BlockSpec((1, c), lambda i: (0, 0))
    wsp = pl.BlockSpec((HALO, c), lambda i: (0, 0))
    vshape = jax.ShapeDtypeStruct((1, c), F32)
    return pl.pallas_call(
        body, name="conv_bwd", grid=grid,
        out_shape=(jax.ShapeDtypeStruct((m, c2), F32), jax.ShapeDtypeStruct((HALO, c), F32), vshape, vshape, vshape),
        in_specs=[pl.BlockSpec((tm, c2), lambda i: (i, 0)), prev, nxt, pl.BlockSpec((tm, c), lambda i: (i, 0)), nxt_c,
                  wsp, vec, vec, vec],
        out_specs=(pl.BlockSpec((tm, c2), lambda i: (i, 0)), wsp, vec, vec, vec),
        compiler_params=_params(grid))(uc, uc, uc, ds, ds, wdw, b, lg, lb)


def _mesh_pos():
    return lax.axis_index("x"), lax.axis_index("y"), lax.axis_index("c")


def _other_chips(x, y):
    return [(1 - x, y), (x, 1 - y), (1 - x, 1 - y)]


def _comm_sems(n, per):
    return [pltpu.SemaphoreType.DMA((n, per)), pltpu.SemaphoreType.DMA((n, per)), pltpu.SemaphoreType.DMA((n,))]


def _comm_call(body, name, arrs, out_shapes, n_sems):
    n = len(arrs)
    return pl.pallas_call(
        body, name=name, out_shape=out_shapes, in_specs=[ANY] * n, out_specs=tuple([ANY] * len(out_shapes)),
        scratch_shapes=_comm_sems(n, n_sems),
    )(*arrs)


class _GatherPlan:
    def __init__(self, ins, outs, send_sems, recv_sems, local_sems):
        self.ins, self.outs = ins, outs
        self.send_sems, self.recv_sems, self.local_sems = send_sems, recv_sems, local_sems
        x, y, c = _mesh_pos()
        self.c, self.me, self.sib = c, (x, y, c), (x, y, 1 - c)
        self.chips = _other_chips(x, y)

    def _slot(self, a, p):
        return self.outs[a].at[4 * p[0] + 2 * p[1] + p[2]]

    def _copy(self, a, k, block, to, own=False):
        return pltpu.make_async_remote_copy(
            src_ref=self.ins[a] if own else self._slot(a, block), dst_ref=self._slot(a, block),
            send_sem=self.send_sems.at[a, k], recv_sem=self.recv_sems.at[a, k], device_id=to, device_id_type=MESH)

    def _local(self, a):
        return pltpu.make_async_copy(self.ins[a], self._slot(a, self.me), self.local_sems.at[a])

    def start(self):
        for a in range(len(self.ins)):
            for j, chip in enumerate(self.chips):
                self._copy(a, 1 + j, self.me, (*chip, self.c), own=True).start()
            self._copy(a, 0, self.me, self.sib, own=True).start()
            self._local(a).start()

    def forward(self, a):
        for j, chip in enumerate(self.chips):
            self._copy(a, 1 + j, (*chip, self.c), self.me).wait_recv()
            self._copy(a, 4 + j, (*chip, self.c), self.sib).start()

    def finish(self):
        for a in range(len(self.ins)):
            self._copy(a, 0, self.sib, self.me).wait_recv()
            for j, chip in enumerate(self.chips):
                self._copy(a, 4 + j, (*chip, 1 - self.c), self.me).wait_recv()
        for a in range(len(self.ins)):
            self._copy(a, 0, self.me, self.sib, own=True).wait_send()
            for j, chip in enumerate(self.chips):
                self._copy(a, 1 + j, self.me, (*chip, self.c), own=True).wait_send()
                self._copy(a, 4 + j, (*chip, self.c), self.sib).wait_send()
            self._local(a).wait()


def _gather_shapes(arrs):
    return tuple(jax.ShapeDtypeStruct((N_DEV,) + a.shape, a.dtype) for a in arrs)


def _all_gather(arrs, name):
    n = len(arrs)

    def body(*refs):
        plan = _GatherPlan(refs[:n], refs[n:2 * n], *refs[2 * n:])
        plan.start()
        for a in range(n):
            plan.forward(a)
        plan.finish()

    return _comm_call(body, name, arrs, _gather_shapes(arrs), 7)


class _ChipExchangePlan:
    def __init__(self, ins, outs, send_sems, recv_sems, local_sems):
        self.ins, self.outs = ins, outs
        self.send_sems, self.recv_sems, self.local_sems = send_sems, recv_sems, local_sems
        x, y, c = _mesh_pos()
        self.pos, self.mine = (x, y, c), 2 * x + y
        self.chips = _other_chips(x, y)

    def _send(self, a, j):
        chip = self.chips[j]
        return pltpu.make_async_remote_copy(
            src_ref=self.ins[a].at[2 * chip[0] + chip[1]], dst_ref=self.outs[a].at[self.mine],
            send_sem=self.send_sems.at[a, j], recv_sem=self.recv_sems.at[a, j], device_id=(*chip, self.pos[2]),
            device_id_type=MESH)

    def _landing(self, a, j):
        chip = self.chips[j]
        ref = self.outs[a].at[2 * chip[0] + chip[1]]
        return pltpu.make_async_remote_copy(src_ref=ref, dst_ref=ref, send_sem=self.send_sems.at[a, j],
                                            recv_sem=self.recv_sems.at[a, j], device_id=self.pos, device_id_type=MESH)

    def _local(self, a):
        return pltpu.make_async_copy(self.ins[a].at[self.mine], self.outs[a].at[self.mine], self.local_sems.at[a])

    def start(self):
        for a in range(len(self.ins)):
            for j in range(3):
                self._send(a, j).start()
            self._local(a).start()

    def finish(self):
        for a in range(len(self.ins)):
            for j in range(3):
                self._landing(a, j).wait_recv()
        for a in range(len(self.ins)):
            for j in range(3):
                self._send(a, j).wait_send()
            self._local(a).wait()


def _pair_tile(kk):
    return _pick(kk, (1024, 704, 576, 512, 256, 128, 64))


def _pair_exchange(parts, core, name):
    _, kk, n = parts.shape
    tr = _pair_tile(kk)
    nr = kk // tr
    grid = (4, nr)

    def body(c_ref, src_ref, recv_ref, send_sems, recv_sems):
        q, r = pl.program_id(0), pl.program_id(1)
        x, y, c = _mesh_pos()
        step = q * nr + r
        slot = step % 2
        dst = recv_ref.at[pl.ds(pl.multiple_of(step * tr, tr), tr), :]
        cp = pltpu.make_async_remote_copy(src_ref=src_ref, dst_ref=dst, send_sem=send_sems.at[slot],
                                          recv_sem=recv_sems.at[slot], device_id=(x, y, 1 - c), device_id_type=MESH)
        cp.start()
        cp.wait_send()
        cp.wait_recv()

    spec = pltpu.PrefetchScalarGridSpec(
        num_scalar_prefetch=1, grid=grid,
        in_specs=[pl.BlockSpec((tr, n), lambda q, r, c: ((2 * q + 1 - c[0]) * nr + r, 0))],
        out_specs=ANY, scratch_shapes=[pltpu.SemaphoreType.DMA((2,)), pltpu.SemaphoreType.DMA((2,))])
    recv = pl.pallas_call(body, name=name, grid_spec=spec, out_shape=jax.ShapeDtypeStruct((4 * kk, n), parts.dtype),
                          compiler_params=_params(grid))(core, parts.reshape(N_DEV * kk, n))
    return recv.reshape(4, kk, n)


def _exchange_shapes(parts):
    return tuple(jax.ShapeDtypeStruct(p.shape, p.dtype) for p in parts)


def _reduce_chip_exchange(parts, name):
    n = len(parts)

    def body(*refs):
        plan = _ChipExchangePlan(refs[:n], refs[n:2 * n], *refs[2 * n:])
        plan.start()
        plan.finish()

    return _comm_call(body, name, parts, _exchange_shapes(parts), 3)


def _pair_add(parts, recv, core, name):
    _, kk, n = recv.shape
    tr = _pair_tile(kk)
    grid = (4, kk // tr)

    def body(c_ref, a_ref, b_ref, o_ref):
        o_ref[...] = (a_ref[...].astype(F32) + b_ref[...].astype(F32)).astype(o_ref.dtype)

    out = pl.BlockSpec((None, tr, n), lambda q, r, c: (q, r, 0))
    spec = pltpu.PrefetchScalarGridSpec(
        num_scalar_prefetch=1, grid=grid,
        in_specs=[pl.BlockSpec((None, None, tr, n), lambda q, r, c: (q, c[0], r, 0)), out], out_specs=out)
    return pl.pallas_call(body, name=name, grid_spec=spec, out_shape=jax.ShapeDtypeStruct(recv.shape, recv.dtype),
                          compiler_params=_params(grid))(core, parts.reshape(4, 2, kk, n), recv)


def _adamw_math(w, g, m, v):
    m2 = ADAM_B1 * m + (1.0 - ADAM_B1) * g
    v2 = ADAM_B2 * v + (1.0 - ADAM_B2) * (g * g)
    m_hat = m2 / (1.0 - ADAM_B1 ** ADAM_STEP)
    v_hat = v2 / (1.0 - ADAM_B2 ** ADAM_STEP)
    delta = -ADAM_LR * (m_hat / (jnp.sqrt(v_hat) + ADAM_EPS) + ADAM_WD * w)
    return delta, m2, v2


def _adamw_sharded(parts, w, m, v, name):
    nl, kk, n = w.shape
    tr = _pick(kk, (256, 128, 64)) if n <= 1024 else _pick(kk, (64, 32))
    grid = (nl, kk // tr)

    def body(*refs):
        p_refs = refs[:nl]
        w_ref, m_ref, v_ref, g_ref, d_ref, m2_ref, v2_ref = refs[nl:]
        layer = pl.program_id(0)
        g = None
        for l, p_ref in enumerate(p_refs):
            s = p_ref[0].astype(F32)
            for q in range(1, 4):
                s = s + p_ref[q].astype(F32)
            g = s if g is None else jnp.where(layer == l, s, g)
        d, m2, v2 = _adamw_math(w_ref[...], g, m_ref[...], v_ref[...])
        g_ref[...] = g
        d_ref[...] = d
        m2_ref[...] = m2
        v2_ref[...] = v2

    spec = pl.BlockSpec((None, tr, n), lambda l, r: (l, r, 0))
    shp = jax.ShapeDtypeStruct(w.shape, F32)
    p_specs = [pl.BlockSpec((4, tr, n), lambda l, r, k=k: (0, jnp.where(l == k, r, 0), 0)) for k in range(nl)]
    return pl.pallas_call(
        body, name=name, grid=grid, out_shape=(shp, shp, shp, shp), in_specs=p_specs + [spec, spec, spec],
        out_specs=(spec, spec, spec, spec), compiler_params=_params(grid))(*parts, w, m, v)


def _sum_slots(g8):
    _, r, _ = g8.shape

    def body(g_ref, o_ref):
        s = g_ref[0]
        for d in range(1, N_DEV):
            s = s + g_ref[d]
        o_ref[...] = s

    return pl.pallas_call(body, name="sum_slots", out_shape=jax.ShapeDtypeStruct((r, 128), F32))(g8)


def _adamw_flat(w, g, m, v):
    def body(w_ref, g_ref, m_ref, v_ref, d_ref, m2_ref, v2_ref):
        d, m2, v2 = _adamw_math(w_ref[...], g_ref[...], m_ref[...], v_ref[...])
        d_ref[...] = d
        m2_ref[...] = m2
        v2_ref[...] = v2

    shp = jax.ShapeDtypeStruct(w.shape, F32)
    return pl.pallas_call(body, name="adamw_flat", out_shape=(shp, shp, shp))(w, g, m, v)


def _pack(arrs):
    flat = jnp.concatenate([a.reshape(-1) for a in arrs])
    pad = (-flat.shape[0]) % 1024
    return jnp.pad(flat, (0, pad)).reshape(-1, 128)


def _unpack(buf, like):
    flat = buf.reshape(-1)
    out, off = [], 0
    for a in like:
        out.append(flat[off:off + a.size].reshape(a.shape))
        off += a.size
    return out


def _heads(x2d, scale=None):
    lp, da = x2d.shape
    xh = x2d.reshape(lp, N_HEADS, da // N_HEADS).transpose(1, 0, 2)
    if scale is not None:
        xh = xh * scale
    return xh


def _unheads(xh):
    nh, lp, dh = xh.shape
    return xh.transpose(1, 0, 2).reshape(lp, nh * dh)


def _layer_fwd(h, l, W, P, gather_after_proj=None):
    d = h.shape[1]
    da, dp = d // 2, d // 4
    dh = da // N_HEADS
    sv = {"h": h}
    u1 = _rms_fwd(h, P["pre_mix_g"][l:l + 1], "rms_pre_mix")
    proj = _mm_nt(u1, W["in"], "mm_in")
    qh = _heads(proj[:, :da], dh ** -0.5).astype(BF16)
    kh = _heads(proj[:, da:2 * da]).astype(BF16)
    vh = _heads(proj[:, 2 * da:3 * da]).astype(BF16)
    u_pool = proj[:, 3 * da:3 * da + dp]
    u_conv = proj[:, 3 * da + dp:]
    if gather_after_proj is None:
        oh, tot, cnt = _attn_fwd(qh, kh, vh)
    else:
        arrs, complete = gather_after_proj
        oh, tot, cnt, *gathered = _attn_fwd(qh, kh, vh, arrs)
        complete(gathered)
    o_attn = _unheads(oh)
    o_pool = _pool_fwd(u_pool, W["pool"], P["pool_scale"][l:l + 1], l)
    s_conv = _conv_fwd(u_conv, W["dw"][l], P["b_dw"][l:l + 1], P["conv_ln_g"][l:l + 1], P["conv_ln_b"][l:l + 1])
    o_conv = _mm_nn([(s_conv, W["pw"])], "mm_pw")
    merged = _merge_fwd(o_attn, o_pool, o_conv, P["mix_out_g"][l:l + 1])
    z1 = _mm_nn([(merged, W["out"])], "mm_out")
    h1 = _resid_rms_fwd(h, z1, P["post_mix_g"][l:l + 1], "resid_post_mix")
    u2 = _rms_fwd(h1, P["pre_ffn_g"][l:l + 1], "rms_pre_ffn")
    a_s, b_s, act_s = _ffn_up(u2, W["gate"], W["up"])
    ff = _mm_nn([(act_s, W["down"])], "mm_down")
    h2 = _resid_rms_fwd(h1, ff, P["post_ffn_g"][l:l + 1], "resid_post_ffn")
    sv.update(u1=u1, qh=qh, kh=kh, vh=vh, tot=tot, cnt=cnt, u_pool=u_pool, u_conv=u_conv, o_attn=o_attn, o_pool=o_pool,
              o_conv=o_conv, s_conv=s_conv, merged=merged, z1=z1, h1=h1, u2=u2, a_s=a_s, b_s=b_s, act_s=act_s, ff=ff)
    return h2, sv


EARLY = ("w_down", "w_gate", "w_up", "w_out", "w_pw")


def _layer_bwd(dh_out, l, W, P, sv, exchange_with_attn=None):
    d = dh_out.shape[1]
    da = d // 2
    dhd = da // N_HEADS
    lp = dh_out.shape[0]
    g = {}
    dff, g["post_ffn_g"] = _rms_bwd(sv["ff"], P["post_ffn_g"][l:l + 1], dh_out, None, BF16, "rms_bwd_post_ffn")
    da_s, db_s = _ffn_bwd_act(dff, W["down"], sv["a_s"], sv["b_s"])
    g["w_down"] = _mm_tn(sv["act_s"], dff, "mm_dw_down")
    du2 = _mm_nn([(da_s, W["gate"]), (db_s, W["up"])], "mm_d_u2")
    g["w_gate"] = _mm_tn(da_s, sv["u2"], "mm_dw_gate")
    g["w_up"] = _mm_tn(db_s, sv["u2"], "mm_dw_up")
    dh1, g["pre_ffn_g"] = _rms_bwd(sv["h1"], P["pre_ffn_g"][l:l + 1], du2, dh_out, F32, "rms_bwd_pre_ffn")
    dz1, g["post_mix_g"] = _rms_bwd(sv["z1"], P["post_mix_g"][l:l + 1], dh1, None, BF16, "rms_bwd_post_mix")
    dmerged = _mm_nt(dz1, W["out"], "mm_d_merged")
    g["w_out"] = _mm_tn(sv["merged"], dz1, "mm_dw_out")
    d_oa, d_op, d_oc, g["mix_out_g"] = _merge_bwd(sv["o_attn"], sv["o_pool"], sv["o_conv"], P["mix_out_g"][l:l + 1], dmerged)
    d_s = _mm_nt(d_oc, W["pw"], "mm_d_sconv")
    g["w_pw"] = _mm_tn(sv["s_conv"], d_oc, "mm_dw_pw")
    d_uc, dwdw, g["b_dw"], g["conv_ln_g"], g["conv_ln_b"] = _conv_bwd(
        sv["u_conv"], d_s, W["dw"][l], P["b_dw"][l:l + 1], P["conv_ln_g"][l:l + 1], P["conv_ln_b"][l:l + 1])
    g["w_dw"] = dwdw[:CONV_WIDTH]
    d_up, g["w_pool"], g["pool_scale"] = _pool_bwd(sv["u_pool"], d_op, W["pool"], P["pool_scale"][l:l + 1], l)
    doh = _heads(d_oa).astype(BF16)
    riders = exchange_with_attn(g) if exchange_with_attn is not None else []
    dqh, dkt, dvt, *exchanged = _attn_bwd(sv["qh"], sv["qh"].transpose(0, 2, 1), sv["kh"], sv["vh"], doh,
                                          doh.transpose(0, 2, 1), sv["tot"], sv["cnt"], dhd ** -0.5, riders)
    dk2, dv2 = (a.transpose(1, 3, 0, 2).reshape(lp, da) for a in (dkt, dvt))
    dproj = jnp.concatenate([_unheads(dqh), dk2, dv2, d_up, d_uc], axis=1).astype(BF16)
    du1 = _mm_nn([(dproj, W["in"])], "mm_d_u1")
    g["w_in"] = _mm_tn(dproj, sv["u1"], "mm_dw_in")
    dh0, g["pre_mix_g"] = _rms_bwd(sv["h"], P["pre_mix_g"][l:l + 1], du1, dh1, F32, "rms_bwd_pre_mix")
    return dh0, g, exchanged


SHARDED = ("w_in", "w_gate", "w_up", "w_down", "w_out", "w_pw")
SMALL = ("pre_mix_g", "w_pool", "pool_scale", "b_dw", "conv_ln_g", "conv_ln_b", "mix_out_g", "post_mix_g",
         "pre_ffn_g", "post_ffn_g", "w_dw", "meta_tokens")
ORDER = ("meta_tokens", "pre_mix_g", "w_in", "w_pool", "pool_scale", "w_dw", "b_dw", "conv_ln_g", "conv_ln_b", "w_pw",
         "mix_out_g", "w_out", "post_mix_g", "pre_ffn_g", "w_gate", "w_up", "w_down", "post_ffn_g")


def kernel(x, meta_tokens, pre_mix_g, w_in, w_pool, pool_scale, w_dw, b_dw, conv_ln_g, conv_ln_b, w_pw, mix_out_g, w_out, post_mix_g, pre_ffn_g, w_gate, w_up, w_down, post_ffn_g, loss_target, m_meta_tokens, m_pre_mix_g, m_w_in, m_w_pool, m_pool_scale, m_w_dw, m_b_dw, m_conv_ln_g, m_conv_ln_b, m_w_pw, m_mix_out_g, m_w_out, m_post_mix_g, m_pre_ffn_g, m_w_gate, m_w_up, m_w_down, m_post_ffn_g, v_meta_tokens, v_pre_mix_g, v_w_in, v_w_pool, v_pool_scale, v_w_dw, v_b_dw, v_conv_ln_g, v_conv_ln_b, v_w_pw, v_mix_out_g, v_w_out, v_post_mix_g, v_pre_ffn_g, v_w_gate, v_w_up, v_w_down, v_post_ffn_g):
    P = dict(meta_tokens=meta_tokens, pre_mix_g=pre_mix_g, w_in=w_in, w_pool=w_pool, pool_scale=pool_scale, w_dw=w_dw,
             b_dw=b_dw, conv_ln_g=conv_ln_g, conv_ln_b=conv_ln_b, w_pw=w_pw, mix_out_g=mix_out_g, w_out=w_out,
             post_mix_g=post_mix_g, pre_ffn_g=pre_ffn_g, w_gate=w_gate, w_up=w_up, w_down=w_down, post_ffn_g=post_ffn_g)
    M = dict(meta_tokens=m_meta_tokens, pre_mix_g=m_pre_mix_g, w_in=m_w_in, w_pool=m_w_pool, pool_scale=m_pool_scale,
             w_dw=m_w_dw, b_dw=m_b_dw, conv_ln_g=m_conv_ln_g, conv_ln_b=m_conv_ln_b, w_pw=m_w_pw, mix_out_g=m_mix_out_g,
             w_out=m_w_out, post_mix_g=m_post_mix_g, pre_ffn_g=m_pre_ffn_g, w_gate=m_w_gate, w_up=m_w_up,
             w_down=m_w_down, post_ffn_g=m_post_ffn_g)
    V = dict(meta_tokens=v_meta_tokens, pre_mix_g=v_pre_mix_g, w_in=v_w_in, w_pool=v_w_pool, pool_scale=v_pool_scale,
             w_dw=v_w_dw, b_dw=v_b_dw, conv_ln_g=v_conv_ln_g, conv_ln_b=v_conv_ln_b, w_pw=v_w_pw, mix_out_g=v_mix_out_g,
             w_out=v_w_out, post_mix_g=v_post_mix_g, pre_ffn_g=v_pre_ffn_g, w_gate=v_w_gate, w_up=v_w_up,
             w_down=v_w_down, post_ffn_g=v_post_ffn_g)
    xi, yi, ci = _mesh_pos()
    dev = 4 * xi + 2 * yi + ci
    n_tok, d = x.shape[1], x.shape[2]
    n_meta = meta_tokens.shape[0]
    n_layers = w_in.shape[0]
    c = d // 4
    l_real = n_meta + n_tok
    lp = -(-l_real // ATT_BLOCK) * ATT_BLOCK

    TRANSPOSED = ("w_in", "w_gate", "w_up")
    shard = {k: (jnp.swapaxes(P[k], 1, 2) if k in TRANSPOSED else P[k]) for k in SHARDED}

    def bf16_shard(k, l):
        return shard[k][l].astype(BF16)

    def whole(gathered):
        return gathered.reshape(-1, gathered.shape[-1])

    g_in0, g_dw, g_meta = _all_gather([bf16_shard("w_in", 0), w_dw, meta_tokens], "gather_first")
    wdw_full = g_dw.transpose(1, 2, 0, 3).reshape(n_layers, CONV_WIDTH, c)
    common = {"pool": w_pool.astype(BF16), "dw": jnp.pad(wdw_full, ((0, 0), (0, HALO - CONV_WIDTH), (0, 0)))}
    W = [dict(common) for _ in range(n_layers)]
    W[0]["in"] = whole(g_in0)
    meta_full = g_meta.transpose(1, 0, 2).reshape(n_meta, d)
    need_order = ("w_pw", "w_out", "w_gate", "w_up", "w_down")
    later_keys = [(k, 0) for k in need_order] + [(k, l) for l in range(1, n_layers) for k in ("w_in",) + need_order]
    later = [bf16_shard(k, l) for k, l in later_keys]

    def complete(gathered):
        for (k, l), g in zip(later_keys, gathered):
            W[l][k[2:]] = whole(g)

    h = jnp.concatenate([meta_full, x[0], jnp.zeros((lp - l_real, d), F32)], axis=0)
    saved = []
    for l in range(n_layers):
        h, sv = _layer_fwd(h, l, W[l], P, (later, complete) if l == 0 else None)
        saved.append(sv)
    target = jnp.pad(loss_target[0], ((n_meta, lp - l_real), (0, 0)))
    loss_part, dh = _loss_grad(h, target, n_meta, n_tok)
    loss = lax.psum(loss_part[0, 0], ("x", "y", "c"))

    core = jnp.reshape(ci, (1,)).astype(jnp.int32)

    def pair_reduce(g, keys, l):
        res = []
        for k in keys:
            p = g[k].reshape(N_DEV, g[k].shape[0] // N_DEV, g[k].shape[1])
            recv = _pair_exchange(p, core, f"pair_exchange_{k}_{l}")
            res.append(_pair_add(p, recv, core, f"pair_add_{k}_{l}"))
        return res

    grads = [None] * n_layers
    riders = []
    pair_sums = []
    for l in reversed(range(1, n_layers)):
        dh, grads[l], _ = _layer_bwd(dh, l, W[l], P, saved[l])
        riders += [(l, k) for k in SHARDED]
        pair_sums += pair_reduce(grads[l], SHARDED, l)

    def ride(g):
        riders.extend((0, k) for k in EARLY)
        return pair_sums + pair_reduce(g, EARLY, 0)

    dh, grads[0], exchanged = _layer_bwd(dh, 0, W[0], P, saved[0], ride)
    by_chip = dict(zip(riders, exchanged))
    (by_chip[(0, "w_in")],) = _reduce_chip_exchange(pair_reduce(grads[0], ("w_in",), 0), "reduce_chips_w_in")
    grad_x = dh[n_meta:l_real][None]
    out = {}
    for k in SHARDED:
        view = (lambda a: jnp.swapaxes(a, 1, 2)) if k in TRANSPOSED else (lambda a: a)
        res = _adamw_sharded([by_chip[(l, k)] for l in range(n_layers)], view(P[k]), view(M[k]), view(V[k]), "adamw_" + k)
        out[k] = tuple(view(r) for r in res)

    small_parts = []
    for k in SMALL:
        if k == "meta_tokens":
            small_parts.append(dh[:n_meta])
        else:
            small_parts.append(jnp.stack([grads[l][k].reshape(P[k].shape[1:] if k != "w_dw" else (CONV_WIDTH, c))
                                          for l in range(n_layers)], axis=0))
    (g8,) = _all_gather([_pack(small_parts)], "gather_small_grads")
    g_small = dict(zip(SMALL, _unpack(_sum_slots(g8), small_parts)))
    g_small["w_dw"] = lax.dynamic_slice_in_dim(g_small["w_dw"], dev * w_dw.shape[2], w_dw.shape[2], axis=2)
    g_small["meta_tokens"] = lax.dynamic_slice_in_dim(g_small["meta_tokens"], dev * meta_tokens.shape[1],
                                                      meta_tokens.shape[1], axis=1)
    like = [P[k] for k in SMALL]
    res = _adamw_flat(_pack(like), _pack([g_small[k] for k in SMALL]), _pack([M[k] for k in SMALL]),
                      _pack([V[k] for k in SMALL]))
    res = [_unpack(r, like) for r in res]
    for i, k in enumerate(SMALL):
        out[k] = (g_small[k], res[0][i], res[1][i], res[2][i])

    return (loss, grad_x, *[out[k][0] for k in ORDER], *[out[k][1] for k in ORDER],
            *[out[k][2] for k in ORDER], *[out[k][3] for k in ORDER])
```

```python
import functools

import jax
import jax.numpy as jnp
from jax import lax
from jax.experimental import pallas as pl
from jax.experimental.pallas import tpu as pltpu

F32 = jnp.float32
BF16 = jnp.bfloat16
EPS = 1e-6
N_HEADS = 16
POOL_WINDOWS = (2, 4, 8, 16)
CONV_WIDTH = 31
HALO = 32
ATT_BLOCK = 128
DEAD_LOG_WEIGHT = 110.0
N_DEV = 8
VMEM_LIMIT = 60 * 1024 * 1024

ADAM_LR = 0.001
ADAM_B1 = 0.9
ADAM_B2 = 0.999
ADAM_EPS = 1e-08
ADAM_WD = 0.01
ADAM_STEP = 10

NN = (((1,), (0,)), ((), ()))
NT = (((1,), (1,)), ((), ()))
TN = (((0,), (0,)), ((), ()))
MESH = pl.DeviceIdType.MESH
ANY = pl.BlockSpec(memory_space=pl.ANY)


def _pick(n, cands):
    for c in cands:
        if c <= n and n % c == 0:
            return c
    return n


def _params(grid):
    return pltpu.CompilerParams(dimension_semantics=("arbitrary",) * len(grid), vmem_limit_bytes=VMEM_LIMIT)


def _dot(a, b, dims=NN):
    return lax.dot_general(a, b, dims, preferred_element_type=F32)


class _Riders:
    def __init__(self, kind, arrays):
        self.gather = kind == "gather"
        self.arrays = list(arrays)
        self.n = len(self.arrays)
        self.out_shapes = _gather_shapes(self.arrays) if self.gather else _exchange_shapes(self.arrays)

    def call(self, body, name, grid, in_specs, out_shapes, out_specs, scratch, operands):
        n, n_in, n_out, n_scr = self.n, len(in_specs), len(out_shapes), len(scratch)

        def wrapped(*refs):
            ins, r_in = refs[:n_in], refs[n_in:n_in + n]
            outs, r_out = refs[n_in + n:n_in + n + n_out], refs[n_in + n + n_out:n_in + 2 * n + n_out]
            scr, sems = refs[n_in + 2 * n + n_out:n_in + 2 * n + n_out + n_scr], refs[n_in + 2 * n + n_out + n_scr:]
            ids = [pl.program_id(a) for a in range(len(grid))]
            first = functools.reduce(jnp.logical_and, [i == 0 for i in ids])
            last = functools.reduce(jnp.logical_and, [i == g - 1 for i, g in zip(ids, grid)])
            plan = (_GatherPlan if self.gather else _ChipExchangePlan)(r_in, r_out, *sems)
            pl.when(first)(plan.start)
            body(*ins, *outs, *scr)

            @pl.when(last)
            def _():
                if self.gather:
                    for a in range(n):
                        plan.forward(a)
                plan.finish()

        res = pl.pallas_call(
            wrapped, name=name, grid=grid, in_specs=list(in_specs) + [ANY] * n,
            out_shape=tuple(out_shapes) + tuple(self.out_shapes), out_specs=tuple(out_specs) + (ANY,) * n,
            scratch_shapes=list(scratch) + _comm_sems(n, 7 if self.gather else 3), compiler_params=_params(grid),
        )(*operands, *self.arrays)
        return tuple(res[:n_out]), list(res[n_out:])


def _call(riders, body, name, grid, in_specs, out_shapes, out_specs, scratch, operands):
    if riders is not None:
        return riders.call(body, name, grid, in_specs, out_shapes, out_specs, scratch, operands)
    return tuple(pl.pallas_call(body, name=name, grid=grid, in_specs=list(in_specs), out_shape=tuple(out_shapes),
                                out_specs=tuple(out_specs), scratch_shapes=list(scratch),
                                compiler_params=_params(grid))(*operands))


def _matmul(name, pairs, specs, out_shape, out_spec, grid, dims, acc_shape, riders=None):
    n = len(pairs)
    nk = grid[-1]
    kaxis = len(grid) - 1

    def body(*refs):
        o_ref, acc = refs[2 * n], refs[2 * n + 1]
        tot = None
        for p in range(n):
            d = _dot(refs[2 * p][...], refs[2 * p + 1][...], dims)
            tot = d if tot is None else tot + d
        if nk == 1:
            o_ref[...] = tot.astype(o_ref.dtype)
            return
        k = pl.program_id(kaxis)

        @pl.when(k == 0)
        def _():
            acc[...] = tot

        @pl.when(k > 0)
        def _():
            acc[...] += tot

        @pl.when(k == nk - 1)
        def _():
            o_ref[...] = acc[...].astype(o_ref.dtype)

    ops, in_specs = [], []
    for (a, b), (sa, sb) in zip(pairs, specs):
        ops += [a, b]
        in_specs += [sa, sb]
    res = _call(riders, body, name, grid, in_specs, [out_shape], [out_spec],
                [pltpu.VMEM(acc_shape if nk > 1 else (8, 128), F32)], ops)
    return res[0] if riders is None else (res[0][0], res[1])


ROW_TILES = (1408, 1024, 512, 256, 128)


def _mm_nt(a, w, name, out_dtype=F32, riders=None):
    m, kk = a.shape
    n = w.shape[0]
    tm = _pick(m, ROW_TILES)
    tn = _pick(n, (1024, 768, 512, 256, 128))
    grid = (m // tm, n // tn, 1)
    sa = pl.BlockSpec((tm, kk), lambda i, j, k: (i, 0))
    sb = pl.BlockSpec((tn, kk), lambda i, j, k: (j, 0))
    return _matmul(name, [(a, w)], [(sa, sb)], jax.ShapeDtypeStruct((m, n), out_dtype),
                   pl.BlockSpec((tm, tn), lambda i, j, k: (i, j)), grid, NT, (tm, tn), riders)


def _mm_nn(pairs, name, out_dtype=F32, riders=None):
    m, kc = pairs[0][0].shape
    n = pairs[0][1].shape[1]
    tm = _pick(m, ROW_TILES if len(pairs) == 1 else (704,) + ROW_TILES[2:])
    tn = _pick(n, (1024, 512, 256, 128))
    tk = _pick(kc, (2048, 1536, 1408, 1024, 768, 512, 256, 128))
    grid = (m // tm, n // tn, kc // tk)
    sa = pl.BlockSpec((tm, tk), lambda i, j, k: (i, k))
    sb = pl.BlockSpec((tk, tn), lambda i, j, k: (k, j))
    return _matmul(name, pairs, [(sa, sb)] * len(pairs), jax.ShapeDtypeStruct((m, n), out_dtype),
                   pl.BlockSpec((tm, tn), lambda i, j, k: (i, j)), grid, NN, (tm, tn), riders)


def _mm_tn(a, b, name, out_dtype=BF16, riders=None):
    l, m = a.shape
    n = b.shape[1]
    tm = _pick(m, (768, 512, 256, 128))
    tn = _pick(n, (1024, 512, 256, 128))
    grid = (m // tm, n // tn, 1)
    sa = pl.BlockSpec((l, tm), lambda i, j, t: (0, i))
    sb = pl.BlockSpec((l, tn), lambda i, j, t: (0, j))
    return _matmul(name, [(a, b)], [(sa, sb)], jax.ShapeDtypeStruct((m, n), out_dtype),
                   pl.BlockSpec((tm, tn), lambda i, j, t: (i, j)), grid, TN, (tm, tn), riders)


def _ffn_up(u, wg, wu, riders=None):
    m, kk = u.shape
    n = wg.shape[0]
    tm = _pick(m, ROW_TILES)
    tn = _pick(n, (512, 256, 128))
    grid = (m // tm, n // tn)

    def body(u_ref, wg_ref, wu_ref, a_ref, b_ref, act_ref):
        uu = u_ref[...]
        a = _dot(uu, wg_ref[...], NT)
        b = _dot(uu, wu_ref[...], NT)
        a_ref[...] = a
        b_ref[...] = b
        act_ref[...] = (a * jax.nn.sigmoid(a) * b).astype(BF16)

    wspec = pl.BlockSpec((tn, kk), lambda i, j: (j, 0))
    ospec = pl.BlockSpec((tm, tn), lambda i, j: (i, j))
    return _call(riders, body, "ffn_up", grid, [pl.BlockSpec((tm, kk), lambda i, j: (i, 0)), wspec, wspec],
                 [jax.ShapeDtypeStruct((m, n), F32), jax.ShapeDtypeStruct((m, n), F32), jax.ShapeDtypeStruct((m, n), BF16)],
                 [ospec, ospec, ospec], [], [u, wg, wu])


def _ffn_bwd_act(dff, wd, a, b, riders=None):
    m, kk = dff.shape
    n = wd.shape[0]
    tm = _pick(m, ROW_TILES)
    tn = _pick(n, (512, 256, 128))
    grid = (m // tm, n // tn)

    def body(d_ref, w_ref, a_ref, b_ref, da_ref, db_ref):
        dact = _dot(d_ref[...], w_ref[...], NT)
        aa, bb = a_ref[...], b_ref[...]
        sg = jax.nn.sigmoid(aa)
        db_ref[...] = (dact * aa * sg).astype(BF16)
        da_ref[...] = (dact * bb * sg * (1.0 + aa * (1.0 - sg))).astype(BF16)

    tspec = pl.BlockSpec((tm, tn), lambda i, j: (i, j))
    return _call(riders, body, "ffn_bwd_act", grid,
                 [pl.BlockSpec((tm, kk), lambda i, j: (i, 0)), pl.BlockSpec((tn, kk), lambda i, j: (j, 0)), tspec, tspec],
                 [jax.ShapeDtypeStruct((m, n), BF16), jax.ShapeDtypeStruct((m, n), BF16)], [tspec, tspec], [], [dff, wd, a, b])


def _rstd(x):
    return lax.rsqrt(jnp.mean(x * x, axis=-1, keepdims=True) + EPS)


def _row_tile(m):
    return _pick(m, (384, 256, 128))


def _rms_fwd(x, g, name):
    m, d = x.shape
    tr = _row_tile(m)
    grid = (m // tr,)

    def body(x_ref, g_ref, o_ref):
        xx = x_ref[...]
        o_ref[...] = (xx * _rstd(xx) * g_ref[...]).astype(BF16)

    row = pl.BlockSpec((tr, d), lambda i: (i, 0))
    return pl.pallas_call(body, name=name, grid=grid, out_shape=jax.ShapeDtypeStruct((m, d), BF16),
                          in_specs=[row, pl.BlockSpec((1, d), lambda i: (0, 0))], out_specs=row,
                          compiler_params=_params(grid))(x, g)


def _resid_rms_fwd(h, z, g, name):
    m, d = h.shape
    tr = _row_tile(m)
    grid = (m // tr,)

    def body(h_ref, z_ref, g_ref, o_ref):
        zz = z_ref[...]
        o_ref[...] = h_ref[...] + zz * _rstd(zz) * g_ref[...]

    row = pl.BlockSpec((tr, d), lambda i: (i, 0))
    return pl.pallas_call(body, name=name, grid=grid, out_shape=jax.ShapeDtypeStruct((m, d), F32),
                          in_specs=[row, row, pl.BlockSpec((1, d), lambda i: (0, 0))], out_specs=row,
                          compiler_params=_params(grid))(h, z, g)


def _rms_bwd_math(x, g, dy):
    r = _rstd(x)
    dyg = dy * g
    dx = r * dyg - x * (r * r * r) * jnp.mean(x * dyg, axis=-1, keepdims=True)
    dg = jnp.sum(dy * x * r, axis=0, keepdims=True)
    return dx, dg


def _rms_bwd(x, g, dy, dres, out_dtype, name):
    m, d = x.shape
    tr = _row_tile(m)
    grid = (m // tr,)
    has_res = dres is not None

    def body(*refs):
        x_ref, g_ref, dy_ref = refs[:3]
        dx_ref, dg_ref = refs[-2:]
        dx, dg = _rms_bwd_math(x_ref[...], g_ref[...], dy_ref[...].astype(F32))
        if has_res:
            dx = dx + refs[3][...]
        dx_ref[...] = dx.astype(out_dtype)

        @pl.when(pl.program_id(0) == 0)
        def _():
            dg_ref[...] = jnp.zeros_like(dg_ref)

        dg_ref[...] += dg

    row = pl.BlockSpec((tr, d), lambda i: (i, 0))
    vec = pl.BlockSpec((1, d), lambda i: (0, 0))
    ops = [x, g, dy] + ([dres] if has_res else [])
    return pl.pallas_call(
        body, name=name, grid=grid,
        out_shape=(jax.ShapeDtypeStruct((m, d), out_dtype), jax.ShapeDtypeStruct((1, d), F32)),
        in_specs=[row, vec, row] + ([row] if has_res else []), out_specs=(row, vec),
        compiler_params=_params(grid))(*ops)


def _merge_fwd(oa, op, oc, g):
    m = oa.shape[0]
    da, dp, dc = oa.shape[1], op.shape[1], oc.shape[1]
    d = da + dp + dc
    tr = _row_tile(m)
    grid = (m // tr,)

    def body(a_ref, p_ref, c_ref, g_ref, o_ref):
        off = 0
        for ref, w in ((a_ref, da), (p_ref, dp), (c_ref, dc)):
            xx = ref[...]
            o_ref[:, off:off + w] = (xx * _rstd(xx) * g_ref[:, off:off + w]).astype(BF16)
            off += w

    specs = [pl.BlockSpec((tr, w), lambda i: (i, 0)) for w in (da, dp, dc)]
    return pl.pallas_call(body, name="merge_fwd", grid=grid, out_shape=jax.ShapeDtypeStruct((m, d), BF16),
                          in_specs=specs + [pl.BlockSpec((1, d), lambda i: (0, 0))],
                          out_specs=pl.BlockSpec((tr, d), lambda i: (i, 0)), compiler_params=_params(grid))(oa, op, oc, g)


def _merge_bwd(oa, op, oc, g, dmerged):
    m = oa.shape[0]
    da, dp, dc = oa.shape[1], op.shape[1], oc.shape[1]
    d = da + dp + dc
    tr = _row_tile(m)
    grid = (m // tr,)

    def body(a_ref, p_ref, c_ref, g_ref, dm_ref, da_ref, dp_ref, dc_ref, dg_ref):
        @pl.when(pl.program_id(0) == 0)
        def _():
            dg_ref[...] = jnp.zeros_like(dg_ref)

        off = 0
        for ref, oref, w in ((a_ref, da_ref, da), (p_ref, dp_ref, dp), (c_ref, dc_ref, dc)):
            dx, dg = _rms_bwd_math(ref[...], g_ref[:, off:off + w], dm_ref[:, off:off + w])
            oref[...] = dx.astype(oref.dtype)
            dg_ref[:, off:off + w] += dg
            off += w

    specs = [pl.BlockSpec((tr, w), lambda i: (i, 0)) for w in (da, dp, dc)]
    vec = pl.BlockSpec((1, d), lambda i: (0, 0))
    return pl.pallas_call(
        body, name="merge_bwd", grid=grid,
        out_shape=(jax.ShapeDtypeStruct((m, da), F32), jax.ShapeDtypeStruct((m, dp), F32),
                   jax.ShapeDtypeStruct((m, dc), BF16), jax.ShapeDtypeStruct((1, d), F32)),
        in_specs=specs + [vec, pl.BlockSpec((tr, d), lambda i: (i, 0))], out_specs=tuple(specs) + (vec,),
        compiler_params=_params(grid))(oa, op, oc, g, dmerged)


def _loss_grad(h, target, n_meta, n_tok):
    m, d = h.shape
    tr = _row_tile(m)
    grid = (m // tr,)

    def body(h_ref, t_ref, loss_ref, dh_ref):
        i = pl.program_id(0)
        row = i * tr + lax.broadcasted_iota(jnp.int32, (tr, 1), 0)
        live = jnp.logical_and(row >= n_meta, row < n_meta + n_tok)
        diff = jnp.where(live, h_ref[...] - t_ref[...], 0.0)
        dh_ref[...] = diff * (1.0 / d)

        @pl.when(i == 0)
        def _():
            loss_ref[...] = jnp.zeros_like(loss_ref)

        loss_ref[...] += jnp.sum(jnp.sum(diff * diff, axis=1, keepdims=True), axis=0, keepdims=True) * (0.5 / d)

    row = pl.BlockSpec((tr, d), lambda i: (i, 0))
    return pl.pallas_call(
        body, name="loss_grad", grid=grid,
        out_shape=(jax.ShapeDtypeStruct((1, 1), F32), jax.ShapeDtypeStruct((m, d), F32)),
        in_specs=[row, row], out_specs=(pl.BlockSpec((1, 1), lambda i: (0, 0)), row),
        compiler_params=_params(grid))(h, target)


def _split_dot(x, tmat):
    hi = x.astype(BF16)
    lo = (x - hi.astype(F32)).astype(BF16)
    return _dot(hi, tmat) + _dot(lo, tmat)


def _softplus(z):
    return jnp.maximum(z, 0.0) + jnp.log(1.0 + jnp.exp(-jnp.abs(z)))


def _attn_tiles(lp):
    t = ATT_BLOCK
    nb = lp // t
    u = 3 if nb % 3 == 0 else (2 if nb % 2 == 0 else 1)
    return t, nb, u


def _attn_masks(tq, t, u):
    row = lax.broadcasted_iota(jnp.int32, (tq, t), 0)
    col = lax.broadcasted_iota(jnp.int32, (tq, t), 1)
    masks = [col + r * t < row for r in range(u)]
    a = lax.broadcasted_iota(jnp.int32, (t, 2 * t), 0)
    s = lax.broadcasted_iota(jnp.int32, (t, 2 * t), 1)
    after = jnp.logical_or(a > s, s >= t).astype(BF16)
    before = jnp.logical_or(a < s, s >= t).astype(BF16)
    return masks, after, before


def _service_heads(arrs, nh):
    sizes = [a.size * a.dtype.itemsize for a in arrs]
    heads, done = [], 0
    for sz in sizes:
        done += sz
        heads.append(min(nh - 1, int(1.1 * nh * done / sum(sizes)) + 1))
    return heads


def _attn_fwd(q, k, v, gather=()):
    nh, lp, dh = q.shape
    t, nb, u = _attn_tiles(lp)
    tq = u * t
    nq = nb // u
    grid = (nh, nq)
    ng = len(gather)
    service = _service_heads(gather, nh)

    def body(q_ref, k_ref, v_ref, *rest):
        o_ref, tot_ref, cnt_ref = rest[ng:ng + 3]
        h, i = pl.program_id(0), pl.program_id(1)
        if ng:
            plan = _GatherPlan(rest[:ng], rest[ng + 3:2 * ng + 3], *rest[2 * ng + 3:])
            pl.when(jnp.logical_and(h == 0, i == 0))(plan.start)
        qb = q_ref[...]
        masks, after, _ = _attn_masks(tq, t, u)
        o_ref[...] = jnp.zeros_like(o_ref)
        tot_ref[...] = jnp.zeros_like(tot_ref)

        def blocks(base, masked):
            outs, sums = [], []
            for r in reversed(range(u)):
                start = pl.multiple_of((base + r) * t, t)
                kb = k_ref[pl.ds(start, t), :]
                z = _dot(qb, kb, NT)
                sp = _softplus(z)
                lnb = jnp.where(masks[r], -sp, 0.0) if masked else -sp
                res = _split_dot(lnb, after)
                outs.append((start, z - sp + res[:, :t], r))
                sums.append(res[:, t:])
            cs = tot_ref[...]
            for (start, logw, r), rs in zip(outs, sums):
                w = jnp.exp(logw + cs)
                if masked:
                    w = jnp.where(masks[r], w, 0.0)
                o_ref[...] += _dot(w.astype(BF16), v_ref[pl.ds(start, t), :])
                cs = cs + rs
            tot_ref[...] = cs

        blocks(i * u, True)

        def live():
            return jnp.max(tot_ref[...]) > -DEAD_LOG_WEIGHT

        def step(state):
            n, _ = state
            blocks((i - 1 - n) * u, False)
            return n + 1, live()

        n_done, _ = lax.while_loop(lambda s: jnp.logical_and(s[0] < i, s[1]), step, (jnp.int32(0), live()))
        cnt_ref[h, i] = n_done.astype(F32)

        if ng:
            for a, head in enumerate(service):
                pl.when(jnp.logical_and(h == head, i == nq - 1))(functools.partial(plan.forward, a))
            pl.when(jnp.logical_and(h == nh - 1, i == nq - 1))(plan.finish)

    blk = pl.BlockSpec((None, tq, dh), lambda h, i: (h, i, 0))
    full = pl.BlockSpec((None, lp, dh), lambda h, i: (h, 0, 0))
    return pl.pallas_call(
        body, name="attn_fwd_gather" if ng else "attn_fwd", grid=grid,
        out_shape=(jax.ShapeDtypeStruct((nh, lp, dh), F32), jax.ShapeDtypeStruct((nh, lp, t), F32),
                   jax.ShapeDtypeStruct((nh, nq), F32)) + _gather_shapes(gather),
        in_specs=[blk, full, full] + [ANY] * ng,
        out_specs=(blk, pl.BlockSpec((None, tq, t), lambda h, i: (h, i, 0)), pl.BlockSpec(memory_space=pltpu.SMEM))
        + (ANY,) * ng,
        scratch_shapes=_comm_sems(ng, 7) if ng else [],
        compiler_params=_params(grid))(q, k, v, *gather)


def _attn_bwd(q, qt, k, v, do, dot_, tot, cnt, scale, exchange=()):
    nh, lp, dh = q.shape
    t, nb, u = _attn_tiles(lp)
    tq = u * t
    nq = nb // u
    grid = (nh, nq)
    ne = len(exchange)

    def body(q_ref, qt_ref, k_ref, v_ref, do_ref, dot_ref, tot_ref, cnt_ref, *rest):
        dq_ref, dk_ref, dv_ref = rest[ne:ne + 3]
        p_ref, g_ref = rest[2 * ne + 3:2 * ne + 5]
        h, i = pl.program_id(0), pl.program_id(1)
        if ne:
            plan = _ChipExchangePlan(rest[:ne], rest[ne + 3:2 * ne + 3], *rest[2 * ne + 5:])
            pl.when(jnp.logical_and(h == 0, i == 0))(plan.start)

        @pl.when(i == 0)
        def _():
            dk_ref[...] = jnp.zeros_like(dk_ref)
            dv_ref[...] = jnp.zeros_like(dv_ref)

        dq_ref[...] = jnp.zeros_like(dq_ref)
        p_ref[...] = jnp.zeros_like(p_ref)
        g_ref[...] = jnp.zeros_like(g_ref)
        qb, qtb, dob, dotb = q_ref[...], qt_ref[...], do_ref[...], dot_ref[...]
        total = tot_ref[...]
        masks, after, before = _attn_masks(tq, t, u)

        def blocks(base, masked):
            part = []
            for r in range(u):
                j = base + r
                start = pl.multiple_of(j * t, t)
                kb = k_ref[pl.ds(start, t), :]
                vb = v_ref[pl.ds(start, t), :]
                z = _dot(qb, kb, NT)
                sp = _softplus(z)
                lnb = jnp.where(masks[r], -sp, 0.0) if masked else -sp
                res = _split_dot(lnb, after)
                part.append((j, kb, z - sp, res[:, :t], res[:, t:], _dot(dob, vb, NT), r))
            pfx = p_ref[...]
            gp = g_ref[...]
            for j, kb, logsig, later_in, rs, dw, r in part:
                pfx = pfx + rs
                w = jnp.exp(logsig + later_in + (total - pfx))
                if masked:
                    w = jnp.where(masks[r], w, 0.0)
                dlogw = w * dw
                res2 = _split_dot(dlogw, before)
                dz = dlogw - jnp.exp(logsig) * (dlogw + res2[:, :t] + gp)
                if masked:
                    dz = jnp.where(masks[r], dz, 0.0)
                gp = gp + res2[:, t:]
                dzb = dz.astype(BF16)
                dq_ref[...] += _dot(dzb, kb)
                dk_ref[j] += _dot(qtb, dzb)
                dv_ref[j] += _dot(dotb, w.astype(BF16))
            p_ref[...] = pfx
            g_ref[...] = gp

        def step(n, carry):
            blocks(n * u, False)
            return carry

        walked = jnp.clip(cnt_ref[h, i].astype(jnp.int32), 0, i)
        lax.fori_loop(i - walked, i, step, 0)
        blocks(i * u, True)
        dq_ref[...] = dq_ref[...] * scale
        if ne:
            pl.when(jnp.logical_and(h == nh - 1, i == nq - 1))(plan.finish)

    blk = pl.BlockSpec((None, tq, dh), lambda h, i: (h, i, 0))
    blk_t = pl.BlockSpec((None, dh, tq), lambda h, i: (h, 0, i))
    full = pl.BlockSpec((None, lp, dh), lambda h, i: (h, 0, 0))
    acc = pl.BlockSpec((None, nb, dh, t), lambda h, i: (h, 0, 0, 0))
    acc_shape = jax.ShapeDtypeStruct((nh, nb, dh, t), F32)
    return pl.pallas_call(
        body, name="attn_bwd_exchange" if ne else "attn_bwd", grid=grid,
        out_shape=(jax.ShapeDtypeStruct((nh, lp, dh), F32), acc_shape, acc_shape) + _exchange_shapes(exchange),
        in_specs=[blk, blk_t, full, full, blk, blk_t, pl.BlockSpec((None, tq, t), lambda h, i: (h, i, 0)),
                  pl.BlockSpec(memory_space=pltpu.SMEM)] + [ANY] * ne,
        out_specs=(blk, acc, acc) + (ANY,) * ne,
        scratch_shapes=[pltpu.VMEM((tq, t), F32), pltpu.VMEM((tq, t), F32)] + (_comm_sems(ne, 3) if ne else []),
        compiler_params=_params(grid))(q, qt, k, v, do, dot_, tot, cnt, *exchange)


def _halo_specs(tm, width, nt):
    per = tm // HALO
    prev = pl.BlockSpec((HALO, width), lambda i: (jnp.maximum(i * per - 1, 0), 0))
    nxt = pl.BlockSpec((HALO, width), lambda i: (jnp.minimum((i + 1) * per, nt * per - 1), 0))
    return prev, nxt


def _shift_down(x, k):
    return x if k == 0 else pltpu.roll(x, k, axis=0)


def _shift_up(x, k):
    return x if k == 0 else pltpu.roll(x, x.shape[0] - k, axis=0)


def _pool_fwd(u, wp, scale, layer):
    m, dp = u.shape
    g = dp // len(POOL_WINDOWS)
    tm = _row_tile(m)
    nt = m // tm
    grid = (nt,)
    prev, _ = _halo_specs(tm, dp, nt)

    def body(u_ref, h_ref, w_ref, s_ref, o_ref):
        i = pl.program_id(0)
        halo = jnp.where(i > 0, h_ref[...], 0.0)
        s = jnp.concatenate([halo, u_ref[...]], axis=0)
        tpos = i * tm + lax.broadcasted_iota(jnp.int32, (tm, 1), 0)
        for gi, win in enumerate(POOL_WINDOWS):
            s = s + _shift_down(s, win // 2)
            cols = slice(gi * g, (gi + 1) * g)
            cnt = jnp.minimum(tpos + 1, win).astype(F32)
            pooled = s[HALO:, cols] / cnt - u_ref[:, cols]
            o_ref[:, cols] = _dot(pooled.astype(BF16), w_ref[gi]) * s_ref[:, cols]

    row = pl.BlockSpec((tm, dp), lambda i: (i, 0))
    return pl.pallas_call(
        body, name="pool_fwd", grid=grid, out_shape=jax.ShapeDtypeStruct((m, dp), F32),
        in_specs=[row, prev, pl.BlockSpec((None, len(POOL_WINDOWS), g, g), lambda i: (layer, 0, 0, 0)),
                  pl.BlockSpec((1, dp), lambda i: (0, 0))],
        out_specs=row, compiler_params=_params(grid))(u, u, wp, scale)


def _pool_bwd(u, dmix, wp, scale, layer):
    m, dp = u.shape
    ng = len(POOL_WINDOWS)
    g = dp // ng
    tm = _row_tile(m)
    nt = m // tm
    grid = (nt,)
    prev, nxt = _halo_specs(tm, dp, nt)

    def body(u_ref, h_ref, dm_ref, dmn_ref, w_ref, s_ref, du_ref, dw_ref, ds_ref):
        i = pl.program_id(0)

        @pl.when(i == 0)
        def _():
            dw_ref[...] = jnp.zeros_like(dw_ref)
            ds_ref[...] = jnp.zeros_like(ds_ref)

        halo = jnp.where(i > 0, h_ref[...], 0.0)
        s = jnp.concatenate([halo, u_ref[...]], axis=0)
        dmn = jnp.where(i < nt - 1, dmn_ref[...], 0.0)
        dmc = jnp.concatenate([dm_ref[...], dmn], axis=0)
        tpos = i * tm + lax.broadcasted_iota(jnp.int32, (tm, 1), 0)
        tpos_x = i * tm + lax.broadcasted_iota(jnp.int32, (tm + HALO, 1), 0)
        for gi, win in enumerate(POOL_WINDOWS):
            s = s + _shift_down(s, win // 2)
            cols = slice(gi * g, (gi + 1) * g)
            cnt = jnp.minimum(tpos + 1, win).astype(F32)
            pooled = (s[HALO:, cols] / cnt - u_ref[:, cols]).astype(BF16)
            wg = w_ref[gi]
            pm = _dot(pooled, wg)
            ds_ref[:, cols] += jnp.sum(dm_ref[:, cols] * pm, axis=0, keepdims=True)
            dpm = (dmc[:, cols] * s_ref[:, cols]).astype(BF16)
            dw_ref[gi] += _dot(pooled, dpm[:tm], TN)
            dpool = _dot(dpm, wg, NT)
            f = dpool / jnp.minimum(tpos_x + 1, win).astype(F32)
            step = 1
            while step < win:
                f = f + _shift_up(f, step)
                step *= 2
            du_ref[:, cols] = f[:tm] - dpool[:tm]

    row = pl.BlockSpec((tm, dp), lambda i: (i, 0))
    return pl.pallas_call(
        body, name="pool_bwd", grid=grid,
        out_shape=(jax.ShapeDtypeStruct((m, dp), F32), jax.ShapeDtypeStruct((ng, g, g), F32),
                   jax.ShapeDtypeStruct((1, dp), F32)),
        in_specs=[row, prev, row, nxt, pl.BlockSpec((None, ng, g, g), lambda i: (layer, 0, 0, 0)),
                  pl.BlockSpec((1, dp), lambda i: (0, 0))],
        out_specs=(row, pl.BlockSpec((ng, g, g), lambda i: (0, 0, 0)), pl.BlockSpec((1, dp), lambda i: (0, 0))),
        compiler_params=_params(grid))(u, u, dmix, dmix, wp, scale)


def _conv_taps(u, w_ref, rows):
    y = None
    for k in range(CONV_WIDTH):
        term = _shift_down(u, CONV_WIDTH - 1 - k)[HALO:HALO + rows] * w_ref[k:k + 1, :]
        y = term if y is None else y + term
    return y


def _layernorm_stats(y):
    mu = jnp.mean(y, axis=-1, keepdims=True)
    yc = y - mu
    rstd = lax.rsqrt(jnp.mean(yc * yc, axis=-1, keepdims=True) + EPS)
    return yc * rstd, rstd


def _conv_fwd(uc, wdw, b, lg, lb):
    m, c2 = uc.shape
    c = c2 // 2
    tm = _row_tile(m)
    nt = m // tm
    grid = (nt,)
    prev, _ = _halo_specs(tm, c2, nt)

    def body(x_ref, h_ref, w_ref, b_ref, g_ref, bb_ref, o_ref):
        i = pl.program_id(0)
        halo = jnp.where(i > 0, h_ref[...], 0.0)
        xc = jnp.concatenate([halo, x_ref[...]], axis=0)
        u = xc[:, :c] * jax.nn.sigmoid(xc[:, c:])
        y = _conv_taps(u, w_ref, tm) + b_ref[...]
        xhat, _ = _layernorm_stats(y)
        ln = xhat * g_ref[...] + bb_ref[...]
        o_ref[...] = (ln * jax.nn.sigmoid(ln)).astype(BF16)

    vec = pl.BlockSpec((1, c), lambda i: (0, 0))
    return pl.pallas_call(
        body, name="conv_fwd", grid=grid, out_shape=jax.ShapeDtypeStruct((m, c), BF16),
        in_specs=[pl.BlockSpec((tm, c2), lambda i: (i, 0)), prev, pl.BlockSpec((HALO, c), lambda i: (0, 0)), vec, vec, vec],
        out_specs=pl.BlockSpec((tm, c), lambda i: (i, 0)), compiler_params=_params(grid))(uc, uc, wdw, b, lg, lb)


def _conv_bwd(uc, ds, wdw, b, lg, lb):
    m, c2 = uc.shape
    c = c2 // 2
    tm = _row_tile(m)
    nt = m // tm
    grid = (nt,)
    prev, nxt = _halo_specs(tm, c2, nt)
    _, nxt_c = _halo_specs(tm, c, nt)

    def body(x_ref, hp_ref, hn_ref, ds_ref, dsn_ref, w_ref, b_ref, g_ref, bb_ref,
             dx_ref, dw_ref, db_ref, dg_ref, dbb_ref):
        i = pl.program_id(0)

        @pl.when(i == 0)
        def _():
            for ref in (dw_ref, db_ref, dg_ref, dbb_ref):
                ref[...] = jnp.zeros_like(ref)

        hp = jnp.where(i > 0, hp_ref[...], 0.0)
        xc = jnp.concatenate([hp, x_ref[...], hn_ref[...]], axis=0)
        sg = jax.nn.sigmoid(xc[:, c:])
        u = xc[:, :c] * sg
        rows = tm + HALO
        y = _conv_taps(u, w_ref, rows) + b_ref[...]
        xhat, rstd = _layernorm_stats(y)
        ln = xhat * g_ref[...] + bb_ref[...]
        sl = jax.nn.sigmoid(ln)
        dsn = jnp.where(i < nt - 1, dsn_ref[...], 0.0)
        dsx = jnp.concatenate([ds_ref[...], dsn], axis=0)
        dln = dsx * sl * (1.0 + ln * (1.0 - sl))
        dxh = dln * g_ref[...]
        dy = rstd * (dxh - jnp.mean(dxh, axis=-1, keepdims=True)
                     - xhat * jnp.mean(dxh * xhat, axis=-1, keepdims=True))
        dyt = dy[:tm]
        dg_ref[...] += jnp.sum(dln[:tm] * xhat[:tm], axis=0, keepdims=True)
        dbb_ref[...] += jnp.sum(dln[:tm], axis=0, keepdims=True)
        db_ref[...] += jnp.sum(dyt, axis=0, keepdims=True)
        du = None
        for k in range(CONV_WIDTH):
            lag = CONV_WIDTH - 1 - k
            dw_ref[k:k + 1, :] += jnp.sum(dyt * _shift_down(u, lag)[HALO:HALO + tm], axis=0, keepdims=True)
            term = _shift_up(dy, lag)[:tm] * w_ref[k:k + 1, :]
            du = term if du is None else du + term
        a_t = xc[HALO:HALO + tm, :c]
        sg_t = sg[HALO:HALO + tm]
        dx_ref[:, :c] = du * sg_t
        dx_ref[:, c:] = du * a_t * sg_t * (1.0 - sg_t)

    vec = pl.BlockSpec((1, c), lambda i: (0, 0))
    wsp = pl.BlockSpec((HALO, c), lambda i: (0, 0))
    vshape = jax.ShapeDtypeStruct((1, c), F32)
    return pl.pallas_call(
        body, name="conv_bwd", grid=grid,
        out_shape=(jax.ShapeDtypeStruct((m, c2), F32), jax.ShapeDtypeStruct((HALO, c), F32), vshape, vshape, vshape),
        in_specs=[pl.BlockSpec((tm, c2), lambda i: (i, 0)), prev, nxt, pl.BlockSpec((tm, c), lambda i: (i, 0)), nxt_c,
                  wsp, vec, vec, vec],
        out_specs=(pl.BlockSpec((tm, c2), lambda i: (i, 0)), wsp, vec, vec, vec),
        compiler_params=_params(grid))(uc, uc, uc, ds, ds, wdw, b, lg, lb)


def _mesh_pos():
    return lax.axis_index("x"), lax.axis_index("y"), lax.axis_index("c")


def _other_chips(x, y):
    return [(1 - x, y), (x, 1 - y), (1 - x, 1 - y)]


def _comm_sems(n, per):
    return [pltpu.SemaphoreType.DMA((n, per)), pltpu.SemaphoreType.DMA((n, per)), pltpu.SemaphoreType.DMA((n,))]


def _comm_call(body, name, arrs, out_shapes, n_sems):
    n = len(arrs)
    return pl.pallas_call(
        body, name=name, out_shape=out_shapes, in_specs=[ANY] * n, out_specs=tuple([ANY] * len(out_shapes)),
        scratch_shapes=_comm_sems(n, n_sems),
    )(*arrs)


class _GatherPlan:
    def __init__(self, ins, outs, send_sems, recv_sems, local_sems):
        self.ins, self.outs = ins, outs
        self.send_sems, self.recv_sems, self.local_sems = send_sems, recv_sems, local_sems
        x, y, c = _mesh_pos()
        self.c, self.me, self.sib = c, (x, y, c), (x, y, 1 - c)
        self.chips = _other_chips(x, y)

    def _slot(self, a, p):
        return self.outs[a].at[4 * p[0] + 2 * p[1] + p[2]]

    def _copy(self, a, k, block, to, own=False):
        return pltpu.make_async_remote_copy(
            src_ref=self.ins[a] if own else self._slot(a, block), dst_ref=self._slot(a, block),
            send_sem=self.send_sems.at[a, k], recv_sem=self.recv_sems.at[a, k], device_id=to, device_id_type=MESH)

    def _local(self, a):
        return pltpu.make_async_copy(self.ins[a], self._slot(a, self.me), self.local_sems.at[a])

    def start(self):
        for a in range(len(self.ins)):
            for j, chip in enumerate(self.chips):
                self._copy(a, 1 + j, self.me, (*chip, self.c), own=True).start()
            self._copy(a, 0, self.me, self.sib, own=True).start()
            self._local(a).start()

    def forward(self, a):
        for j, chip in enumerate(self.chips):
            self._copy(a, 1 + j, (*chip, self.c), self.me).wait_recv()
            self._copy(a, 4 + j, (*chip, self.c), self.sib).start()

    def finish(self):
        for a in range(len(self.ins)):
            self._copy(a, 0, self.sib, self.me).wait_recv()
            for j, chip in enumerate(self.chips):
                self._copy(a, 4 + j, (*chip, 1 - self.c), self.me).wait_recv()
        for a in range(len(self.ins)):
            self._copy(a, 0, self.me, self.sib, own=True).wait_send()
            for j, chip in enumerate(self.chips):
                self._copy(a, 1 + j, self.me, (*chip, self.c), own=True).wait_send()
                self._copy(a, 4 + j, (*chip, self.c), self.sib).wait_send()
            self._local(a).wait()


def _gather_shapes(arrs):
    return tuple(jax.ShapeDtypeStruct((N_DEV,) + a.shape, a.dtype) for a in arrs)


def _all_gather(arrs, name):
    n = len(arrs)

    def body(*refs):
        plan = _GatherPlan(refs[:n], refs[n:2 * n], *refs[2 * n:])
        plan.start()
        for a in range(n):
            plan.forward(a)
        plan.finish()

    return _comm_call(body, name, arrs, _gather_shapes(arrs), 7)


class _ChipExchangePlan:
    def __init__(self, ins, outs, send_sems, recv_sems, local_sems):
        self.ins, self.outs = ins, outs
        self.send_sems, self.recv_sems, self.local_sems = send_sems, recv_sems, local_sems
        x, y, c = _mesh_pos()
        self.pos, self.mine = (x, y, c), 2 * x + y
        self.chips = _other_chips(x, y)

    def _send(self, a, j):
        chip = self.chips[j]
        return pltpu.make_async_remote_copy(
            src_ref=self.ins[a].at[2 * chip[0] + chip[1]], dst_ref=self.outs[a].at[self.mine],
            send_sem=self.send_sems.at[a, j], recv_sem=self.recv_sems.at[a, j], device_id=(*chip, self.pos[2]),
            device_id_type=MESH)

    def _landing(self, a, j):
        chip = self.chips[j]
        ref = self.outs[a].at[2 * chip[0] + chip[1]]
        return pltpu.make_async_remote_copy(src_ref=ref, dst_ref=ref, send_sem=self.send_sems.at[a, j],
                                            recv_sem=self.recv_sems.at[a, j], device_id=self.pos, device_id_type=MESH)

    def _local(self, a):
        return pltpu.make_async_copy(self.ins[a].at[self.mine], self.outs[a].at[self.mine], self.local_sems.at[a])

    def start(self):
        for a in range(len(self.ins)):
            for j in range(3):
                self._send(a, j).start()
            self._local(a).start()

    def finish(self):
        for a in range(len(self.ins)):
            for j in range(3):
                self._landing(a, j).wait_recv()
        for a in range(len(self.ins)):
            for j in range(3):
                self._send(a, j).wait_send()
            self._local(a).wait()


def _pair_tile(kk):
    return _pick(kk, (1024, 704, 576, 512, 256, 128, 64))


def _pair_exchange(parts, core, name):
    _, kk, n = parts.shape
    tr = _pair_tile(kk)
    nr = kk // tr
    grid = (4, nr)

    def body(c_ref, src_ref, recv_ref, send_sems, recv_sems):
        q, r = pl.program_id(0), pl.program_id(1)
        x, y, c = _mesh_pos()
        step = q * nr + r
        slot = step % 2
        dst = recv_ref.at[pl.ds(pl.multiple_of(step * tr, tr), tr), :]
        cp = pltpu.make_async_remote_copy(src_ref=src_ref, dst_ref=dst, send_sem=send_sems.at[slot],
                                          recv_sem=recv_sems.at[slot], device_id=(x, y, 1 - c), device_id_type=MESH)
        cp.start()
        cp.wait_send()
        cp.wait_recv()

    spec = pltpu.PrefetchScalarGridSpec(
        num_scalar_prefetch=1, grid=grid,
        in_specs=[pl.BlockSpec((tr, n), lambda q, r, c: ((2 * q + 1 - c[0]) * nr + r, 0))],
        out_specs=ANY, scratch_shapes=[pltpu.SemaphoreType.DMA((2,)), pltpu.SemaphoreType.DMA((2,))])
    recv = pl.pallas_call(body, name=name, grid_spec=spec, out_shape=jax.ShapeDtypeStruct((4 * kk, n), parts.dtype),
                          compiler_params=_params(grid))(core, parts.reshape(N_DEV * kk, n))
    return recv.reshape(4, kk, n)


def _exchange_shapes(parts):
    return tuple(jax.ShapeDtypeStruct(p.shape, p.dtype) for p in parts)


def _reduce_chip_exchange(parts, name):
    n = len(parts)

    def body(*refs):
        plan = _ChipExchangePlan(refs[:n], refs[n:2 * n], *refs[2 * n:])
        plan.start()
        plan.finish()

    return _comm_call(body, name, parts, _exchange_shapes(parts), 3)


def _pair_add(parts, recv, core, name):
    _, kk, n = recv.shape
    tr = _pair_tile(kk)
    grid = (4, kk // tr)

    def body(c_ref, a_ref, b_ref, o_ref):
        o_ref[...] = (a_ref[...].astype(F32) + b_ref[...].astype(F32)).astype(o_ref.dtype)

    out = pl.BlockSpec((None, tr, n), lambda q, r, c: (q, r, 0))
    spec = pltpu.PrefetchScalarGridSpec(
        num_scalar_prefetch=1, grid=grid,
        in_specs=[pl.BlockSpec((None, None, tr, n), lambda q, r, c: (q, c[0], r, 0)), out], out_specs=out)
    return pl.pallas_call(body, name=name, grid_spec=spec, out_shape=jax.ShapeDtypeStruct(recv.shape, recv.dtype),
                          compiler_params=_params(grid))(core, parts.reshape(4, 2, kk, n), recv)


def _adamw_math(w, g, m, v):
    m2 = ADAM_B1 * m + (1.0 - ADAM_B1) * g
    v2 = ADAM_B2 * v + (1.0 - ADAM_B2) * (g * g)
    m_hat = m2 / (1.0 - ADAM_B1 ** ADAM_STEP)
    v_hat = v2 / (1.0 - ADAM_B2 ** ADAM_STEP)
    delta = -ADAM_LR * (m_hat / (jnp.sqrt(v_hat) + ADAM_EPS) + ADAM_WD * w)
    return delta, m2, v2


def _adamw_sharded(parts, w, m, v, name):
    nl, kk, n = w.shape
    tr = _pick(kk, (256, 128, 64)) if n <= 1024 else _pick(kk, (64, 32))
    grid = (nl, kk // tr)

    def body(*refs):
        p_refs = refs[:nl]
        w_ref, m_ref, v_ref, g_ref, d_ref, m2_ref, v2_ref = refs[nl:]
        layer = pl.program_id(0)
        g = None
        for l, p_ref in enumerate(p_refs):
            s = p_ref[0].astype(F32)
            for q in range(1, 4):
                s = s + p_ref[q].astype(F32)
            g = s if g is None else jnp.where(layer == l, s, g)
        d, m2, v2 = _adamw_math(w_ref[...], g, m_ref[...], v_ref[...])
        g_ref[...] = g
        d_ref[...] = d
        m2_ref[...] = m2
        v2_ref[...] = v2

    spec = pl.BlockSpec((None, tr, n), lambda l, r: (l, r, 0))
    shp = jax.ShapeDtypeStruct(w.shape, F32)
    p_specs = [pl.BlockSpec((4, tr, n), lambda l, r, k=k: (0, jnp.where(l == k, r, 0), 0)) for k in range(nl)]
    return pl.pallas_call(
        body, name=name, grid=grid, out_shape=(shp, shp, shp, shp), in_specs=p_specs + [spec, spec, spec],
        out_specs=(spec, spec, spec, spec), compiler_params=_params(grid))(*parts, w, m, v)


def _sum_slots(g8):
    _, r, _ = g8.shape

    def body(g_ref, o_ref):
        s = g_ref[0]
        for d in range(1, N_DEV):
            s = s + g_ref[d]
        o_ref[...] = s

    return pl.pallas_call(body, name="sum_slots", out_shape=jax.ShapeDtypeStruct((r, 128), F32))(g8)


def _adamw_flat(w, g, m, v):
    def body(w_ref, g_ref, m_ref, v_ref, d_ref, m2_ref, v2_ref):
        d, m2, v2 = _adamw_math(w_ref[...], g_ref[...], m_ref[...], v_ref[...])
        d_ref[...] = d
        m2_ref[...] = m2
        v2_ref[...] = v2

    shp = jax.ShapeDtypeStruct(w.shape, F32)
    return pl.pallas_call(body, name="adamw_flat", out_shape=(shp, shp, shp))(w, g, m, v)


def _pack(arrs):
    flat = jnp.concatenate([a.reshape(-1) for a in arrs])
    pad = (-flat.shape[0]) % 1024
    return jnp.pad(flat, (0, pad)).reshape(-1, 128)


def _unpack(buf, like):
    flat = buf.reshape(-1)
    out, off = [], 0
    for a in like:
        out.append(flat[off:off + a.size].reshape(a.shape))
        off += a.size
    return out


def _heads(x2d, scale=None):
    lp, da = x2d.shape
    xh = x2d.reshape(lp, N_HEADS, da // N_HEADS).transpose(1, 0, 2)
    if scale is not None:
        xh = xh * scale
    return xh


def _unheads(xh):
    nh, lp, dh = xh.shape
    return xh.transpose(1, 0, 2).reshape(lp, nh * dh)


def _hosted(hosts, site, kind, fn):
    if hosts and site in hosts:
        arrays, done = hosts[site]
        res, got = fn(_Riders(kind, arrays))
        done(got)
        return res
    return fn(None)


def _layer_fwd(h, l, W, P, hosts=None):
    d = h.shape[1]
    da, dp = d // 2, d // 4
    dh = da // N_HEADS
    sv = {"h": h}
    u1 = _rms_fwd(h, P["pre_mix_g"][l:l + 1], "rms_pre_mix")
    proj = _mm_nt(u1, W["in"], "mm_in")
    qh = _heads(proj[:, :da], dh ** -0.5).astype(BF16)
    kh = _heads(proj[:, da:2 * da]).astype(BF16)
    vh = _heads(proj[:, 2 * da:3 * da]).astype(BF16)
    u_pool = proj[:, 3 * da:3 * da + dp]
    u_conv = proj[:, 3 * da + dp:]
    if hosts and "attn" in hosts:
        arrs, done = hosts["attn"]
        oh, tot, cnt, *gathered = _attn_fwd(qh, kh, vh, arrs)
        done(gathered)
    else:
        oh, tot, cnt = _attn_fwd(qh, kh, vh)
    o_attn = _unheads(oh)
    o_pool = _pool_fwd(u_pool, W["pool"], P["pool_scale"][l:l + 1], l)
    s_conv = _conv_fwd(u_conv, W["dw"][l], P["b_dw"][l:l + 1], P["conv_ln_g"][l:l + 1], P["conv_ln_b"][l:l + 1])
    o_conv = _mm_nn([(s_conv, W["pw"])], "mm_pw")
    merged = _merge_fwd(o_attn, o_pool, o_conv, P["mix_out_g"][l:l + 1])
    z1 = _mm_nn([(merged, W["out"])], "mm_out")
    h1 = _resid_rms_fwd(h, z1, P["post_mix_g"][l:l + 1], "resid_post_mix")
    u2 = _rms_fwd(h1, P["pre_ffn_g"][l:l + 1], "rms_pre_ffn")
    a_s, b_s, act_s = _hosted(hosts, "ffn_up", "gather", lambda r: _ffn_up(u2, W["gate"], W["up"], r))
    ff = _hosted(hosts, "mm_down", "gather", lambda r: _mm_nn([(act_s, W["down"])], "mm_down", riders=r))
    h2 = _resid_rms_fwd(h1, ff, P["post_ffn_g"][l:l + 1], "resid_post_ffn")
    sv.update(u1=u1, qh=qh, kh=kh, vh=vh, tot=tot, cnt=cnt, u_pool=u_pool, u_conv=u_conv, o_attn=o_attn, o_pool=o_pool,
              o_conv=o_conv, s_conv=s_conv, merged=merged, z1=z1, h1=h1, u2=u2, a_s=a_s, b_s=b_s, act_s=act_s, ff=ff)
    return h2, sv


EARLY = ("w_down", "w_gate", "w_up", "w_out", "w_pw")


def _layer_bwd(dh_out, l, W, P, sv, exchange_with_attn=None, hosts=None):
    d = dh_out.shape[1]
    da = d // 2
    dhd = da // N_HEADS
    lp = dh_out.shape[0]
    g = {}
    dff, g["post_ffn_g"] = _rms_bwd(sv["ff"], P["post_ffn_g"][l:l + 1], dh_out, None, BF16, "rms_bwd_post_ffn")
    da_s, db_s = _hosted(hosts, "ffn_bwd_act", "exchange",
                         lambda r: _ffn_bwd_act(dff, W["down"], sv["a_s"], sv["b_s"], r))
    g["w_down"] = _mm_tn(sv["act_s"], dff, "mm_dw_down")
    du2 = _hosted(hosts, "mm_d_u2", "exchange",
                  lambda r: _mm_nn([(da_s, W["gate"]), (db_s, W["up"])], "mm_d_u2", riders=r))
    g["w_gate"] = _hosted(hosts, "mm_dw_gate", "exchange", lambda r: _mm_tn(da_s, sv["u2"], "mm_dw_gate", riders=r))
    g["w_up"] = _mm_tn(db_s, sv["u2"], "mm_dw_up")
    dh1, g["pre_ffn_g"] = _rms_bwd(sv["h1"], P["pre_ffn_g"][l:l + 1], du2, dh_out, F32, "rms_bwd_pre_ffn")
    dz1, g["post_mix_g"] = _rms_bwd(sv["z1"], P["post_mix_g"][l:l + 1], dh1, None, BF16, "rms_bwd_post_mix")
    dmerged = _mm_nt(dz1, W["out"], "mm_d_merged")
    g["w_out"] = _mm_tn(sv["merged"], dz1, "mm_dw_out")
    d_oa, d_op, d_oc, g["mix_out_g"] = _merge_bwd(sv["o_attn"], sv["o_pool"], sv["o_conv"], P["mix_out_g"][l:l + 1], dmerged)
    d_s = _mm_nt(d_oc, W["pw"], "mm_d_sconv")
    g["w_pw"] = _mm_tn(sv["s_conv"], d_oc, "mm_dw_pw")
    d_uc, dwdw, g["b_dw"], g["conv_ln_g"], g["conv_ln_b"] = _conv_bwd(
        sv["u_conv"], d_s, W["dw"][l], P["b_dw"][l:l + 1], P["conv_ln_g"][l:l + 1], P["conv_ln_b"][l:l + 1])
    g["w_dw"] = dwdw[:CONV_WIDTH]
    d_up, g["w_pool"], g["pool_scale"] = _pool_bwd(sv["u_pool"], d_op, W["pool"], P["pool_scale"][l:l + 1], l)
    doh = _heads(d_oa).astype(BF16)
    riders = exchange_with_attn(g) if exchange_with_attn is not None else []
    dqh, dkt, dvt, *exchanged = _attn_bwd(sv["qh"], sv["qh"].transpose(0, 2, 1), sv["kh"], sv["vh"], doh,
                                          doh.transpose(0, 2, 1), sv["tot"], sv["cnt"], dhd ** -0.5, riders)
    dk2, dv2 = (a.transpose(1, 3, 0, 2).reshape(lp, da) for a in (dkt, dvt))
    dproj = jnp.concatenate([_unheads(dqh), dk2, dv2, d_up, d_uc], axis=1).astype(BF16)
    du1 = _mm_nn([(dproj, W["in"])], "mm_d_u1")
    g["w_in"] = _mm_tn(dproj, sv["u1"], "mm_dw_in")
    dh0, g["pre_mix_g"] = _rms_bwd(sv["h"], P["pre_mix_g"][l:l + 1], du1, dh1, F32, "rms_bwd_pre_mix")
    return dh0, g, exchanged


SHARDED = ("w_in", "w_gate", "w_up", "w_down", "w_out", "w_pw")
SMALL = ("pre_mix_g", "w_pool", "pool_scale", "b_dw", "conv_ln_g", "conv_ln_b", "mix_out_g", "post_mix_g",
         "pre_ffn_g", "post_ffn_g", "w_dw", "meta_tokens")
ORDER = ("meta_tokens", "pre_mix_g", "w_in", "w_pool", "pool_scale", "w_dw", "b_dw", "conv_ln_g", "conv_ln_b", "w_pw",
         "mix_out_g", "w_out", "post_mix_g", "pre_ffn_g", "w_gate", "w_up", "w_down", "post_ffn_g")


def kernel(x, meta_tokens, pre_mix_g, w_in, w_pool, pool_scale, w_dw, b_dw, conv_ln_g, conv_ln_b, w_pw, mix_out_g, w_out, post_mix_g, pre_ffn_g, w_gate, w_up, w_down, post_ffn_g, loss_target, m_meta_tokens, m_pre_mix_g, m_w_in, m_w_pool, m_pool_scale, m_w_dw, m_b_dw, m_conv_ln_g, m_conv_ln_b, m_w_pw, m_mix_out_g, m_w_out, m_post_mix_g, m_pre_ffn_g, m_w_gate, m_w_up, m_w_down, m_post_ffn_g, v_meta_tokens, v_pre_mix_g, v_w_in, v_w_pool, v_pool_scale, v_w_dw, v_b_dw, v_conv_ln_g, v_conv_ln_b, v_w_pw, v_mix_out_g, v_w_out, v_post_mix_g, v_pre_ffn_g, v_w_gate, v_w_up, v_w_down, v_post_ffn_g):
    P = dict(meta_tokens=meta_tokens, pre_mix_g=pre_mix_g, w_in=w_in, w_pool=w_pool, pool_scale=pool_scale, w_dw=w_dw,
             b_dw=b_dw, conv_ln_g=conv_ln_g, conv_ln_b=conv_ln_b, w_pw=w_pw, mix_out_g=mix_out_g, w_out=w_out,
             post_mix_g=post_mix_g, pre_ffn_g=pre_ffn_g, w_gate=w_gate, w_up=w_up, w_down=w_down, post_ffn_g=post_ffn_g)
    M = dict(meta_tokens=m_meta_tokens, pre_mix_g=m_pre_mix_g, w_in=m_w_in, w_pool=m_w_pool, pool_scale=m_pool_scale,
             w_dw=m_w_dw, b_dw=m_b_dw, conv_ln_g=m_conv_ln_g, conv_ln_b=m_conv_ln_b, w_pw=m_w_pw, mix_out_g=m_mix_out_g,
             w_out=m_w_out, post_mix_g=m_post_mix_g, pre_ffn_g=m_pre_ffn_g, w_gate=m_w_gate, w_up=m_w_up,
             w_down=m_w_down, post_ffn_g=m_post_ffn_g)
    V = dict(meta_tokens=v_meta_tokens, pre_mix_g=v_pre_mix_g, w_in=v_w_in, w_pool=v_w_pool, pool_scale=v_pool_scale,
             w_dw=v_w_dw, b_dw=v_b_dw, conv_ln_g=v_conv_ln_g, conv_ln_b=v_conv_ln_b, w_pw=v_w_pw, mix_out_g=v_mix_out_g,
             w_out=v_w_out, post_mix_g=v_post_mix_g, pre_ffn_g=v_pre_ffn_g, w_gate=v_w_gate, w_up=v_w_up,
             w_down=v_w_down, post_ffn_g=v_post_ffn_g)
    xi, yi, ci = _mesh_pos()
    dev = 4 * xi + 2 * yi + ci
    n_tok, d = x.shape[1], x.shape[2]
    n_meta = meta_tokens.shape[0]
    n_layers = w_in.shape[0]
    c = d // 4
    l_real = n_meta + n_tok
    lp = -(-l_real // ATT_BLOCK) * ATT_BLOCK

    TRANSPOSED = ("w_in", "w_gate", "w_up")
    shard = {k: (jnp.swapaxes(P[k], 1, 2) if k in TRANSPOSED else P[k]) for k in SHARDED}

    def bf16_shard(k, l):
        return shard[k][l].astype(BF16)

    def whole(gathered):
        return gathered.reshape(-1, gathered.shape[-1])

    g_in0, g_dw, g_meta = _all_gather([bf16_shard("w_in", 0), w_dw, meta_tokens], "gather_first")
    wdw_full = g_dw.transpose(1, 2, 0, 3).reshape(n_layers, CONV_WIDTH, c)
    common = {"pool": w_pool.astype(BF16), "dw": jnp.pad(wdw_full, ((0, 0), (0, HALO - CONV_WIDTH), (0, 0)))}
    W = [dict(common) for _ in range(n_layers)]
    W[0]["in"] = whole(g_in0)
    meta_full = g_meta.transpose(1, 0, 2).reshape(n_meta, d)
    def gather_site(keys):
        def done(gathered):
            for (k, l), g in zip(keys, gathered):
                W[l][k[2:]] = whole(g)
        return [bf16_shard(k, l) for k, l in keys], done

    def fwd_hosts(l):
        own = ("w_up", "w_down") if l > 0 else ("w_pw", "w_out", "w_gate", "w_up", "w_down")
        hosts = {"attn": gather_site([(k, l) for k in own])}
        if l + 1 < n_layers:
            hosts["ffn_up"] = gather_site([(k, l + 1) for k in ("w_in", "w_pw", "w_out")])
            hosts["mm_down"] = gather_site([("w_gate", l + 1)])
        return hosts

    h = jnp.concatenate([meta_full, x[0], jnp.zeros((lp - l_real, d), F32)], axis=0)
    saved = []
    for l in range(n_layers):
        h, sv = _layer_fwd(h, l, W[l], P, fwd_hosts(l))
        saved.append(sv)
    target = jnp.pad(loss_target[0], ((n_meta, lp - l_real), (0, 0)))
    loss_part, dh = _loss_grad(h, target, n_meta, n_tok)
    loss = lax.psum(loss_part[0, 0], ("x", "y", "c"))

    core = jnp.reshape(ci, (1,)).astype(jnp.int32)
    by_chip = {}

    def pair_reduce(g, keys, l):
        res = {}
        for k in keys:
            p = g[k].reshape(N_DEV, g[k].shape[0] // N_DEV, g[k].shape[1])
            recv = _pair_exchange(p, core, f"pair_exchange_{k}_{l}")
            res[k] = _pair_add(p, recv, core, f"pair_add_{k}_{l}")
        return res

    def exchange_site(sums, keys, l):
        def done(exchanged):
            by_chip.update({(l, k): e for k, e in zip(keys, exchanged)})
        return [sums[k] for k in keys], done

    def ride(g):
        sums = pair_reduce(g, EARLY, 0)
        return [sums[k] for k in EARLY]

    grads = [None] * n_layers
    hosts = None
    for l in reversed(range(n_layers)):
        dh, grads[l], exchanged = _layer_bwd(dh, l, W[l], P, saved[l], ride if l == 0 else None, hosts)
        if l == 0:
            by_chip.update({(0, k): e for k, e in zip(EARLY, exchanged)})
            sums = pair_reduce(grads[0], ("w_in",), 0)
            (by_chip[(0, "w_in")],) = _reduce_chip_exchange([sums["w_in"]], "reduce_chips_w_in")
        else:
            sums = pair_reduce(grads[l], SHARDED, l)
            hosts = {"ffn_bwd_act": exchange_site(sums, ("w_down",), l),
                     "mm_d_u2": exchange_site(sums, ("w_gate", "w_up"), l),
                     "mm_dw_gate": exchange_site(sums, ("w_in", "w_out", "w_pw"), l)}
    grad_x = dh[n_meta:l_real][None]
    out = {}
    for k in SHARDED:
        view = (lambda a: jnp.swapaxes(a, 1, 2)) if k in TRANSPOSED else (lambda a: a)
        res = _adamw_sharded([by_chip[(l, k)] for l in range(n_layers)], view(P[k]), view(M[k]), view(V[k]), "adamw_" + k)
        out[k] = tuple(view(r) for r in res)

    small_parts = []
    for k in SMALL:
        if k == "meta_tokens":
            small_parts.append(dh[:n_meta])
        else:
            small_parts.append(jnp.stack([grads[l][k].reshape(P[k].shape[1:] if k != "w_dw" else (CONV_WIDTH, c))
                                          for l in range(n_layers)], axis=0))
    (g8,) = _all_gather([_pack(small_parts)], "gather_small_grads")
    g_small = dict(zip(SMALL, _unpack(_sum_slots(g8), small_parts)))
    g_small["w_dw"] = lax.dynamic_slice_in_dim(g_small["w_dw"], dev * w_dw.shape[2], w_dw.shape[2], axis=2)
    g_small["meta_tokens"] = lax.dynamic_slice_in_dim(g_small["meta_tokens"], dev * meta_tokens.shape[1],
                                                      meta_tokens.shape[1], axis=1)
    like = [P[k] for k in SMALL]
    res = _adamw_flat(_pack(like), _pack([g_small[k] for k in SMALL]), _pack([M[k] for k in SMALL]),
                      _pack([V[k] for k in SMALL]))
    res = [_unpack(r, like) for r in res]
    for i, k in enumerate(SMALL):
        out[k] = (g_small[k], res[0][i], res[1][i], res[2][i])

    return (loss, grad_x, *[out[k][0] for k in ORDER], *[out[k][1] for k in ORDER],
            *[out[k][2] for k in ORDER], *[out[k][3] for k in ORDER])
```

```python
import functools

import jax
import jax.numpy as jnp
from jax import lax
from jax.experimental import pallas as pl
from jax.experimental.pallas import tpu as pltpu

F32 = jnp.float32
BF16 = jnp.bfloat16
EPS = 1e-6
N_HEADS = 16
POOL_WINDOWS = (2, 4, 8, 16)
CONV_WIDTH = 31
HALO = 32
ATT_BLOCK = 128
DEAD_LOG_WEIGHT = 110.0
N_DEV = 8
VMEM_LIMIT = 60 * 1024 * 1024

ADAM_LR = 0.001
ADAM_B1 = 0.9
ADAM_B2 = 0.999
ADAM_EPS = 1e-08
ADAM_WD = 0.01
ADAM_STEP = 10

NN = (((1,), (0,)), ((), ()))
NT = (((1,), (1,)), ((), ()))
TN = (((0,), (0,)), ((), ()))
MESH = pl.DeviceIdType.MESH
ANY = pl.BlockSpec(memory_space=pl.ANY)


def _pick(n, cands):
    for c in cands:
        if c <= n and n % c == 0:
            return c
    return n


def _params(grid):
    return pltpu.CompilerParams(dimension_semantics=("arbitrary",) * len(grid), vmem_limit_bytes=VMEM_LIMIT)


def _dot(a, b, dims=NN):
    return lax.dot_general(a, b, dims, preferred_element_type=F32)


class _Riders:
    def __init__(self, kind, arrays):
        self.gather = kind == "gather"
        self.arrays = list(arrays)
        self.n = len(self.arrays)
        self.out_shapes = _gather_shapes(self.arrays) if self.gather else _exchange_shapes(self.arrays)

    def call(self, body, name, grid, in_specs, out_shapes, out_specs, scratch, operands):
        n, n_in, n_out, n_scr = self.n, len(in_specs), len(out_shapes), len(scratch)

        def wrapped(*refs):
            ins, r_in = refs[:n_in], refs[n_in:n_in + n]
            outs, r_out = refs[n_in + n:n_in + n + n_out], refs[n_in + n + n_out:n_in + 2 * n + n_out]
            scr, sems = refs[n_in + 2 * n + n_out:n_in + 2 * n + n_out + n_scr], refs[n_in + 2 * n + n_out + n_scr:]
            ids = [pl.program_id(a) for a in range(len(grid))]
            first = functools.reduce(jnp.logical_and, [i == 0 for i in ids])
            last = functools.reduce(jnp.logical_and, [i == g - 1 for i, g in zip(ids, grid)])
            plan = (_GatherPlan if self.gather else _ChipExchangePlan)(r_in, r_out, *sems)
            pl.when(first)(plan.start)
            body(*ins, *outs, *scr)

            @pl.when(last)
            def _():
                if self.gather:
                    for a in range(n):
                        plan.forward(a)
                plan.finish()

        res = pl.pallas_call(
            wrapped, name=name, grid=grid, in_specs=list(in_specs) + [ANY] * n,
            out_shape=tuple(out_shapes) + tuple(self.out_shapes), out_specs=tuple(out_specs) + (ANY,) * n,
            scratch_shapes=list(scratch) + _comm_sems(n, 7 if self.gather else 3), compiler_params=_params(grid),
        )(*operands, *self.arrays)
        return tuple(res[:n_out]), list(res[n_out:])


def _call(riders, body, name, grid, in_specs, out_shapes, out_specs, scratch, operands):
    if riders is not None:
        return riders.call(body, name, grid, in_specs, out_shapes, out_specs, scratch, operands)
    return tuple(pl.pallas_call(body, name=name, grid=grid, in_specs=list(in_specs), out_shape=tuple(out_shapes),
                                out_specs=tuple(out_specs), scratch_shapes=list(scratch),
                                compiler_params=_params(grid))(*operands))


def _matmul(name, pairs, specs, out_shape, out_spec, grid, dims, acc_shape, riders=None):
    n = len(pairs)
    nk = grid[-1]
    kaxis = len(grid) - 1

    def body(*refs):
        o_ref, acc = refs[2 * n], refs[2 * n + 1]
        tot = None
        for p in range(n):
            d = _dot(refs[2 * p][...], refs[2 * p + 1][...], dims)
            tot = d if tot is None else tot + d
        if nk == 1:
            o_ref[...] = tot.astype(o_ref.dtype)
            return
        k = pl.program_id(kaxis)

        @pl.when(k == 0)
        def _():
            acc[...] = tot

        @pl.when(k > 0)
        def _():
            acc[...] += tot

        @pl.when(k == nk - 1)
        def _():
            o_ref[...] = acc[...].astype(o_ref.dtype)

    ops, in_specs = [], []
    for (a, b), (sa, sb) in zip(pairs, specs):
        ops += [a, b]
        in_specs += [sa, sb]
    res = _call(riders, body, name, grid, in_specs, [out_shape], [out_spec],
                [pltpu.VMEM(acc_shape if nk > 1 else (8, 128), F32)], ops)
    return res[0] if riders is None else (res[0][0], res[1])


ROW_TILES = (1408, 1024, 512, 256, 128)


def _mm_nt(a, w, name, out_dtype=F32, riders=None):
    m, kk = a.shape
    n = w.shape[0]
    tm = _pick(m, ROW_TILES)
    tn = _pick(n, (1024, 768, 512, 256, 128))
    grid = (m // tm, n // tn, 1)
    sa = pl.BlockSpec((tm, kk), lambda i, j, k: (i, 0))
    sb = pl.BlockSpec((tn, kk), lambda i, j, k: (j, 0))
    return _matmul(name, [(a, w)], [(sa, sb)], jax.ShapeDtypeStruct((m, n), out_dtype),
                   pl.BlockSpec((tm, tn), lambda i, j, k: (i, j)), grid, NT, (tm, tn), riders)


def _mm_nn(pairs, name, out_dtype=F32, riders=None):
    m, kc = pairs[0][0].shape
    n = pairs[0][1].shape[1]
    tm = _pick(m, ROW_TILES if len(pairs) == 1 else (704,) + ROW_TILES[2:])
    tn = _pick(n, (1024, 512, 256, 128))
    tk = _pick(kc, (2048, 1536, 1408, 1024, 768, 512, 256, 128))
    grid = (m // tm, n // tn, kc // tk)
    sa = pl.BlockSpec((tm, tk), lambda i, j, k: (i, k))
    sb = pl.BlockSpec((tk, tn), lambda i, j, k: (k, j))
    return _matmul(name, pairs, [(sa, sb)] * len(pairs), jax.ShapeDtypeStruct((m, n), out_dtype),
                   pl.BlockSpec((tm, tn), lambda i, j, k: (i, j)), grid, NN, (tm, tn), riders)


def _mm_tn(a, b, name, out_dtype=BF16, riders=None):
    l, m = a.shape
    n = b.shape[1]
    tm = _pick(m, (768, 512, 256, 128))
    tn = _pick(n, (1024, 512, 256, 128))
    grid = (m // tm, n // tn, 1)
    sa = pl.BlockSpec((l, tm), lambda i, j, t: (0, i))
    sb = pl.BlockSpec((l, tn), lambda i, j, t: (0, j))
    return _matmul(name, [(a, b)], [(sa, sb)], jax.ShapeDtypeStruct((m, n), out_dtype),
                   pl.BlockSpec((tm, tn), lambda i, j, t: (i, j)), grid, TN, (tm, tn), riders)


def _ffn_up(u, wg, wu, riders=None):
    m, kk = u.shape
    n = wg.shape[0]
    tm = _pick(m, ROW_TILES)
    tn = _pick(n, (512, 256, 128))
    grid = (m // tm, n // tn)

    def body(u_ref, wg_ref, wu_ref, a_ref, b_ref, act_ref):
        uu = u_ref[...]
        a = _dot(uu, wg_ref[...], NT)
        b = _dot(uu, wu_ref[...], NT)
        a_ref[...] = a
        b_ref[...] = b
        act_ref[...] = (a * jax.nn.sigmoid(a) * b).astype(BF16)

    wspec = pl.BlockSpec((tn, kk), lambda i, j: (j, 0))
    ospec = pl.BlockSpec((tm, tn), lambda i, j: (i, j))
    return _call(riders, body, "ffn_up", grid, [pl.BlockSpec((tm, kk), lambda i, j: (i, 0)), wspec, wspec],
                 [jax.ShapeDtypeStruct((m, n), F32), jax.ShapeDtypeStruct((m, n), F32), jax.ShapeDtypeStruct((m, n), BF16)],
                 [ospec, ospec, ospec], [], [u, wg, wu])


def _ffn_bwd_act(dff, wd, a, b, riders=None):
    m, kk = dff.shape
    n = wd.shape[0]
    tm = _pick(m, ROW_TILES)
    tn = _pick(n, (512, 256, 128))
    grid = (m // tm, n // tn)

    def body(d_ref, w_ref, a_ref, b_ref, da_ref, db_ref):
        dact = _dot(d_ref[...], w_ref[...], NT)
        aa, bb = a_ref[...], b_ref[...]
        sg = jax.nn.sigmoid(aa)
        db_ref[...] = (dact * aa * sg).astype(BF16)
        da_ref[...] = (dact * bb * sg * (1.0 + aa * (1.0 - sg))).astype(BF16)

    tspec = pl.BlockSpec((tm, tn), lambda i, j: (i, j))
    return _call(riders, body, "ffn_bwd_act", grid,
                 [pl.BlockSpec((tm, kk), lambda i, j: (i, 0)), pl.BlockSpec((tn, kk), lambda i, j: (j, 0)), tspec, tspec],
                 [jax.ShapeDtypeStruct((m, n), BF16), jax.ShapeDtypeStruct((m, n), BF16)], [tspec, tspec], [], [dff, wd, a, b])


def _rstd(x):
    return lax.rsqrt(jnp.mean(x * x, axis=-1, keepdims=True) + EPS)


def _row_tile(m):
    return _pick(m, (384, 256, 128))


def _rms_fwd(x, g, name):
    m, d = x.shape
    tr = _row_tile(m)
    grid = (m // tr,)

    def body(x_ref, g_ref, o_ref):
        xx = x_ref[...]
        o_ref[...] = (xx * _rstd(xx) * g_ref[...]).astype(BF16)

    row = pl.BlockSpec((tr, d), lambda i: (i, 0))
    return pl.pallas_call(body, name=name, grid=grid, out_shape=jax.ShapeDtypeStruct((m, d), BF16),
                          in_specs=[row, pl.BlockSpec((1, d), lambda i: (0, 0))], out_specs=row,
                          compiler_params=_params(grid))(x, g)


def _resid_rms_fwd(h, z, g, name):
    m, d = h.shape
    tr = _row_tile(m)
    grid = (m // tr,)

    def body(h_ref, z_ref, g_ref, o_ref):
        zz = z_ref[...]
        o_ref[...] = h_ref[...] + zz * _rstd(zz) * g_ref[...]

    row = pl.BlockSpec((tr, d), lambda i: (i, 0))
    return pl.pallas_call(body, name=name, grid=grid, out_shape=jax.ShapeDtypeStruct((m, d), F32),
                          in_specs=[row, row, pl.BlockSpec((1, d), lambda i: (0, 0))], out_specs=row,
                          compiler_params=_params(grid))(h, z, g)


def _rms_bwd_math(x, g, dy):
    r = _rstd(x)
    dyg = dy * g
    dx = r * dyg - x * (r * r * r) * jnp.mean(x * dyg, axis=-1, keepdims=True)
    dg = jnp.sum(dy * x * r, axis=0, keepdims=True)
    return dx, dg


def _rms_bwd(x, g, dy, dres, out_dtype, name):
    m, d = x.shape
    tr = _row_tile(m)
    grid = (m // tr,)
    has_res = dres is not None

    def body(*refs):
        x_ref, g_ref, dy_ref = refs[:3]
        dx_ref, dg_ref = refs[-2:]
        dx, dg = _rms_bwd_math(x_ref[...], g_ref[...], dy_ref[...].astype(F32))
        if has_res:
            dx = dx + refs[3][...]
        dx_ref[...] = dx.astype(out_dtype)

        @pl.when(pl.program_id(0) == 0)
        def _():
            dg_ref[...] = jnp.zeros_like(dg_ref)

        dg_ref[...] += dg

    row = pl.BlockSpec((tr, d), lambda i: (i, 0))
    vec = pl.BlockSpec((1, d), lambda i: (0, 0))
    ops = [x, g, dy] + ([dres] if has_res else [])
    return pl.pallas_call(
        body, name=name, grid=grid,
        out_shape=(jax.ShapeDtypeStruct((m, d), out_dtype), jax.ShapeDtypeStruct((1, d), F32)),
        in_specs=[row, vec, row] + ([row] if has_res else []), out_specs=(row, vec),
        compiler_params=_params(grid))(*ops)


def _merge_fwd(oa, op, oc, g):
    m = oa.shape[0]
    da, dp, dc = oa.shape[1], op.shape[1], oc.shape[1]
    d = da + dp + dc
    tr = _row_tile(m)
    grid = (m // tr,)

    def body(a_ref, p_ref, c_ref, g_ref, o_ref):
        off = 0
        for ref, w in ((a_ref, da), (p_ref, dp), (c_ref, dc)):
            xx = ref[...]
            o_ref[:, off:off + w] = (xx * _rstd(xx) * g_ref[:, off:off + w]).astype(BF16)
            off += w

    specs = [pl.BlockSpec((tr, w), lambda i: (i, 0)) for w in (da, dp, dc)]
    return pl.pallas_call(body, name="merge_fwd", grid=grid, out_shape=jax.ShapeDtypeStruct((m, d), BF16),
                          in_specs=specs + [pl.BlockSpec((1, d), lambda i: (0, 0))],
                          out_specs=pl.BlockSpec((tr, d), lambda i: (i, 0)), compiler_params=_params(grid))(oa, op, oc, g)


def _merge_bwd(oa, op, oc, g, dmerged):
    m = oa.shape[0]
    da, dp, dc = oa.shape[1], op.shape[1], oc.shape[1]
    d = da + dp + dc
    tr = _row_tile(m)
    grid = (m // tr,)

    def body(a_ref, p_ref, c_ref, g_ref, dm_ref, da_ref, dp_ref, dc_ref, dg_ref):
        @pl.when(pl.program_id(0) == 0)
        def _():
            dg_ref[...] = jnp.zeros_like(dg_ref)

        off = 0
        for ref, oref, w in ((a_ref, da_ref, da), (p_ref, dp_ref, dp), (c_ref, dc_ref, dc)):
            dx, dg = _rms_bwd_math(ref[...], g_ref[:, off:off + w], dm_ref[:, off:off + w])
            oref[...] = dx.astype(oref.dtype)
            dg_ref[:, off:off + w] += dg
            off += w

    specs = [pl.BlockSpec((tr, w), lambda i: (i, 0)) for w in (da, dp, dc)]
    vec = pl.BlockSpec((1, d), lambda i: (0, 0))
    return pl.pallas_call(
        body, name="merge_bwd", grid=grid,
        out_shape=(jax.ShapeDtypeStruct((m, da), F32), jax.ShapeDtypeStruct((m, dp), F32),
                   jax.ShapeDtypeStruct((m, dc), BF16), jax.ShapeDtypeStruct((1, d), F32)),
        in_specs=specs + [vec, pl.BlockSpec((tr, d), lambda i: (i, 0))], out_specs=tuple(specs) + (vec,),
        compiler_params=_params(grid))(oa, op, oc, g, dmerged)


def _loss_grad(h, target, n_meta, n_tok):
    m, d = h.shape
    tr = _row_tile(m)
    grid = (m // tr,)

    def body(h_ref, t_ref, loss_ref, dh_ref):
        i = pl.program_id(0)
        row = i * tr + lax.broadcasted_iota(jnp.int32, (tr, 1), 0)
        live = jnp.logical_and(row >= n_meta, row < n_meta + n_tok)
        diff = jnp.where(live, h_ref[...] - t_ref[...], 0.0)
        dh_ref[...] = diff * (1.0 / d)

        @pl.when(i == 0)
        def _():
            loss_ref[...] = jnp.zeros_like(loss_ref)

        loss_ref[...] += jnp.sum(jnp.sum(diff * diff, axis=1, keepdims=True), axis=0, keepdims=True) * (0.5 / d)

    row = pl.BlockSpec((tr, d), lambda i: (i, 0))
    return pl.pallas_call(
        body, name="loss_grad", grid=grid,
        out_shape=(jax.ShapeDtypeStruct((1, 1), F32), jax.ShapeDtypeStruct((m, d), F32)),
        in_specs=[row, row], out_specs=(pl.BlockSpec((1, 1), lambda i: (0, 0)), row),
        compiler_params=_params(grid))(h, target)


def _split_dot(x, tmat):
    hi = x.astype(BF16)
    lo = (x - hi.astype(F32)).astype(BF16)
    return _dot(hi, tmat) + _dot(lo, tmat)


def _softplus(z):
    return jnp.maximum(z, 0.0) + jnp.log(1.0 + jnp.exp(-jnp.abs(z)))


def _attn_tiles(lp):
    t = ATT_BLOCK
    nb = lp // t
    u = 3 if nb % 3 == 0 else (2 if nb % 2 == 0 else 1)
    return t, nb, u


def _attn_masks(tq, t, u):
    row = lax.broadcasted_iota(jnp.int32, (tq, t), 0)
    col = lax.broadcasted_iota(jnp.int32, (tq, t), 1)
    masks = [col + r * t < row for r in range(u)]
    a = lax.broadcasted_iota(jnp.int32, (t, 2 * t), 0)
    s = lax.broadcasted_iota(jnp.int32, (t, 2 * t), 1)
    after = jnp.logical_or(a > s, s >= t).astype(BF16)
    before = jnp.logical_or(a < s, s >= t).astype(BF16)
    return masks, after, before


def _service_heads(arrs, nh):
    sizes = [a.size * a.dtype.itemsize for a in arrs]
    heads, done = [], 0
    for sz in sizes:
        done += sz
        heads.append(min(nh - 1, int(1.1 * nh * done / sum(sizes)) + 1))
    return heads


def _attn_fwd(q, k, v, gather=()):
    nh, lp, dh = q.shape
    t, nb, u = _attn_tiles(lp)
    tq = u * t
    nq = nb // u
    grid = (nh, nq)
    ng = len(gather)
    service = _service_heads(gather, nh)

    def body(q_ref, k_ref, v_ref, *rest):
        o_ref, tot_ref, cnt_ref = rest[ng:ng + 3]
        h, i = pl.program_id(0), pl.program_id(1)
        if ng:
            plan = _GatherPlan(rest[:ng], rest[ng + 3:2 * ng + 3], *rest[2 * ng + 3:])
            pl.when(jnp.logical_and(h == 0, i == 0))(plan.start)
        qb = q_ref[...]
        masks, after, _ = _attn_masks(tq, t, u)
        o_ref[...] = jnp.zeros_like(o_ref)
        tot_ref[...] = jnp.zeros_like(tot_ref)

        def blocks(base, masked):
            outs = []
            for r in reversed(range(u)):
                lo = r * t if masked else 0
                start = pl.multiple_of((base + r) * t, t)
                kb = k_ref[pl.ds(start, t), :]
                z = _dot(qb[lo:], kb, NT)
                sp = _softplus(z)
                lnb = jnp.where(masks[r][lo:], -sp, 0.0) if masked else -sp
                res = _split_dot(lnb, after)
                outs.append((start, z - sp + res[:, :t], res[:, t:], r, lo))
            for start, logw, rs, r, lo in outs:
                cs = tot_ref[lo:, :]
                w = jnp.exp(logw + cs)
                if masked:
                    w = jnp.where(masks[r][lo:], w, 0.0)
                o_ref[lo:, :] += _dot(w.astype(BF16), v_ref[pl.ds(start, t), :])
                tot_ref[lo:, :] = cs + rs

        blocks(i * u, True)

        def live():
            return jnp.max(tot_ref[...]) > -DEAD_LOG_WEIGHT

        def step(state):
            n, _ = state
            blocks((i - 1 - n) * u, False)
            return n + 1, live()

        n_done, _ = lax.while_loop(lambda s: jnp.logical_and(s[0] < i, s[1]), step, (jnp.int32(0), live()))
        cnt_ref[h, i] = n_done.astype(F32)

        if ng:
            for a, head in enumerate(service):
                pl.when(jnp.logical_and(h == head, i == nq - 1))(functools.partial(plan.forward, a))
            pl.when(jnp.logical_and(h == nh - 1, i == nq - 1))(plan.finish)

    blk = pl.BlockSpec((None, tq, dh), lambda h, i: (h, i, 0))
    full = pl.BlockSpec((None, lp, dh), lambda h, i: (h, 0, 0))
    return pl.pallas_call(
        body, name="attn_fwd_gather" if ng else "attn_fwd", grid=grid,
        out_shape=(jax.ShapeDtypeStruct((nh, lp, dh), F32), jax.ShapeDtypeStruct((nh, lp, t), F32),
                   jax.ShapeDtypeStruct((nh, nq), F32)) + _gather_shapes(gather),
        in_specs=[blk, full, full] + [ANY] * ng,
        out_specs=(blk, pl.BlockSpec((None, tq, t), lambda h, i: (h, i, 0)), pl.BlockSpec(memory_space=pltpu.SMEM))
        + (ANY,) * ng,
        scratch_shapes=_comm_sems(ng, 7) if ng else [],
        compiler_params=_params(grid))(q, k, v, *gather)


def _attn_bwd(q, qt, k, v, do, dot_, tot, cnt, scale, exchange=()):
    nh, lp, dh = q.shape
    t, nb, u = _attn_tiles(lp)
    tq = u * t
    nq = nb // u
    grid = (nh, nq)
    ne = len(exchange)

    def body(q_ref, qt_ref, k_ref, v_ref, do_ref, dot_ref, tot_ref, cnt_ref, *rest):
        dq_ref, dk_ref, dv_ref = rest[ne:ne + 3]
        p_ref, g_ref = rest[2 * ne + 3:2 * ne + 5]
        h, i = pl.program_id(0), pl.program_id(1)
        if ne:
            plan = _ChipExchangePlan(rest[:ne], rest[ne + 3:2 * ne + 3], *rest[2 * ne + 5:])
            pl.when(jnp.logical_and(h == 0, i == 0))(plan.start)

        @pl.when(i == 0)
        def _():
            dk_ref[...] = jnp.zeros_like(dk_ref)
            dv_ref[...] = jnp.zeros_like(dv_ref)

        dq_ref[...] = jnp.zeros_like(dq_ref)
        p_ref[...] = jnp.zeros_like(p_ref)
        g_ref[...] = jnp.zeros_like(g_ref)
        qb, qtb, dob, dotb = q_ref[...], qt_ref[...], do_ref[...], dot_ref[...]
        total = tot_ref[...]
        masks, after, before = _attn_masks(tq, t, u)

        def blocks(base, masked):
            part = []
            for r in range(u):
                lo = r * t if masked else 0
                j = base + r
                start = pl.multiple_of(j * t, t)
                kb = k_ref[pl.ds(start, t), :]
                vb = v_ref[pl.ds(start, t), :]
                z = _dot(qb[lo:], kb, NT)
                sp = _softplus(z)
                lnb = jnp.where(masks[r][lo:], -sp, 0.0) if masked else -sp
                res = _split_dot(lnb, after)
                part.append((j, kb, z - sp, res[:, :t], res[:, t:], _dot(dob[lo:], vb, NT), r, lo))
            for j, kb, logsig, later_in, rs, dw, r, lo in part:
                pfx = p_ref[lo:, :] + rs
                w = jnp.exp(logsig + later_in + (total[lo:] - pfx))
                if masked:
                    w = jnp.where(masks[r][lo:], w, 0.0)
                dlogw = w * dw
                res2 = _split_dot(dlogw, before)
                gp = g_ref[lo:, :]
                dz = dlogw - jnp.exp(logsig) * (dlogw + res2[:, :t] + gp)
                if masked:
                    dz = jnp.where(masks[r][lo:], dz, 0.0)
                p_ref[lo:, :] = pfx
                g_ref[lo:, :] = gp + res2[:, t:]
                dzb = dz.astype(BF16)
                dq_ref[lo:, :] += _dot(dzb, kb)
                dk_ref[j] += _dot(qtb[:, lo:], dzb)
                dv_ref[j] += _dot(dotb[:, lo:], w.astype(BF16))

        def step(n, carry):
            blocks(n * u, False)
            return carry

        walked = jnp.clip(cnt_ref[h, i].astype(jnp.int32), 0, i)
        lax.fori_loop(i - walked, i, step, 0)
        blocks(i * u, True)
        dq_ref[...] = dq_ref[...] * scale
        if ne:
            pl.when(jnp.logical_and(h == nh - 1, i == nq - 1))(plan.finish)

    blk = pl.BlockSpec((None, tq, dh), lambda h, i: (h, i, 0))
    blk_t = pl.BlockSpec((None, dh, tq), lambda h, i: (h, 0, i))
    full = pl.BlockSpec((None, lp, dh), lambda h, i: (h, 0, 0))
    acc = pl.BlockSpec((None, nb, dh, t), lambda h, i: (h, 0, 0, 0))
    acc_shape = jax.ShapeDtypeStruct((nh, nb, dh, t), F32)
    return pl.pallas_call(
        body, name="attn_bwd_exchange" if ne else "attn_bwd", grid=grid,
        out_shape=(jax.ShapeDtypeStruct((nh, lp, dh), F32), acc_shape, acc_shape) + _exchange_shapes(exchange),
        in_specs=[blk, blk_t, full, full, blk, blk_t, pl.BlockSpec((None, tq, t), lambda h, i: (h, i, 0)),
                  pl.BlockSpec(memory_space=pltpu.SMEM)] + [ANY] * ne,
        out_specs=(blk, acc, acc) + (ANY,) * ne,
        scratch_shapes=[pltpu.VMEM((tq, t), F32), pltpu.VMEM((tq, t), F32)] + (_comm_sems(ne, 3) if ne else []),
        compiler_params=_params(grid))(q, qt, k, v, do, dot_, tot, cnt, *exchange)


def _halo_specs(tm, width, nt):
    per = tm // HALO
    prev = pl.BlockSpec((HALO, width), lambda i: (jnp.maximum(i * per - 1, 0), 0))
    nxt = pl.BlockSpec((HALO, width), lambda i: (jnp.minimum((i + 1) * per, nt * per - 1), 0))
    return prev, nxt


def _shift_down(x, k):
    return x if k == 0 else pltpu.roll(x, k, axis=0)


def _shift_up(x, k):
    return x if k == 0 else pltpu.roll(x, x.shape[0] - k, axis=0)


def _pool_fwd(u, wp, scale, layer):
    m, dp = u.shape
    g = dp // len(POOL_WINDOWS)
    tm = _row_tile(m)
    nt = m // tm
    grid = (nt,)
    prev, _ = _halo_specs(tm, dp, nt)

    def body(u_ref, h_ref, w_ref, s_ref, o_ref):
        i = pl.program_id(0)
        halo = jnp.where(i > 0, h_ref[...], 0.0)
        s = jnp.concatenate([halo, u_ref[...]], axis=0)
        tpos = i * tm + lax.broadcasted_iota(jnp.int32, (tm, 1), 0)
        for gi, win in enumerate(POOL_WINDOWS):
            s = s + _shift_down(s, win // 2)
            cols = slice(gi * g, (gi + 1) * g)
            cnt = jnp.minimum(tpos + 1, win).astype(F32)
            pooled = s[HALO:, cols] / cnt - u_ref[:, cols]
            o_ref[:, cols] = _dot(pooled.astype(BF16), w_ref[gi]) * s_ref[:, cols]

    row = pl.BlockSpec((tm, dp), lambda i: (i, 0))
    return pl.pallas_call(
        body, name="pool_fwd", grid=grid, out_shape=jax.ShapeDtypeStruct((m, dp), F32),
        in_specs=[row, prev, pl.BlockSpec((None, len(POOL_WINDOWS), g, g), lambda i: (layer, 0, 0, 0)),
                  pl.BlockSpec((1, dp), lambda i: (0, 0))],
        out_specs=row, compiler_params=_params(grid))(u, u, wp, scale)


def _pool_bwd(u, dmix, wp, scale, layer):
    m, dp = u.shape
    ng = len(POOL_WINDOWS)
    g = dp // ng
    tm = _row_tile(m)
    nt = m // tm
    grid = (nt,)
    prev, nxt = _halo_specs(tm, dp, nt)

    def body(u_ref, h_ref, dm_ref, dmn_ref, w_ref, s_ref, du_ref, dw_ref, ds_ref):
        i = pl.program_id(0)

        @pl.when(i == 0)
        def _():
            dw_ref[...] = jnp.zeros_like(dw_ref)
            ds_ref[...] = jnp.zeros_like(ds_ref)

        halo = jnp.where(i > 0, h_ref[...], 0.0)
        s = jnp.concatenate([halo, u_ref[...]], axis=0)
        dmn = jnp.where(i < nt - 1, dmn_ref[...], 0.0)
        dmc = jnp.concatenate([dm_ref[...], dmn], axis=0)
        tpos = i * tm + lax.broadcasted_iota(jnp.int32, (tm, 1), 0)
        tpos_x = i * tm + lax.broadcasted_iota(jnp.int32, (tm + HALO, 1), 0)
        for gi, win in enumerate(POOL_WINDOWS):
            s = s + _shift_down(s, win // 2)
            cols = slice(gi * g, (gi + 1) * g)
            cnt = jnp.minimum(tpos + 1, win).astype(F32)
            pooled = (s[HALO:, cols] / cnt - u_ref[:, cols]).astype(BF16)
            wg = w_ref[gi]
            pm = _dot(pooled, wg)
            ds_ref[:, cols] += jnp.sum(dm_ref[:, cols] * pm, axis=0, keepdims=True)
            dpm = (dmc[:, cols] * s_ref[:, cols]).astype(BF16)
            dw_ref[gi] += _dot(pooled, dpm[:tm], TN)
            dpool = _dot(dpm, wg, NT)
            f = dpool / jnp.minimum(tpos_x + 1, win).astype(F32)
            step = 1
            while step < win:
                f = f + _shift_up(f, step)
                step *= 2
            du_ref[:, cols] = f[:tm] - dpool[:tm]

    row = pl.BlockSpec((tm, dp), lambda i: (i, 0))
    return pl.pallas_call(
        body, name="pool_bwd", grid=grid,
        out_shape=(jax.ShapeDtypeStruct((m, dp), F32), jax.ShapeDtypeStruct((ng, g, g), F32),
                   jax.ShapeDtypeStruct((1, dp), F32)),
        in_specs=[row, prev, row, nxt, pl.BlockSpec((None, ng, g, g), lambda i: (layer, 0, 0, 0)),
                  pl.BlockSpec((1, dp), lambda i: (0, 0))],
        out_specs=(row, pl.BlockSpec((ng, g, g), lambda i: (0, 0, 0)), pl.BlockSpec((1, dp), lambda i: (0, 0))),
        compiler_params=_params(grid))(u, u, dmix, dmix, wp, scale)


def _conv_taps(u, w_ref, rows):
    y = None
    for k in range(CONV_WIDTH):
        term = _shift_down(u, CONV_WIDTH - 1 - k)[HALO:HALO + rows] * w_ref[k:k + 1, :]
        y = term if y is None else y + term
    return y


def _layernorm_stats(y):
    mu = jnp.mean(y, axis=-1, keepdims=True)
    yc = y - mu
    rstd = lax.rsqrt(jnp.mean(yc * yc, axis=-1, keepdims=True) + EPS)
    return yc * rstd, rstd


def _conv_fwd(uc, wdw, b, lg, lb):
    m, c2 = uc.shape
    c = c2 // 2
    tm = _row_tile(m)
    nt = m // tm
    grid = (nt,)
    prev, _ = _halo_specs(tm, c2, nt)

    def body(x_ref, h_ref, w_ref, b_ref, g_ref, bb_ref, o_ref):
        i = pl.program_id(0)
        halo = jnp.where(i > 0, h_ref[...], 0.0)
        xc = jnp.concatenate([halo, x_ref[...]], axis=0)
        u = xc[:, :c] * jax.nn.sigmoid(xc[:, c:])
        y = _conv_taps(u, w_ref, tm) + b_ref[...]
        xhat, _ = _layernorm_stats(y)
        ln = xhat * g_ref[...] + bb_ref[...]
        o_ref[...] = (ln * jax.nn.sigmoid(ln)).astype(BF16)

    vec = pl.BlockSpec((1, c), lambda i: (0, 0))
    return pl.pallas_call(
        body, name="conv_fwd", grid=grid, out_shape=jax.ShapeDtypeStruct((m, c), BF16),
        in_specs=[pl.BlockSpec((tm, c2), lambda i: (i, 0)), prev, pl.BlockSpec((HALO, c), lambda i: (0, 0)), vec, vec, vec],
        out_specs=pl.BlockSpec((tm, c), lambda i: (i, 0)), compiler_params=_params(grid))(uc, uc, wdw, b, lg, lb)


def _conv_bwd(uc, ds, wdw, b, lg, lb):
    m, c2 = uc.shape
    c = c2 // 2
    tm = _row_tile(m)
    nt = m // tm
    grid = (nt,)
    prev, nxt = _halo_specs(tm, c2, nt)
    _, nxt_c = _halo_specs(tm, c, nt)

    def body(x_ref, hp_ref, hn_ref, ds_ref, dsn_ref, w_ref, b_ref, g_ref, bb_ref,
             dx_ref, dw_ref, db_ref, dg_ref, dbb_ref):
        i = pl.program_id(0)

        @pl.when(i == 0)
        def _():
            for ref in (dw_ref, db_ref, dg_ref, dbb_ref):
                ref[...] = jnp.zeros_like(ref)

        hp = jnp.where(i > 0, hp_ref[...], 0.0)
        xc = jnp.concatenate([hp, x_ref[...], hn_ref[...]], axis=0)
        sg = jax.nn.sigmoid(xc[:, c:])
        u = xc[:, :c] * sg
        rows = tm + HALO
        y = _conv_taps(u, w_ref, rows) + b_ref[...]
        xhat, rstd = _layernorm_stats(y)
        ln = xhat * g_ref[...] + bb_ref[...]
        sl = jax.nn.sigmoid(ln)
        dsn = jnp.where(i < nt - 1, dsn_ref[...], 0.0)
        dsx = jnp.concatenate([ds_ref[...], dsn], axis=0)
        dln = dsx * sl * (1.0 + ln * (1.0 - sl))
        dxh = dln * g_ref[...]
        dy = rstd * (dxh - jnp.mean(dxh, axis=-1, keepdims=True)
                     - xhat * jnp.mean(dxh * xhat, axis=-1, keepdims=True))
        dyt = dy[:tm]
        dg_ref[...] += jnp.sum(dln[:tm] * xhat[:tm], axis=0, keepdims=True)
        dbb_ref[...] += jnp.sum(dln[:tm], axis=0, keepdims=True)
        db_ref[...] += jnp.sum(dyt, axis=0, keepdims=True)
        du = None
        for k in range(CONV_WIDTH):
            lag = CONV_WIDTH - 1 - k
            dw_ref[k:k + 1, :] += jnp.sum(dyt * _shift_down(u, lag)[HALO:HALO + tm], axis=0, keepdims=True)
            term = _shift_up(dy, lag)[:tm] * w_ref[k:k + 1, :]
            du = term if du is None else du + term
        a_t = xc[HALO:HALO + tm, :c]
        sg_t = sg[HALO:HALO + tm]
        dx_ref[:, :c] = du * sg_t
        dx_ref[:, c:] = du * a_t * sg_t * (1.0 - sg_t)

    vec = pl.BlockSpec((1, c), lambda i: (0, 0))
    wsp = pl.BlockSpec((HALO, c), lambda i: (0, 0))
    vshape = jax.ShapeDtypeStruct((1, c), F32)
    return pl.pallas_call(
        body, name="conv_bwd", grid=grid,
        out_shape=(jax.ShapeDtypeStruct((m, c2), F32), jax.ShapeDtypeStruct((HALO, c), F32), vshape, vshape, vshape),
        in_specs=[pl.BlockSpec((tm, c2), lambda i: (i, 0)), prev, nxt, pl.BlockSpec((tm, c), lambda i: (i, 0)), nxt_c,
                  wsp, vec, vec, vec],
        out_specs=(pl.BlockSpec((tm, c2), lambda i: (i, 0)), wsp, vec, vec, vec),
        compiler_params=_params(grid))(uc, uc, uc, ds, ds, wdw, b, lg, lb)


def _mesh_pos():
    return lax.axis_index("x"), lax.axis_index("y"), lax.axis_index("c")


def _other_chips(x, y):
    return [(1 - x, y), (x, 1 - y), (1 - x, 1 - y)]


def _comm_sems(n, per):
    return [pltpu.SemaphoreType.DMA((n, per)), pltpu.SemaphoreType.DMA((n, per)), pltpu.SemaphoreType.DMA((n,))]


def _comm_call(body, name, arrs, out_shapes, n_sems):
    n = len(arrs)
    return pl.pallas_call(
        body, name=name, out_shape=out_shapes, in_specs=[ANY] * n, out_specs=tuple([ANY] * len(out_shapes)),
        scratch_shapes=_comm_sems(n, n_sems),
    )(*arrs)


class _GatherPlan:
    def __init__(self, ins, outs, send_sems, recv_sems, local_sems):
        self.ins, self.outs = ins, outs
        self.send_sems, self.recv_sems, self.local_sems = send_sems, recv_sems, local_sems
        x, y, c = _mesh_pos()
        self.c, self.me, self.sib = c, (x, y, c), (x, y, 1 - c)
        self.chips = _other_chips(x, y)

    def _slot(self, a, p):
        return self.outs[a].at[4 * p[0] + 2 * p[1] + p[2]]

    def _copy(self, a, k, block, to, own=False):
        return pltpu.make_async_remote_copy(
            src_ref=self.ins[a] if own else self._slot(a, block), dst_ref=self._slot(a, block),
            send_sem=self.send_sems.at[a, k], recv_sem=self.recv_sems.at[a, k], device_id=to, device_id_type=MESH)

    def _local(self, a):
        return pltpu.make_async_copy(self.ins[a], self._slot(a, self.me), self.local_sems.at[a])

    def start(self):
        for a in range(len(self.ins)):
            for j, chip in enumerate(self.chips):
                self._copy(a, 1 + j, self.me, (*chip, self.c), own=True).start()
            self._copy(a, 0, self.me, self.sib, own=True).start()
            self._local(a).start()

    def forward(self, a):
        for j, chip in enumerate(self.chips):
            self._copy(a, 1 + j, (*chip, self.c), self.me).wait_recv()
            self._copy(a, 4 + j, (*chip, self.c), self.sib).start()

    def finish(self):
        for a in range(len(self.ins)):
            self._copy(a, 0, self.sib, self.me).wait_recv()
            for j, chip in enumerate(self.chips):
                self._copy(a, 4 + j, (*chip, 1 - self.c), self.me).wait_recv()
        for a in range(len(self.ins)):
            self._copy(a, 0, self.me, self.sib, own=True).wait_send()
            for j, chip in enumerate(self.chips):
                self._copy(a, 1 + j, self.me, (*chip, self.c), own=True).wait_send()
                self._copy(a, 4 + j, (*chip, self.c), self.sib).wait_send()
            self._local(a).wait()


def _gather_shapes(arrs):
    return tuple(jax.ShapeDtypeStruct((N_DEV,) + a.shape, a.dtype) for a in arrs)


def _all_gather(arrs, name):
    n = len(arrs)

    def body(*refs):
        plan = _GatherPlan(refs[:n], refs[n:2 * n], *refs[2 * n:])
        plan.start()
        for a in range(n):
            plan.forward(a)
        plan.finish()

    return _comm_call(body, name, arrs, _gather_shapes(arrs), 7)


class _ChipExchangePlan:
    def __init__(self, ins, outs, send_sems, recv_sems, local_sems):
        self.ins, self.outs = ins, outs
        self.send_sems, self.recv_sems, self.local_sems = send_sems, recv_sems, local_sems
        x, y, c = _mesh_pos()
        self.pos, self.mine = (x, y, c), 2 * x + y
        self.chips = _other_chips(x, y)

    def _send(self, a, j):
        chip = self.chips[j]
        return pltpu.make_async_remote_copy(
            src_ref=self.ins[a].at[2 * chip[0] + chip[1]], dst_ref=self.outs[a].at[self.mine],
            send_sem=self.send_sems.at[a, j], recv_sem=self.recv_sems.at[a, j], device_id=(*chip, self.pos[2]),
            device_id_type=MESH)

    def _landing(self, a, j):
        chip = self.chips[j]
        ref = self.outs[a].at[2 * chip[0] + chip[1]]
        return pltpu.make_async_remote_copy(src_ref=ref, dst_ref=ref, send_sem=self.send_sems.at[a, j],
                                            recv_sem=self.recv_sems.at[a, j], device_id=self.pos, device_id_type=MESH)

    def _local(self, a):
        return pltpu.make_async_copy(self.ins[a].at[self.mine], self.outs[a].at[self.mine], self.local_sems.at[a])

    def start(self):
        for a in range(len(self.ins)):
            for j in range(3):
                self._send(a, j).start()
            self._local(a).start()

    def finish(self):
        for a in range(len(self.ins)):
            for j in range(3):
                self._landing(a, j).wait_recv()
        for a in range(len(self.ins)):
            for j in range(3):
                self._send(a, j).wait_send()
            self._local(a).wait()


def _pair_tile(kk):
    return _pick(kk, (1024, 704, 576, 512, 256, 128, 64))


def _pair_exchange(parts, core, name):
    _, kk, n = parts.shape
    tr = _pair_tile(kk)
    nr = kk // tr
    grid = (4, nr)

    def body(c_ref, src_ref, recv_ref, send_sems, recv_sems):
        q, r = pl.program_id(0), pl.program_id(1)
        x, y, c = _mesh_pos()
        step = q * nr + r
        slot = step % 2
        dst = recv_ref.at[pl.ds(pl.multiple_of(step * tr, tr), tr), :]
        cp = pltpu.make_async_remote_copy(src_ref=src_ref, dst_ref=dst, send_sem=send_sems.at[slot],
                                          recv_sem=recv_sems.at[slot], device_id=(x, y, 1 - c), device_id_type=MESH)
        cp.start()
        cp.wait_send()
        cp.wait_recv()

    spec = pltpu.PrefetchScalarGridSpec(
        num_scalar_prefetch=1, grid=grid,
        in_specs=[pl.BlockSpec((tr, n), lambda q, r, c: ((2 * q + 1 - c[0]) * nr + r, 0))],
        out_specs=ANY, scratch_shapes=[pltpu.SemaphoreType.DMA((2,)), pltpu.SemaphoreType.DMA((2,))])
    recv = pl.pallas_call(body, name=name, grid_spec=spec, out_shape=jax.ShapeDtypeStruct((4 * kk, n), parts.dtype),
                          compiler_params=_params(grid))(core, parts.reshape(N_DEV * kk, n))
    return recv.reshape(4, kk, n)


def _exchange_shapes(parts):
    return tuple(jax.ShapeDtypeStruct(p.shape, p.dtype) for p in parts)


def _reduce_chip_exchange(parts, name):
    n = len(parts)

    def body(*refs):
        plan = _ChipExchangePlan(refs[:n], refs[n:2 * n], *refs[2 * n:])
        plan.start()
        plan.finish()

    return _comm_call(body, name, parts, _exchange_shapes(parts), 3)


def _pair_add(parts, recv, core, name):
    _, kk, n = recv.shape
    tr = _pair_tile(kk)
    grid = (4, kk // tr)

    def body(c_ref, a_ref, b_ref, o_ref):
        o_ref[...] = (a_ref[...].astype(F32) + b_ref[...].astype(F32)).astype(o_ref.dtype)

    out = pl.BlockSpec((None, tr, n), lambda q, r, c: (q, r, 0))
    spec = pltpu.PrefetchScalarGridSpec(
        num_scalar_prefetch=1, grid=grid,
        in_specs=[pl.BlockSpec((None, None, tr, n), lambda q, r, c: (q, c[0], r, 0)), out], out_specs=out)
    return pl.pallas_call(body, name=name, grid_spec=spec, out_shape=jax.ShapeDtypeStruct(recv.shape, recv.dtype),
                          compiler_params=_params(grid))(core, parts.reshape(4, 2, kk, n), recv)


def _adamw_math(w, g, m, v):
    m2 = ADAM_B1 * m + (1.0 - ADAM_B1) * g
    v2 = ADAM_B2 * v + (1.0 - ADAM_B2) * (g * g)
    m_hat = m2 / (1.0 - ADAM_B1 ** ADAM_STEP)
    v_hat = v2 / (1.0 - ADAM_B2 ** ADAM_STEP)
    delta = -ADAM_LR * (m_hat / (jnp.sqrt(v_hat) + ADAM_EPS) + ADAM_WD * w)
    return delta, m2, v2


def _adamw_sharded(parts, w, m, v, name):
    nl, kk, n = w.shape
    tr = _pick(kk, (256, 128, 64)) if n <= 1024 else _pick(kk, (64, 32))
    grid = (nl, kk // tr)

    def body(*refs):
        p_refs = refs[:nl]
        w_ref, m_ref, v_ref, g_ref, d_ref, m2_ref, v2_ref = refs[nl:]
        layer = pl.program_id(0)
        g = None
        for l, p_ref in enumerate(p_refs):
            s = p_ref[0].astype(F32)
            for q in range(1, 4):
                s = s + p_ref[q].astype(F32)
            g = s if g is None else jnp.where(layer == l, s, g)
        d, m2, v2 = _adamw_math(w_ref[...], g, m_ref[...], v_ref[...])
        g_ref[...] = g
        d_ref[...] = d
        m2_ref[...] = m2
        v2_ref[...] = v2

    spec = pl.BlockSpec((None, tr, n), lambda l, r: (l, r, 0))
    shp = jax.ShapeDtypeStruct(w.shape, F32)
    p_specs = [pl.BlockSpec((4, tr, n), lambda l, r, k=k: (0, jnp.where(l == k, r, 0), 0)) for k in range(nl)]
    return pl.pallas_call(
        body, name=name, grid=grid, out_shape=(shp, shp, shp, shp), in_specs=p_specs + [spec, spec, spec],
        out_specs=(spec, spec, spec, spec), compiler_params=_params(grid))(*parts, w, m, v)


def _sum_slots(g8):
    _, r, _ = g8.shape

    def body(g_ref, o_ref):
        s = g_ref[0]
        for d in range(1, N_DEV):
            s = s + g_ref[d]
        o_ref[...] = s

    return pl.pallas_call(body, name="sum_slots", out_shape=jax.ShapeDtypeStruct((r, 128), F32))(g8)


def _adamw_flat(w, g, m, v):
    def body(w_ref, g_ref, m_ref, v_ref, d_ref, m2_ref, v2_ref):
        d, m2, v2 = _adamw_math(w_ref[...], g_ref[...], m_ref[...], v_ref[...])
        d_ref[...] = d
        m2_ref[...] = m2
        v2_ref[...] = v2

    shp = jax.ShapeDtypeStruct(w.shape, F32)
    return pl.pallas_call(body, name="adamw_flat", out_shape=(shp, shp, shp))(w, g, m, v)


def _pack(arrs):
    flat = jnp.concatenate([a.reshape(-1) for a in arrs])
    pad = (-flat.shape[0]) % 1024
    return jnp.pad(flat, (0, pad)).reshape(-1, 128)


def _unpack(buf, like):
    flat = buf.reshape(-1)
    out, off = [], 0
    for a in like:
        out.append(flat[off:off + a.size].reshape(a.shape))
        off += a.size
    return out


def _heads(x2d, scale=None):
    lp, da = x2d.shape
    xh = x2d.reshape(lp, N_HEADS, da // N_HEADS).transpose(1, 0, 2)
    if scale is not None:
        xh = xh * scale
    return xh


def _unheads(xh):
    nh, lp, dh = xh.shape
    return xh.transpose(1, 0, 2).reshape(lp, nh * dh)


def _hosted(hosts, site, kind, fn):
    if hosts and site in hosts:
        arrays, done = hosts[site]
        res, got = fn(_Riders(kind, arrays))
        done(got)
        return res
    return fn(None)


def _layer_fwd(h, l, W, P, hosts=None):
    d = h.shape[1]
    da, dp = d // 2, d // 4
    dh = da // N_HEADS
    sv = {"h": h}
    u1 = _rms_fwd(h, P["pre_mix_g"][l:l + 1], "rms_pre_mix")
    proj = _mm_nt(u1, W["in"], "mm_in")
    qh = _heads(proj[:, :da], dh ** -0.5).astype(BF16)
    kh = _heads(proj[:, da:2 * da]).astype(BF16)
    vh = _heads(proj[:, 2 * da:3 * da]).astype(BF16)
    u_pool = proj[:, 3 * da:3 * da + dp]
    u_conv = proj[:, 3 * da + dp:]
    if hosts and "attn" in hosts:
        arrs, done = hosts["attn"]
        oh, tot, cnt, *gathered = _attn_fwd(qh, kh, vh, arrs)
        done(gathered)
    else:
        oh, tot, cnt = _attn_fwd(qh, kh, vh)
    o_attn = _unheads(oh)
    o_pool = _pool_fwd(u_pool, W["pool"], P["pool_scale"][l:l + 1], l)
    s_conv = _conv_fwd(u_conv, W["dw"][l], P["b_dw"][l:l + 1], P["conv_ln_g"][l:l + 1], P["conv_ln_b"][l:l + 1])
    o_conv = _mm_nn([(s_conv, W["pw"])], "mm_pw")
    merged = _merge_fwd(o_attn, o_pool, o_conv, P["mix_out_g"][l:l + 1])
    z1 = _mm_nn([(merged, W["out"])], "mm_out")
    h1 = _resid_rms_fwd(h, z1, P["post_mix_g"][l:l + 1], "resid_post_mix")
    u2 = _rms_fwd(h1, P["pre_ffn_g"][l:l + 1], "rms_pre_ffn")
    a_s, b_s, act_s = _hosted(hosts, "ffn_up", "gather", lambda r: _ffn_up(u2, W["gate"], W["up"], r))
    ff = _hosted(hosts, "mm_down", "gather", lambda r: _mm_nn([(act_s, W["down"])], "mm_down", riders=r))
    h2 = _resid_rms_fwd(h1, ff, P["post_ffn_g"][l:l + 1], "resid_post_ffn")
    sv.update(u1=u1, qh=qh, kh=kh, vh=vh, tot=tot, cnt=cnt, u_pool=u_pool, u_conv=u_conv, o_attn=o_attn, o_pool=o_pool,
              o_conv=o_conv, s_conv=s_conv, merged=merged, z1=z1, h1=h1, u2=u2, a_s=a_s, b_s=b_s, act_s=act_s, ff=ff)
    return h2, sv


EARLY = ("w_down", "w_gate", "w_up", "w_out", "w_pw")


def _layer_bwd(dh_out, l, W, P, sv, exchange_with_attn=None, hosts=None, exchange_w_in=None):
    d = dh_out.shape[1]
    da = d // 2
    dhd = da // N_HEADS
    lp = dh_out.shape[0]
    g = {}
    dff, g["post_ffn_g"] = _rms_bwd(sv["ff"], P["post_ffn_g"][l:l + 1], dh_out, None, BF16, "rms_bwd_post_ffn")
    da_s, db_s = _hosted(hosts, "ffn_bwd_act", "exchange",
                         lambda r: _ffn_bwd_act(dff, W["down"], sv["a_s"], sv["b_s"], r))
    g["w_down"] = _mm_tn(sv["act_s"], dff, "mm_dw_down")
    du2 = _hosted(hosts, "mm_d_u2", "exchange",
                  lambda r: _mm_nn([(da_s, W["gate"]), (db_s, W["up"])], "mm_d_u2", riders=r))
    g["w_gate"] = _hosted(hosts, "mm_dw_gate", "exchange", lambda r: _mm_tn(da_s, sv["u2"], "mm_dw_gate", riders=r))
    g["w_up"] = _mm_tn(db_s, sv["u2"], "mm_dw_up")
    dh1, g["pre_ffn_g"] = _rms_bwd(sv["h1"], P["pre_ffn_g"][l:l + 1], du2, dh_out, F32, "rms_bwd_pre_ffn")
    dz1, g["post_mix_g"] = _rms_bwd(sv["z1"], P["post_mix_g"][l:l + 1], dh1, None, BF16, "rms_bwd_post_mix")
    dmerged = _mm_nt(dz1, W["out"], "mm_d_merged")
    g["w_out"] = _mm_tn(sv["merged"], dz1, "mm_dw_out")
    d_oa, d_op, d_oc, g["mix_out_g"] = _merge_bwd(sv["o_attn"], sv["o_pool"], sv["o_conv"], P["mix_out_g"][l:l + 1], dmerged)
    d_s = _mm_nt(d_oc, W["pw"], "mm_d_sconv")
    g["w_pw"] = _mm_tn(sv["s_conv"], d_oc, "mm_dw_pw")
    d_uc, dwdw, g["b_dw"], g["conv_ln_g"], g["conv_ln_b"] = _conv_bwd(
        sv["u_conv"], d_s, W["dw"][l], P["b_dw"][l:l + 1], P["conv_ln_g"][l:l + 1], P["conv_ln_b"][l:l + 1])
    g["w_dw"] = dwdw[:CONV_WIDTH]
    d_up, g["w_pool"], g["pool_scale"] = _pool_bwd(sv["u_pool"], d_op, W["pool"], P["pool_scale"][l:l + 1], l)
    doh = _heads(d_oa).astype(BF16)
    riders = exchange_with_attn(g) if exchange_with_attn is not None else []
    dqh, dkt, dvt, *exchanged = _attn_bwd(sv["qh"], sv["qh"].transpose(0, 2, 1), sv["kh"], sv["vh"], doh,
                                          doh.transpose(0, 2, 1), sv["tot"], sv["cnt"], dhd ** -0.5, riders)
    dk2, dv2 = (a.transpose(1, 3, 0, 2).reshape(lp, da) for a in (dkt, dvt))
    dproj = jnp.concatenate([_unheads(dqh), dk2, dv2, d_up, d_uc], axis=1).astype(BF16)
    g["w_in"] = _mm_tn(dproj, sv["u1"], "mm_dw_in")
    late = {"mm_d_u1": exchange_w_in(g)} if exchange_w_in is not None else None
    du1 = _hosted(late, "mm_d_u1", "exchange", lambda r: _mm_nn([(dproj, W["in"])], "mm_d_u1", riders=r))
    dh0, g["pre_mix_g"] = _rms_bwd(sv["h"], P["pre_mix_g"][l:l + 1], du1, dh1, F32, "rms_bwd_pre_mix")
    return dh0, g, exchanged


SHARDED = ("w_in", "w_gate", "w_up", "w_down", "w_out", "w_pw")
SMALL = ("pre_mix_g", "w_pool", "pool_scale", "b_dw", "conv_ln_g", "conv_ln_b", "mix_out_g", "post_mix_g",
         "pre_ffn_g", "post_ffn_g", "w_dw", "meta_tokens")
ORDER = ("meta_tokens", "pre_mix_g", "w_in", "w_pool", "pool_scale", "w_dw", "b_dw", "conv_ln_g", "conv_ln_b", "w_pw",
         "mix_out_g", "w_out", "post_mix_g", "pre_ffn_g", "w_gate", "w_up", "w_down", "post_ffn_g")


def kernel(x, meta_tokens, pre_mix_g, w_in, w_pool, pool_scale, w_dw, b_dw, conv_ln_g, conv_ln_b, w_pw, mix_out_g, w_out, post_mix_g, pre_ffn_g, w_gate, w_up, w_down, post_ffn_g, loss_target, m_meta_tokens, m_pre_mix_g, m_w_in, m_w_pool, m_pool_scale, m_w_dw, m_b_dw, m_conv_ln_g, m_conv_ln_b, m_w_pw, m_mix_out_g, m_w_out, m_post_mix_g, m_pre_ffn_g, m_w_gate, m_w_up, m_w_down, m_post_ffn_g, v_meta_tokens, v_pre_mix_g, v_w_in, v_w_pool, v_pool_scale, v_w_dw, v_b_dw, v_conv_ln_g, v_conv_ln_b, v_w_pw, v_mix_out_g, v_w_out, v_post_mix_g, v_pre_ffn_g, v_w_gate, v_w_up, v_w_down, v_post_ffn_g):
    P = dict(meta_tokens=meta_tokens, pre_mix_g=pre_mix_g, w_in=w_in, w_pool=w_pool, pool_scale=pool_scale, w_dw=w_dw,
             b_dw=b_dw, conv_ln_g=conv_ln_g, conv_ln_b=conv_ln_b, w_pw=w_pw, mix_out_g=mix_out_g, w_out=w_out,
             post_mix_g=post_mix_g, pre_ffn_g=pre_ffn_g, w_gate=w_gate, w_up=w_up, w_down=w_down, post_ffn_g=post_ffn_g)
    M = dict(meta_tokens=m_meta_tokens, pre_mix_g=m_pre_mix_g, w_in=m_w_in, w_pool=m_w_pool, pool_scale=m_pool_scale,
             w_dw=m_w_dw, b_dw=m_b_dw, conv_ln_g=m_conv_ln_g, conv_ln_b=m_conv_ln_b, w_pw=m_w_pw, mix_out_g=m_mix_out_g,
             w_out=m_w_out, post_mix_g=m_post_mix_g, pre_ffn_g=m_pre_ffn_g, w_gate=m_w_gate, w_up=m_w_up,
             w_down=m_w_down, post_ffn_g=m_post_ffn_g)
    V = dict(meta_tokens=v_meta_tokens, pre_mix_g=v_pre_mix_g, w_in=v_w_in, w_pool=v_w_pool, pool_scale=v_pool_scale,
             w_dw=v_w_dw, b_dw=v_b_dw, conv_ln_g=v_conv_ln_g, conv_ln_b=v_conv_ln_b, w_pw=v_w_pw, mix_out_g=v_mix_out_g,
             w_out=v_w_out, post_mix_g=v_post_mix_g, pre_ffn_g=v_pre_ffn_g, w_gate=v_w_gate, w_up=v_w_up,
             w_down=v_w_down, post_ffn_g=v_post_ffn_g)
    xi, yi, ci = _mesh_pos()
    dev = 4 * xi + 2 * yi + ci
    n_tok, d = x.shape[1], x.shape[2]
    n_meta = meta_tokens.shape[0]
    n_layers = w_in.shape[0]
    c = d // 4
    l_real = n_meta + n_tok
    lp = -(-l_real // ATT_BLOCK) * ATT_BLOCK

    TRANSPOSED = ("w_in", "w_gate", "w_up")
    shard = {k: (jnp.swapaxes(P[k], 1, 2) if k in TRANSPOSED else P[k]) for k in SHARDED}

    def bf16_shard(k, l):
        return shard[k][l].astype(BF16)

    def whole(gathered):
        return gathered.reshape(-1, gathered.shape[-1])

    g_in0, g_dw, g_meta = _all_gather([bf16_shard("w_in", 0), w_dw, meta_tokens], "gather_first")
    wdw_full = g_dw.transpose(1, 2, 0, 3).reshape(n_layers, CONV_WIDTH, c)
    common = {"pool": w_pool.astype(BF16), "dw": jnp.pad(wdw_full, ((0, 0), (0, HALO - CONV_WIDTH), (0, 0)))}
    W = [dict(common) for _ in range(n_layers)]
    W[0]["in"] = whole(g_in0)
    meta_full = g_meta.transpose(1, 0, 2).reshape(n_meta, d)
    def gather_site(keys):
        def done(gathered):
            for (k, l), g in zip(keys, gathered):
                W[l][k[2:]] = whole(g)
        return [bf16_shard(k, l) for k, l in keys], done

    def fwd_hosts(l):
        own = ("w_up", "w_down") if l > 0 else ("w_pw", "w_out", "w_gate", "w_up", "w_down")
        hosts = {"attn": gather_site([(k, l) for k in own])}
        if l + 1 < n_layers:
            hosts["ffn_up"] = gather_site([(k, l + 1) for k in ("w_in", "w_pw", "w_out")])
            hosts["mm_down"] = gather_site([("w_gate", l + 1)])
        return hosts

    h = jnp.concatenate([meta_full, x[0], jnp.zeros((lp - l_real, d), F32)], axis=0)
    saved = []
    for l in range(n_layers):
        h, sv = _layer_fwd(h, l, W[l], P, fwd_hosts(l))
        saved.append(sv)
    target = jnp.pad(loss_target[0], ((n_meta, lp - l_real), (0, 0)))
    loss_part, dh = _loss_grad(h, target, n_meta, n_tok)
    loss = lax.psum(loss_part[0, 0], ("x", "y", "c"))

    core = jnp.reshape(ci, (1,)).astype(jnp.int32)
    by_chip = {}

    def pair_reduce(g, keys, l):
        res = {}
        for k in keys:
            p = g[k].reshape(N_DEV, g[k].shape[0] // N_DEV, g[k].shape[1])
            recv = _pair_exchange(p, core, f"pair_exchange_{k}_{l}")
            res[k] = _pair_add(p, recv, core, f"pair_add_{k}_{l}")
        return res

    def exchange_site(sums, keys, l):
        def done(exchanged):
            by_chip.update({(l, k): e for k, e in zip(keys, exchanged)})
        return [sums[k] for k in keys], done

    def ride(g):
        sums = pair_reduce(g, EARLY, 0)
        return [sums[k] for k in EARLY]

    grads = [None] * n_layers
    hosts = None
    for l in reversed(range(n_layers)):
        last = (lambda g: exchange_site(pair_reduce(g, ("w_in",), 0), ("w_in",), 0)) if l == 0 else None
        dh, grads[l], exchanged = _layer_bwd(dh, l, W[l], P, saved[l], ride if l == 0 else None, hosts, last)
        if l == 0:
            by_chip.update({(0, k): e for k, e in zip(EARLY, exchanged)})
        else:
            sums = pair_reduce(grads[l], SHARDED, l)
            hosts = {"ffn_bwd_act": exchange_site(sums, ("w_down",), l),
                     "mm_d_u2": exchange_site(sums, ("w_gate", "w_up"), l),
                     "mm_dw_gate": exchange_site(sums, ("w_in", "w_out", "w_pw"), l)}
    grad_x = dh[n_meta:l_real][None]
    out = {}
    for k in SHARDED:
        view = (lambda a: jnp.swapaxes(a, 1, 2)) if k in TRANSPOSED else (lambda a: a)
        res = _adamw_sharded([by_chip[(l, k)] for l in range(n_layers)], view(P[k]), view(M[k]), view(V[k]), "adamw_" + k)
        out[k] = tuple(view(r) for r in res)

    small_parts = []
    for k in SMALL:
        if k == "meta_tokens":
            small_parts.append(dh[:n_meta])
        else:
            small_parts.append(jnp.stack([grads[l][k].reshape(P[k].shape[1:] if k != "w_dw" else (CONV_WIDTH, c))
                                          for l in range(n_layers)], axis=0))
    (g8,) = _all_gather([_pack(small_parts)], "gather_small_grads")
    g_small = dict(zip(SMALL, _unpack(_sum_slots(g8), small_parts)))
    g_small["w_dw"] = lax.dynamic_slice_in_dim(g_small["w_dw"], dev * w_dw.shape[2], w_dw.shape[2], axis=2)
    g_small["meta_tokens"] = lax.dynamic_slice_in_dim(g_small["meta_tokens"], dev * meta_tokens.shape[1],
                                                      meta_tokens.shape[1], axis=1)
    like = [P[k] for k in SMALL]
    res = _adamw_flat(_pack(like), _pack([g_small[k] for k in SMALL]), _pack([M[k] for k in SMALL]),
                      _pack([V[k] for k in SMALL]))
    res = [_unpack(r, like) for r in res]
    for i, k in enumerate(SMALL):
        out[k] = (g_small[k], res[0][i], res[1][i], res[2][i])

    return (loss, grad_x, *[out[k][0] for k in ORDER], *[out[k][1] for k in ORDER],
            *[out[k][2] for k in ORDER], *[out[k][3] for k in ORDER])
```

```python
import functools

import jax
import jax.numpy as jnp
from jax import lax
from jax.experimental import pallas as pl
from jax.experimental.pallas import tpu as pltpu

F32 = jnp.float32
BF16 = jnp.bfloat16
EPS = 1e-6
N_HEADS = 16
POOL_WINDOWS = (2, 4, 8, 16)
CONV_WIDTH = 31
HALO = 32
ATT_BLOCK = 128
DEAD_LOG_WEIGHT = 110.0
N_DEV = 8
VMEM_LIMIT = 60 * 1024 * 1024

ADAM_LR = 0.001
ADAM_B1 = 0.9
ADAM_B2 = 0.999
ADAM_EPS = 1e-08
ADAM_WD = 0.01
ADAM_STEP = 10

NN = (((1,), (0,)), ((), ()))
NT = (((1,), (1,)), ((), ()))
TN = (((0,), (0,)), ((), ()))
MESH = pl.DeviceIdType.MESH
ANY = pl.BlockSpec(memory_space=pl.ANY)


def _pick(n, cands):
    for c in cands:
        if c <= n and n % c == 0:
            return c
    return n


def _params(grid):
    return pltpu.CompilerParams(dimension_semantics=("arbitrary",) * len(grid), vmem_limit_bytes=VMEM_LIMIT)


def _dot(a, b, dims=NN):
    return lax.dot_general(a, b, dims, preferred_element_type=F32)


class _Riders:
    def __init__(self, kind, arrays):
        self.gather = kind == "gather"
        self.arrays = list(arrays)
        self.n = len(self.arrays)
        self.out_shapes = _gather_shapes(self.arrays) if self.gather else _exchange_shapes(self.arrays)

    def call(self, body, name, grid, in_specs, out_shapes, out_specs, scratch, operands):
        n, n_in, n_out, n_scr = self.n, len(in_specs), len(out_shapes), len(scratch)

        def wrapped(*refs):
            ins, r_in = refs[:n_in], refs[n_in:n_in + n]
            outs, r_out = refs[n_in + n:n_in + n + n_out], refs[n_in + n + n_out:n_in + 2 * n + n_out]
            scr, sems = refs[n_in + 2 * n + n_out:n_in + 2 * n + n_out + n_scr], refs[n_in + 2 * n + n_out + n_scr:]
            ids = [pl.program_id(a) for a in range(len(grid))]
            first = functools.reduce(jnp.logical_and, [i == 0 for i in ids])
            last = functools.reduce(jnp.logical_and, [i == g - 1 for i, g in zip(ids, grid)])
            plan = (_GatherPlan if self.gather else _ChipExchangePlan)(r_in, r_out, *sems)
            pl.when(first)(plan.start)
            body(*ins, *outs, *scr)

            @pl.when(last)
            def _():
                if self.gather:
                    for a in range(n):
                        plan.forward(a)
                plan.finish()

        res = pl.pallas_call(
            wrapped, name=name, grid=grid, in_specs=list(in_specs) + [ANY] * n,
            out_shape=tuple(out_shapes) + tuple(self.out_shapes), out_specs=tuple(out_specs) + (ANY,) * n,
            scratch_shapes=list(scratch) + _comm_sems(n, 7 if self.gather else 3), compiler_params=_params(grid),
        )(*operands, *self.arrays)
        return tuple(res[:n_out]), list(res[n_out:])


def _call(riders, body, name, grid, in_specs, out_shapes, out_specs, scratch, operands):
    if riders is not None:
        return riders.call(body, name, grid, in_specs, out_shapes, out_specs, scratch, operands)
    return tuple(pl.pallas_call(body, name=name, grid=grid, in_specs=list(in_specs), out_shape=tuple(out_shapes),
                                out_specs=tuple(out_specs), scratch_shapes=list(scratch),
                                compiler_params=_params(grid))(*operands))


def _matmul(name, pairs, specs, out_shape, out_spec, grid, dims, acc_shape, riders=None):
    n = len(pairs)
    nk = grid[-1]
    kaxis = len(grid) - 1

    def body(*refs):
        o_ref, acc = refs[2 * n], refs[2 * n + 1]
        tot = None
        for p in range(n):
            d = _dot(refs[2 * p][...], refs[2 * p + 1][...], dims)
            tot = d if tot is None else tot + d
        if nk == 1:
            o_ref[...] = tot.astype(o_ref.dtype)
            return
        k = pl.program_id(kaxis)

        @pl.when(k == 0)
        def _():
            acc[...] = tot

        @pl.when(k > 0)
        def _():
            acc[...] += tot

        @pl.when(k == nk - 1)
        def _():
            o_ref[...] = acc[...].astype(o_ref.dtype)

    ops, in_specs = [], []
    for (a, b), (sa, sb) in zip(pairs, specs):
        ops += [a, b]
        in_specs += [sa, sb]
    res = _call(riders, body, name, grid, in_specs, [out_shape], [out_spec],
                [pltpu.VMEM(acc_shape if nk > 1 else (8, 128), F32)], ops)
    return res[0] if riders is None else (res[0][0], res[1])


ROW_TILES = (1408, 1024, 512, 256, 128)


def _mm_nt(a, w, name, out_dtype=F32, riders=None):
    m, kk = a.shape
    n = w.shape[0]
    tm = _pick(m, ROW_TILES)
    tn = _pick(n, (1024, 768, 512, 256, 128))
    grid = (m // tm, n // tn, 1)
    sa = pl.BlockSpec((tm, kk), lambda i, j, k: (i, 0))
    sb = pl.BlockSpec((tn, kk), lambda i, j, k: (j, 0))
    return _matmul(name, [(a, w)], [(sa, sb)], jax.ShapeDtypeStruct((m, n), out_dtype),
                   pl.BlockSpec((tm, tn), lambda i, j, k: (i, j)), grid, NT, (tm, tn), riders)


def _mm_nn(pairs, name, out_dtype=F32, riders=None):
    m, kc = pairs[0][0].shape
    n = pairs[0][1].shape[1]
    tm = _pick(m, ROW_TILES if len(pairs) == 1 else (704,) + ROW_TILES[2:])
    tn = _pick(n, (1024, 512, 256, 128))
    tk = _pick(kc, (2048, 1536, 1408, 1024, 768, 512, 256, 128))
    grid = (m // tm, n // tn, kc // tk)
    sa = pl.BlockSpec((tm, tk), lambda i, j, k: (i, k))
    sb = pl.BlockSpec((tk, tn), lambda i, j, k: (k, j))
    return _matmul(name, pairs, [(sa, sb)] * len(pairs), jax.ShapeDtypeStruct((m, n), out_dtype),
                   pl.BlockSpec((tm, tn), lambda i, j, k: (i, j)), grid, NN, (tm, tn), riders)


def _mm_tn(a, b, name, out_dtype=BF16, riders=None):
    l, m = a.shape
    n = b.shape[1]
    tm = _pick(m, (768, 512, 256, 128))
    tn = _pick(n, (1024, 512, 256, 128))
    grid = (m // tm, n // tn, 1)
    sa = pl.BlockSpec((l, tm), lambda i, j, t: (0, i))
    sb = pl.BlockSpec((l, tn), lambda i, j, t: (0, j))
    return _matmul(name, [(a, b)], [(sa, sb)], jax.ShapeDtypeStruct((m, n), out_dtype),
                   pl.BlockSpec((tm, tn), lambda i, j, t: (i, j)), grid, TN, (tm, tn), riders)


def _ffn_up(u, wg, wu, riders=None):
    m, kk = u.shape
    n = wg.shape[0]
    tm = _pick(m, ROW_TILES)
    tn = _pick(n, (512, 256, 128))
    grid = (m // tm, n // tn)

    def body(u_ref, wg_ref, wu_ref, a_ref, b_ref, act_ref):
        uu = u_ref[...]
        a = _dot(uu, wg_ref[...], NT)
        b = _dot(uu, wu_ref[...], NT)
        a_ref[...] = a
        b_ref[...] = b
        act_ref[...] = (a * jax.nn.sigmoid(a) * b).astype(BF16)

    wspec = pl.BlockSpec((tn, kk), lambda i, j: (j, 0))
    ospec = pl.BlockSpec((tm, tn), lambda i, j: (i, j))
    return _call(riders, body, "ffn_up", grid, [pl.BlockSpec((tm, kk), lambda i, j: (i, 0)), wspec, wspec],
                 [jax.ShapeDtypeStruct((m, n), F32), jax.ShapeDtypeStruct((m, n), F32), jax.ShapeDtypeStruct((m, n), BF16)],
                 [ospec, ospec, ospec], [], [u, wg, wu])


def _ffn_bwd_act(dff, wd, a, b, riders=None):
    m, kk = dff.shape
    n = wd.shape[0]
    tm = _pick(m, ROW_TILES)
    tn = _pick(n, (512, 256, 128))
    grid = (m // tm, n // tn)

    def body(d_ref, w_ref, a_ref, b_ref, da_ref, db_ref):
        dact = _dot(d_ref[...], w_ref[...], NT)
        aa, bb = a_ref[...], b_ref[...]
        sg = jax.nn.sigmoid(aa)
        db_ref[...] = (dact * aa * sg).astype(BF16)
        da_ref[...] = (dact * bb * sg * (1.0 + aa * (1.0 - sg))).astype(BF16)

    tspec = pl.BlockSpec((tm, tn), lambda i, j: (i, j))
    return _call(riders, body, "ffn_bwd_act", grid,
                 [pl.BlockSpec((tm, kk), lambda i, j: (i, 0)), pl.BlockSpec((tn, kk), lambda i, j: (j, 0)), tspec, tspec],
                 [jax.ShapeDtypeStruct((m, n), BF16), jax.ShapeDtypeStruct((m, n), BF16)], [tspec, tspec], [], [dff, wd, a, b])


def _rstd(x):
    return lax.rsqrt(jnp.mean(x * x, axis=-1, keepdims=True) + EPS)


def _row_tile(m):
    return _pick(m, (384, 256, 128))


def _rms_fwd(x, g, name):
    m, d = x.shape
    tr = _row_tile(m)
    grid = (m // tr,)

    def body(x_ref, g_ref, o_ref):
        xx = x_ref[...]
        o_ref[...] = (xx * _rstd(xx) * g_ref[...]).astype(BF16)

    row = pl.BlockSpec((tr, d), lambda i: (i, 0))
    return pl.pallas_call(body, name=name, grid=grid, out_shape=jax.ShapeDtypeStruct((m, d), BF16),
                          in_specs=[row, pl.BlockSpec((1, d), lambda i: (0, 0))], out_specs=row,
                          compiler_params=_params(grid))(x, g)


def _resid_rms_fwd(h, z, g, g_next, name):
    m, d = h.shape
    tr = _row_tile(m)
    grid = (m // tr,)
    has_next = g_next is not None

    def body(h_ref, z_ref, g_ref, *rest):
        zz = z_ref[...]
        hh = h_ref[...] + zz * _rstd(zz) * g_ref[...]
        rest[-2 if has_next else -1][...] = hh
        if has_next:
            rest[-1][...] = (hh * _rstd(hh) * rest[0][...]).astype(BF16)

    row = pl.BlockSpec((tr, d), lambda i: (i, 0))
    vec = pl.BlockSpec((1, d), lambda i: (0, 0))
    shapes = [jax.ShapeDtypeStruct((m, d), F32)] + ([jax.ShapeDtypeStruct((m, d), BF16)] if has_next else [])
    res = pl.pallas_call(body, name=name, grid=grid, out_shape=tuple(shapes),
                         in_specs=[row, row, vec] + ([vec] if has_next else []), out_specs=tuple([row] * len(shapes)),
                         compiler_params=_params(grid))(h, z, g, *([g_next] if has_next else []))
    return (res[0], res[1]) if has_next else (res[0], None)


def _rms_bwd_math(x, g, dy):
    r = _rstd(x)
    dyg = dy * g
    dx = r * dyg - x * (r * r * r) * jnp.mean(x * dyg, axis=-1, keepdims=True)
    dg = jnp.sum(dy * x * r, axis=0, keepdims=True)
    return dx, dg


def _rms_bwd(x, g, dy, dres, out_dtype, name):
    m, d = x.shape
    tr = _row_tile(m)
    grid = (m // tr,)
    has_res = dres is not None

    def body(*refs):
        x_ref, g_ref, dy_ref = refs[:3]
        dx_ref, dg_ref = refs[-2:]
        dx, dg = _rms_bwd_math(x_ref[...], g_ref[...], dy_ref[...].astype(F32))
        if has_res:
            dx = dx + refs[3][...]
        dx_ref[...] = dx.astype(out_dtype)

        @pl.when(pl.program_id(0) == 0)
        def _():
            dg_ref[...] = jnp.zeros_like(dg_ref)

        dg_ref[...] += dg

    row = pl.BlockSpec((tr, d), lambda i: (i, 0))
    vec = pl.BlockSpec((1, d), lambda i: (0, 0))
    ops = [x, g, dy] + ([dres] if has_res else [])
    return pl.pallas_call(
        body, name=name, grid=grid,
        out_shape=(jax.ShapeDtypeStruct((m, d), out_dtype), jax.ShapeDtypeStruct((1, d), F32)),
        in_specs=[row, vec, row] + ([row] if has_res else []), out_specs=(row, vec),
        compiler_params=_params(grid))(*ops)


def _merge_fwd(oa, op, oc, g):
    m = oa.shape[0]
    da, dp, dc = oa.shape[1], op.shape[1], oc.shape[1]
    d = da + dp + dc
    tr = _row_tile(m)
    grid = (m // tr,)

    def body(a_ref, p_ref, c_ref, g_ref, o_ref):
        off = 0
        for ref, w in ((a_ref, da), (p_ref, dp), (c_ref, dc)):
            xx = ref[...]
            o_ref[:, off:off + w] = (xx * _rstd(xx) * g_ref[:, off:off + w]).astype(BF16)
            off += w

    specs = [pl.BlockSpec((tr, w), lambda i: (i, 0)) for w in (da, dp, dc)]
    return pl.pallas_call(body, name="merge_fwd", grid=grid, out_shape=jax.ShapeDtypeStruct((m, d), BF16),
                          in_specs=specs + [pl.BlockSpec((1, d), lambda i: (0, 0))],
                          out_specs=pl.BlockSpec((tr, d), lambda i: (i, 0)), compiler_params=_params(grid))(oa, op, oc, g)


def _merge_bwd(oa, op, oc, g, dmerged):
    m = oa.shape[0]
    da, dp, dc = oa.shape[1], op.shape[1], oc.shape[1]
    d = da + dp + dc
    tr = _row_tile(m)
    grid = (m // tr,)

    def body(a_ref, p_ref, c_ref, g_ref, dm_ref, da_ref, dp_ref, dc_ref, dg_ref):
        @pl.when(pl.program_id(0) == 0)
        def _():
            dg_ref[...] = jnp.zeros_like(dg_ref)

        off = 0
        for ref, oref, w in ((a_ref, da_ref, da), (p_ref, dp_ref, dp), (c_ref, dc_ref, dc)):
            dx, dg = _rms_bwd_math(ref[...], g_ref[:, off:off + w], dm_ref[:, off:off + w])
            oref[...] = dx.astype(oref.dtype)
            dg_ref[:, off:off + w] += dg
            off += w

    specs = [pl.BlockSpec((tr, w), lambda i: (i, 0)) for w in (da, dp, dc)]
    vec = pl.BlockSpec((1, d), lambda i: (0, 0))
    return pl.pallas_call(
        body, name="merge_bwd", grid=grid,
        out_shape=(jax.ShapeDtypeStruct((m, da), F32), jax.ShapeDtypeStruct((m, dp), F32),
                   jax.ShapeDtypeStruct((m, dc), BF16), jax.ShapeDtypeStruct((1, d), F32)),
        in_specs=specs + [vec, pl.BlockSpec((tr, d), lambda i: (i, 0))], out_specs=tuple(specs) + (vec,),
        compiler_params=_params(grid))(oa, op, oc, g, dmerged)


def _loss_grad(h, target, n_meta, n_tok):
    m, d = h.shape
    tr = _row_tile(m)
    grid = (m // tr,)

    def body(h_ref, t_ref, loss_ref, dh_ref):
        i = pl.program_id(0)
        row = i * tr + lax.broadcasted_iota(jnp.int32, (tr, 1), 0)
        live = jnp.logical_and(row >= n_meta, row < n_meta + n_tok)
        diff = jnp.where(live, h_ref[...] - t_ref[...], 0.0)
        dh_ref[...] = diff * (1.0 / d)

        @pl.when(i == 0)
        def _():
            loss_ref[...] = jnp.zeros_like(loss_ref)

        loss_ref[...] += jnp.sum(jnp.sum(diff * diff, axis=1, keepdims=True), axis=0, keepdims=True) * (0.5 / d)

    row = pl.BlockSpec((tr, d), lambda i: (i, 0))
    return pl.pallas_call(
        body, name="loss_grad", grid=grid,
        out_shape=(jax.ShapeDtypeStruct((1, 1), F32), jax.ShapeDtypeStruct((m, d), F32)),
        in_specs=[row, row], out_specs=(pl.BlockSpec((1, 1), lambda i: (0, 0)), row),
        compiler_params=_params(grid))(h, target)


def _split_dot(x, tmat):
    hi = x.astype(BF16)
    lo = (x - hi.astype(F32)).astype(BF16)
    return _dot(hi, tmat) + _dot(lo, tmat)


def _softplus(z):
    return jnp.maximum(z, 0.0) + jnp.log(1.0 + jnp.exp(-jnp.abs(z)))


def _attn_tiles(lp):
    t = ATT_BLOCK
    nb = lp // t
    u = 3 if nb % 3 == 0 else (2 if nb % 2 == 0 else 1)
    return t, nb, u


def _attn_masks(tq, t, u):
    row = lax.broadcasted_iota(jnp.int32, (tq, t), 0)
    col = lax.broadcasted_iota(jnp.int32, (tq, t), 1)
    masks = [col + r * t < row for r in range(u)]
    a = lax.broadcasted_iota(jnp.int32, (t, 2 * t), 0)
    s = lax.broadcasted_iota(jnp.int32, (t, 2 * t), 1)
    after = jnp.logical_or(a > s, s >= t).astype(BF16)
    before = jnp.logical_or(a < s, s >= t).astype(BF16)
    return masks, after, before


def _service_heads(arrs, nh):
    sizes = [a.size * a.dtype.itemsize for a in arrs]
    heads, done = [], 0
    for sz in sizes:
        done += sz
        heads.append(min(nh - 1, int(1.1 * nh * done / sum(sizes)) + 1))
    return heads


def _attn_fwd(q, k, v, gather=()):
    nh, lp, dh = q.shape
    t, nb, u = _attn_tiles(lp)
    tq = u * t
    nq = nb // u
    grid = (nh, nq)
    ng = len(gather)
    service = _service_heads(gather, nh)

    def body(q_ref, k_ref, v_ref, *rest):
        o_ref, tot_ref, cnt_ref = rest[ng:ng + 3]
        h, i = pl.program_id(0), pl.program_id(1)
        if ng:
            plan = _GatherPlan(rest[:ng], rest[ng + 3:2 * ng + 3], *rest[2 * ng + 3:])
            pl.when(jnp.logical_and(h == 0, i == 0))(plan.start)
        qb = q_ref[...]
        masks, after, _ = _attn_masks(tq, t, u)
        o_ref[...] = jnp.zeros_like(o_ref)
        tot_ref[...] = jnp.zeros_like(tot_ref)

        def blocks(base, masked):
            outs = []
            for r in reversed(range(u)):
                lo = r * t if masked else 0
                start = pl.multiple_of((base + r) * t, t)
                kb = k_ref[pl.ds(start, t), :]
                z = _dot(qb[lo:], kb, NT)
                sp = _softplus(z)
                lnb = jnp.where(masks[r][lo:], -sp, 0.0) if masked else -sp
                res = _split_dot(lnb, after)
                outs.append((start, z - sp + res[:, :t], res[:, t:], r, lo))
            for start, logw, rs, r, lo in outs:
                cs = tot_ref[lo:, :]
                w = jnp.exp(logw + cs)
                if masked:
                    w = jnp.where(masks[r][lo:], w, 0.0)
                o_ref[lo:, :] += _dot(w.astype(BF16), v_ref[pl.ds(start, t), :])
                tot_ref[lo:, :] = cs + rs

        blocks(i * u, True)

        def live():
            return jnp.max(tot_ref[...]) > -DEAD_LOG_WEIGHT

        def step(state):
            n, _ = state
            blocks((i - 1 - n) * u, False)
            return n + 1, live()

        n_done, _ = lax.while_loop(lambda s: jnp.logical_and(s[0] < i, s[1]), step, (jnp.int32(0), live()))
        cnt_ref[h, i] = n_done.astype(F32)

        if ng:
            for a, head in enumerate(service):
                pl.when(jnp.logical_and(h == head, i == nq - 1))(functools.partial(plan.forward, a))
            pl.when(jnp.logical_and(h == nh - 1, i == nq - 1))(plan.finish)

    blk = pl.BlockSpec((None, tq, dh), lambda h, i: (h, i, 0))
    full = pl.BlockSpec((None, lp, dh), lambda h, i: (h, 0, 0))
    return pl.pallas_call(
        body, name="attn_fwd_gather" if ng else "attn_fwd", grid=grid,
        out_shape=(jax.ShapeDtypeStruct((nh, lp, dh), F32), jax.ShapeDtypeStruct((nh, lp, t), F32),
                   jax.ShapeDtypeStruct((nh, nq), F32)) + _gather_shapes(gather),
        in_specs=[blk, full, full] + [ANY] * ng,
        out_specs=(blk, pl.BlockSpec((None, tq, t), lambda h, i: (h, i, 0)), pl.BlockSpec(memory_space=pltpu.SMEM))
        + (ANY,) * ng,
        scratch_shapes=_comm_sems(ng, 7) if ng else [],
        compiler_params=_params(grid))(q, k, v, *gather)


def _attn_bwd(q, qt, k, v, do, dot_, tot, cnt, scale, exchange=()):
    nh, lp, dh = q.shape
    t, nb, u = _attn_tiles(lp)
    tq = u * t
    nq = nb // u
    grid = (nh, nq)
    ne = len(exchange)

    def body(q_ref, qt_ref, k_ref, v_ref, do_ref, dot_ref, tot_ref, cnt_ref, *rest):
        dq_ref, dk_ref, dv_ref = rest[ne:ne + 3]
        p_ref, g_ref = rest[2 * ne + 3:2 * ne + 5]
        h, i = pl.program_id(0), pl.program_id(1)
        if ne:
            plan = _ChipExchangePlan(rest[:ne], rest[ne + 3:2 * ne + 3], *rest[2 * ne + 5:])
            pl.when(jnp.logical_and(h == 0, i == 0))(plan.start)

        @pl.when(i == 0)
        def _():
            dk_ref[...] = jnp.zeros_like(dk_ref)
            dv_ref[...] = jnp.zeros_like(dv_ref)

        dq_ref[...] = jnp.zeros_like(dq_ref)
        p_ref[...] = jnp.zeros_like(p_ref)
        g_ref[...] = jnp.zeros_like(g_ref)
        qb, qtb, dob, dotb = q_ref[...], qt_ref[...], do_ref[...], dot_ref[...]
        total = tot_ref[...]
        masks, after, before = _attn_masks(tq, t, u)

        def blocks(base, masked):
            part = []
            for r in range(u):
                lo = r * t if masked else 0
                j = base + r
                start = pl.multiple_of(j * t, t)
                kb = k_ref[pl.ds(start, t), :]
                vb = v_ref[pl.ds(start, t), :]
                z = _dot(qb[lo:], kb, NT)
                sp = _softplus(z)
                lnb = jnp.where(masks[r][lo:], -sp, 0.0) if masked else -sp
                res = _split_dot(lnb, after)
                part.append((j, kb, z - sp, res[:, :t], res[:, t:], _dot(dob[lo:], vb, NT), r, lo))
            for j, kb, logsig, later_in, rs, dw, r, lo in part:
                pfx = p_ref[lo:, :] + rs
                w = jnp.exp(logsig + later_in + (total[lo:] - pfx))
                if masked:
                    w = jnp.where(masks[r][lo:], w, 0.0)
                dlogw = w * dw
                res2 = _split_dot(dlogw, before)
                gp = g_ref[lo:, :]
                dz = dlogw - jnp.exp(logsig) * (dlogw + res2[:, :t] + gp)
                if masked:
                    dz = jnp.where(masks[r][lo:], dz, 0.0)
                p_ref[lo:, :] = pfx
                g_ref[lo:, :] = gp + res2[:, t:]
                dzb = dz.astype(BF16)
                dq_ref[lo:, :] += _dot(dzb, kb)
                dk_ref[j] += _dot(qtb[:, lo:], dzb)
                dv_ref[j] += _dot(dotb[:, lo:], w.astype(BF16))

        def step(n, carry):
            blocks(n * u, False)
            return carry

        walked = jnp.clip(cnt_ref[h, i].astype(jnp.int32), 0, i)
        lax.fori_loop(i - walked, i, step, 0)
        blocks(i * u, True)
        dq_ref[...] = dq_ref[...] * scale
        if ne:
            pl.when(jnp.logical_and(h == nh - 1, i == nq - 1))(plan.finish)

    blk = pl.BlockSpec((None, tq, dh), lambda h, i: (h, i, 0))
    blk_t = pl.BlockSpec((None, dh, tq), lambda h, i: (h, 0, i))
    full = pl.BlockSpec((None, lp, dh), lambda h, i: (h, 0, 0))
    acc = pl.BlockSpec((None, nb, dh, t), lambda h, i: (h, 0, 0, 0))
    acc_shape = jax.ShapeDtypeStruct((nh, nb, dh, t), F32)
    return pl.pallas_call(
        body, name="attn_bwd_exchange" if ne else "attn_bwd", grid=grid,
        out_shape=(jax.ShapeDtypeStruct((nh, lp, dh), F32), acc_shape, acc_shape) + _exchange_shapes(exchange),
        in_specs=[blk, blk_t, full, full, blk, blk_t, pl.BlockSpec((None, tq, t), lambda h, i: (h, i, 0)),
                  pl.BlockSpec(memory_space=pltpu.SMEM)] + [ANY] * ne,
        out_specs=(blk, acc, acc) + (ANY,) * ne,
        scratch_shapes=[pltpu.VMEM((tq, t), F32), pltpu.VMEM((tq, t), F32)] + (_comm_sems(ne, 3) if ne else []),
        compiler_params=_params(grid))(q, qt, k, v, do, dot_, tot, cnt, *exchange)


def _halo_specs(tm, width, nt):
    per = tm // HALO
    prev = pl.BlockSpec((HALO, width), lambda i: (jnp.maximum(i * per - 1, 0), 0))
    nxt = pl.BlockSpec((HALO, width), lambda i: (jnp.minimum((i + 1) * per, nt * per - 1), 0))
    return prev, nxt


def _shift_down(x, k):
    return x if k == 0 else pltpu.roll(x, k, axis=0)


def _shift_up(x, k):
    return x if k == 0 else pltpu.roll(x, x.shape[0] - k, axis=0)


def _pool_fwd(u, wp, scale, layer):
    m, dp = u.shape
    g = dp // len(POOL_WINDOWS)
    tm = _row_tile(m)
    nt = m // tm
    grid = (nt,)
    prev, _ = _halo_specs(tm, dp, nt)

    def body(u_ref, h_ref, w_ref, s_ref, o_ref):
        i = pl.program_id(0)
        halo = jnp.where(i > 0, h_ref[...], 0.0)
        s = jnp.concatenate([halo, u_ref[...]], axis=0)
        tpos = i * tm + lax.broadcasted_iota(jnp.int32, (tm, 1), 0)
        for gi, win in enumerate(POOL_WINDOWS):
            s = s + _shift_down(s, win // 2)
            cols = slice(gi * g, (gi + 1) * g)
            cnt = jnp.minimum(tpos + 1, win).astype(F32)
            pooled = s[HALO:, cols] / cnt - u_ref[:, cols]
            o_ref[:, cols] = _dot(pooled.astype(BF16), w_ref[gi]) * s_ref[:, cols]

    row = pl.BlockSpec((tm, dp), lambda i: (i, 0))
    return pl.pallas_call(
        body, name="pool_fwd", grid=grid, out_shape=jax.ShapeDtypeStruct((m, dp), F32),
        in_specs=[row, prev, pl.BlockSpec((None, len(POOL_WINDOWS), g, g), lambda i: (layer, 0, 0, 0)),
                  pl.BlockSpec((1, dp), lambda i: (0, 0))],
        out_specs=row, compiler_params=_params(grid))(u, u, wp, scale)


def _pool_bwd(u, dmix, wp, scale, layer):
    m, dp = u.shape
    ng = len(POOL_WINDOWS)
    g = dp // ng
    tm = _row_tile(m)
    nt = m // tm
    grid = (nt,)
    prev, nxt = _halo_specs(tm, dp, nt)

    def body(u_ref, h_ref, dm_ref, dmn_ref, w_ref, s_ref, du_ref, dw_ref, ds_ref):
        i = pl.program_id(0)

        @pl.when(i == 0)
        def _():
            dw_ref[...] = jnp.zeros_like(dw_ref)
            ds_ref[...] = jnp.zeros_like(ds_ref)

        halo = jnp.where(i > 0, h_ref[...], 0.0)
        s = jnp.concatenate([halo, u_ref[...]], axis=0)
        dmn = jnp.where(i < nt - 1, dmn_ref[...], 0.0)
        dmc = jnp.concatenate([dm_ref[...], dmn], axis=0)
        tpos = i * tm + lax.broadcasted_iota(jnp.int32, (tm, 1), 0)
        tpos_x = i * tm + lax.broadcasted_iota(jnp.int32, (tm + HALO, 1), 0)
        for gi, win in enumerate(POOL_WINDOWS):
            s = s + _shift_down(s, win // 2)
            cols = slice(gi * g, (gi + 1) * g)
            cnt = jnp.minimum(tpos + 1, win).astype(F32)
            pooled = (s[HALO:, cols] / cnt - u_ref[:, cols]).astype(BF16)
            wg = w_ref[gi]
            pm = _dot(pooled, wg)
            ds_ref[:, cols] += jnp.sum(dm_ref[:, cols] * pm, axis=0, keepdims=True)
            dpm = (dmc[:, cols] * s_ref[:, cols]).astype(BF16)
            dw_ref[gi] += _dot(pooled, dpm[:tm], TN)
            dpool = _dot(dpm, wg, NT)
            f = dpool / jnp.minimum(tpos_x + 1, win).astype(F32)
            step = 1
            while step < win:
                f = f + _shift_up(f, step)
                step *= 2
            du_ref[:, cols] = f[:tm] - dpool[:tm]

    row = pl.BlockSpec((tm, dp), lambda i: (i, 0))
    return pl.pallas_call(
        body, name="pool_bwd", grid=grid,
        out_shape=(jax.ShapeDtypeStruct((m, dp), F32), jax.ShapeDtypeStruct((ng, g, g), F32),
                   jax.ShapeDtypeStruct((1, dp), F32)),
        in_specs=[row, prev, row, nxt, pl.BlockSpec((None, ng, g, g), lambda i: (layer, 0, 0, 0)),
                  pl.BlockSpec((1, dp), lambda i: (0, 0))],
        out_specs=(row, pl.BlockSpec((ng, g, g), lambda i: (0, 0, 0)), pl.BlockSpec((1, dp), lambda i: (0, 0))),
        compiler_params=_params(grid))(u, u, dmix, dmix, wp, scale)


def _conv_taps(u, w_ref, rows):
    y = None
    for k in range(CONV_WIDTH):
        term = _shift_down(u, CONV_WIDTH - 1 - k)[HALO:HALO + rows] * w_ref[k:k + 1, :]
        y = term if y is None else y + term
    return y


def _layernorm_stats(y):
    mu = jnp.mean(y, axis=-1, keepdims=True)
    yc = y - mu
    rstd = lax.rsqrt(jnp.mean(yc * yc, axis=-1, keepdims=True) + EPS)
    return yc * rstd, rstd


def _conv_fwd(uc, wdw, b, lg, lb):
    m, c2 = uc.shape
    c = c2 // 2
    tm = _row_tile(m)
    nt = m // tm
    grid = (nt,)
    prev, _ = _halo_specs(tm, c2, nt)

    def body(x_ref, h_ref, w_ref, b_ref, g_ref, bb_ref, o_ref):
        i = pl.program_id(0)
        halo = jnp.where(i > 0, h_ref[...], 0.0)
        xc = jnp.concatenate([halo, x_ref[...]], axis=0)
        u = xc[:, :c] * jax.nn.sigmoid(xc[:, c:])
        y = _conv_taps(u, w_ref, tm) + b_ref[...]
        xhat, _ = _layernorm_stats(y)
        ln = xhat * g_ref[...] + bb_ref[...]
        o_ref[...] = (ln * jax.nn.sigmoid(ln)).astype(BF16)

    vec = pl.BlockSpec((1, c), lambda i: (0, 0))
    return pl.pallas_call(
        body, name="conv_fwd", grid=grid, out_shape=jax.ShapeDtypeStruct((m, c), BF16),
        in_specs=[pl.BlockSpec((tm, c2), lambda i: (i, 0)), prev, pl.BlockSpec((HALO, c), lambda i: (0, 0)), vec, vec, vec],
        out_specs=pl.BlockSpec((tm, c), lambda i: (i, 0)), compiler_params=_params(grid))(uc, uc, wdw, b, lg, lb)


def _conv_bwd(uc, ds, wdw, b, lg, lb):
    m, c2 = uc.shape
    c = c2 // 2
    tm = _row_tile(m)
    nt = m // tm
    grid = (nt,)
    prev, nxt = _halo_specs(tm, c2, nt)
    _, nxt_c = _halo_specs(tm, c, nt)

    def body(x_ref, hp_ref, hn_ref, ds_ref, dsn_ref, w_ref, b_ref, g_ref, bb_ref,
             dx_ref, dw_ref, db_ref, dg_ref, dbb_ref):
        i = pl.program_id(0)

        @pl.when(i == 0)
        def _():
            for ref in (dw_ref, db_ref, dg_ref, dbb_ref):
                ref[...] = jnp.zeros_like(ref)

        hp = jnp.where(i > 0, hp_ref[...], 0.0)
        xc = jnp.concatenate([hp, x_ref[...], hn_ref[...]], axis=0)
        sg = jax.nn.sigmoid(xc[:, c:])
        u = xc[:, :c] * sg
        rows = tm + HALO
        y = _conv_taps(u, w_ref, rows) + b_ref[...]
        xhat, rstd = _layernorm_stats(y)
        ln = xhat * g_ref[...] + bb_ref[...]
        sl = jax.nn.sigmoid(ln)
        dsn = jnp.where(i < nt - 1, dsn_ref[...], 0.0)
        dsx = jnp.concatenate([ds_ref[...], dsn], axis=0)
        dln = dsx * sl * (1.0 + ln * (1.0 - sl))
        dxh = dln * g_ref[...]
        dy = rstd * (dxh - jnp.mean(dxh, axis=-1, keepdims=True)
                     - xhat * jnp.mean(dxh * xhat, axis=-1, keepdims=True))
        dyt = dy[:tm]
        dg_ref[...] += jnp.sum(dln[:tm] * xhat[:tm], axis=0, keepdims=True)
        dbb_ref[...] += jnp.sum(dln[:tm], axis=0, keepdims=True)
        db_ref[...] += jnp.sum(dyt, axis=0, keepdims=True)
        du = None
        for k in range(CONV_WIDTH):
            lag = CONV_WIDTH - 1 - k
            dw_ref[k:k + 1, :] += jnp.sum(dyt * _shift_down(u, lag)[HALO:HALO + tm], axis=0, keepdims=True)
            term = _shift_up(dy, lag)[:tm] * w_ref[k:k + 1, :]
            du = term if du is None else du + term
        a_t = xc[HALO:HALO + tm, :c]
        sg_t = sg[HALO:HALO + tm]
        dx_ref[:, :c] = du * sg_t
        dx_ref[:, c:] = du * a_t * sg_t * (1.0 - sg_t)

    vec = pl.BlockSpec((1, c), lambda i: (0, 0))
    wsp = pl.BlockSpec((HALO, c), lambda i: (0, 0))
    vshape = jax.ShapeDtypeStruct((1, c), F32)
    return pl.pallas_call(
        body, name="conv_bwd", grid=grid,
        out_shape=(jax.ShapeDtypeStruct((m, c2), F32), jax.ShapeDtypeStruct((HALO, c), F32), vshape, vshape, vshape),
        in_specs=[pl.BlockSpec((tm, c2), lambda i: (i, 0)), prev, nxt, pl.BlockSpec((tm, c), lambda i: (i, 0)), nxt_c,
                  wsp, vec, vec, vec],
        out_specs=(pl.BlockSpec((tm, c2), lambda i: (i, 0)), wsp, vec, vec, vec),
        compiler_params=_params(grid))(uc, uc, uc, ds, ds, wdw, b, lg, lb)


def _mesh_pos():
    return lax.axis_index("x"), lax.axis_index("y"), lax.axis_index("c")


def _other_chips(x, y):
    return [(1 - x, y), (x, 1 - y), (1 - x, 1 - y)]


def _comm_sems(n, per):
    return [pltpu.SemaphoreType.DMA((n, per)), pltpu.SemaphoreType.DMA((n, per)), pltpu.SemaphoreType.DMA((n,))]


def _comm_call(body, name, arrs, out_shapes, n_sems):
    n = len(arrs)
    return pl.pallas_call(
        body, name=name, out_shape=out_shapes, in_specs=[ANY] * n, out_specs=tuple([ANY] * len(out_shapes)),
        scratch_shapes=_comm_sems(n, n_sems),
    )(*arrs)


class _GatherPlan:
    def __init__(self, ins, outs, send_sems, recv_sems, local_sems):
        self.ins, self.outs = ins, outs
        self.send_sems, self.recv_sems, self.local_sems = send_sems, recv_sems, local_sems
        x, y, c = _mesh_pos()
        self.c, self.me, self.sib = c, (x, y, c), (x, y, 1 - c)
        self.chips = _other_chips(x, y)

    def _slot(self, a, p):
        return self.outs[a].at[4 * p[0] + 2 * p[1] + p[2]]

    def _copy(self, a, k, block, to, own=False):
        return pltpu.make_async_remote_copy(
            src_ref=self.ins[a] if own else self._slot(a, block), dst_ref=self._slot(a, block),
            send_sem=self.send_sems.at[a, k], recv_sem=self.recv_sems.at[a, k], device_id=to, device_id_type=MESH)

    def _local(self, a):
        return pltpu.make_async_copy(self.ins[a], self._slot(a, self.me), self.local_sems.at[a])

    def start(self):
        for a in range(len(self.ins)):
            for j, chip in enumerate(self.chips):
                self._copy(a, 1 + j, self.me, (*chip, self.c), own=True).start()
            self._copy(a, 0, self.me, self.sib, own=True).start()
            self._local(a).start()

    def forward(self, a):
        for j, chip in enumerate(self.chips):
            self._copy(a, 1 + j, (*chip, self.c), self.me).wait_recv()
            self._copy(a, 4 + j, (*chip, self.c), self.sib).start()

    def finish(self):
        for a in range(len(self.ins)):
            self._copy(a, 0, self.sib, self.me).wait_recv()
            for j, chip in enumerate(self.chips):
                self._copy(a, 4 + j, (*chip, 1 - self.c), self.me).wait_recv()
        for a in range(len(self.ins)):
            self._copy(a, 0, self.me, self.sib, own=True).wait_send()
            for j, chip in enumerate(self.chips):
                self._copy(a, 1 + j, self.me, (*chip, self.c), own=True).wait_send()
                self._copy(a, 4 + j, (*chip, self.c), self.sib).wait_send()
            self._local(a).wait()


def _gather_shapes(arrs):
    return tuple(jax.ShapeDtypeStruct((N_DEV,) + a.shape, a.dtype) for a in arrs)


def _all_gather(arrs, name):
    n = len(arrs)

    def body(*refs):
        plan = _GatherPlan(refs[:n], refs[n:2 * n], *refs[2 * n:])
        plan.start()
        for a in range(n):
            plan.forward(a)
        plan.finish()

    return _comm_call(body, name, arrs, _gather_shapes(arrs), 7)


class _ChipExchangePlan:
    def __init__(self, ins, outs, send_sems, recv_sems, local_sems):
        self.ins, self.outs = ins, outs
        self.send_sems, self.recv_sems, self.local_sems = send_sems, recv_sems, local_sems
        x, y, c = _mesh_pos()
        self.pos, self.mine = (x, y, c), 2 * x + y
        self.chips = _other_chips(x, y)

    def _send(self, a, j):
        chip = self.chips[j]
        return pltpu.make_async_remote_copy(
            src_ref=self.ins[a].at[2 * chip[0] + chip[1]], dst_ref=self.outs[a].at[self.mine],
            send_sem=self.send_sems.at[a, j], recv_sem=self.recv_sems.at[a, j], device_id=(*chip, self.pos[2]),
            device_id_type=MESH)

    def _landing(self, a, j):
        chip = self.chips[j]
        ref = self.outs[a].at[2 * chip[0] + chip[1]]
        return pltpu.make_async_remote_copy(src_ref=ref, dst_ref=ref, send_sem=self.send_sems.at[a, j],
                                            recv_sem=self.recv_sems.at[a, j], device_id=self.pos, device_id_type=MESH)

    def _local(self, a):
        return pltpu.make_async_copy(self.ins[a].at[self.mine], self.outs[a].at[self.mine], self.local_sems.at[a])

    def start(self):
        for a in range(len(self.ins)):
            for j in range(3):
                self._send(a, j).start()
            self._local(a).start()

    def finish(self):
        for a in range(len(self.ins)):
            for j in range(3):
                self._landing(a, j).wait_recv()
        for a in range(len(self.ins)):
            for j in range(3):
                self._send(a, j).wait_send()
            self._local(a).wait()


def _pair_tile(kk):
    return _pick(kk, (1024, 704, 576, 512, 256, 128, 64))


def _pair_exchange(parts, core, name):
    _, kk, n = parts.shape
    tr = _pair_tile(kk)
    nr = kk // tr
    grid = (4, nr)

    def body(c_ref, src_ref, recv_ref, send_sems, recv_sems):
        q, r = pl.program_id(0), pl.program_id(1)
        x, y, c = _mesh_pos()
        step = q * nr + r
        slot = step % 2
        dst = recv_ref.at[pl.ds(pl.multiple_of(step * tr, tr), tr), :]
        cp = pltpu.make_async_remote_copy(src_ref=src_ref, dst_ref=dst, send_sem=send_sems.at[slot],
                                          recv_sem=recv_sems.at[slot], device_id=(x, y, 1 - c), device_id_type=MESH)
        cp.start()
        cp.wait_send()
        cp.wait_recv()

    spec = pltpu.PrefetchScalarGridSpec(
        num_scalar_prefetch=1, grid=grid,
        in_specs=[pl.BlockSpec((tr, n), lambda q, r, c: ((2 * q + 1 - c[0]) * nr + r, 0))],
        out_specs=ANY, scratch_shapes=[pltpu.SemaphoreType.DMA((2,)), pltpu.SemaphoreType.DMA((2,))])
    recv = pl.pallas_call(body, name=name, grid_spec=spec, out_shape=jax.ShapeDtypeStruct((4 * kk, n), parts.dtype),
                          compiler_params=_params(grid))(core, parts.reshape(N_DEV * kk, n))
    return recv.reshape(4, kk, n)


def _exchange_shapes(parts):
    return tuple(jax.ShapeDtypeStruct(p.shape, p.dtype) for p in parts)


def _pair_add(parts, recv, core, name):
    _, kk, n = recv.shape
    tr = _pair_tile(kk)
    grid = (4, kk // tr)

    def body(c_ref, a_ref, b_ref, o_ref):
        o_ref[...] = (a_ref[...].astype(F32) + b_ref[...].astype(F32)).astype(o_ref.dtype)

    out = pl.BlockSpec((None, tr, n), lambda q, r, c: (q, r, 0))
    spec = pltpu.PrefetchScalarGridSpec(
        num_scalar_prefetch=1, grid=grid,
        in_specs=[pl.BlockSpec((None, None, tr, n), lambda q, r, c: (q, c[0], r, 0)), out], out_specs=out)
    return pl.pallas_call(body, name=name, grid_spec=spec, out_shape=jax.ShapeDtypeStruct(recv.shape, recv.dtype),
                          compiler_params=_params(grid))(core, parts.reshape(4, 2, kk, n), recv)


def _adamw_math(w, g, m, v):
    m2 = ADAM_B1 * m + (1.0 - ADAM_B1) * g
    v2 = ADAM_B2 * v + (1.0 - ADAM_B2) * (g * g)
    m_hat = m2 / (1.0 - ADAM_B1 ** ADAM_STEP)
    v_hat = v2 / (1.0 - ADAM_B2 ** ADAM_STEP)
    delta = -ADAM_LR * (m_hat / (jnp.sqrt(v_hat) + ADAM_EPS) + ADAM_WD * w)
    return delta, m2, v2


def _adamw_sharded(parts, w, m, v, name):
    nl, kk, n = w.shape
    tr = _pick(kk, (256, 128, 64)) if n <= 1024 else _pick(kk, (64, 32))
    grid = (nl, kk // tr)

    def body(*refs):
        p_refs = refs[:nl]
        w_ref, m_ref, v_ref, g_ref, d_ref, m2_ref, v2_ref = refs[nl:]
        layer = pl.program_id(0)
        g = None
        for l, p_ref in enumerate(p_refs):
            s = p_ref[0].astype(F32)
            for q in range(1, 4):
                s = s + p_ref[q].astype(F32)
            g = s if g is None else jnp.where(layer == l, s, g)
        d, m2, v2 = _adamw_math(w_ref[...], g, m_ref[...], v_ref[...])
        g_ref[...] = g
        d_ref[...] = d
        m2_ref[...] = m2
        v2_ref[...] = v2

    spec = pl.BlockSpec((None, tr, n), lambda l, r: (l, r, 0))
    shp = jax.ShapeDtypeStruct(w.shape, F32)
    p_specs = [pl.BlockSpec((4, tr, n), lambda l, r, k=k: (0, jnp.where(l == k, r, 0), 0)) for k in range(nl)]
    return pl.pallas_call(
        body, name=name, grid=grid, out_shape=(shp, shp, shp, shp), in_specs=p_specs + [spec, spec, spec],
        out_specs=(spec, spec, spec, spec), compiler_params=_params(grid))(*parts, w, m, v)


def _sum_slots(g8):
    _, r, _ = g8.shape

    def body(g_ref, o_ref):
        s = g_ref[0]
        for d in range(1, N_DEV):
            s = s + g_ref[d]
        o_ref[...] = s

    return pl.pallas_call(body, name="sum_slots", out_shape=jax.ShapeDtypeStruct((r, 128), F32))(g8)


def _adamw_flat(w, g, m, v):
    def body(w_ref, g_ref, m_ref, v_ref, d_ref, m2_ref, v2_ref):
        d, m2, v2 = _adamw_math(w_ref[...], g_ref[...], m_ref[...], v_ref[...])
        d_ref[...] = d
        m2_ref[...] = m2
        v2_ref[...] = v2

    shp = jax.ShapeDtypeStruct(w.shape, F32)
    return pl.pallas_call(body, name="adamw_flat", out_shape=(shp, shp, shp))(w, g, m, v)


def _pack(arrs):
    flat = jnp.concatenate([a.reshape(-1) for a in arrs])
    pad = (-flat.shape[0]) % 1024
    return jnp.pad(flat, (0, pad)).reshape(-1, 128)


def _unpack(buf, like):
    flat = buf.reshape(-1)
    out, off = [], 0
    for a in like:
        out.append(flat[off:off + a.size].reshape(a.shape))
        off += a.size
    return out


def _heads(x2d, scale=None):
    lp, da = x2d.shape
    xh = x2d.reshape(lp, N_HEADS, da // N_HEADS).transpose(1, 0, 2)
    if scale is not None:
        xh = xh * scale
    return xh


def _unheads(xh):
    nh, lp, dh = xh.shape
    return xh.transpose(1, 0, 2).reshape(lp, nh * dh)


def _hosted(hosts, site, kind, fn):
    if hosts and site in hosts:
        arrays, done = hosts[site]
        res, got = fn(_Riders(kind, arrays))
        done(got)
        return res
    return fn(None)


def _layer_fwd(h, l, W, P, hosts=None, u1=None):
    d = h.shape[1]
    da, dp = d // 2, d // 4
    dh = da // N_HEADS
    sv = {"h": h}
    if u1 is None:
        u1 = _rms_fwd(h, P["pre_mix_g"][l:l + 1], "rms_pre_mix")
    proj = _hosted(hosts, "mm_in", "gather", lambda r: _mm_nt(u1, W["in"], "mm_in", riders=r))
    qh = _heads(proj[:, :da], dh ** -0.5).astype(BF16)
    kh = _heads(proj[:, da:2 * da]).astype(BF16)
    vh = _heads(proj[:, 2 * da:3 * da]).astype(BF16)
    u_pool = proj[:, 3 * da:3 * da + dp]
    u_conv = proj[:, 3 * da + dp:]
    if hosts and "attn" in hosts:
        arrs, done = hosts["attn"]
        oh, tot, cnt, *gathered = _attn_fwd(qh, kh, vh, arrs)
        done(gathered)
    else:
        oh, tot, cnt = _attn_fwd(qh, kh, vh)
    o_attn = _unheads(oh)
    o_pool = _pool_fwd(u_pool, W["pool"], P["pool_scale"][l:l + 1], l)
    s_conv = _conv_fwd(u_conv, W["dw"][l], P["b_dw"][l:l + 1], P["conv_ln_g"][l:l + 1], P["conv_ln_b"][l:l + 1])
    o_conv = _mm_nn([(s_conv, W["pw"])], "mm_pw")
    merged = _merge_fwd(o_attn, o_pool, o_conv, P["mix_out_g"][l:l + 1])
    z1 = _mm_nn([(merged, W["out"])], "mm_out")
    h1, u2 = _resid_rms_fwd(h, z1, P["post_mix_g"][l:l + 1], P["pre_ffn_g"][l:l + 1], "resid_post_mix")
    a_s, b_s, act_s = _hosted(hosts, "ffn_up", "gather", lambda r: _ffn_up(u2, W["gate"], W["up"], r))
    ff = _hosted(hosts, "mm_down", "gather", lambda r: _mm_nn([(act_s, W["down"])], "mm_down", riders=r))
    g_next = P["pre_mix_g"][l + 1:l + 2] if l + 1 < P["pre_mix_g"].shape[0] else None
    h2, u1_next = _resid_rms_fwd(h1, ff, P["post_ffn_g"][l:l + 1], g_next, "resid_post_ffn")
    sv.update(u1=u1, qh=qh, kh=kh, vh=vh, tot=tot, cnt=cnt, u_pool=u_pool, u_conv=u_conv, o_attn=o_attn, o_pool=o_pool,
              o_conv=o_conv, s_conv=s_conv, merged=merged, z1=z1, h1=h1, u2=u2, a_s=a_s, b_s=b_s, act_s=act_s, ff=ff)
    return h2, sv, u1_next


EARLY = ("w_down", "w_gate", "w_up", "w_out", "w_pw")


def _layer_bwd(dh_out, l, W, P, sv, exchange_with_attn=None, hosts=None, exchange_w_in=None):
    d = dh_out.shape[1]
    da = d // 2
    dhd = da // N_HEADS
    lp = dh_out.shape[0]
    g = {}
    dff, g["post_ffn_g"] = _rms_bwd(sv["ff"], P["post_ffn_g"][l:l + 1], dh_out, None, BF16, "rms_bwd_post_ffn")
    da_s, db_s = _hosted(hosts, "ffn_bwd_act", "exchange",
                         lambda r: _ffn_bwd_act(dff, W["down"], sv["a_s"], sv["b_s"], r))
    g["w_down"] = _mm_tn(sv["act_s"], dff, "mm_dw_down")
    du2 = _hosted(hosts, "mm_d_u2", "exchange",
                  lambda r: _mm_nn([(da_s, W["gate"]), (db_s, W["up"])], "mm_d_u2", riders=r))
    g["w_gate"] = _hosted(hosts, "mm_dw_gate", "exchange", lambda r: _mm_tn(da_s, sv["u2"], "mm_dw_gate", riders=r))
    g["w_up"] = _mm_tn(db_s, sv["u2"], "mm_dw_up")
    dh1, g["pre_ffn_g"] = _rms_bwd(sv["h1"], P["pre_ffn_g"][l:l + 1], du2, dh_out, F32, "rms_bwd_pre_ffn")
    dz1, g["post_mix_g"] = _rms_bwd(sv["z1"], P["post_mix_g"][l:l + 1], dh1, None, BF16, "rms_bwd_post_mix")
    dmerged = _mm_nt(dz1, W["out"], "mm_d_merged")
    g["w_out"] = _mm_tn(sv["merged"], dz1, "mm_dw_out")
    d_oa, d_op, d_oc, g["mix_out_g"] = _merge_bwd(sv["o_attn"], sv["o_pool"], sv["o_conv"], P["mix_out_g"][l:l + 1], dmerged)
    d_s = _mm_nt(d_oc, W["pw"], "mm_d_sconv")
    g["w_pw"] = _mm_tn(sv["s_conv"], d_oc, "mm_dw_pw")
    d_uc, dwdw, g["b_dw"], g["conv_ln_g"], g["conv_ln_b"] = _conv_bwd(
        sv["u_conv"], d_s, W["dw"][l], P["b_dw"][l:l + 1], P["conv_ln_g"][l:l + 1], P["conv_ln_b"][l:l + 1])
    g["w_dw"] = dwdw[:CONV_WIDTH]
    d_up, g["w_pool"], g["pool_scale"] = _pool_bwd(sv["u_pool"], d_op, W["pool"], P["pool_scale"][l:l + 1], l)
    doh = _heads(d_oa).astype(BF16)
    riders = exchange_with_attn(g) if exchange_with_attn is not None else []
    dqh, dkt, dvt, *exchanged = _attn_bwd(sv["qh"], sv["qh"].transpose(0, 2, 1), sv["kh"], sv["vh"], doh,
                                          doh.transpose(0, 2, 1), sv["tot"], sv["cnt"], dhd ** -0.5, riders)
    dk2, dv2 = (a.transpose(1, 3, 0, 2).reshape(lp, da) for a in (dkt, dvt))
    dproj = jnp.concatenate([_unheads(dqh), dk2, dv2, d_up, d_uc], axis=1).astype(BF16)
    g["w_in"] = _mm_tn(dproj, sv["u1"], "mm_dw_in")
    late = {"mm_d_u1": exchange_w_in(g)} if exchange_w_in is not None else None
    du1 = _hosted(late, "mm_d_u1", "exchange", lambda r: _mm_nn([(dproj, W["in"])], "mm_d_u1", riders=r))
    dh0, g["pre_mix_g"] = _rms_bwd(sv["h"], P["pre_mix_g"][l:l + 1], du1, dh1, F32, "rms_bwd_pre_mix")
    return dh0, g, exchanged


SHARDED = ("w_in", "w_gate", "w_up", "w_down", "w_out", "w_pw")
SMALL = ("pre_mix_g", "w_pool", "pool_scale", "b_dw", "conv_ln_g", "conv_ln_b", "mix_out_g", "post_mix_g",
         "pre_ffn_g", "post_ffn_g", "w_dw", "meta_tokens")
ORDER = ("meta_tokens", "pre_mix_g", "w_in", "w_pool", "pool_scale", "w_dw", "b_dw", "conv_ln_g", "conv_ln_b", "w_pw",
         "mix_out_g", "w_out", "post_mix_g", "pre_ffn_g", "w_gate", "w_up", "w_down", "post_ffn_g")


def kernel(x, meta_tokens, pre_mix_g, w_in, w_pool, pool_scale, w_dw, b_dw, conv_ln_g, conv_ln_b, w_pw, mix_out_g, w_out, post_mix_g, pre_ffn_g, w_gate, w_up, w_down, post_ffn_g, loss_target, m_meta_tokens, m_pre_mix_g, m_w_in, m_w_pool, m_pool_scale, m_w_dw, m_b_dw, m_conv_ln_g, m_conv_ln_b, m_w_pw, m_mix_out_g, m_w_out, m_post_mix_g, m_pre_ffn_g, m_w_gate, m_w_up, m_w_down, m_post_ffn_g, v_meta_tokens, v_pre_mix_g, v_w_in, v_w_pool, v_pool_scale, v_w_dw, v_b_dw, v_conv_ln_g, v_conv_ln_b, v_w_pw, v_mix_out_g, v_w_out, v_post_mix_g, v_pre_ffn_g, v_w_gate, v_w_up, v_w_down, v_post_ffn_g):
    P = dict(meta_tokens=meta_tokens, pre_mix_g=pre_mix_g, w_in=w_in, w_pool=w_pool, pool_scale=pool_scale, w_dw=w_dw,
             b_dw=b_dw, conv_ln_g=conv_ln_g, conv_ln_b=conv_ln_b, w_pw=w_pw, mix_out_g=mix_out_g, w_out=w_out,
             post_mix_g=post_mix_g, pre_ffn_g=pre_ffn_g, w_gate=w_gate, w_up=w_up, w_down=w_down, post_ffn_g=post_ffn_g)
    M = dict(meta_tokens=m_meta_tokens, pre_mix_g=m_pre_mix_g, w_in=m_w_in, w_pool=m_w_pool, pool_scale=m_pool_scale,
             w_dw=m_w_dw, b_dw=m_b_dw, conv_ln_g=m_conv_ln_g, conv_ln_b=m_conv_ln_b, w_pw=m_w_pw, mix_out_g=m_mix_out_g,
             w_out=m_w_out, post_mix_g=m_post_mix_g, pre_ffn_g=m_pre_ffn_g, w_gate=m_w_gate, w_up=m_w_up,
             w_down=m_w_down, post_ffn_g=m_post_ffn_g)
    V = dict(meta_tokens=v_meta_tokens, pre_mix_g=v_pre_mix_g, w_in=v_w_in, w_pool=v_w_pool, pool_scale=v_pool_scale,
             w_dw=v_w_dw, b_dw=v_b_dw, conv_ln_g=v_conv_ln_g, conv_ln_b=v_conv_ln_b, w_pw=v_w_pw, mix_out_g=v_mix_out_g,
             w_out=v_w_out, post_mix_g=v_post_mix_g, pre_ffn_g=v_pre_ffn_g, w_gate=v_w_gate, w_up=v_w_up,
             w_down=v_w_down, post_ffn_g=v_post_ffn_g)
    xi, yi, ci = _mesh_pos()
    dev = 4 * xi + 2 * yi + ci
    n_tok, d = x.shape[1], x.shape[2]
    n_meta = meta_tokens.shape[0]
    n_layers = w_in.shape[0]
    c = d // 4
    l_real = n_meta + n_tok
    lp = -(-l_real // ATT_BLOCK) * ATT_BLOCK

    TRANSPOSED = ("w_in", "w_gate", "w_up")
    shard = {k: (jnp.swapaxes(P[k], 1, 2) if k in TRANSPOSED else P[k]) for k in SHARDED}

    def bf16_shard(k, l):
        return shard[k][l].astype(BF16)

    def whole(gathered):
        return gathered.reshape(-1, gathered.shape[-1])

    g_in0, g_dw, g_meta = _all_gather([bf16_shard("w_in", 0), w_dw, meta_tokens], "gather_first")
    wdw_full = g_dw.transpose(1, 2, 0, 3).reshape(n_layers, CONV_WIDTH, c)
    common = {"pool": w_pool.astype(BF16), "dw": jnp.pad(wdw_full, ((0, 0), (0, HALO - CONV_WIDTH), (0, 0)))}
    W = [dict(common) for _ in range(n_layers)]
    W[0]["in"] = whole(g_in0)
    meta_full = g_meta.transpose(1, 0, 2).reshape(n_meta, d)
    def gather_site(keys):
        def done(gathered):
            for (k, l), g in zip(keys, gathered):
                W[l][k[2:]] = whole(g)
        return [bf16_shard(k, l) for k, l in keys], done

    def fwd_hosts(l):
        first, then = (("w_up",), ("w_down",)) if l > 0 else (("w_pw", "w_out"), ("w_gate", "w_up", "w_down"))
        hosts = {"mm_in": gather_site([(k, l) for k in first]), "attn": gather_site([(k, l) for k in then])}
        if l + 1 < n_layers:
            hosts["ffn_up"] = gather_site([(k, l + 1) for k in ("w_in", "w_pw", "w_out")])
            hosts["mm_down"] = gather_site([("w_gate", l + 1)])
        return hosts

    h = jnp.concatenate([meta_full, x[0], jnp.zeros((lp - l_real, d), F32)], axis=0)
    saved, u1 = [], None
    for l in range(n_layers):
        h, sv, u1 = _layer_fwd(h, l, W[l], P, fwd_hosts(l), u1)
        saved.append(sv)
    target = jnp.pad(loss_target[0], ((n_meta, lp - l_real), (0, 0)))
    loss_part, dh = _loss_grad(h, target, n_meta, n_tok)
    loss = lax.psum(loss_part[0, 0], ("x", "y", "c"))

    core = jnp.reshape(ci, (1,)).astype(jnp.int32)
    by_chip = {}

    def pair_reduce(g, keys, l):
        res = {}
        for k in keys:
            p = g[k].reshape(N_DEV, g[k].shape[0] // N_DEV, g[k].shape[1])
            recv = _pair_exchange(p, core, f"pair_exchange_{k}_{l}")
            res[k] = _pair_add(p, recv, core, f"pair_add_{k}_{l}")
        return res

    def exchange_site(sums, keys, l):
        def done(exchanged):
            by_chip.update({(l, k): e for k, e in zip(keys, exchanged)})
        return [sums[k] for k in keys], done

    def ride(g):
        sums = pair_reduce(g, EARLY, 0)
        return [sums[k] for k in EARLY]

    grads = [None] * n_layers
    hosts = None
    for l in reversed(range(n_layers)):
        last = (lambda g: exchange_site(pair_reduce(g, ("w_in",), 0), ("w_in",), 0)) if l == 0 else None
        dh, grads[l], exchanged = _layer_bwd(dh, l, W[l], P, saved[l], ride if l == 0 else None, hosts, last)
        if l == 0:
            by_chip.update({(0, k): e for k, e in zip(EARLY, exchanged)})
        else:
            sums = pair_reduce(grads[l], SHARDED, l)
            hosts = {"ffn_bwd_act": exchange_site(sums, ("w_down",), l),
                     "mm_d_u2": exchange_site(sums, ("w_gate", "w_up"), l),
                     "mm_dw_gate": exchange_site(sums, ("w_in", "w_out", "w_pw"), l)}
    grad_x = dh[n_meta:l_real][None]
    out = {}
    for k in SHARDED:
        view = (lambda a: jnp.swapaxes(a, 1, 2)) if k in TRANSPOSED else (lambda a: a)
        res = _adamw_sharded([by_chip[(l, k)] for l in range(n_layers)], view(P[k]), view(M[k]), view(V[k]), "adamw_" + k)
        out[k] = tuple(view(r) for r in res)

    small_parts = []
    for k in SMALL:
        if k == "meta_tokens":
            small_parts.append(dh[:n_meta])
        else:
            small_parts.append(jnp.stack([grads[l][k].reshape(P[k].shape[1:] if k != "w_dw" else (CONV_WIDTH, c))
                                          for l in range(n_layers)], axis=0))
    (g8,) = _all_gather([_pack(small_parts)], "gather_small_grads")
    g_small = dict(zip(SMALL, _unpack(_sum_slots(g8), small_parts)))
    g_small["w_dw"] = lax.dynamic_slice_in_dim(g_small["w_dw"], dev * w_dw.shape[2], w_dw.shape[2], axis=2)
    g_small["meta_tokens"] = lax.dynamic_slice_in_dim(g_small["meta_tokens"], dev * meta_tokens.shape[1],
                                                      meta_tokens.shape[1], axis=1)
    like = [P[k] for k in SMALL]
    res = _adamw_flat(_pack(like), _pack([g_small[k] for k in SMALL]), _pack([M[k] for k in SMALL]),
                      _pack([V[k] for k in SMALL]))
    res = [_unpack(r, like) for r in res]
    for i, k in enumerate(SMALL):
        out[k] = (g_small[k], res[0][i], res[1][i], res[2][i])

    return (loss, grad_x, *[out[k][0] for k in ORDER], *[out[k][1] for k in ORDER],
            *[out[k][2] for k in ORDER], *[out[k][3] for k in ORDER])
```

```python
import functools

import jax
import jax.numpy as jnp
from jax import lax
from jax.experimental import pallas as pl
from jax.experimental.pallas import tpu as pltpu

F32 = jnp.float32
BF16 = jnp.bfloat16
EPS = 1e-6
N_HEADS = 16
POOL_WINDOWS = (2, 4, 8, 16)
CONV_WIDTH = 31
HALO = 32
ATT_BLOCK = 128
DEAD_LOG_WEIGHT = 110.0
GATHER_BYTES_PER_HEAD = 450_000
N_DEV = 8
VMEM_LIMIT = 60 * 1024 * 1024

ADAM_LR = 0.001
ADAM_B1 = 0.9
ADAM_B2 = 0.999
ADAM_EPS = 1e-08
ADAM_WD = 0.01
ADAM_STEP = 10

NN = (((1,), (0,)), ((), ()))
NT = (((1,), (1,)), ((), ()))
TN = (((0,), (0,)), ((), ()))
MESH = pl.DeviceIdType.MESH
ANY = pl.BlockSpec(memory_space=pl.ANY)


def _pick(n, cands):
    for c in cands:
        if c <= n and n % c == 0:
            return c
    return n


def _params(grid):
    return pltpu.CompilerParams(dimension_semantics=("arbitrary",) * len(grid), vmem_limit_bytes=VMEM_LIMIT)


def _dot(a, b, dims=NN):
    return lax.dot_general(a, b, dims, preferred_element_type=F32)


class _Riders:
    def __init__(self, kind, arrays):
        self.gather = kind == "gather"
        self.arrays = list(arrays)
        self.n = len(self.arrays)
        self.out_shapes = _gather_shapes(self.arrays) if self.gather else _exchange_shapes(self.arrays)

    def call(self, body, name, grid, in_specs, out_shapes, out_specs, scratch, operands):
        n, n_in, n_out, n_scr = self.n, len(in_specs), len(out_shapes), len(scratch)

        def wrapped(*refs):
            ins, r_in = refs[:n_in], refs[n_in:n_in + n]
            outs, r_out = refs[n_in + n:n_in + n + n_out], refs[n_in + n + n_out:n_in + 2 * n + n_out]
            scr, sems = refs[n_in + 2 * n + n_out:n_in + 2 * n + n_out + n_scr], refs[n_in + 2 * n + n_out + n_scr:]
            ids = [pl.program_id(a) for a in range(len(grid))]
            first = functools.reduce(jnp.logical_and, [i == 0 for i in ids])
            last = functools.reduce(jnp.logical_and, [i == g - 1 for i, g in zip(ids, grid)])
            plan = (_GatherPlan if self.gather else _ChipExchangePlan)(r_in, r_out, *sems)
            pl.when(first)(plan.start)
            body(*ins, *outs, *scr)

            @pl.when(last)
            def _():
                if self.gather:
                    for a in range(n):
                        plan.forward(a)
                plan.finish()

        res = pl.pallas_call(
            wrapped, name=name, grid=grid, in_specs=list(in_specs) + [ANY] * n,
            out_shape=tuple(out_shapes) + tuple(self.out_shapes), out_specs=tuple(out_specs) + (ANY,) * n,
            scratch_shapes=list(scratch) + _comm_sems(n, 7 if self.gather else 3), compiler_params=_params(grid),
        )(*operands, *self.arrays)
        return tuple(res[:n_out]), list(res[n_out:])


def _call(riders, body, name, grid, in_specs, out_shapes, out_specs, scratch, operands):
    if riders is not None:
        return riders.call(body, name, grid, in_specs, out_shapes, out_specs, scratch, operands)
    return tuple(pl.pallas_call(body, name=name, grid=grid, in_specs=list(in_specs), out_shape=tuple(out_shapes),
                                out_specs=tuple(out_specs), scratch_shapes=list(scratch),
                                compiler_params=_params(grid))(*operands))


def _matmul(name, pairs, specs, out_shape, out_spec, grid, dims, acc_shape, riders=None):
    n = len(pairs)
    nk = grid[-1]
    kaxis = len(grid) - 1

    def body(*refs):
        o_ref, acc = refs[2 * n], refs[2 * n + 1]
        tot = None
        for p in range(n):
            d = _dot(refs[2 * p][...], refs[2 * p + 1][...], dims)
            tot = d if tot is None else tot + d
        if nk == 1:
            o_ref[...] = tot.astype(o_ref.dtype)
            return
        k = pl.program_id(kaxis)

        @pl.when(k == 0)
        def _():
            acc[...] = tot

        @pl.when(k > 0)
        def _():
            acc[...] += tot

        @pl.when(k == nk - 1)
        def _():
            o_ref[...] = acc[...].astype(o_ref.dtype)

    ops, in_specs = [], []
    for (a, b), (sa, sb) in zip(pairs, specs):
        ops += [a, b]
        in_specs += [sa, sb]
    res = _call(riders, body, name, grid, in_specs, [out_shape], [out_spec],
                [pltpu.VMEM(acc_shape if nk > 1 else (8, 128), F32)], ops)
    return res[0] if riders is None else (res[0][0], res[1])


ROW_TILES = (1408, 1024, 512, 256, 128)


def _mm_nt(a, w, name, out_dtype=F32, riders=None):
    m, kk = a.shape
    n = w.shape[0]
    tm = _pick(m, ROW_TILES)
    tn = _pick(n, (1024, 768, 512, 256, 128))
    grid = (m // tm, n // tn, 1)
    sa = pl.BlockSpec((tm, kk), lambda i, j, k: (i, 0))
    sb = pl.BlockSpec((tn, kk), lambda i, j, k: (j, 0))
    return _matmul(name, [(a, w)], [(sa, sb)], jax.ShapeDtypeStruct((m, n), out_dtype),
                   pl.BlockSpec((tm, tn), lambda i, j, k: (i, j)), grid, NT, (tm, tn), riders)


def _mm_nn(pairs, name, out_dtype=F32, riders=None):
    m, kc = pairs[0][0].shape
    n = pairs[0][1].shape[1]
    tm = _pick(m, ROW_TILES if len(pairs) == 1 else (704,) + ROW_TILES[2:])
    tn = _pick(n, (1024, 512, 256, 128))
    tk = _pick(kc, (2048, 1536, 1408, 1024, 768, 512, 256, 128))
    grid = (m // tm, n // tn, kc // tk)
    sa = pl.BlockSpec((tm, tk), lambda i, j, k: (i, k))
    sb = pl.BlockSpec((tk, tn), lambda i, j, k: (k, j))
    return _matmul(name, pairs, [(sa, sb)] * len(pairs), jax.ShapeDtypeStruct((m, n), out_dtype),
                   pl.BlockSpec((tm, tn), lambda i, j, k: (i, j)), grid, NN, (tm, tn), riders)


def _mm_tn(a, b, name, out_dtype=BF16, riders=None):
    l, m = a.shape
    n = b.shape[1]
    tm = _pick(m, (768, 512, 256, 128))
    tn = _pick(n, (1024, 512, 256, 128))
    grid = (m // tm, n // tn, 1)
    sa = pl.BlockSpec((l, tm), lambda i, j, t: (0, i))
    sb = pl.BlockSpec((l, tn), lambda i, j, t: (0, j))
    return _matmul(name, [(a, b)], [(sa, sb)], jax.ShapeDtypeStruct((m, n), out_dtype),
                   pl.BlockSpec((tm, tn), lambda i, j, t: (i, j)), grid, TN, (tm, tn), riders)


def _ffn_up(u, wg, wu, riders=None):
    m, kk = u.shape
    n = wg.shape[0]
    tm = _pick(m, ROW_TILES)
    tn = _pick(n, (512, 256, 128))
    grid = (m // tm, n // tn)

    def body(u_ref, wg_ref, wu_ref, a_ref, b_ref, act_ref):
        uu = u_ref[...]
        a = _dot(uu, wg_ref[...], NT)
        b = _dot(uu, wu_ref[...], NT)
        a_ref[...] = a
        b_ref[...] = b
        act_ref[...] = (a * jax.nn.sigmoid(a) * b).astype(BF16)

    wspec = pl.BlockSpec((tn, kk), lambda i, j: (j, 0))
    ospec = pl.BlockSpec((tm, tn), lambda i, j: (i, j))
    return _call(riders, body, "ffn_up", grid, [pl.BlockSpec((tm, kk), lambda i, j: (i, 0)), wspec, wspec],
                 [jax.ShapeDtypeStruct((m, n), F32), jax.ShapeDtypeStruct((m, n), F32), jax.ShapeDtypeStruct((m, n), BF16)],
                 [ospec, ospec, ospec], [], [u, wg, wu])


def _ffn_bwd_act(dff, wd, a, b, riders=None):
    m, kk = dff.shape
    n = wd.shape[0]
    tm = _pick(m, ROW_TILES)
    tn = _pick(n, (512, 256, 128))
    grid = (m // tm, n // tn)

    def body(d_ref, w_ref, a_ref, b_ref, da_ref, db_ref):
        dact = _dot(d_ref[...], w_ref[...], NT)
        aa, bb = a_ref[...], b_ref[...]
        sg = jax.nn.sigmoid(aa)
        db_ref[...] = (dact * aa * sg).astype(BF16)
        da_ref[...] = (dact * bb * sg * (1.0 + aa * (1.0 - sg))).astype(BF16)

    tspec = pl.BlockSpec((tm, tn), lambda i, j: (i, j))
    return _call(riders, body, "ffn_bwd_act", grid,
                 [pl.BlockSpec((tm, kk), lambda i, j: (i, 0)), pl.BlockSpec((tn, kk), lambda i, j: (j, 0)), tspec, tspec],
                 [jax.ShapeDtypeStruct((m, n), BF16), jax.ShapeDtypeStruct((m, n), BF16)], [tspec, tspec], [], [dff, wd, a, b])


def _rstd(x):
    return lax.rsqrt(jnp.mean(x * x, axis=-1, keepdims=True) + EPS)


def _row_tile(m):
    return _pick(m, (384, 256, 128))


def _rms_fwd(x, g, name):
    m, d = x.shape
    tr = _row_tile(m)
    grid = (m // tr,)

    def body(x_ref, g_ref, o_ref):
        xx = x_ref[...]
        o_ref[...] = (xx * _rstd(xx) * g_ref[...]).astype(BF16)

    row = pl.BlockSpec((tr, d), lambda i: (i, 0))
    return pl.pallas_call(body, name=name, grid=grid, out_shape=jax.ShapeDtypeStruct((m, d), BF16),
                          in_specs=[row, pl.BlockSpec((1, d), lambda i: (0, 0))], out_specs=row,
                          compiler_params=_params(grid))(x, g)


def _resid_rms_fwd(h, z, g, g_next, name):
    m, d = h.shape
    tr = _row_tile(m)
    grid = (m // tr,)
    has_next = g_next is not None

    def body(h_ref, z_ref, g_ref, *rest):
        zz = z_ref[...]
        hh = h_ref[...] + zz * _rstd(zz) * g_ref[...]
        rest[-2 if has_next else -1][...] = hh
        if has_next:
            rest[-1][...] = (hh * _rstd(hh) * rest[0][...]).astype(BF16)

    row = pl.BlockSpec((tr, d), lambda i: (i, 0))
    vec = pl.BlockSpec((1, d), lambda i: (0, 0))
    shapes = [jax.ShapeDtypeStruct((m, d), F32)] + ([jax.ShapeDtypeStruct((m, d), BF16)] if has_next else [])
    res = pl.pallas_call(body, name=name, grid=grid, out_shape=tuple(shapes),
                         in_specs=[row, row, vec] + ([vec] if has_next else []), out_specs=tuple([row] * len(shapes)),
                         compiler_params=_params(grid))(h, z, g, *([g_next] if has_next else []))
    return (res[0], res[1]) if has_next else (res[0], None)


def _rms_bwd_math(x, g, dy):
    r = _rstd(x)
    dyg = dy * g
    dx = r * dyg - x * (r * r * r) * jnp.mean(x * dyg, axis=-1, keepdims=True)
    dg = jnp.sum(dy * x * r, axis=0, keepdims=True)
    return dx, dg


def _rms_bwd(x, g, dy, dres, out_dtype, name):
    m, d = x.shape
    tr = _row_tile(m)
    grid = (m // tr,)
    has_res = dres is not None

    def body(*refs):
        x_ref, g_ref, dy_ref = refs[:3]
        dx_ref, dg_ref = refs[-2:]
        dx, dg = _rms_bwd_math(x_ref[...], g_ref[...], dy_ref[...].astype(F32))
        if has_res:
            dx = dx + refs[3][...]
        dx_ref[...] = dx.astype(out_dtype)

        @pl.when(pl.program_id(0) == 0)
        def _():
            dg_ref[...] = jnp.zeros_like(dg_ref)

        dg_ref[...] += dg

    row = pl.BlockSpec((tr, d), lambda i: (i, 0))
    vec = pl.BlockSpec((1, d), lambda i: (0, 0))
    ops = [x, g, dy] + ([dres] if has_res else [])
    return pl.pallas_call(
        body, name=name, grid=grid,
        out_shape=(jax.ShapeDtypeStruct((m, d), out_dtype), jax.ShapeDtypeStruct((1, d), F32)),
        in_specs=[row, vec, row] + ([row] if has_res else []), out_specs=(row, vec),
        compiler_params=_params(grid))(*ops)


def _merge_fwd(oa, op, oc, g):
    m = oa.shape[0]
    da, dp, dc = oa.shape[1], op.shape[1], oc.shape[1]
    d = da + dp + dc
    tr = _row_tile(m)
    grid = (m // tr,)

    def body(a_ref, p_ref, c_ref, g_ref, o_ref):
        off = 0
        for ref, w in ((a_ref, da), (p_ref, dp), (c_ref, dc)):
            xx = ref[...]
            o_ref[:, off:off + w] = (xx * _rstd(xx) * g_ref[:, off:off + w]).astype(BF16)
            off += w

    specs = [pl.BlockSpec((tr, w), lambda i: (i, 0)) for w in (da, dp, dc)]
    return pl.pallas_call(body, name="merge_fwd", grid=grid, out_shape=jax.ShapeDtypeStruct((m, d), BF16),
                          in_specs=specs + [pl.BlockSpec((1, d), lambda i: (0, 0))],
                          out_specs=pl.BlockSpec((tr, d), lambda i: (i, 0)), compiler_params=_params(grid))(oa, op, oc, g)


def _merge_bwd(oa, op, oc, g, dmerged):
    m = oa.shape[0]
    da, dp, dc = oa.shape[1], op.shape[1], oc.shape[1]
    d = da + dp + dc
    tr = _row_tile(m)
    grid = (m // tr,)

    def body(a_ref, p_ref, c_ref, g_ref, dm_ref, da_ref, dp_ref, dc_ref, dg_ref):
        @pl.when(pl.program_id(0) == 0)
        def _():
            dg_ref[...] = jnp.zeros_like(dg_ref)

        off = 0
        for ref, oref, w in ((a_ref, da_ref, da), (p_ref, dp_ref, dp), (c_ref, dc_ref, dc)):
            dx, dg = _rms_bwd_math(ref[...], g_ref[:, off:off + w], dm_ref[:, off:off + w])
            oref[...] = dx.astype(oref.dtype)
            dg_ref[:, off:off + w] += dg
            off += w

    specs = [pl.BlockSpec((tr, w), lambda i: (i, 0)) for w in (da, dp, dc)]
    vec = pl.BlockSpec((1, d), lambda i: (0, 0))
    return pl.pallas_call(
        body, name="merge_bwd", grid=grid,
        out_shape=(jax.ShapeDtypeStruct((m, da), F32), jax.ShapeDtypeStruct((m, dp), F32),
                   jax.ShapeDtypeStruct((m, dc), BF16), jax.ShapeDtypeStruct((1, d), F32)),
        in_specs=specs + [vec, pl.BlockSpec((tr, d), lambda i: (i, 0))], out_specs=tuple(specs) + (vec,),
        compiler_params=_params(grid))(oa, op, oc, g, dmerged)


def _loss_grad(h, target, n_meta, n_tok):
    m, d = h.shape
    tr = _row_tile(m)
    grid = (m // tr,)

    def body(h_ref, t_ref, loss_ref, dh_ref):
        i = pl.program_id(0)
        row = i * tr + lax.broadcasted_iota(jnp.int32, (tr, 1), 0)
        live = jnp.logical_and(row >= n_meta, row < n_meta + n_tok)
        diff = jnp.where(live, h_ref[...] - t_ref[...], 0.0)
        dh_ref[...] = diff * (1.0 / d)

        @pl.when(i == 0)
        def _():
            loss_ref[...] = jnp.zeros_like(loss_ref)

        loss_ref[...] += jnp.sum(jnp.sum(diff * diff, axis=1, keepdims=True), axis=0, keepdims=True) * (0.5 / d)

    row = pl.BlockSpec((tr, d), lambda i: (i, 0))
    return pl.pallas_call(
        body, name="loss_grad", grid=grid,
        out_shape=(jax.ShapeDtypeStruct((1, 1), F32), jax.ShapeDtypeStruct((m, d), F32)),
        in_specs=[row, row], out_specs=(pl.BlockSpec((1, 1), lambda i: (0, 0)), row),
        compiler_params=_params(grid))(h, target)


def _split_dot(x, tmat):
    hi = x.astype(BF16)
    lo = (x - hi.astype(F32)).astype(BF16)
    return _dot(hi, tmat) + _dot(lo, tmat)


def _softplus(z):
    return jnp.maximum(z, 0.0) + jnp.log(1.0 + jnp.exp(-jnp.abs(z)))


def _attn_tiles(lp):
    t = ATT_BLOCK
    nb = lp // t
    u = 3 if nb % 3 == 0 else (2 if nb % 2 == 0 else 1)
    return t, nb, u


def _attn_masks(tq, t, u):
    row = lax.broadcasted_iota(jnp.int32, (tq, t), 0)
    col = lax.broadcasted_iota(jnp.int32, (tq, t), 1)
    masks = [col + r * t < row for r in range(u)]
    a = lax.broadcasted_iota(jnp.int32, (t, 2 * t), 0)
    s = lax.broadcasted_iota(jnp.int32, (t, 2 * t), 1)
    after = jnp.logical_or(a > s, s >= t).astype(BF16)
    before = jnp.logical_or(a < s, s >= t).astype(BF16)
    return masks, after, before


def _service_heads(arrs, nh):
    heads, done = [], 0
    for a in arrs:
        done += a.size * a.dtype.itemsize
        heads.append(min(nh - 1, done // GATHER_BYTES_PER_HEAD + 1))
    return heads


def _attn_fwd(q, k, v, gather=()):
    nh, lp, dh = q.shape
    t, nb, u = _attn_tiles(lp)
    tq = u * t
    nq = nb // u
    grid = (nh, nq)
    ng = len(gather)
    service = _service_heads(gather, nh)

    def body(q_ref, k_ref, v_ref, *rest):
        o_ref, tot_ref, cnt_ref = rest[ng:ng + 3]
        h, i = pl.program_id(0), pl.program_id(1)
        if ng:
            plan = _GatherPlan(rest[:ng], rest[ng + 3:2 * ng + 3], *rest[2 * ng + 3:])
            pl.when(jnp.logical_and(h == 0, i == 0))(plan.start)
        qb = q_ref[...]
        masks, after, _ = _attn_masks(tq, t, u)
        o_ref[...] = jnp.zeros_like(o_ref)
        tot_ref[...] = jnp.zeros_like(tot_ref)

        def blocks(base, masked):
            outs = []
            for r in reversed(range(u)):
                lo = r * t if masked else 0
                start = pl.multiple_of((base + r) * t, t)
                kb = k_ref[pl.ds(start, t), :]
                z = _dot(qb[lo:], kb, NT)
                sp = _softplus(z)
                lnb = jnp.where(masks[r][lo:], -sp, 0.0) if masked else -sp
                res = _split_dot(lnb, after)
                outs.append((start, z - sp + res[:, :t], res[:, t:], r, lo))
            for start, logw, rs, r, lo in outs:
                cs = tot_ref[lo:, :]
                w = jnp.exp(logw + cs)
                if masked:
                    w = jnp.where(masks[r][lo:], w, 0.0)
                o_ref[lo:, :] += _dot(w.astype(BF16), v_ref[pl.ds(start, t), :])
                tot_ref[lo:, :] = cs + rs

        blocks(i * u, True)

        def live():
            return jnp.max(tot_ref[...]) > -DEAD_LOG_WEIGHT

        def step(state):
            n, _ = state
            blocks((i - 1 - n) * u, False)
            return n + 1, live()

        n_done, _ = lax.while_loop(lambda s: jnp.logical_and(s[0] < i, s[1]), step, (jnp.int32(0), live()))
        cnt_ref[h, i] = n_done.astype(F32)

        if ng:
            for a, head in enumerate(service):
                pl.when(jnp.logical_and(h == head, i == nq - 1))(functools.partial(plan.forward, a))
            pl.when(jnp.logical_and(h == nh - 1, i == nq - 1))(plan.finish)

    blk = pl.BlockSpec((None, tq, dh), lambda h, i: (h, i, 0))
    full = pl.BlockSpec((None, lp, dh), lambda h, i: (h, 0, 0))
    return pl.pallas_call(
        body, name="attn_fwd_gather" if ng else "attn_fwd", grid=grid,
        out_shape=(jax.ShapeDtypeStruct((nh, lp, dh), F32), jax.ShapeDtypeStruct((nh, lp, t), F32),
                   jax.ShapeDtypeStruct((nh, nq), F32)) + _gather_shapes(gather),
        in_specs=[blk, full, full] + [ANY] * ng,
        out_specs=(blk, pl.BlockSpec((None, tq, t), lambda h, i: (h, i, 0)), pl.BlockSpec(memory_space=pltpu.SMEM))
        + (ANY,) * ng,
        scratch_shapes=_comm_sems(ng, 7) if ng else [],
        compiler_params=_params(grid))(q, k, v, *gather)


def _attn_bwd(q, qt, k, v, do, dot_, tot, cnt, scale, exchange=()):
    nh, lp, dh = q.shape
    t, nb, u = _attn_tiles(lp)
    tq = u * t
    nq = nb // u
    grid = (nh, nq)
    ne = len(exchange)

    def body(q_ref, qt_ref, k_ref, v_ref, do_ref, dot_ref, tot_ref, cnt_ref, *rest):
        dq_ref, dk_ref, dv_ref = rest[ne:ne + 3]
        p_ref, g_ref = rest[2 * ne + 3:2 * ne + 5]
        h, i = pl.program_id(0), pl.program_id(1)
        if ne:
            plan = _ChipExchangePlan(rest[:ne], rest[ne + 3:2 * ne + 3], *rest[2 * ne + 5:])
            pl.when(jnp.logical_and(h == 0, i == 0))(plan.start)

        @pl.when(i == 0)
        def _():
            dk_ref[...] = jnp.zeros_like(dk_ref)
            dv_ref[...] = jnp.zeros_like(dv_ref)

        dq_ref[...] = jnp.zeros_like(dq_ref)
        p_ref[...] = jnp.zeros_like(p_ref)
        g_ref[...] = jnp.zeros_like(g_ref)
        qb, qtb, dob, dotb = q_ref[...], qt_ref[...], do_ref[...], dot_ref[...]
        total = tot_ref[...]
        masks, after, before = _attn_masks(tq, t, u)

        def blocks(base, masked):
            part = []
            for r in range(u):
                lo = r * t if masked else 0
                j = base + r
                start = pl.multiple_of(j * t, t)
                kb = k_ref[pl.ds(start, t), :]
                vb = v_ref[pl.ds(start, t), :]
                z = _dot(qb[lo:], kb, NT)
                sp = _softplus(z)
                lnb = jnp.where(masks[r][lo:], -sp, 0.0) if masked else -sp
                res = _split_dot(lnb, after)
                part.append((j, kb, z - sp, res[:, :t], res[:, t:], _dot(dob[lo:], vb, NT), r, lo))
            for j, kb, logsig, later_in, rs, dw, r, lo in part:
                pfx = p_ref[lo:, :] + rs
                w = jnp.exp(logsig + later_in + (total[lo:] - pfx))
                if masked:
                    w = jnp.where(masks[r][lo:], w, 0.0)
                dlogw = w * dw
                res2 = _split_dot(dlogw, before)
                gp = g_ref[lo:, :]
                dz = dlogw - jnp.exp(logsig) * (dlogw + res2[:, :t] + gp)
                if masked:
                    dz = jnp.where(masks[r][lo:], dz, 0.0)
                p_ref[lo:, :] = pfx
                g_ref[lo:, :] = gp + res2[:, t:]
                dzb = dz.astype(BF16)
                dq_ref[lo:, :] += _dot(dzb, kb)
                dk_ref[j] += _dot(qtb[:, lo:], dzb)
                dv_ref[j] += _dot(dotb[:, lo:], w.astype(BF16))

        def step(n, carry):
            blocks(n * u, False)
            return carry

        walked = jnp.clip(cnt_ref[h, i].astype(jnp.int32), 0, i)
        lax.fori_loop(i - walked, i, step, 0)
        blocks(i * u, True)
        dq_ref[...] = dq_ref[...] * scale
        if ne:
            pl.when(jnp.logical_and(h == nh - 1, i == nq - 1))(plan.finish)

    blk = pl.BlockSpec((None, tq, dh), lambda h, i: (h, i, 0))
    blk_t = pl.BlockSpec((None, dh, tq), lambda h, i: (h, 0, i))
    full = pl.BlockSpec((None, lp, dh), lambda h, i: (h, 0, 0))
    acc = pl.BlockSpec((None, nb, dh, t), lambda h, i: (h, 0, 0, 0))
    acc_shape = jax.ShapeDtypeStruct((nh, nb, dh, t), F32)
    return pl.pallas_call(
        body, name="attn_bwd_exchange" if ne else "attn_bwd", grid=grid,
        out_shape=(jax.ShapeDtypeStruct((nh, lp, dh), F32), acc_shape, acc_shape) + _exchange_shapes(exchange),
        in_specs=[blk, blk_t, full, full, blk, blk_t, pl.BlockSpec((None, tq, t), lambda h, i: (h, i, 0)),
                  pl.BlockSpec(memory_space=pltpu.SMEM)] + [ANY] * ne,
        out_specs=(blk, acc, acc) + (ANY,) * ne,
        scratch_shapes=[pltpu.VMEM((tq, t), F32), pltpu.VMEM((tq, t), F32)] + (_comm_sems(ne, 3) if ne else []),
        compiler_params=_params(grid))(q, qt, k, v, do, dot_, tot, cnt, *exchange)


def _halo_specs(tm, width, nt):
    per = tm // HALO
    prev = pl.BlockSpec((HALO, width), lambda i: (jnp.maximum(i * per - 1, 0), 0))
    nxt = pl.BlockSpec((HALO, width), lambda i: (jnp.minimum((i + 1) * per, nt * per - 1), 0))
    return prev, nxt


def _shift_down(x, k):
    return x if k == 0 else pltpu.roll(x, k, axis=0)


def _shift_up(x, k):
    return x if k == 0 else pltpu.roll(x, x.shape[0] - k, axis=0)


def _pool_fwd(u, wp, scale, layer):
    m, dp = u.shape
    g = dp // len(POOL_WINDOWS)
    tm = _row_tile(m)
    nt = m // tm
    grid = (nt,)
    prev, _ = _halo_specs(tm, dp, nt)

    def body(u_ref, h_ref, w_ref, s_ref, o_ref):
        i = pl.program_id(0)
        halo = jnp.where(i > 0, h_ref[...], 0.0)
        s = jnp.concatenate([halo, u_ref[...]], axis=0)
        tpos = i * tm + lax.broadcasted_iota(jnp.int32, (tm, 1), 0)
        for gi, win in enumerate(POOL_WINDOWS):
            s = s + _shift_down(s, win // 2)
            cols = slice(gi * g, (gi + 1) * g)
            cnt = jnp.minimum(tpos + 1, win).astype(F32)
            pooled = s[HALO:, cols] / cnt - u_ref[:, cols]
            o_ref[:, cols] = _dot(pooled.astype(BF16), w_ref[gi]) * s_ref[:, cols]

    row = pl.BlockSpec((tm, dp), lambda i: (i, 0))
    return pl.pallas_call(
        body, name="pool_fwd", grid=grid, out_shape=jax.ShapeDtypeStruct((m, dp), F32),
        in_specs=[row, prev, pl.BlockSpec((None, len(POOL_WINDOWS), g, g), lambda i: (layer, 0, 0, 0)),
                  pl.BlockSpec((1, dp), lambda i: (0, 0))],
        out_specs=row, compiler_params=_params(grid))(u, u, wp, scale)


def _pool_bwd(u, dmix, wp, scale, layer):
    m, dp = u.shape
    ng = len(POOL_WINDOWS)
    g = dp // ng
    tm = _row_tile(m)
    nt = m // tm
    grid = (nt,)
    prev, nxt = _halo_specs(tm, dp, nt)

    def body(u_ref, h_ref, dm_ref, dmn_ref, w_ref, s_ref, du_ref, dw_ref, ds_ref):
        i = pl.program_id(0)

        @pl.when(i == 0)
        def _():
            dw_ref[...] = jnp.zeros_like(dw_ref)
            ds_ref[...] = jnp.zeros_like(ds_ref)

        halo = jnp.where(i > 0, h_ref[...], 0.0)
        s = jnp.concatenate([halo, u_ref[...]], axis=0)
        dmn = jnp.where(i < nt - 1, dmn_ref[...], 0.0)
        dmc = jnp.concatenate([dm_ref[...], dmn], axis=0)
        tpos = i * tm + lax.broadcasted_iota(jnp.int32, (tm, 1), 0)
        tpos_x = i * tm + lax.broadcasted_iota(jnp.int32, (tm + HALO, 1), 0)
        for gi, win in enumerate(POOL_WINDOWS):
            s = s + _shift_down(s, win // 2)
            cols = slice(gi * g, (gi + 1) * g)
            cnt = jnp.minimum(tpos + 1, win).astype(F32)
            pooled = (s[HALO:, cols] / cnt - u_ref[:, cols]).astype(BF16)
            wg = w_ref[gi]
            pm = _dot(pooled, wg)
            ds_ref[:, cols] += jnp.sum(dm_ref[:, cols] * pm, axis=0, keepdims=True)
            dpm = (dmc[:, cols] * s_ref[:, cols]).astype(BF16)
            dw_ref[gi] += _dot(pooled, dpm[:tm], TN)
            dpool = _dot(dpm, wg, NT)
            f = dpool / jnp.minimum(tpos_x + 1, win).astype(F32)
            step = 1
            while step < win:
                f = f + _shift_up(f, step)
                step *= 2
            du_ref[:, cols] = f[:tm] - dpool[:tm]

    row = pl.BlockSpec((tm, dp), lambda i: (i, 0))
    return pl.pallas_call(
        body, name="pool_bwd", grid=grid,
        out_shape=(jax.ShapeDtypeStruct((m, dp), F32), jax.ShapeDtypeStruct((ng, g, g), F32),
                   jax.ShapeDtypeStruct((1, dp), F32)),
        in_specs=[row, prev, row, nxt, pl.BlockSpec((None, ng, g, g), lambda i: (layer, 0, 0, 0)),
                  pl.BlockSpec((1, dp), lambda i: (0, 0))],
        out_specs=(row, pl.BlockSpec((ng, g, g), lambda i: (0, 0, 0)), pl.BlockSpec((1, dp), lambda i: (0, 0))),
        compiler_params=_params(grid))(u, u, dmix, dmix, wp, scale)


def _conv_taps(u, w_ref, rows):
    y = None
    for k in range(CONV_WIDTH):
        term = _shift_down(u, CONV_WIDTH - 1 - k)[HALO:HALO + rows] * w_ref[k:k + 1, :]
        y = term if y is None else y + term
    return y


def _layernorm_stats(y):
    mu = jnp.mean(y, axis=-1, keepdims=True)
    yc = y - mu
    rstd = lax.rsqrt(jnp.mean(yc * yc, axis=-1, keepdims=True) + EPS)
    return yc * rstd, rstd


def _conv_fwd(uc, wdw, b, lg, lb):
    m, c2 = uc.shape
    c = c2 // 2
    tm = _row_tile(m)
    nt = m // tm
    grid = (nt,)
    prev, _ = _halo_specs(tm, c2, nt)

    def body(x_ref, h_ref, w_ref, b_ref, g_ref, bb_ref, o_ref):
        i = pl.program_id(0)
        halo = jnp.where(i > 0, h_ref[...], 0.0)
        xc = jnp.concatenate([halo, x_ref[...]], axis=0)
        u = xc[:, :c] * jax.nn.sigmoid(xc[:, c:])
        y = _conv_taps(u, w_ref, tm) + b_ref[...]
        xhat, _ = _layernorm_stats(y)
        ln = xhat * g_ref[...] + bb_ref[...]
        o_ref[...] = (ln * jax.nn.sigmoid(ln)).astype(BF16)

    vec = pl.BlockSpec((1, c), lambda i: (0, 0))
    return pl.pallas_call(
        body, name="conv_fwd", grid=grid, out_shape=jax.ShapeDtypeStruct((m, c), BF16),
        in_specs=[pl.BlockSpec((tm, c2), lambda i: (i, 0)), prev, pl.BlockSpec((HALO, c), lambda i: (0, 0)), vec, vec, vec],
        out_specs=pl.BlockSpec((tm, c), lambda i: (i, 0)), compiler_params=_params(grid))(uc, uc, wdw, b, lg, lb)


def _conv_bwd(uc, ds, wdw, b, lg, lb):
    m, c2 = uc.shape
    c = c2 // 2
    tm = _row_tile(m)
    nt = m // tm
    grid = (nt,)
    prev, nxt = _halo_specs(tm, c2, nt)
    _, nxt_c = _halo_specs(tm, c, nt)

    def body(x_ref, hp_ref, hn_ref, ds_ref, dsn_ref, w_ref, b_ref, g_ref, bb_ref,
             dx_ref, dw_ref, db_ref, dg_ref, dbb_ref):
        i = pl.program_id(0)

        @pl.when(i == 0)
        def _():
            for ref in (dw_ref, db_ref, dg_ref, dbb_ref):
                ref[...] = jnp.zeros_like(ref)

        hp = jnp.where(i > 0, hp_ref[...], 0.0)
        xc = jnp.concatenate([hp, x_ref[...], hn_ref[...]], axis=0)
        sg = jax.nn.sigmoid(xc[:, c:])
        u = xc[:, :c] * sg
        rows = tm + HALO
        y = _conv_taps(u, w_ref, rows) + b_ref[...]
        xhat, rstd = _layernorm_stats(y)
        ln = xhat * g_ref[...] + bb_ref[...]
        sl = jax.nn.sigmoid(ln)
        dsn = jnp.where(i < nt - 1, dsn_ref[...], 0.0)
        dsx = jnp.concatenate([ds_ref[...], dsn], axis=0)
        dln = dsx * sl * (1.0 + ln * (1.0 - sl))
        dxh = dln * g_ref[...]
        dy = rstd * (dxh - jnp.mean(dxh, axis=-1, keepdims=True)
                     - xhat * jnp.mean(dxh * xhat, axis=-1, keepdims=True))
        dyt = dy[:tm]
        dg_ref[...] += jnp.sum(dln[:tm] * xhat[:tm], axis=0, keepdims=True)
        dbb_ref[...] += jnp.sum(dln[:tm], axis=0, keepdims=True)
        db_ref[...] += jnp.sum(dyt, axis=0, keepdims=True)
        du = None
        for k in range(CONV_WIDTH):
            lag = CONV_WIDTH - 1 - k
            dw_ref[k:k + 1, :] += jnp.sum(dyt * _shift_down(u, lag)[HALO:HALO + tm], axis=0, keepdims=True)
            term = _shift_up(dy, lag)[:tm] * w_ref[k:k + 1, :]
            du = term if du is None else du + term
        a_t = xc[HALO:HALO + tm, :c]
        sg_t = sg[HALO:HALO + tm]
        dx_ref[:, :c] = du * sg_t
        dx_ref[:, c:] = du * a_t * sg_t * (1.0 - sg_t)

    vec = pl.BlockSpec((1, c), lambda i: (0, 0))
    wsp = pl.BlockSpec((HALO, c), lambda i: (0, 0))
    vshape = jax.ShapeDtypeStruct((1, c), F32)
    return pl.pallas_call(
        body, name="conv_bwd", grid=grid,
        out_shape=(jax.ShapeDtypeStruct((m, c2), F32), jax.ShapeDtypeStruct((HALO, c), F32), vshape, vshape, vshape),
        in_specs=[pl.BlockSpec((tm, c2), lambda i: (i, 0)), prev, nxt, pl.BlockSpec((tm, c), lambda i: (i, 0)), nxt_c,
                  wsp, vec, vec, vec],
        out_specs=(pl.BlockSpec((tm, c2), lambda i: (i, 0)), wsp, vec, vec, vec),
        compiler_params=_params(grid))(uc, uc, uc, ds, ds, wdw, b, lg, lb)


def _mesh_pos():
    return lax.axis_index("x"), lax.axis_index("y"), lax.axis_index("c")


def _other_chips(x, y):
    return [(1 - x, y), (x, 1 - y), (1 - x, 1 - y)]


def _comm_sems(n, per):
    return [pltpu.SemaphoreType.DMA((n, per)), pltpu.SemaphoreType.DMA((n, per)), pltpu.SemaphoreType.DMA((n,))]


def _comm_call(body, name, arrs, out_shapes, n_sems):
    n = len(arrs)
    return pl.pallas_call(
        body, name=name, out_shape=out_shapes, in_specs=[ANY] * n, out_specs=tuple([ANY] * len(out_shapes)),
        scratch_shapes=_comm_sems(n, n_sems),
    )(*arrs)


class _GatherPlan:
    def __init__(self, ins, outs, send_sems, recv_sems, local_sems):
        self.ins, self.outs = ins, outs
        self.send_sems, self.recv_sems, self.local_sems = send_sems, recv_sems, local_sems
        x, y, c = _mesh_pos()
        self.c, self.me, self.sib = c, (x, y, c), (x, y, 1 - c)
        self.chips = _other_chips(x, y)

    def _slot(self, a, p):
        return self.outs[a].at[4 * p[0] + 2 * p[1] + p[2]]

    def _copy(self, a, k, block, to, own=False):
        return pltpu.make_async_remote_copy(
            src_ref=self.ins[a] if own else self._slot(a, block), dst_ref=self._slot(a, block),
            send_sem=self.send_sems.at[a, k], recv_sem=self.recv_sems.at[a, k], device_id=to, device_id_type=MESH)

    def _local(self, a):
        return pltpu.make_async_copy(self.ins[a], self._slot(a, self.me), self.local_sems.at[a])

    def start(self):
        for a in range(len(self.ins)):
            for j, chip in enumerate(self.chips):
                self._copy(a, 1 + j, self.me, (*chip, self.c), own=True).start()
            self._copy(a, 0, self.me, self.sib, own=True).start()
            self._local(a).start()

    def forward(self, a):
        for j, chip in enumerate(self.chips):
            self._copy(a, 1 + j, (*chip, self.c), self.me).wait_recv()
            self._copy(a, 4 + j, (*chip, self.c), self.sib).start()

    def finish(self):
        for a in range(len(self.ins)):
            self._copy(a, 0, self.sib, self.me).wait_recv()
            for j, chip in enumerate(self.chips):
                self._copy(a, 4 + j, (*chip, 1 - self.c), self.me).wait_recv()
        for a in range(len(self.ins)):
            self._copy(a, 0, self.me, self.sib, own=True).wait_send()
            for j, chip in enumerate(self.chips):
                self._copy(a, 1 + j, self.me, (*chip, self.c), own=True).wait_send()
                self._copy(a, 4 + j, (*chip, self.c), self.sib).wait_send()
            self._local(a).wait()


def _gather_shapes(arrs):
    return tuple(jax.ShapeDtypeStruct((N_DEV,) + a.shape, a.dtype) for a in arrs)


def _all_gather(arrs, name):
    n = len(arrs)

    def body(*refs):
        plan = _GatherPlan(refs[:n], refs[n:2 * n], *refs[2 * n:])
        plan.start()
        for a in range(n):
            plan.forward(a)
        plan.finish()

    return _comm_call(body, name, arrs, _gather_shapes(arrs), 7)


class _ChipExchangePlan:
    def __init__(self, ins, outs, send_sems, recv_sems, local_sems):
        self.ins, self.outs = ins, outs
        self.send_sems, self.recv_sems, self.local_sems = send_sems, recv_sems, local_sems
        x, y, c = _mesh_pos()
        self.pos, self.mine = (x, y, c), 2 * x + y
        self.chips = _other_chips(x, y)

    def _send(self, a, j):
        chip = self.chips[j]
        return pltpu.make_async_remote_copy(
            src_ref=self.ins[a].at[2 * chip[0] + chip[1]], dst_ref=self.outs[a].at[self.mine],
            send_sem=self.send_sems.at[a, j], recv_sem=self.recv_sems.at[a, j], device_id=(*chip, self.pos[2]),
            device_id_type=MESH)

    def _landing(self, a, j):
        chip = self.chips[j]
        ref = self.outs[a].at[2 * chip[0] + chip[1]]
        return pltpu.make_async_remote_copy(src_ref=ref, dst_ref=ref, send_sem=self.send_sems.at[a, j],
                                            recv_sem=self.recv_sems.at[a, j], device_id=self.pos, device_id_type=MESH)

    def _local(self, a):
        return pltpu.make_async_copy(self.ins[a].at[self.mine], self.outs[a].at[self.mine], self.local_sems.at[a])

    def start(self):
        for a in range(len(self.ins)):
            for j in range(3):
                self._send(a, j).start()
            self._local(a).start()

    def finish(self):
        for a in range(len(self.ins)):
            for j in range(3):
                self._landing(a, j).wait_recv()
        for a in range(len(self.ins)):
            for j in range(3):
                self._send(a, j).wait_send()
            self._local(a).wait()


def _pair_tile(kk):
    return _pick(kk, (1024, 704, 576, 512, 256, 128, 64))


def _pair_exchange(parts, core, name):
    _, kk, n = parts.shape
    tr = _pair_tile(kk)
    nr = kk // tr
    grid = (4, nr)

    def body(c_ref, src_ref, recv_ref, send_sems, recv_sems):
        q, r = pl.program_id(0), pl.program_id(1)
        x, y, c = _mesh_pos()
        step = q * nr + r
        slot = step % 2
        dst = recv_ref.at[pl.ds(pl.multiple_of(step * tr, tr), tr), :]
        cp = pltpu.make_async_remote_copy(src_ref=src_ref, dst_ref=dst, send_sem=send_sems.at[slot],
                                          recv_sem=recv_sems.at[slot], device_id=(x, y, 1 - c), device_id_type=MESH)
        cp.start()
        cp.wait_send()
        cp.wait_recv()

    spec = pltpu.PrefetchScalarGridSpec(
        num_scalar_prefetch=1, grid=grid,
        in_specs=[pl.BlockSpec((tr, n), lambda q, r, c: ((2 * q + 1 - c[0]) * nr + r, 0))],
        out_specs=ANY, scratch_shapes=[pltpu.SemaphoreType.DMA((2,)), pltpu.SemaphoreType.DMA((2,))])
    recv = pl.pallas_call(body, name=name, grid_spec=spec, out_shape=jax.ShapeDtypeStruct((4 * kk, n), parts.dtype),
                          compiler_params=_params(grid))(core, parts.reshape(N_DEV * kk, n))
    return recv.reshape(4, kk, n)


def _exchange_shapes(parts):
    return tuple(jax.ShapeDtypeStruct(p.shape, p.dtype) for p in parts)


def _pair_add(parts, recv, core, name):
    _, kk, n = recv.shape
    tr = _pair_tile(kk)
    grid = (4, kk // tr)

    def body(c_ref, a_ref, b_ref, o_ref):
        o_ref[...] = (a_ref[...].astype(F32) + b_ref[...].astype(F32)).astype(o_ref.dtype)

    out = pl.BlockSpec((None, tr, n), lambda q, r, c: (q, r, 0))
    spec = pltpu.PrefetchScalarGridSpec(
        num_scalar_prefetch=1, grid=grid,
        in_specs=[pl.BlockSpec((None, None, tr, n), lambda q, r, c: (q, c[0], r, 0)), out], out_specs=out)
    return pl.pallas_call(body, name=name, grid_spec=spec, out_shape=jax.ShapeDtypeStruct(recv.shape, recv.dtype),
                          compiler_params=_params(grid))(core, parts.reshape(4, 2, kk, n), recv)


def _adamw_math(w, g, m, v):
    m2 = ADAM_B1 * m + (1.0 - ADAM_B1) * g
    v2 = ADAM_B2 * v + (1.0 - ADAM_B2) * (g * g)
    m_hat = m2 / (1.0 - ADAM_B1 ** ADAM_STEP)
    v_hat = v2 / (1.0 - ADAM_B2 ** ADAM_STEP)
    delta = -ADAM_LR * (m_hat / (jnp.sqrt(v_hat) + ADAM_EPS) + ADAM_WD * w)
    return delta, m2, v2


def _adamw_sharded(parts, w, m, v, name):
    nl, kk, n = w.shape
    tr = _pick(kk, (256, 128, 64)) if n <= 1024 else _pick(kk, (64, 32))
    grid = (nl, kk // tr)

    def body(*refs):
        p_refs = refs[:nl]
        w_ref, m_ref, v_ref, g_ref, d_ref, m2_ref, v2_ref = refs[nl:]
        layer = pl.program_id(0)
        g = None
        for l, p_ref in enumerate(p_refs):
            s = p_ref[0].astype(F32)
            for q in range(1, 4):
                s = s + p_ref[q].astype(F32)
            g = s if g is None else jnp.where(layer == l, s, g)
        d, m2, v2 = _adamw_math(w_ref[...], g, m_ref[...], v_ref[...])
        g_ref[...] = g
        d_ref[...] = d
        m2_ref[...] = m2
        v2_ref[...] = v2

    spec = pl.BlockSpec((None, tr, n), lambda l, r: (l, r, 0))
    shp = jax.ShapeDtypeStruct(w.shape, F32)
    p_specs = [pl.BlockSpec((4, tr, n), lambda l, r, k=k: (0, jnp.where(l == k, r, 0), 0)) for k in range(nl)]
    return pl.pallas_call(
        body, name=name, grid=grid, out_shape=(shp, shp, shp, shp), in_specs=p_specs + [spec, spec, spec],
        out_specs=(spec, spec, spec, spec), compiler_params=_params(grid))(*parts, w, m, v)


def _sum_slots(g8):
    _, r, _ = g8.shape

    def body(g_ref, o_ref):
        s = g_ref[0]
        for d in range(1, N_DEV):
            s = s + g_ref[d]
        o_ref[...] = s

    return pl.pallas_call(body, name="sum_slots", out_shape=jax.ShapeDtypeStruct((r, 128), F32))(g8)


def _adamw_flat(w, g, m, v):
    def body(w_ref, g_ref, m_ref, v_ref, d_ref, m2_ref, v2_ref):
        d, m2, v2 = _adamw_math(w_ref[...], g_ref[...], m_ref[...], v_ref[...])
        d_ref[...] = d
        m2_ref[...] = m2
        v2_ref[...] = v2

    shp = jax.ShapeDtypeStruct(w.shape, F32)
    return pl.pallas_call(body, name="adamw_flat", out_shape=(shp, shp, shp))(w, g, m, v)


def _pack(arrs):
    flat = jnp.concatenate([a.reshape(-1) for a in arrs])
    pad = (-flat.shape[0]) % 1024
    return jnp.pad(flat, (0, pad)).reshape(-1, 128)


def _unpack(buf, like):
    flat = buf.reshape(-1)
    out, off = [], 0
    for a in like:
        out.append(flat[off:off + a.size].reshape(a.shape))
        off += a.size
    return out


def _heads(x2d, scale=None):
    lp, da = x2d.shape
    xh = x2d.reshape(lp, N_HEADS, da // N_HEADS).transpose(1, 0, 2)
    if scale is not None:
        xh = xh * scale
    return xh


def _unheads(xh):
    nh, lp, dh = xh.shape
    return xh.transpose(1, 0, 2).reshape(lp, nh * dh)


def _hosted(hosts, site, kind, fn):
    if hosts and site in hosts:
        arrays, done = hosts[site]
        res, got = fn(_Riders(kind, arrays))
        done(got)
        return res
    return fn(None)


def _layer_fwd(h, l, W, P, hosts=None, u1=None):
    d = h.shape[1]
    da, dp = d // 2, d // 4
    dh = da // N_HEADS
    sv = {"h": h}
    if u1 is None:
        u1 = _rms_fwd(h, P["pre_mix_g"][l:l + 1], "rms_pre_mix")
    proj = _hosted(hosts, "mm_in", "gather", lambda r: _mm_nt(u1, W["in"], "mm_in", riders=r))
    qh = _heads(proj[:, :da], dh ** -0.5).astype(BF16)
    kh = _heads(proj[:, da:2 * da]).astype(BF16)
    vh = _heads(proj[:, 2 * da:3 * da]).astype(BF16)
    u_pool = proj[:, 3 * da:3 * da + dp]
    u_conv = proj[:, 3 * da + dp:]
    if hosts and "attn" in hosts:
        arrs, done = hosts["attn"]
        oh, tot, cnt, *gathered = _attn_fwd(qh, kh, vh, arrs)
        done(gathered)
    else:
        oh, tot, cnt = _attn_fwd(qh, kh, vh)
    o_attn = _unheads(oh)
    o_pool = _pool_fwd(u_pool, W["pool"], P["pool_scale"][l:l + 1], l)
    s_conv = _conv_fwd(u_conv, W["dw"][l], P["b_dw"][l:l + 1], P["conv_ln_g"][l:l + 1], P["conv_ln_b"][l:l + 1])
    o_conv = _mm_nn([(s_conv, W["pw"])], "mm_pw")
    merged = _merge_fwd(o_attn, o_pool, o_conv, P["mix_out_g"][l:l + 1])
    z1 = _mm_nn([(merged, W["out"])], "mm_out")
    h1, u2 = _resid_rms_fwd(h, z1, P["post_mix_g"][l:l + 1], P["pre_ffn_g"][l:l + 1], "resid_post_mix")
    a_s, b_s, act_s = _hosted(hosts, "ffn_up", "gather", lambda r: _ffn_up(u2, W["gate"], W["up"], r))
    ff = _hosted(hosts, "mm_down", "gather", lambda r: _mm_nn([(act_s, W["down"])], "mm_down", riders=r))
    g_next = P["pre_mix_g"][l + 1:l + 2] if l + 1 < P["pre_mix_g"].shape[0] else None
    h2, u1_next = _resid_rms_fwd(h1, ff, P["post_ffn_g"][l:l + 1], g_next, "resid_post_ffn")
    sv.update(u1=u1, qh=qh, kh=kh, vh=vh, tot=tot, cnt=cnt, u_pool=u_pool, u_conv=u_conv, o_attn=o_attn, o_pool=o_pool,
              o_conv=o_conv, s_conv=s_conv, merged=merged, z1=z1, h1=h1, u2=u2, a_s=a_s, b_s=b_s, act_s=act_s, ff=ff)
    return h2, sv, u1_next


EARLY = ("w_down", "w_gate", "w_up", "w_out", "w_pw")


def _layer_bwd(dh_out, l, W, P, sv, exchange_with_attn=None, hosts=None, exchange_w_in=None):
    d = dh_out.shape[1]
    da = d // 2
    dhd = da // N_HEADS
    lp = dh_out.shape[0]
    g = {}
    dff, g["post_ffn_g"] = _rms_bwd(sv["ff"], P["post_ffn_g"][l:l + 1], dh_out, None, BF16, "rms_bwd_post_ffn")
    da_s, db_s = _hosted(hosts, "ffn_bwd_act", "exchange",
                         lambda r: _ffn_bwd_act(dff, W["down"], sv["a_s"], sv["b_s"], r))
    g["w_down"] = _mm_tn(sv["act_s"], dff, "mm_dw_down")
    du2 = _hosted(hosts, "mm_d_u2", "exchange",
                  lambda r: _mm_nn([(da_s, W["gate"]), (db_s, W["up"])], "mm_d_u2", riders=r))
    g["w_gate"] = _hosted(hosts, "mm_dw_gate", "exchange", lambda r: _mm_tn(da_s, sv["u2"], "mm_dw_gate", riders=r))
    g["w_up"] = _mm_tn(db_s, sv["u2"], "mm_dw_up")
    dh1, g["pre_ffn_g"] = _rms_bwd(sv["h1"], P["pre_ffn_g"][l:l + 1], du2, dh_out, F32, "rms_bwd_pre_ffn")
    dz1, g["post_mix_g"] = _rms_bwd(sv["z1"], P["post_mix_g"][l:l + 1], dh1, None, BF16, "rms_bwd_post_mix")
    dmerged = _mm_nt(dz1, W["out"], "mm_d_merged")
    g["w_out"] = _mm_tn(sv["merged"], dz1, "mm_dw_out")
    d_oa, d_op, d_oc, g["mix_out_g"] = _merge_bwd(sv["o_attn"], sv["o_pool"], sv["o_conv"], P["mix_out_g"][l:l + 1], dmerged)
    d_s = _mm_nt(d_oc, W["pw"], "mm_d_sconv")
    g["w_pw"] = _mm_tn(sv["s_conv"], d_oc, "mm_dw_pw")
    d_uc, dwdw, g["b_dw"], g["conv_ln_g"], g["conv_ln_b"] = _conv_bwd(
        sv["u_conv"], d_s, W["dw"][l], P["b_dw"][l:l + 1], P["conv_ln_g"][l:l + 1], P["conv_ln_b"][l:l + 1])
    g["w_dw"] = dwdw[:CONV_WIDTH]
    d_up, g["w_pool"], g["pool_scale"] = _pool_bwd(sv["u_pool"], d_op, W["pool"], P["pool_scale"][l:l + 1], l)
    doh = _heads(d_oa).astype(BF16)
    riders = exchange_with_attn(g) if exchange_with_attn is not None else []
    dqh, dkt, dvt, *exchanged = _attn_bwd(sv["qh"], sv["qh"].transpose(0, 2, 1), sv["kh"], sv["vh"], doh,
                                          doh.transpose(0, 2, 1), sv["tot"], sv["cnt"], dhd ** -0.5, riders)
    dk2, dv2 = (a.transpose(1, 3, 0, 2).reshape(lp, da) for a in (dkt, dvt))
    dproj = jnp.concatenate([_unheads(dqh), dk2, dv2, d_up, d_uc], axis=1).astype(BF16)
    g["w_in"] = _mm_tn(dproj, sv["u1"], "mm_dw_in")
    late = {"mm_d_u1": exchange_w_in(g)} if exchange_w_in is not None else None
    du1 = _hosted(late, "mm_d_u1", "exchange", lambda r: _mm_nn([(dproj, W["in"])], "mm_d_u1", riders=r))
    dh0, g["pre_mix_g"] = _rms_bwd(sv["h"], P["pre_mix_g"][l:l + 1], du1, dh1, F32, "rms_bwd_pre_mix")
    return dh0, g, exchanged


SHARDED = ("w_in", "w_gate", "w_up", "w_down", "w_out", "w_pw")
SMALL = ("pre_mix_g", "w_pool", "pool_scale", "b_dw", "conv_ln_g", "conv_ln_b", "mix_out_g", "post_mix_g",
         "pre_ffn_g", "post_ffn_g", "w_dw", "meta_tokens")
ORDER = ("meta_tokens", "pre_mix_g", "w_in", "w_pool", "pool_scale", "w_dw", "b_dw", "conv_ln_g", "conv_ln_b", "w_pw",
         "mix_out_g", "w_out", "post_mix_g", "pre_ffn_g", "w_gate", "w_up", "w_down", "post_ffn_g")


def kernel(x, meta_tokens, pre_mix_g, w_in, w_pool, pool_scale, w_dw, b_dw, conv_ln_g, conv_ln_b, w_pw, mix_out_g, w_out, post_mix_g, pre_ffn_g, w_gate, w_up, w_down, post_ffn_g, loss_target, m_meta_tokens, m_pre_mix_g, m_w_in, m_w_pool, m_pool_scale, m_w_dw, m_b_dw, m_conv_ln_g, m_conv_ln_b, m_w_pw, m_mix_out_g, m_w_out, m_post_mix_g, m_pre_ffn_g, m_w_gate, m_w_up, m_w_down, m_post_ffn_g, v_meta_tokens, v_pre_mix_g, v_w_in, v_w_pool, v_pool_scale, v_w_dw, v_b_dw, v_conv_ln_g, v_conv_ln_b, v_w_pw, v_mix_out_g, v_w_out, v_post_mix_g, v_pre_ffn_g, v_w_gate, v_w_up, v_w_down, v_post_ffn_g):
    P = dict(meta_tokens=meta_tokens, pre_mix_g=pre_mix_g, w_in=w_in, w_pool=w_pool, pool_scale=pool_scale, w_dw=w_dw,
             b_dw=b_dw, conv_ln_g=conv_ln_g, conv_ln_b=conv_ln_b, w_pw=w_pw, mix_out_g=mix_out_g, w_out=w_out,
             post_mix_g=post_mix_g, pre_ffn_g=pre_ffn_g, w_gate=w_gate, w_up=w_up, w_down=w_down, post_ffn_g=post_ffn_g)
    M = dict(meta_tokens=m_meta_tokens, pre_mix_g=m_pre_mix_g, w_in=m_w_in, w_pool=m_w_pool, pool_scale=m_pool_scale,
             w_dw=m_w_dw, b_dw=m_b_dw, conv_ln_g=m_conv_ln_g, conv_ln_b=m_conv_ln_b, w_pw=m_w_pw, mix_out_g=m_mix_out_g,
             w_out=m_w_out, post_mix_g=m_post_mix_g, pre_ffn_g=m_pre_ffn_g, w_gate=m_w_gate, w_up=m_w_up,
             w_down=m_w_down, post_ffn_g=m_post_ffn_g)
    V = dict(meta_tokens=v_meta_tokens, pre_mix_g=v_pre_mix_g, w_in=v_w_in, w_pool=v_w_pool, pool_scale=v_pool_scale,
             w_dw=v_w_dw, b_dw=v_b_dw, conv_ln_g=v_conv_ln_g, conv_ln_b=v_conv_ln_b, w_pw=v_w_pw, mix_out_g=v_mix_out_g,
             w_out=v_w_out, post_mix_g=v_post_mix_g, pre_ffn_g=v_pre_ffn_g, w_gate=v_w_gate, w_up=v_w_up,
             w_down=v_w_down, post_ffn_g=v_post_ffn_g)
    xi, yi, ci = _mesh_pos()
    dev = 4 * xi + 2 * yi + ci
    n_tok, d = x.shape[1], x.shape[2]
    n_meta = meta_tokens.shape[0]
    n_layers = w_in.shape[0]
    c = d // 4
    l_real = n_meta + n_tok
    lp = -(-l_real // ATT_BLOCK) * ATT_BLOCK

    TRANSPOSED = ("w_in", "w_gate", "w_up")
    shard = {k: (jnp.swapaxes(P[k], 1, 2) if k in TRANSPOSED else P[k]) for k in SHARDED}

    def bf16_shard(k, l):
        return shard[k][l].astype(BF16)

    def whole(gathered):
        return gathered.reshape(-1, gathered.shape[-1])

    g_in0, g_dw, g_meta = _all_gather([bf16_shard("w_in", 0), w_dw, meta_tokens], "gather_first")
    wdw_full = g_dw.transpose(1, 2, 0, 3).reshape(n_layers, CONV_WIDTH, c)
    common = {"pool": w_pool.astype(BF16), "dw": jnp.pad(wdw_full, ((0, 0), (0, HALO - CONV_WIDTH), (0, 0)))}
    W = [dict(common) for _ in range(n_layers)]
    W[0]["in"] = whole(g_in0)
    meta_full = g_meta.transpose(1, 0, 2).reshape(n_meta, d)
    def gather_site(keys):
        def done(gathered):
            for (k, l), g in zip(keys, gathered):
                W[l][k[2:]] = whole(g)
        return [bf16_shard(k, l) for k, l in keys], done

    def fwd_hosts(l):
        first, then = ((), ("w_up", "w_down")) if l > 0 else (("w_pw", "w_out"), ("w_gate", "w_up", "w_down"))
        hosts = {"attn": gather_site([(k, l) for k in then])}
        if first:
            hosts["mm_in"] = gather_site([(k, l) for k in first])
        if l + 1 < n_layers:
            hosts["ffn_up"] = gather_site([(k, l + 1) for k in ("w_in", "w_pw", "w_out")])
            hosts["mm_down"] = gather_site([("w_gate", l + 1)])
        return hosts

    h = jnp.concatenate([meta_full, x[0], jnp.zeros((lp - l_real, d), F32)], axis=0)
    saved, u1 = [], None
    for l in range(n_layers):
        h, sv, u1 = _layer_fwd(h, l, W[l], P, fwd_hosts(l), u1)
        saved.append(sv)
    target = jnp.pad(loss_target[0], ((n_meta, lp - l_real), (0, 0)))
    loss_part, dh = _loss_grad(h, target, n_meta, n_tok)
    loss = lax.psum(loss_part[0, 0], ("x", "y", "c"))

    core = jnp.reshape(ci, (1,)).astype(jnp.int32)
    by_chip = {}

    def pair_reduce(g, keys, l):
        res = {}
        for k in keys:
            p = g[k].reshape(N_DEV, g[k].shape[0] // N_DEV, g[k].shape[1])
            recv = _pair_exchange(p, core, f"pair_exchange_{k}_{l}")
            res[k] = _pair_add(p, recv, core, f"pair_add_{k}_{l}")
        return res

    def exchange_site(sums, keys, l):
        def done(exchanged):
            by_chip.update({(l, k): e for k, e in zip(keys, exchanged)})
        return [sums[k] for k in keys], done

    def ride(g):
        sums = pair_reduce(g, EARLY, 0)
        return [sums[k] for k in EARLY]

    grads = [None] * n_layers
    hosts = None
    for l in reversed(range(n_layers)):
        last = (lambda g: exchange_site(pair_reduce(g, ("w_in",), 0), ("w_in",), 0)) if l == 0 else None
        dh, grads[l], exchanged = _layer_bwd(dh, l, W[l], P, saved[l], ride if l == 0 else None, hosts, last)
        if l == 0:
            by_chip.update({(0, k): e for k, e in zip(EARLY, exchanged)})
        else:
            sums = pair_reduce(grads[l], SHARDED, l)
            hosts = {"ffn_bwd_act": exchange_site(sums, ("w_down",), l),
                     "mm_d_u2": exchange_site(sums, ("w_gate", "w_up"), l),
                     "mm_dw_gate": exchange_site(sums, ("w_in", "w_out", "w_pw"), l)}
    grad_x = dh[n_meta:l_real][None]
    out = {}
    for k in SHARDED:
        view = (lambda a: jnp.swapaxes(a, 1, 2)) if k in TRANSPOSED else (lambda a: a)
        res = _adamw_sharded([by_chip[(l, k)] for l in range(n_layers)], view(P[k]), view(M[k]), view(V[k]), "adamw_" + k)
        out[k] = tuple(view(r) for r in res)

    small_parts = []
    for k in SMALL:
        if k == "meta_tokens":
            small_parts.append(dh[:n_meta])
        else:
            small_parts.append(jnp.stack([grads[l][k].reshape(P[k].shape[1:] if k != "w_dw" else (CONV_WIDTH, c))
                                          for l in range(n_layers)], axis=0))
    (g8,) = _all_gather([_pack(small_parts)], "gather_small_grads")
    g_small = dict(zip(SMALL, _unpack(_sum_slots(g8), small_parts)))
    g_small["w_dw"] = lax.dynamic_slice_in_dim(g_small["w_dw"], dev * w_dw.shape[2], w_dw.shape[2], axis=2)
    g_small["meta_tokens"] = lax.dynamic_slice_in_dim(g_small["meta_tokens"], dev * meta_tokens.shape[1],
                                                      meta_tokens.shape[1], axis=1)
    like = [P[k] for k in SMALL]
    res = _adamw_flat(_pack(like), _pack([g_small[k] for k in SMALL]), _pack([M[k] for k in SMALL]),
                      _pack([V[k] for k in SMALL]))
    res = [_unpack(r, like) for r in res]
    for i, k in enumerate(SMALL):
        out[k] = (g_small[k], res[0][i], res[1][i], res[2][i])

    return (loss, grad_x, *[out[k][0] for k in ORDER], *[out[k][1] for k in ORDER],
            *[out[k][2] for k in ORDER], *[out[k][3] for k in ORDER])
```

```python
import functools

import jax
import jax.numpy as jnp
from jax import lax
from jax.experimental import pallas as pl
from jax.experimental.pallas import tpu as pltpu

F32 = jnp.float32
BF16 = jnp.bfloat16
EPS = 1e-6
N_HEADS = 16
POOL_WINDOWS = (2, 4, 8, 16)
CONV_WIDTH = 31
HALO = 32
ATT_BLOCK = 128
DEAD_LOG_WEIGHT = 110.0
GATHER_BYTES_PER_HEAD = 450_000
N_DEV = 8
VMEM_LIMIT = 60 * 1024 * 1024

ADAM_LR = 0.001
ADAM_B1 = 0.9
ADAM_B2 = 0.999
ADAM_EPS = 1e-08
ADAM_WD = 0.01
ADAM_STEP = 10

NN = (((1,), (0,)), ((), ()))
NT = (((1,), (1,)), ((), ()))
TN = (((0,), (0,)), ((), ()))
MESH = pl.DeviceIdType.MESH
ANY = pl.BlockSpec(memory_space=pl.ANY)


def _pick(n, cands):
    for c in cands:
        if c <= n and n % c == 0:
            return c
    return n


def _params(grid):
    return pltpu.CompilerParams(dimension_semantics=("arbitrary",) * len(grid), vmem_limit_bytes=VMEM_LIMIT)


def _dot(a, b, dims=NN):
    return lax.dot_general(a, b, dims, preferred_element_type=F32)


class _Riders:
    def __init__(self, kind, arrays):
        self.gather = kind == "gather"
        self.arrays = list(arrays)
        self.n = len(self.arrays)
        self.out_shapes = _gather_shapes(self.arrays) if self.gather else _exchange_shapes(self.arrays)

    def call(self, body, name, grid, in_specs, out_shapes, out_specs, scratch, operands):
        n, n_in, n_out, n_scr = self.n, len(in_specs), len(out_shapes), len(scratch)

        def wrapped(*refs):
            ins, r_in = refs[:n_in], refs[n_in:n_in + n]
            outs, r_out = refs[n_in + n:n_in + n + n_out], refs[n_in + n + n_out:n_in + 2 * n + n_out]
            scr, sems = refs[n_in + 2 * n + n_out:n_in + 2 * n + n_out + n_scr], refs[n_in + 2 * n + n_out + n_scr:]
            ids = [pl.program_id(a) for a in range(len(grid))]
            first = functools.reduce(jnp.logical_and, [i == 0 for i in ids])
            last = functools.reduce(jnp.logical_and, [i == g - 1 for i, g in zip(ids, grid)])
            plan = (_GatherPlan if self.gather else _ChipExchangePlan)(r_in, r_out, *sems)
            pl.when(first)(plan.start)
            body(*ins, *outs, *scr)

            @pl.when(last)
            def _():
                if self.gather:
                    for a in range(n):
                        plan.forward(a)
                plan.finish()

        res = pl.pallas_call(
            wrapped, name=name, grid=grid, in_specs=list(in_specs) + [ANY] * n,
            out_shape=tuple(out_shapes) + tuple(self.out_shapes), out_specs=tuple(out_specs) + (ANY,) * n,
            scratch_shapes=list(scratch) + _comm_sems(n, 7 if self.gather else 3), compiler_params=_params(grid),
        )(*operands, *self.arrays)
        return tuple(res[:n_out]), list(res[n_out:])


def _call(riders, body, name, grid, in_specs, out_shapes, out_specs, scratch, operands):
    if riders is not None:
        return riders.call(body, name, grid, in_specs, out_shapes, out_specs, scratch, operands)
    return tuple(pl.pallas_call(body, name=name, grid=grid, in_specs=list(in_specs), out_shape=tuple(out_shapes),
                                out_specs=tuple(out_specs), scratch_shapes=list(scratch),
                                compiler_params=_params(grid))(*operands))


def _matmul(name, pairs, specs, out_shape, out_spec, grid, dims, acc_shape, riders=None):
    n = len(pairs)
    nk = grid[-1]
    kaxis = len(grid) - 1

    def body(*refs):
        o_ref, acc = refs[2 * n], refs[2 * n + 1]
        tot = None
        for p in range(n):
            d = _dot(refs[2 * p][...], refs[2 * p + 1][...], dims)
            tot = d if tot is None else tot + d
        if nk == 1:
            o_ref[...] = tot.astype(o_ref.dtype)
            return
        k = pl.program_id(kaxis)

        @pl.when(k == 0)
        def _():
            acc[...] = tot

        @pl.when(k > 0)
        def _():
            acc[...] += tot

        @pl.when(k == nk - 1)
        def _():
            o_ref[...] = acc[...].astype(o_ref.dtype)

    ops, in_specs = [], []
    for (a, b), (sa, sb) in zip(pairs, specs):
        ops += [a, b]
        in_specs += [sa, sb]
    res = _call(riders, body, name, grid, in_specs, [out_shape], [out_spec],
                [pltpu.VMEM(acc_shape if nk > 1 else (8, 128), F32)], ops)
    return res[0] if riders is None else (res[0][0], res[1])


ROW_TILES = (1408, 1024, 512, 256, 128)


def _mm_nt(a, w, name, out_dtype=F32, riders=None):
    m, kk = a.shape
    n = w.shape[0]
    tm = _pick(m, ROW_TILES)
    tn = _pick(n, (1024, 768, 512, 256, 128))
    grid = (m // tm, n // tn, 1)
    sa = pl.BlockSpec((tm, kk), lambda i, j, k: (i, 0))
    sb = pl.BlockSpec((tn, kk), lambda i, j, k: (j, 0))
    return _matmul(name, [(a, w)], [(sa, sb)], jax.ShapeDtypeStruct((m, n), out_dtype),
                   pl.BlockSpec((tm, tn), lambda i, j, k: (i, j)), grid, NT, (tm, tn), riders)


def _mm_nn(pairs, name, out_dtype=F32, riders=None):
    m, kc = pairs[0][0].shape
    n = pairs[0][1].shape[1]
    tm = _pick(m, ROW_TILES if len(pairs) == 1 else (704,) + ROW_TILES[2:])
    tn = _pick(n, (1024, 512, 256, 128))
    tk = _pick(kc, (2048, 1536, 1408, 1024, 768, 512, 256, 128))
    grid = (m // tm, n // tn, kc // tk)
    sa = pl.BlockSpec((tm, tk), lambda i, j, k: (i, k))
    sb = pl.BlockSpec((tk, tn), lambda i, j, k: (k, j))
    return _matmul(name, pairs, [(sa, sb)] * len(pairs), jax.ShapeDtypeStruct((m, n), out_dtype),
                   pl.BlockSpec((tm, tn), lambda i, j, k: (i, j)), grid, NN, (tm, tn), riders)


def _mm_tn(a, b, name, out_dtype=BF16, riders=None):
    l, m = a.shape
    n = b.shape[1]
    tm = _pick(m, (768, 512, 256, 128))
    tn = _pick(n, (1024, 512, 256, 128))
    grid = (m // tm, n // tn, 1)
    sa = pl.BlockSpec((l, tm), lambda i, j, t: (0, i))
    sb = pl.BlockSpec((l, tn), lambda i, j, t: (0, j))
    return _matmul(name, [(a, b)], [(sa, sb)], jax.ShapeDtypeStruct((m, n), out_dtype),
                   pl.BlockSpec((tm, tn), lambda i, j, t: (i, j)), grid, TN, (tm, tn), riders)


def _ffn_up(u, wg, wu, riders=None):
    m, kk = u.shape
    n = wg.shape[0]
    tm = _pick(m, ROW_TILES)
    tn = _pick(n, (512, 256, 128))
    grid = (m // tm, n // tn)

    def body(u_ref, wg_ref, wu_ref, a_ref, b_ref, act_ref):
        uu = u_ref[...]
        a = _dot(uu, wg_ref[...], NT)
        b = _dot(uu, wu_ref[...], NT)
        a_ref[...] = a
        b_ref[...] = b
        act_ref[...] = (a * jax.nn.sigmoid(a) * b).astype(BF16)

    wspec = pl.BlockSpec((tn, kk), lambda i, j: (j, 0))
    ospec = pl.BlockSpec((tm, tn), lambda i, j: (i, j))
    return _call(riders, body, "ffn_up", grid, [pl.BlockSpec((tm, kk), lambda i, j: (i, 0)), wspec, wspec],
                 [jax.ShapeDtypeStruct((m, n), F32), jax.ShapeDtypeStruct((m, n), F32), jax.ShapeDtypeStruct((m, n), BF16)],
                 [ospec, ospec, ospec], [], [u, wg, wu])


def _ffn_bwd_act(dff, wd, a, b, riders=None):
    m, kk = dff.shape
    n = wd.shape[0]
    tm = _pick(m, ROW_TILES)
    tn = _pick(n, (512, 256, 128))
    grid = (m // tm, n // tn)

    def body(d_ref, w_ref, a_ref, b_ref, da_ref, db_ref):
        dact = _dot(d_ref[...], w_ref[...], NT)
        aa, bb = a_ref[...], b_ref[...]
        sg = jax.nn.sigmoid(aa)
        db_ref[...] = (dact * aa * sg).astype(BF16)
        da_ref[...] = (dact * bb * sg * (1.0 + aa * (1.0 - sg))).astype(BF16)

    tspec = pl.BlockSpec((tm, tn), lambda i, j: (i, j))
    return _call(riders, body, "ffn_bwd_act", grid,
                 [pl.BlockSpec((tm, kk), lambda i, j: (i, 0)), pl.BlockSpec((tn, kk), lambda i, j: (j, 0)), tspec, tspec],
                 [jax.ShapeDtypeStruct((m, n), BF16), jax.ShapeDtypeStruct((m, n), BF16)], [tspec, tspec], [], [dff, wd, a, b])


def _rstd(x):
    return lax.rsqrt(jnp.mean(x * x, axis=-1, keepdims=True) + EPS)


def _row_tile(m):
    return _pick(m, (384, 256, 128))


def _rms_fwd(x, g, name):
    m, d = x.shape
    tr = _row_tile(m)
    grid = (m // tr,)

    def body(x_ref, g_ref, o_ref):
        xx = x_ref[...]
        o_ref[...] = (xx * _rstd(xx) * g_ref[...]).astype(BF16)

    row = pl.BlockSpec((tr, d), lambda i: (i, 0))
    return pl.pallas_call(body, name=name, grid=grid, out_shape=jax.ShapeDtypeStruct((m, d), BF16),
                          in_specs=[row, pl.BlockSpec((1, d), lambda i: (0, 0))], out_specs=row,
                          compiler_params=_params(grid))(x, g)


def _resid_rms_fwd(h, z, g, g_next, name):
    m, d = h.shape
    tr = _row_tile(m)
    grid = (m // tr,)
    has_next = g_next is not None

    def body(h_ref, z_ref, g_ref, *rest):
        zz = z_ref[...]
        hh = h_ref[...] + zz * _rstd(zz) * g_ref[...]
        rest[-2 if has_next else -1][...] = hh
        if has_next:
            rest[-1][...] = (hh * _rstd(hh) * rest[0][...]).astype(BF16)

    row = pl.BlockSpec((tr, d), lambda i: (i, 0))
    vec = pl.BlockSpec((1, d), lambda i: (0, 0))
    shapes = [jax.ShapeDtypeStruct((m, d), F32)] + ([jax.ShapeDtypeStruct((m, d), BF16)] if has_next else [])
    res = pl.pallas_call(body, name=name, grid=grid, out_shape=tuple(shapes),
                         in_specs=[row, row, vec] + ([vec] if has_next else []), out_specs=tuple([row] * len(shapes)),
                         compiler_params=_params(grid))(h, z, g, *([g_next] if has_next else []))
    return (res[0], res[1]) if has_next else (res[0], None)


def _rms_bwd_math(x, g, dy):
    r = _rstd(x)
    dyg = dy * g
    dx = r * dyg - x * (r * r * r) * jnp.mean(x * dyg, axis=-1, keepdims=True)
    dg = jnp.sum(dy * x * r, axis=0, keepdims=True)
    return dx, dg


def _rms_bwd(x, g, dy, dres, out_dtype, name):
    m, d = x.shape
    tr = _row_tile(m)
    grid = (m // tr,)
    has_res = dres is not None

    def body(*refs):
        x_ref, g_ref, dy_ref = refs[:3]
        dx_ref, dg_ref = refs[-2:]
        dx, dg = _rms_bwd_math(x_ref[...], g_ref[...], dy_ref[...].astype(F32))
        if has_res:
            dx = dx + refs[3][...]
        dx_ref[...] = dx.astype(out_dtype)

        @pl.when(pl.program_id(0) == 0)
        def _():
            dg_ref[...] = jnp.zeros_like(dg_ref)

        dg_ref[...] += dg

    row = pl.BlockSpec((tr, d), lambda i: (i, 0))
    vec = pl.BlockSpec((1, d), lambda i: (0, 0))
    ops = [x, g, dy] + ([dres] if has_res else [])
    return pl.pallas_call(
        body, name=name, grid=grid,
        out_shape=(jax.ShapeDtypeStruct((m, d), out_dtype), jax.ShapeDtypeStruct((1, d), F32)),
        in_specs=[row, vec, row] + ([row] if has_res else []), out_specs=(row, vec),
        compiler_params=_params(grid))(*ops)


def _merge_fwd(oa, op, oc, g):
    m = oa.shape[0]
    da, dp, dc = oa.shape[1], op.shape[1], oc.shape[1]
    d = da + dp + dc
    tr = _row_tile(m)
    grid = (m // tr,)

    def body(a_ref, p_ref, c_ref, g_ref, o_ref):
        off = 0
        for ref, w in ((a_ref, da), (p_ref, dp), (c_ref, dc)):
            xx = ref[...]
            o_ref[:, off:off + w] = (xx * _rstd(xx) * g_ref[:, off:off + w]).astype(BF16)
            off += w

    specs = [pl.BlockSpec((tr, w), lambda i: (i, 0)) for w in (da, dp, dc)]
    return pl.pallas_call(body, name="merge_fwd", grid=grid, out_shape=jax.ShapeDtypeStruct((m, d), BF16),
                          in_specs=specs + [pl.BlockSpec((1, d), lambda i: (0, 0))],
                          out_specs=pl.BlockSpec((tr, d), lambda i: (i, 0)), compiler_params=_params(grid))(oa, op, oc, g)


def _merge_bwd(oa, op, oc, g, dmerged):
    m = oa.shape[0]
    da, dp, dc = oa.shape[1], op.shape[1], oc.shape[1]
    d = da + dp + dc
    tr = _row_tile(m)
    grid = (m // tr,)

    def body(a_ref, p_ref, c_ref, g_ref, dm_ref, da_ref, dp_ref, dc_ref, dg_ref):
        @pl.when(pl.program_id(0) == 0)
        def _():
            dg_ref[...] = jnp.zeros_like(dg_ref)

        off = 0
        for ref, oref, w in ((a_ref, da_ref, da), (p_ref, dp_ref, dp), (c_ref, dc_ref, dc)):
            dx, dg = _rms_bwd_math(ref[...], g_ref[:, off:off + w], dm_ref[:, off:off + w])
            oref[...] = dx.astype(oref.dtype)
            dg_ref[:, off:off + w] += dg
            off += w

    specs = [pl.BlockSpec((tr, w), lambda i: (i, 0)) for w in (da, dp, dc)]
    vec = pl.BlockSpec((1, d), lambda i: (0, 0))
    return pl.pallas_call(
        body, name="merge_bwd", grid=grid,
        out_shape=(jax.ShapeDtypeStruct((m, da), F32), jax.ShapeDtypeStruct((m, dp), F32),
                   jax.ShapeDtypeStruct((m, dc), BF16), jax.ShapeDtypeStruct((1, d), F32)),
        in_specs=specs + [vec, pl.BlockSpec((tr, d), lambda i: (i, 0))], out_specs=tuple(specs) + (vec,),
        compiler_params=_params(grid))(oa, op, oc, g, dmerged)


def _loss_grad(h, target, n_meta, n_tok):
    m, d = h.shape
    tr = _row_tile(m)
    grid = (m // tr,)

    def body(h_ref, t_ref, loss_ref, dh_ref):
        i = pl.program_id(0)
        row = i * tr + lax.broadcasted_iota(jnp.int32, (tr, 1), 0)
        live = jnp.logical_and(row >= n_meta, row < n_meta + n_tok)
        diff = jnp.where(live, h_ref[...] - t_ref[...], 0.0)
        dh_ref[...] = diff * (1.0 / d)

        @pl.when(i == 0)
        def _():
            loss_ref[...] = jnp.zeros_like(loss_ref)

        loss_ref[...] += jnp.sum(jnp.sum(diff * diff, axis=1, keepdims=True), axis=0, keepdims=True) * (0.5 / d)

    row = pl.BlockSpec((tr, d), lambda i: (i, 0))
    return pl.pallas_call(
        body, name="loss_grad", grid=grid,
        out_shape=(jax.ShapeDtypeStruct((1, 1), F32), jax.ShapeDtypeStruct((m, d), F32)),
        in_specs=[row, row], out_specs=(pl.BlockSpec((1, 1), lambda i: (0, 0)), row),
        compiler_params=_params(grid))(h, target)


def _split_dot(x, tmat):
    hi = x.astype(BF16)
    lo = (x - hi.astype(F32)).astype(BF16)
    return _dot(hi, tmat) + _dot(lo, tmat)


def _softplus(z):
    return jnp.maximum(z, 0.0) + jnp.log(1.0 + jnp.exp(-jnp.abs(z)))


def _attn_tiles(lp):
    t = ATT_BLOCK
    nb = lp // t
    u = 3 if nb % 3 == 0 else (2 if nb % 2 == 0 else 1)
    return t, nb, u


def _attn_masks(tq, t, u):
    row = lax.broadcasted_iota(jnp.int32, (tq, t), 0)
    col = lax.broadcasted_iota(jnp.int32, (tq, t), 1)
    masks = [col + r * t < row for r in range(u)]
    a = lax.broadcasted_iota(jnp.int32, (t, 2 * t), 0)
    s = lax.broadcasted_iota(jnp.int32, (t, 2 * t), 1)
    after = jnp.logical_or(a > s, s >= t).astype(BF16)
    before = jnp.logical_or(a < s, s >= t).astype(BF16)
    return masks, after, before


def _service_heads(arrs, nh):
    heads, done = [], 0
    for a in arrs:
        done += a.size * a.dtype.itemsize
        heads.append(min(nh - 1, done // GATHER_BYTES_PER_HEAD + 1))
    return heads


def _attn_fwd(q, k, v, gather=()):
    nh, lp, dh = q.shape
    t, nb, u = _attn_tiles(lp)
    tq = u * t
    nq = nb // u
    grid = (nh, nq)
    ng = len(gather)
    service = _service_heads(gather, nh)

    def body(q_ref, k_ref, v_ref, *rest):
        o_ref, tot_ref, cnt_ref = rest[ng:ng + 3]
        h, i = pl.program_id(0), pl.program_id(1)
        if ng:
            plan = _GatherPlan(rest[:ng], rest[ng + 3:2 * ng + 3], *rest[2 * ng + 3:])
            pl.when(jnp.logical_and(h == 0, i == 0))(plan.start)
        qb = q_ref[...]
        masks, after, _ = _attn_masks(tq, t, u)
        o_ref[...] = jnp.zeros_like(o_ref)
        tot_ref[...] = jnp.zeros_like(tot_ref)

        def blocks(base, masked):
            outs = []
            for r in reversed(range(u)):
                lo = r * t if masked else 0
                start = pl.multiple_of((base + r) * t, t)
                kb = k_ref[pl.ds(start, t), :]
                z = _dot(qb[lo:], kb, NT)
                sp = _softplus(z)
                lnb = jnp.where(masks[r][lo:], -sp, 0.0) if masked else -sp
                res = _split_dot(lnb, after)
                outs.append((start, z - sp + res[:, :t], res[:, t:], r, lo))
            for start, logw, rs, r, lo in outs:
                cs = tot_ref[lo:, :]
                w = jnp.exp(logw + cs)
                if masked:
                    w = jnp.where(masks[r][lo:], w, 0.0)
                o_ref[lo:, :] += _dot(w.astype(BF16), v_ref[pl.ds(start, t), :])
                tot_ref[lo:, :] = cs + rs

        blocks(i * u, True)

        def live():
            return jnp.max(tot_ref[...]) > -DEAD_LOG_WEIGHT

        def step(state):
            n, _ = state
            blocks((i - 1 - n) * u, False)
            return n + 1, live()

        n_done, _ = lax.while_loop(lambda s: jnp.logical_and(s[0] < i, s[1]), step, (jnp.int32(0), live()))
        cnt_ref[h, i] = n_done.astype(F32)

        if ng:
            for a, head in enumerate(service):
                pl.when(jnp.logical_and(h == head, i == nq - 1))(functools.partial(plan.forward, a))
            pl.when(jnp.logical_and(h == nh - 1, i == nq - 1))(plan.finish)

    blk = pl.BlockSpec((None, tq, dh), lambda h, i: (h, i, 0))
    full = pl.BlockSpec((None, lp, dh), lambda h, i: (h, 0, 0))
    return pl.pallas_call(
        body, name="attn_fwd_gather" if ng else "attn_fwd", grid=grid,
        out_shape=(jax.ShapeDtypeStruct((nh, lp, dh), F32), jax.ShapeDtypeStruct((nh, lp, t), F32),
                   jax.ShapeDtypeStruct((nh, nq), F32)) + _gather_shapes(gather),
        in_specs=[blk, full, full] + [ANY] * ng,
        out_specs=(blk, pl.BlockSpec((None, tq, t), lambda h, i: (h, i, 0)), pl.BlockSpec(memory_space=pltpu.SMEM))
        + (ANY,) * ng,
        scratch_shapes=_comm_sems(ng, 7) if ng else [],
        compiler_params=_params(grid))(q, k, v, *gather)


def _attn_bwd(q, qt, k, v, do, dot_, tot, cnt, scale, exchange=()):
    nh, lp, dh = q.shape
    t, nb, u = _attn_tiles(lp)
    tq = u * t
    nq = nb // u
    grid = (nh, nq)
    ne = len(exchange)

    def body(q_ref, qt_ref, k_ref, v_ref, do_ref, dot_ref, tot_ref, cnt_ref, *rest):
        dq_ref, dk_ref, dv_ref = rest[ne:ne + 3]
        p_ref, g_ref = rest[2 * ne + 3:2 * ne + 5]
        h, i = pl.program_id(0), pl.program_id(1)
        if ne:
            plan = _ChipExchangePlan(rest[:ne], rest[ne + 3:2 * ne + 3], *rest[2 * ne + 5:])
            pl.when(jnp.logical_and(h == 0, i == 0))(plan.start)

        @pl.when(i == 0)
        def _():
            dk_ref[...] = jnp.zeros_like(dk_ref)
            dv_ref[...] = jnp.zeros_like(dv_ref)

        dq_ref[...] = jnp.zeros_like(dq_ref)
        p_ref[...] = jnp.zeros_like(p_ref)
        g_ref[...] = jnp.zeros_like(g_ref)
        qb, qtb, dob, dotb = q_ref[...], qt_ref[...], do_ref[...], dot_ref[...]
        total = tot_ref[...]
        masks, after, before = _attn_masks(tq, t, u)

        def blocks(base, masked):
            part = []
            for r in range(u):
                lo = r * t if masked else 0
                j = base + r
                start = pl.multiple_of(j * t, t)
                kb = k_ref[pl.ds(start, t), :]
                vb = v_ref[pl.ds(start, t), :]
                z = _dot(qb[lo:], kb, NT)
                sp = _softplus(z)
                lnb = jnp.where(masks[r][lo:], -sp, 0.0) if masked else -sp
                res = _split_dot(lnb, after)
                part.append((j, kb, z - sp, res[:, :t], res[:, t:], _dot(dob[lo:], vb, NT), r, lo))
            for j, kb, logsig, later_in, rs, dw, r, lo in part:
                pfx = p_ref[lo:, :] + rs
                w = jnp.exp(logsig + later_in + (total[lo:] - pfx))
                if masked:
                    w = jnp.where(masks[r][lo:], w, 0.0)
                dlogw = w * dw
                res2 = _dot(dlogw.astype(BF16), before)
                gp = g_ref[lo:, :]
                dz = dlogw - jnp.exp(logsig) * (dlogw + res2[:, :t] + gp)
                if masked:
                    dz = jnp.where(masks[r][lo:], dz, 0.0)
                p_ref[lo:, :] = pfx
                g_ref[lo:, :] = gp + res2[:, t:]
                dzb = dz.astype(BF16)
                dq_ref[lo:, :] += _dot(dzb, kb)
                dk_ref[j] += _dot(qtb[:, lo:], dzb)
                dv_ref[j] += _dot(dotb[:, lo:], w.astype(BF16))

        def step(n, carry):
            blocks(n * u, False)
            return carry

        walked = jnp.clip(cnt_ref[h, i].astype(jnp.int32), 0, i)
        lax.fori_loop(i - walked, i, step, 0)
        blocks(i * u, True)
        dq_ref[...] = dq_ref[...] * scale
        if ne:
            pl.when(jnp.logical_and(h == nh - 1, i == nq - 1))(plan.finish)

    blk = pl.BlockSpec((None, tq, dh), lambda h, i: (h, i, 0))
    blk_t = pl.BlockSpec((None, dh, tq), lambda h, i: (h, 0, i))
    full = pl.BlockSpec((None, lp, dh), lambda h, i: (h, 0, 0))
    acc = pl.BlockSpec((None, nb, dh, t), lambda h, i: (h, 0, 0, 0))
    acc_shape = jax.ShapeDtypeStruct((nh, nb, dh, t), F32)
    return pl.pallas_call(
        body, name="attn_bwd_exchange" if ne else "attn_bwd", grid=grid,
        out_shape=(jax.ShapeDtypeStruct((nh, lp, dh), F32), acc_shape, acc_shape) + _exchange_shapes(exchange),
        in_specs=[blk, blk_t, full, full, blk, blk_t, pl.BlockSpec((None, tq, t), lambda h, i: (h, i, 0)),
                  pl.BlockSpec(memory_space=pltpu.SMEM)] + [ANY] * ne,
        out_specs=(blk, acc, acc) + (ANY,) * ne,
        scratch_shapes=[pltpu.VMEM((tq, t), F32), pltpu.VMEM((tq, t), F32)] + (_comm_sems(ne, 3) if ne else []),
        compiler_params=_params(grid))(q, qt, k, v, do, dot_, tot, cnt, *exchange)


def _halo_specs(tm, width, nt):
    per = tm // HALO
    prev = pl.BlockSpec((HALO, width), lambda i: (jnp.maximum(i * per - 1, 0), 0))
    nxt = pl.BlockSpec((HALO, width), lambda i: (jnp.minimum((i + 1) * per, nt * per - 1), 0))
    return prev, nxt


def _shift_down(x, k):
    return x if k == 0 else pltpu.roll(x, k, axis=0)


def _shift_up(x, k):
    return x if k == 0 else pltpu.roll(x, x.shape[0] - k, axis=0)


def _pool_fwd(u, wp, scale, layer):
    m, dp = u.shape
    g = dp // len(POOL_WINDOWS)
    tm = _row_tile(m)
    nt = m // tm
    grid = (nt,)
    prev, _ = _halo_specs(tm, dp, nt)

    def body(u_ref, h_ref, w_ref, s_ref, o_ref):
        i = pl.program_id(0)
        halo = jnp.where(i > 0, h_ref[...], 0.0)
        s = jnp.concatenate([halo, u_ref[...]], axis=0)
        tpos = i * tm + lax.broadcasted_iota(jnp.int32, (tm, 1), 0)
        for gi, win in enumerate(POOL_WINDOWS):
            s = s + _shift_down(s, win // 2)
            cols = slice(gi * g, (gi + 1) * g)
            cnt = jnp.minimum(tpos + 1, win).astype(F32)
            pooled = s[HALO:, cols] / cnt - u_ref[:, cols]
            o_ref[:, cols] = _dot(pooled.astype(BF16), w_ref[gi]) * s_ref[:, cols]

    row = pl.BlockSpec((tm, dp), lambda i: (i, 0))
    return pl.pallas_call(
        body, name="pool_fwd", grid=grid, out_shape=jax.ShapeDtypeStruct((m, dp), F32),
        in_specs=[row, prev, pl.BlockSpec((None, len(POOL_WINDOWS), g, g), lambda i: (layer, 0, 0, 0)),
                  pl.BlockSpec((1, dp), lambda i: (0, 0))],
        out_specs=row, compiler_params=_params(grid))(u, u, wp, scale)


def _pool_bwd(u, dmix, wp, scale, layer):
    m, dp = u.shape
    ng = len(POOL_WINDOWS)
    g = dp // ng
    tm = _row_tile(m)
    nt = m // tm
    grid = (nt,)
    prev, nxt = _halo_specs(tm, dp, nt)

    def body(u_ref, h_ref, dm_ref, dmn_ref, w_ref, s_ref, du_ref, dw_ref, ds_ref):
        i = pl.program_id(0)

        @pl.when(i == 0)
        def _():
            dw_ref[...] = jnp.zeros_like(dw_ref)
            ds_ref[...] = jnp.zeros_like(ds_ref)

        halo = jnp.where(i > 0, h_ref[...], 0.0)
        s = jnp.concatenate([halo, u_ref[...]], axis=0)
        dmn = jnp.where(i < nt - 1, dmn_ref[...], 0.0)
        dmc = jnp.concatenate([dm_ref[...], dmn], axis=0)
        tpos = i * tm + lax.broadcasted_iota(jnp.int32, (tm, 1), 0)
        tpos_x = i * tm + lax.broadcasted_iota(jnp.int32, (tm + HALO, 1), 0)
        for gi, win in enumerate(POOL_WINDOWS):
            s = s + _shift_down(s, win // 2)
            cols = slice(gi * g, (gi + 1) * g)
            cnt = jnp.minimum(tpos + 1, win).astype(F32)
            pooled = (s[HALO:, cols] / cnt - u_ref[:, cols]).astype(BF16)
            wg = w_ref[gi]
            pm = _dot(pooled, wg)
            ds_ref[:, cols] += jnp.sum(dm_ref[:, cols] * pm, axis=0, keepdims=True)
            dpm = (dmc[:, cols] * s_ref[:, cols]).astype(BF16)
            dw_ref[gi] += _dot(pooled, dpm[:tm], TN)
            dpool = _dot(dpm, wg, NT)
            f = dpool / jnp.minimum(tpos_x + 1, win).astype(F32)
            step = 1
            while step < win:
                f = f + _shift_up(f, step)
                step *= 2
            du_ref[:, cols] = f[:tm] - dpool[:tm]

    row = pl.BlockSpec((tm, dp), lambda i: (i, 0))
    return pl.pallas_call(
        body, name="pool_bwd", grid=grid,
        out_shape=(jax.ShapeDtypeStruct((m, dp), F32), jax.ShapeDtypeStruct((ng, g, g), F32),
                   jax.ShapeDtypeStruct((1, dp), F32)),
        in_specs=[row, prev, row, nxt, pl.BlockSpec((None, ng, g, g), lambda i: (layer, 0, 0, 0)),
                  pl.BlockSpec((1, dp), lambda i: (0, 0))],
        out_specs=(row, pl.BlockSpec((ng, g, g), lambda i: (0, 0, 0)), pl.BlockSpec((1, dp), lambda i: (0, 0))),
        compiler_params=_params(grid))(u, u, dmix, dmix, wp, scale)


def _conv_taps(u, w_ref, rows):
    y = None
    for k in range(CONV_WIDTH):
        term = _shift_down(u, CONV_WIDTH - 1 - k)[HALO:HALO + rows] * w_ref[k:k + 1, :]
        y = term if y is None else y + term
    return y


def _layernorm_stats(y):
    mu = jnp.mean(y, axis=-1, keepdims=True)
    yc = y - mu
    rstd = lax.rsqrt(jnp.mean(yc * yc, axis=-1, keepdims=True) + EPS)
    return yc * rstd, rstd


def _conv_fwd(uc, wdw, b, lg, lb):
    m, c2 = uc.shape
    c = c2 // 2
    tm = _row_tile(m)
    nt = m // tm
    grid = (nt,)
    prev, _ = _halo_specs(tm, c2, nt)

    def body(x_ref, h_ref, w_ref, b_ref, g_ref, bb_ref, o_ref):
        i = pl.program_id(0)
        halo = jnp.where(i > 0, h_ref[...], 0.0)
        xc = jnp.concatenate([halo, x_ref[...]], axis=0)
        u = xc[:, :c] * jax.nn.sigmoid(xc[:, c:])
        y = _conv_taps(u, w_ref, tm) + b_ref[...]
        xhat, _ = _layernorm_stats(y)
        ln = xhat * g_ref[...] + bb_ref[...]
        o_ref[...] = (ln * jax.nn.sigmoid(ln)).astype(BF16)

    vec = pl.BlockSpec((1, c), lambda i: (0, 0))
    return pl.pallas_call(
        body, name="conv_fwd", grid=grid, out_shape=jax.ShapeDtypeStruct((m, c), BF16),
        in_specs=[pl.BlockSpec((tm, c2), lambda i: (i, 0)), prev, pl.BlockSpec((HALO, c), lambda i: (0, 0)), vec, vec, vec],
        out_specs=pl.BlockSpec((tm, c), lambda i: (i, 0)), compiler_params=_params(grid))(uc, uc, wdw, b, lg, lb)


def _conv_bwd(uc, ds, wdw, b, lg, lb):
    m, c2 = uc.shape
    c = c2 // 2
    tm = _row_tile(m)
    nt = m // tm
    grid = (nt,)
    prev, nxt = _halo_specs(tm, c2, nt)
    _, nxt_c = _halo_specs(tm, c, nt)

    def body(x_ref, hp_ref, hn_ref, ds_ref, dsn_ref, w_ref, b_ref, g_ref, bb_ref,
             dx_ref, dw_ref, db_ref, dg_ref, dbb_ref):
        i = pl.program_id(0)

        @pl.when(i == 0)
        def _():
            for ref in (dw_ref, db_ref, dg_ref, dbb_ref):
                ref[...] = jnp.zeros_like(ref)

        hp = jnp.where(i > 0, hp_ref[...], 0.0)
        xc = jnp.concatenate([hp, x_ref[...], hn_ref[...]], axis=0)
        sg = jax.nn.sigmoid(xc[:, c:])
        u = xc[:, :c] * sg
        rows = tm + HALO
        y = _conv_taps(u, w_ref, rows) + b_ref[...]
        xhat, rstd = _layernorm_stats(y)
        ln = xhat * g_ref[...] + bb_ref[...]
        sl = jax.nn.sigmoid(ln)
        dsn = jnp.where(i < nt - 1, dsn_ref[...], 0.0)
        dsx = jnp.concatenate([ds_ref[...], dsn], axis=0)
        dln = dsx * sl * (1.0 + ln * (1.0 - sl))
        dxh = dln * g_ref[...]
        dy = rstd * (dxh - jnp.mean(dxh, axis=-1, keepdims=True)
                     - xhat * jnp.mean(dxh * xhat, axis=-1, keepdims=True))
        dyt = dy[:tm]
        dg_ref[...] += jnp.sum(dln[:tm] * xhat[:tm], axis=0, keepdims=True)
        dbb_ref[...] += jnp.sum(dln[:tm], axis=0, keepdims=True)
        db_ref[...] += jnp.sum(dyt, axis=0, keepdims=True)
        du = None
        for k in range(CONV_WIDTH):
            lag = CONV_WIDTH - 1 - k
            dw_ref[k:k + 1, :] += jnp.sum(dyt * _shift_down(u, lag)[HALO:HALO + tm], axis=0, keepdims=True)
            term = _shift_up(dy, lag)[:tm] * w_ref[k:k + 1, :]
            du = term if du is None else du + term
        a_t = xc[HALO:HALO + tm, :c]
        sg_t = sg[HALO:HALO + tm]
        dx_ref[:, :c] = du * sg_t
        dx_ref[:, c:] = du * a_t * sg_t * (1.0 - sg_t)

    vec = pl.BlockSpec((1, c), lambda i: (0, 0))
    wsp = pl.BlockSpec((HALO, c), lambda i: (0, 0))
    vshape = jax.ShapeDtypeStruct((1, c), F32)
    return pl.pallas_call(
        body, name="conv_bwd", grid=grid,
        out_shape=(jax.ShapeDtypeStruct((m, c2), F32), jax.ShapeDtypeStruct((HALO, c), F32), vshape, vshape, vshape),
        in_specs=[pl.BlockSpec((tm, c2), lambda i: (i, 0)), prev, nxt, pl.BlockSpec((tm, c), lambda i: (i, 0)), nxt_c,
                  wsp, vec, vec, vec],
        out_specs=(pl.BlockSpec((tm, c2), lambda i: (i, 0)), wsp, vec, vec, vec),
        compiler_params=_params(grid))(uc, uc, uc, ds, ds, wdw, b, lg, lb)


def _mesh_pos():
    return lax.axis_index("x"), lax.axis_index("y"), lax.axis_index("c")


def _other_chips(x, y):
    return [(1 - x, y), (x, 1 - y), (1 - x, 1 - y)]


def _comm_sems(n, per):
    return [pltpu.SemaphoreType.DMA((n, per)), pltpu.SemaphoreType.DMA((n, per)), pltpu.SemaphoreType.DMA((n,))]


def _comm_call(body, name, arrs, out_shapes, n_sems):
    n = len(arrs)
    return pl.pallas_call(
        body, name=name, out_shape=out_shapes, in_specs=[ANY] * n, out_specs=tuple([ANY] * len(out_shapes)),
        scratch_shapes=_comm_sems(n, n_sems),
    )(*arrs)


class _GatherPlan:
    def __init__(self, ins, outs, send_sems, recv_sems, local_sems):
        self.ins, self.outs = ins, outs
        self.send_sems, self.recv_sems, self.local_sems = send_sems, recv_sems, local_sems
        x, y, c = _mesh_pos()
        self.c, self.me, self.sib = c, (x, y, c), (x, y, 1 - c)
        self.chips = _other_chips(x, y)

    def _slot(self, a, p):
        return self.outs[a].at[4 * p[0] + 2 * p[1] + p[2]]

    def _copy(self, a, k, block, to, own=False):
        return pltpu.make_async_remote_copy(
            src_ref=self.ins[a] if own else self._slot(a, block), dst_ref=self._slot(a, block),
            send_sem=self.send_sems.at[a, k], recv_sem=self.recv_sems.at[a, k], device_id=to, device_id_type=MESH)

    def _local(self, a):
        return pltpu.make_async_copy(self.ins[a], self._slot(a, self.me), self.local_sems.at[a])

    def start(self):
        for a in range(len(self.ins)):
            for j, chip in enumerate(self.chips):
                self._copy(a, 1 + j, self.me, (*chip, self.c), own=True).start()
            self._copy(a, 0, self.me, self.sib, own=True).start()
            self._local(a).start()

    def forward(self, a):
        for j, chip in enumerate(self.chips):
            self._copy(a, 1 + j, (*chip, self.c), self.me).wait_recv()
            self._copy(a, 4 + j, (*chip, self.c), self.sib).start()

    def finish(self):
        for a in range(len(self.ins)):
            self._copy(a, 0, self.sib, self.me).wait_recv()
            for j, chip in enumerate(self.chips):
                self._copy(a, 4 + j, (*chip, 1 - self.c), self.me).wait_recv()
        for a in range(len(self.ins)):
            self._copy(a, 0, self.me, self.sib, own=True).wait_send()
            for j, chip in enumerate(self.chips):
                self._copy(a, 1 + j, self.me, (*chip, self.c), own=True).wait_send()
                self._copy(a, 4 + j, (*chip, self.c), self.sib).wait_send()
            self._local(a).wait()


def _gather_shapes(arrs):
    return tuple(jax.ShapeDtypeStruct((N_DEV,) + a.shape, a.dtype) for a in arrs)


def _all_gather(arrs, name):
    n = len(arrs)

    def body(*refs):
        plan = _GatherPlan(refs[:n], refs[n:2 * n], *refs[2 * n:])
        plan.start()
        for a in range(n):
            plan.forward(a)
        plan.finish()

    return _comm_call(body, name, arrs, _gather_shapes(arrs), 7)


class _ChipExchangePlan:
    def __init__(self, ins, outs, send_sems, recv_sems, local_sems):
        self.ins, self.outs = ins, outs
        self.send_sems, self.recv_sems, self.local_sems = send_sems, recv_sems, local_sems
        x, y, c = _mesh_pos()
        self.pos, self.mine = (x, y, c), 2 * x + y
        self.chips = _other_chips(x, y)

    def _send(self, a, j):
        chip = self.chips[j]
        return pltpu.make_async_remote_copy(
            src_ref=self.ins[a].at[2 * chip[0] + chip[1]], dst_ref=self.outs[a].at[self.mine],
            send_sem=self.send_sems.at[a, j], recv_sem=self.recv_sems.at[a, j], device_id=(*chip, self.pos[2]),
            device_id_type=MESH)

    def _landing(self, a, j):
        chip = self.chips[j]
        ref = self.outs[a].at[2 * chip[0] + chip[1]]
        return pltpu.make_async_remote_copy(src_ref=ref, dst_ref=ref, send_sem=self.send_sems.at[a, j],
                                            recv_sem=self.recv_sems.at[a, j], device_id=self.pos, device_id_type=MESH)

    def _local(self, a):
        return pltpu.make_async_copy(self.ins[a].at[self.mine], self.outs[a].at[self.mine], self.local_sems.at[a])

    def start(self):
        for a in range(len(self.ins)):
            for j in range(3):
                self._send(a, j).start()
            self._local(a).start()

    def finish(self):
        for a in range(len(self.ins)):
            for j in range(3):
                self._landing(a, j).wait_recv()
        for a in range(len(self.ins)):
            for j in range(3):
                self._send(a, j).wait_send()
            self._local(a).wait()


def _pair_tile(kk):
    return _pick(kk, (1024, 704, 576, 512, 256, 128, 64))


def _pair_exchange(parts, core, name):
    _, kk, n = parts.shape
    tr = _pair_tile(kk)
    nr = kk // tr
    grid = (4, nr)

    def body(c_ref, src_ref, recv_ref, send_sems, recv_sems):
        q, r = pl.program_id(0), pl.program_id(1)
        x, y, c = _mesh_pos()
        step = q * nr + r
        slot = step % 2
        dst = recv_ref.at[pl.ds(pl.multiple_of(step * tr, tr), tr), :]
        cp = pltpu.make_async_remote_copy(src_ref=src_ref, dst_ref=dst, send_sem=send_sems.at[slot],
                                          recv_sem=recv_sems.at[slot], device_id=(x, y, 1 - c), device_id_type=MESH)
        cp.start()
        cp.wait_send()
        cp.wait_recv()

    spec = pltpu.PrefetchScalarGridSpec(
        num_scalar_prefetch=1, grid=grid,
        in_specs=[pl.BlockSpec((tr, n), lambda q, r, c: ((2 * q + 1 - c[0]) * nr + r, 0))],
        out_specs=ANY, scratch_shapes=[pltpu.SemaphoreType.DMA((2,)), pltpu.SemaphoreType.DMA((2,))])
    recv = pl.pallas_call(body, name=name, grid_spec=spec, out_shape=jax.ShapeDtypeStruct((4 * kk, n), parts.dtype),
                          compiler_params=_params(grid))(core, parts.reshape(N_DEV * kk, n))
    return recv.reshape(4, kk, n)


def _exchange_shapes(parts):
    return tuple(jax.ShapeDtypeStruct(p.shape, p.dtype) for p in parts)


def _pair_add(parts, recv, core, name):
    _, kk, n = recv.shape
    tr = _pair_tile(kk)
    grid = (4, kk // tr)

    def body(c_ref, a_ref, b_ref, o_ref):
        o_ref[...] = (a_ref[...].astype(F32) + b_ref[...].astype(F32)).astype(o_ref.dtype)

    out = pl.BlockSpec((None, tr, n), lambda q, r, c: (q, r, 0))
    spec = pltpu.PrefetchScalarGridSpec(
        num_scalar_prefetch=1, grid=grid,
        in_specs=[pl.BlockSpec((None, None, tr, n), lambda q, r, c: (q, c[0], r, 0)), out], out_specs=out)
    return pl.pallas_call(body, name=name, grid_spec=spec, out_shape=jax.ShapeDtypeStruct(recv.shape, recv.dtype),
                          compiler_params=_params(grid))(core, parts.reshape(4, 2, kk, n), recv)


def _adamw_math(w, g, m, v):
    m2 = ADAM_B1 * m + (1.0 - ADAM_B1) * g
    v2 = ADAM_B2 * v + (1.0 - ADAM_B2) * (g * g)
    m_hat = m2 / (1.0 - ADAM_B1 ** ADAM_STEP)
    v_hat = v2 / (1.0 - ADAM_B2 ** ADAM_STEP)
    delta = -ADAM_LR * (m_hat / (jnp.sqrt(v_hat) + ADAM_EPS) + ADAM_WD * w)
    return delta, m2, v2


def _adamw_sharded(parts, w, m, v, name):
    nl, kk, n = w.shape
    tr = _pick(kk, (256, 128, 64)) if n <= 1024 else _pick(kk, (64, 32))
    grid = (nl, kk // tr)

    def body(*refs):
        p_refs = refs[:nl]
        w_ref, m_ref, v_ref, g_ref, d_ref, m2_ref, v2_ref = refs[nl:]
        layer = pl.program_id(0)
        g = None
        for l, p_ref in enumerate(p_refs):
            s = p_ref[0].astype(F32)
            for q in range(1, 4):
                s = s + p_ref[q].astype(F32)
            g = s if g is None else jnp.where(layer == l, s, g)
        d, m2, v2 = _adamw_math(w_ref[...], g, m_ref[...], v_ref[...])
        g_ref[...] = g
        d_ref[...] = d
        m2_ref[...] = m2
        v2_ref[...] = v2

    spec = pl.BlockSpec((None, tr, n), lambda l, r: (l, r, 0))
    shp = jax.ShapeDtypeStruct(w.shape, F32)
    p_specs = [pl.BlockSpec((4, tr, n), lambda l, r, k=k: (0, jnp.where(l == k, r, 0), 0)) for k in range(nl)]
    return pl.pallas_call(
        body, name=name, grid=grid, out_shape=(shp, shp, shp, shp), in_specs=p_specs + [spec, spec, spec],
        out_specs=(spec, spec, spec, spec), compiler_params=_params(grid))(*parts, w, m, v)


def _sum_slots(g8):
    _, r, _ = g8.shape

    def body(g_ref, o_ref):
        s = g_ref[0]
        for d in range(1, N_DEV):
            s = s + g_ref[d]
        o_ref[...] = s

    return pl.pallas_call(body, name="sum_slots", out_shape=jax.ShapeDtypeStruct((r, 128), F32))(g8)


def _adamw_flat(w, g, m, v):
    def body(w_ref, g_ref, m_ref, v_ref, d_ref, m2_ref, v2_ref):
        d, m2, v2 = _adamw_math(w_ref[...], g_ref[...], m_ref[...], v_ref[...])
        d_ref[...] = d
        m2_ref[...] = m2
        v2_ref[...] = v2

    shp = jax.ShapeDtypeStruct(w.shape, F32)
    return pl.pallas_call(body, name="adamw_flat", out_shape=(shp, shp, shp))(w, g, m, v)


def _pack(arrs):
    flat = jnp.concatenate([a.reshape(-1) for a in arrs])
    pad = (-flat.shape[0]) % 1024
    return jnp.pad(flat, (0, pad)).reshape(-1, 128)


def _unpack(buf, like):
    flat = buf.reshape(-1)
    out, off = [], 0
    for a in like:
        out.append(flat[off:off + a.size].reshape(a.shape))
        off += a.size
    return out


def _heads(x2d, scale=None):
    lp, da = x2d.shape
    xh = x2d.reshape(lp, N_HEADS, da // N_HEADS).transpose(1, 0, 2)
    if scale is not None:
        xh = xh * scale
    return xh


def _unheads(xh):
    nh, lp, dh = xh.shape
    return xh.transpose(1, 0, 2).reshape(lp, nh * dh)


def _hosted(hosts, site, kind, fn):
    if hosts and site in hosts:
        arrays, done = hosts[site]
        res, got = fn(_Riders(kind, arrays))
        done(got)
        return res
    return fn(None)


def _layer_fwd(h, l, W, P, hosts=None, u1=None):
    d = h.shape[1]
    da, dp = d // 2, d // 4
    dh = da // N_HEADS
    sv = {"h": h}
    if u1 is None:
        u1 = _rms_fwd(h, P["pre_mix_g"][l:l + 1], "rms_pre_mix")
    proj = _hosted(hosts, "mm_in", "gather", lambda r: _mm_nt(u1, W["in"], "mm_in", riders=r))
    qh = _heads(proj[:, :da], dh ** -0.5).astype(BF16)
    kh = _heads(proj[:, da:2 * da]).astype(BF16)
    vh = _heads(proj[:, 2 * da:3 * da]).astype(BF16)
    u_pool = proj[:, 3 * da:3 * da + dp]
    u_conv = proj[:, 3 * da + dp:]
    if hosts and "attn" in hosts:
        arrs, done = hosts["attn"]
        oh, tot, cnt, *gathered = _attn_fwd(qh, kh, vh, arrs)
        done(gathered)
    else:
        oh, tot, cnt = _attn_fwd(qh, kh, vh)
    o_attn = _unheads(oh)
    o_pool = _pool_fwd(u_pool, W["pool"], P["pool_scale"][l:l + 1], l)
    s_conv = _conv_fwd(u_conv, W["dw"][l], P["b_dw"][l:l + 1], P["conv_ln_g"][l:l + 1], P["conv_ln_b"][l:l + 1])
    o_conv = _mm_nn([(s_conv, W["pw"])], "mm_pw")
    merged = _merge_fwd(o_attn, o_pool, o_conv, P["mix_out_g"][l:l + 1])
    z1 = _mm_nn([(merged, W["out"])], "mm_out")
    h1, u2 = _resid_rms_fwd(h, z1, P["post_mix_g"][l:l + 1], P["pre_ffn_g"][l:l + 1], "resid_post_mix")
    a_s, b_s, act_s = _hosted(hosts, "ffn_up", "gather", lambda r: _ffn_up(u2, W["gate"], W["up"], r))
    ff = _hosted(hosts, "mm_down", "gather", lambda r: _mm_nn([(act_s, W["down"])], "mm_down", riders=r))
    g_next = P["pre_mix_g"][l + 1:l + 2] if l + 1 < P["pre_mix_g"].shape[0] else None
    h2, u1_next = _resid_rms_fwd(h1, ff, P["post_ffn_g"][l:l + 1], g_next, "resid_post_ffn")
    sv.update(u1=u1, qh=qh, kh=kh, vh=vh, tot=tot, cnt=cnt, u_pool=u_pool, u_conv=u_conv, o_attn=o_attn, o_pool=o_pool,
              o_conv=o_conv, s_conv=s_conv, merged=merged, z1=z1, h1=h1, u2=u2, a_s=a_s, b_s=b_s, act_s=act_s, ff=ff)
    return h2, sv, u1_next


EARLY = ("w_down", "w_gate", "w_up", "w_out", "w_pw")


def _layer_bwd(dh_out, l, W, P, sv, exchange_with_attn=None, hosts=None, exchange_w_in=None):
    d = dh_out.shape[1]
    da = d // 2
    dhd = da // N_HEADS
    lp = dh_out.shape[0]
    g = {}
    dff, g["post_ffn_g"] = _rms_bwd(sv["ff"], P["post_ffn_g"][l:l + 1], dh_out, None, BF16, "rms_bwd_post_ffn")
    da_s, db_s = _hosted(hosts, "ffn_bwd_act", "exchange",
                         lambda r: _ffn_bwd_act(dff, W["down"], sv["a_s"], sv["b_s"], r))
    g["w_down"] = _mm_tn(sv["act_s"], dff, "mm_dw_down")
    du2 = _hosted(hosts, "mm_d_u2", "exchange",
                  lambda r: _mm_nn([(da_s, W["gate"]), (db_s, W["up"])], "mm_d_u2", riders=r))
    g["w_gate"] = _hosted(hosts, "mm_dw_gate", "exchange", lambda r: _mm_tn(da_s, sv["u2"], "mm_dw_gate", riders=r))
    g["w_up"] = _mm_tn(db_s, sv["u2"], "mm_dw_up")
    dh1, g["pre_ffn_g"] = _rms_bwd(sv["h1"], P["pre_ffn_g"][l:l + 1], du2, dh_out, F32, "rms_bwd_pre_ffn")
    dz1, g["post_mix_g"] = _rms_bwd(sv["z1"], P["post_mix_g"][l:l + 1], dh1, None, BF16, "rms_bwd_post_mix")
    dmerged = _mm_nt(dz1, W["out"], "mm_d_merged")
    g["w_out"] = _mm_tn(sv["merged"], dz1, "mm_dw_out")
    d_oa, d_op, d_oc, g["mix_out_g"] = _merge_bwd(sv["o_attn"], sv["o_pool"], sv["o_conv"], P["mix_out_g"][l:l + 1], dmerged)
    d_s = _mm_nt(d_oc, W["pw"], "mm_d_sconv")
    g["w_pw"] = _mm_tn(sv["s_conv"], d_oc, "mm_dw_pw")
    d_uc, dwdw, g["b_dw"], g["conv_ln_g"], g["conv_ln_b"] = _conv_bwd(
        sv["u_conv"], d_s, W["dw"][l], P["b_dw"][l:l + 1], P["conv_ln_g"][l:l + 1], P["conv_ln_b"][l:l + 1])
    g["w_dw"] = dwdw[:CONV_WIDTH]
    d_up, g["w_pool"], g["pool_scale"] = _pool_bwd(sv["u_pool"], d_op, W["pool"], P["pool_scale"][l:l + 1], l)
    doh = _heads(d_oa).astype(BF16)
    riders = exchange_with_attn(g) if exchange_with_attn is not None else []
    dqh, dkt, dvt, *exchanged = _attn_bwd(sv["qh"], sv["qh"].transpose(0, 2, 1), sv["kh"], sv["vh"], doh,
                                          doh.transpose(0, 2, 1), sv["tot"], sv["cnt"], dhd ** -0.5, riders)
    dk2, dv2 = (a.transpose(1, 3, 0, 2).reshape(lp, da) for a in (dkt, dvt))
    dproj = jnp.concatenate([_unheads(dqh), dk2, dv2, d_up, d_uc], axis=1).astype(BF16)
    g["w_in"] = _mm_tn(dproj, sv["u1"], "mm_dw_in")
    late = {"mm_d_u1": exchange_w_in(g)} if exchange_w_in is not None else None
    du1 = _hosted(late, "mm_d_u1", "exchange", lambda r: _mm_nn([(dproj, W["in"])], "mm_d_u1", riders=r))
    dh0, g["pre_mix_g"] = _rms_bwd(sv["h"], P["pre_mix_g"][l:l + 1], du1, dh1, F32, "rms_bwd_pre_mix")
    return dh0, g, exchanged


SHARDED = ("w_in", "w_gate", "w_up", "w_down", "w_out", "w_pw")
SMALL = ("pre_mix_g", "w_pool", "pool_scale", "b_dw", "conv_ln_g", "conv_ln_b", "mix_out_g", "post_mix_g",
         "pre_ffn_g", "post_ffn_g", "w_dw", "meta_tokens")
ORDER = ("meta_tokens", "pre_mix_g", "w_in", "w_pool", "pool_scale", "w_dw", "b_dw", "conv_ln_g", "conv_ln_b", "w_pw",
         "mix_out_g", "w_out", "post_mix_g", "pre_ffn_g", "w_gate", "w_up", "w_down", "post_ffn_g")


def kernel(x, meta_tokens, pre_mix_g, w_in, w_pool, pool_scale, w_dw, b_dw, conv_ln_g, conv_ln_b, w_pw, mix_out_g, w_out, post_mix_g, pre_ffn_g, w_gate, w_up, w_down, post_ffn_g, loss_target, m_meta_tokens, m_pre_mix_g, m_w_in, m_w_pool, m_pool_scale, m_w_dw, m_b_dw, m_conv_ln_g, m_conv_ln_b, m_w_pw, m_mix_out_g, m_w_out, m_post_mix_g, m_pre_ffn_g, m_w_gate, m_w_up, m_w_down, m_post_ffn_g, v_meta_tokens, v_pre_mix_g, v_w_in, v_w_pool, v_pool_scale, v_w_dw, v_b_dw, v_conv_ln_g, v_conv_ln_b, v_w_pw, v_mix_out_g, v_w_out, v_post_mix_g, v_pre_ffn_g, v_w_gate, v_w_up, v_w_down, v_post_ffn_g):
    P = dict(meta_tokens=meta_tokens, pre_mix_g=pre_mix_g, w_in=w_in, w_pool=w_pool, pool_scale=pool_scale, w_dw=w_dw,
             b_dw=b_dw, conv_ln_g=conv_ln_g, conv_ln_b=conv_ln_b, w_pw=w_pw, mix_out_g=mix_out_g, w_out=w_out,
             post_mix_g=post_mix_g, pre_ffn_g=pre_ffn_g, w_gate=w_gate, w_up=w_up, w_down=w_down, post_ffn_g=post_ffn_g)
    M = dict(meta_tokens=m_meta_tokens, pre_mix_g=m_pre_mix_g, w_in=m_w_in, w_pool=m_w_pool, pool_scale=m_pool_scale,
             w_dw=m_w_dw, b_dw=m_b_dw, conv_ln_g=m_conv_ln_g, conv_ln_b=m_conv_ln_b, w_pw=m_w_pw, mix_out_g=m_mix_out_g,
             w_out=m_w_out, post_mix_g=m_post_mix_g, pre_ffn_g=m_pre_ffn_g, w_gate=m_w_gate, w_up=m_w_up,
             w_down=m_w_down, post_ffn_g=m_post_ffn_g)
    V = dict(meta_tokens=v_meta_tokens, pre_mix_g=v_pre_mix_g, w_in=v_w_in, w_pool=v_w_pool, pool_scale=v_pool_scale,
             w_dw=v_w_dw, b_dw=v_b_dw, conv_ln_g=v_conv_ln_g, conv_ln_b=v_conv_ln_b, w_pw=v_w_pw, mix_out_g=v_mix_out_g,
             w_out=v_w_out, post_mix_g=v_post_mix_g, pre_ffn_g=v_pre_ffn_g, w_gate=v_w_gate, w_up=v_w_up,
             w_down=v_w_down, post_ffn_g=v_post_ffn_g)
    xi, yi, ci = _mesh_pos()
    dev = 4 * xi + 2 * yi + ci
    n_tok, d = x.shape[1], x.shape[2]
    n_meta = meta_tokens.shape[0]
    n_layers = w_in.shape[0]
    c = d // 4
    l_real = n_meta + n_tok
    lp = -(-l_real // ATT_BLOCK) * ATT_BLOCK

    TRANSPOSED = ("w_in", "w_gate", "w_up")
    shard = {k: (jnp.swapaxes(P[k], 1, 2) if k in TRANSPOSED else P[k]) for k in SHARDED}

    def bf16_shard(k, l):
        return shard[k][l].astype(BF16)

    def whole(gathered):
        return gathered.reshape(-1, gathered.shape[-1])

    g_in0, g_dw, g_meta = _all_gather([bf16_shard("w_in", 0), w_dw, meta_tokens], "gather_first")
    wdw_full = g_dw.transpose(1, 2, 0, 3).reshape(n_layers, CONV_WIDTH, c)
    common = {"pool": w_pool.astype(BF16), "dw": jnp.pad(wdw_full, ((0, 0), (0, HALO - CONV_WIDTH), (0, 0)))}
    W = [dict(common) for _ in range(n_layers)]
    W[0]["in"] = whole(g_in0)
    meta_full = g_meta.transpose(1, 0, 2).reshape(n_meta, d)
    def gather_site(keys):
        def done(gathered):
            for (k, l), g in zip(keys, gathered):
                W[l][k[2:]] = whole(g)
        return [bf16_shard(k, l) for k, l in keys], done

    def fwd_hosts(l):
        first, then = ((), ("w_up", "w_down")) if l > 0 else (("w_pw", "w_out"), ("w_gate", "w_up", "w_down"))
        hosts = {"attn": gather_site([(k, l) for k in then])}
        if first:
            hosts["mm_in"] = gather_site([(k, l) for k in first])
        if l + 1 < n_layers:
            hosts["ffn_up"] = gather_site([(k, l + 1) for k in ("w_in", "w_pw", "w_out")])
            hosts["mm_down"] = gather_site([("w_gate", l + 1)])
        return hosts

    h = jnp.concatenate([meta_full, x[0], jnp.zeros((lp - l_real, d), F32)], axis=0)
    saved, u1 = [], None
    for l in range(n_layers):
        h, sv, u1 = _layer_fwd(h, l, W[l], P, fwd_hosts(l), u1)
        saved.append(sv)
    target = jnp.pad(loss_target[0], ((n_meta, lp - l_real), (0, 0)))
    loss_part, dh = _loss_grad(h, target, n_meta, n_tok)
    loss = lax.psum(loss_part[0, 0], ("x", "y", "c"))

    core = jnp.reshape(ci, (1,)).astype(jnp.int32)
    by_chip = {}

    def pair_reduce(g, keys, l):
        res = {}
        for k in keys:
            p = g[k].reshape(N_DEV, g[k].shape[0] // N_DEV, g[k].shape[1])
            recv = _pair_exchange(p, core, f"pair_exchange_{k}_{l}")
            res[k] = _pair_add(p, recv, core, f"pair_add_{k}_{l}")
        return res

    def exchange_site(sums, keys, l):
        def done(exchanged):
            by_chip.update({(l, k): e for k, e in zip(keys, exchanged)})
        return [sums[k] for k in keys], done

    def ride(g):
        sums = pair_reduce(g, EARLY, 0)
        return [sums[k] for k in EARLY]

    grads = [None] * n_layers
    hosts = None
    for l in reversed(range(n_layers)):
        last = (lambda g: exchange_site(pair_reduce(g, ("w_in",), 0), ("w_in",), 0)) if l == 0 else None
        dh, grads[l], exchanged = _layer_bwd(dh, l, W[l], P, saved[l], ride if l == 0 else None, hosts, last)
        if l == 0:
            by_chip.update({(0, k): e for k, e in zip(EARLY, exchanged)})
        else:
            sums = pair_reduce(grads[l], SHARDED, l)
            hosts = {"ffn_bwd_act": exchange_site(sums, ("w_down",), l),
                     "mm_d_u2": exchange_site(sums, ("w_gate", "w_up"), l),
                     "mm_dw_gate": exchange_site(sums, ("w_in", "w_out", "w_pw"), l)}
    grad_x = dh[n_meta:l_real][None]
    out = {}
    for k in SHARDED:
        view = (lambda a: jnp.swapaxes(a, 1, 2)) if k in TRANSPOSED else (lambda a: a)
        res = _adamw_sharded([by_chip[(l, k)] for l in range(n_layers)], view(P[k]), view(M[k]), view(V[k]), "adamw_" + k)
        out[k] = tuple(view(r) for r in res)

    small_parts = []
    for k in SMALL:
        if k == "meta_tokens":
            small_parts.append(dh[:n_meta])
        else:
            small_parts.append(jnp.stack([grads[l][k].reshape(P[k].shape[1:] if k != "w_dw" else (CONV_WIDTH, c))
                                          for l in range(n_layers)], axis=0))
    (g8,) = _all_gather([_pack(small_parts)], "gather_small_grads")
    g_small = dict(zip(SMALL, _unpack(_sum_slots(g8), small_parts)))
    g_small["w_dw"] = lax.dynamic_slice_in_dim(g_small["w_dw"], dev * w_dw.shape[2], w_dw.shape[2], axis=2)
    g_small["meta_tokens"] = lax.dynamic_slice_in_dim(g_small["meta_tokens"], dev * meta_tokens.shape[1],
                                                      meta_tokens.shape[1], axis=1)
    like = [P[k] for k in SMALL]
    res = _adamw_flat(_pack(like), _pack([g_small[k] for k in SMALL]), _pack([M[k] for k in SMALL]),
                      _pack([V[k] for k in SMALL]))
    res = [_unpack(r, like) for r in res]
    for i, k in enumerate(SMALL):
        out[k] = (g_small[k], res[0][i], res[1][i], res[2][i])

    return (loss, grad_x, *[out[k][0] for k in ORDER], *[out[k][1] for k in ORDER],
            *[out[k][2] for k in ORDER], *[out[k][3] for k in ORDER])
```

```python
import functools

import jax
import jax.numpy as jnp
from jax import lax
from jax.experimental import pallas as pl
from jax.experimental.pallas import tpu as pltpu

F32 = jnp.float32
BF16 = jnp.bfloat16
EPS = 1e-6
N_HEADS = 16
POOL_WINDOWS = (2, 4, 8, 16)
CONV_WIDTH = 31
HALO = 32
ATT_BLOCK = 128
DEAD_LOG_WEIGHT = 110.0
GATHER_BYTES_PER_HEAD = 450_000
N_DEV = 8
VMEM_LIMIT = 60 * 1024 * 1024

ADAM_LR = 0.001
ADAM_B1 = 0.9
ADAM_B2 = 0.999
ADAM_EPS = 1e-08
ADAM_WD = 0.01
ADAM_STEP = 10

NN = (((1,), (0,)), ((), ()))
NT = (((1,), (1,)), ((), ()))
TN = (((0,), (0,)), ((), ()))
MESH = pl.DeviceIdType.MESH
ANY = pl.BlockSpec(memory_space=pl.ANY)


def _pick(n, cands):
    for c in cands:
        if c <= n and n % c == 0:
            return c
    return n


def _params(grid):
    return pltpu.CompilerParams(dimension_semantics=("arbitrary",) * len(grid), vmem_limit_bytes=VMEM_LIMIT)


def _dot(a, b, dims=NN):
    return lax.dot_general(a, b, dims, preferred_element_type=F32)


class _Riders:
    def __init__(self, kind, arrays):
        self.gather = kind == "gather"
        self.arrays = list(arrays)
        self.n = len(self.arrays)
        self.out_shapes = _gather_shapes(self.arrays) if self.gather else _exchange_shapes(self.arrays)

    def call(self, body, name, grid, in_specs, out_shapes, out_specs, scratch, operands):
        n, n_in, n_out, n_scr = self.n, len(in_specs), len(out_shapes), len(scratch)

        def wrapped(*refs):
            ins, r_in = refs[:n_in], refs[n_in:n_in + n]
            outs, r_out = refs[n_in + n:n_in + n + n_out], refs[n_in + n + n_out:n_in + 2 * n + n_out]
            scr, sems = refs[n_in + 2 * n + n_out:n_in + 2 * n + n_out + n_scr], refs[n_in + 2 * n + n_out + n_scr:]
            ids = [pl.program_id(a) for a in range(len(grid))]
            first = functools.reduce(jnp.logical_and, [i == 0 for i in ids])
            last = functools.reduce(jnp.logical_and, [i == g - 1 for i, g in zip(ids, grid)])
            plan = (_GatherPlan if self.gather else _ChipExchangePlan)(r_in, r_out, *sems)
            pl.when(first)(plan.start)
            body(*ins, *outs, *scr)

            @pl.when(last)
            def _():
                if self.gather:
                    for a in range(n):
                        plan.forward(a)
                plan.finish()

        res = pl.pallas_call(
            wrapped, name=name, grid=grid, in_specs=list(in_specs) + [ANY] * n,
            out_shape=tuple(out_shapes) + tuple(self.out_shapes), out_specs=tuple(out_specs) + (ANY,) * n,
            scratch_shapes=list(scratch) + _comm_sems(n, 7 if self.gather else 3), compiler_params=_params(grid),
        )(*operands, *self.arrays)
        return tuple(res[:n_out]), list(res[n_out:])


def _call(riders, body, name, grid, in_specs, out_shapes, out_specs, scratch, operands):
    if riders is not None:
        return riders.call(body, name, grid, in_specs, out_shapes, out_specs, scratch, operands)
    return tuple(pl.pallas_call(body, name=name, grid=grid, in_specs=list(in_specs), out_shape=tuple(out_shapes),
                                out_specs=tuple(out_specs), scratch_shapes=list(scratch),
                                compiler_params=_params(grid))(*operands))


def _matmul(name, pairs, specs, out_shape, out_spec, grid, dims, acc_shape, riders=None):
    n = len(pairs)
    nk = grid[-1]
    kaxis = len(grid) - 1

    def body(*refs):
        o_ref, acc = refs[2 * n], refs[2 * n + 1]
        tot = None
        for p in range(n):
            d = _dot(refs[2 * p][...], refs[2 * p + 1][...], dims)
            tot = d if tot is None else tot + d
        if nk == 1:
            o_ref[...] = tot.astype(o_ref.dtype)
            return
        k = pl.program_id(kaxis)

        @pl.when(k == 0)
        def _():
            acc[...] = tot

        @pl.when(k > 0)
        def _():
            acc[...] += tot

        @pl.when(k == nk - 1)
        def _():
            o_ref[...] = acc[...].astype(o_ref.dtype)

    ops, in_specs = [], []
    for (a, b), (sa, sb) in zip(pairs, specs):
        ops += [a, b]
        in_specs += [sa, sb]
    res = _call(riders, body, name, grid, in_specs, [out_shape], [out_spec],
                [pltpu.VMEM(acc_shape if nk > 1 else (8, 128), F32)], ops)
    return res[0] if riders is None else (res[0][0], res[1])


ROW_TILES = (1408, 1024, 512, 256, 128)


def _mm_nt(a, w, name, out_dtype=F32, riders=None):
    m, kk = a.shape
    n = w.shape[0]
    tm = _pick(m, ROW_TILES)
    tn = _pick(n, (1024, 768, 512, 256, 128))
    grid = (m // tm, n // tn, 1)
    sa = pl.BlockSpec((tm, kk), lambda i, j, k: (i, 0))
    sb = pl.BlockSpec((tn, kk), lambda i, j, k: (j, 0))
    return _matmul(name, [(a, w)], [(sa, sb)], jax.ShapeDtypeStruct((m, n), out_dtype),
                   pl.BlockSpec((tm, tn), lambda i, j, k: (i, j)), grid, NT, (tm, tn), riders)


def _mm_nn(pairs, name, out_dtype=F32, riders=None):
    m, kc = pairs[0][0].shape
    n = pairs[0][1].shape[1]
    tm = _pick(m, ROW_TILES if len(pairs) == 1 else (704,) + ROW_TILES[2:])
    tn = _pick(n, (1024, 512, 256, 128))
    tk = _pick(kc, (2048, 1536, 1408, 1024, 768, 512, 256, 128))
    grid = (m // tm, n // tn, kc // tk)
    sa = pl.BlockSpec((tm, tk), lambda i, j, k: (i, k))
    sb = pl.BlockSpec((tk, tn), lambda i, j, k: (k, j))
    return _matmul(name, pairs, [(sa, sb)] * len(pairs), jax.ShapeDtypeStruct((m, n), out_dtype),
                   pl.BlockSpec((tm, tn), lambda i, j, k: (i, j)), grid, NN, (tm, tn), riders)


def _mm_tn(a, b, name, out_dtype=BF16, riders=None):
    l, m = a.shape
    n = b.shape[1]
    tm = _pick(m, (768, 512, 256, 128))
    tn = _pick(n, (1024, 512, 256, 128))
    grid = (m // tm, n // tn, 1)
    sa = pl.BlockSpec((l, tm), lambda i, j, t: (0, i))
    sb = pl.BlockSpec((l, tn), lambda i, j, t: (0, j))
    return _matmul(name, [(a, b)], [(sa, sb)], jax.ShapeDtypeStruct((m, n), out_dtype),
                   pl.BlockSpec((tm, tn), lambda i, j, t: (i, j)), grid, TN, (tm, tn), riders)


def _ffn_up(u, wg, wu, riders=None):
    m, kk = u.shape
    n = wg.shape[0]
    tm = _pick(m, ROW_TILES)
    tn = _pick(n, (512, 256, 128))
    grid = (m // tm, n // tn)

    def body(u_ref, wg_ref, wu_ref, a_ref, b_ref, act_ref):
        uu = u_ref[...]
        a = _dot(uu, wg_ref[...], NT)
        b = _dot(uu, wu_ref[...], NT)
        a_ref[...] = a
        b_ref[...] = b
        act_ref[...] = (a * jax.nn.sigmoid(a) * b).astype(BF16)

    wspec = pl.BlockSpec((tn, kk), lambda i, j: (j, 0))
    ospec = pl.BlockSpec((tm, tn), lambda i, j: (i, j))
    return _call(riders, body, "ffn_up", grid, [pl.BlockSpec((tm, kk), lambda i, j: (i, 0)), wspec, wspec],
                 [jax.ShapeDtypeStruct((m, n), F32), jax.ShapeDtypeStruct((m, n), F32), jax.ShapeDtypeStruct((m, n), BF16)],
                 [ospec, ospec, ospec], [], [u, wg, wu])


def _ffn_bwd_act(dff, wd, a, b, riders=None):
    m, kk = dff.shape
    n = wd.shape[0]
    tm = _pick(m, ROW_TILES)
    tn = _pick(n, (512, 256, 128))
    grid = (m // tm, n // tn)

    def body(d_ref, w_ref, a_ref, b_ref, da_ref, db_ref):
        dact = _dot(d_ref[...], w_ref[...], NT)
        aa, bb = a_ref[...], b_ref[...]
        sg = jax.nn.sigmoid(aa)
        db_ref[...] = (dact * aa * sg).astype(BF16)
        da_ref[...] = (dact * bb * sg * (1.0 + aa * (1.0 - sg))).astype(BF16)

    tspec = pl.BlockSpec((tm, tn), lambda i, j: (i, j))
    return _call(riders, body, "ffn_bwd_act", grid,
                 [pl.BlockSpec((tm, kk), lambda i, j: (i, 0)), pl.BlockSpec((tn, kk), lambda i, j: (j, 0)), tspec, tspec],
                 [jax.ShapeDtypeStruct((m, n), BF16), jax.ShapeDtypeStruct((m, n), BF16)], [tspec, tspec], [], [dff, wd, a, b])


def _rstd(x):
    return lax.rsqrt(jnp.mean(x * x, axis=-1, keepdims=True) + EPS)


def _row_tile(m):
    return _pick(m, (384, 256, 128))


def _rms_fwd(x, g, name):
    m, d = x.shape
    tr = _row_tile(m)
    grid = (m // tr,)

    def body(x_ref, g_ref, o_ref):
        xx = x_ref[...]
        o_ref[...] = (xx * _rstd(xx) * g_ref[...]).astype(BF16)

    row = pl.BlockSpec((tr, d), lambda i: (i, 0))
    return pl.pallas_call(body, name=name, grid=grid, out_shape=jax.ShapeDtypeStruct((m, d), BF16),
                          in_specs=[row, pl.BlockSpec((1, d), lambda i: (0, 0))], out_specs=row,
                          compiler_params=_params(grid))(x, g)


def _resid_rms_fwd(h, z, g, g_next, name):
    m, d = h.shape
    tr = _row_tile(m)
    grid = (m // tr,)
    has_next = g_next is not None

    def body(h_ref, z_ref, g_ref, *rest):
        zz = z_ref[...]
        hh = h_ref[...] + zz * _rstd(zz) * g_ref[...]
        rest[-2 if has_next else -1][...] = hh
        if has_next:
            rest[-1][...] = (hh * _rstd(hh) * rest[0][...]).astype(BF16)

    row = pl.BlockSpec((tr, d), lambda i: (i, 0))
    vec = pl.BlockSpec((1, d), lambda i: (0, 0))
    shapes = [jax.ShapeDtypeStruct((m, d), F32)] + ([jax.ShapeDtypeStruct((m, d), BF16)] if has_next else [])
    res = pl.pallas_call(body, name=name, grid=grid, out_shape=tuple(shapes),
                         in_specs=[row, row, vec] + ([vec] if has_next else []), out_specs=tuple([row] * len(shapes)),
                         compiler_params=_params(grid))(h, z, g, *([g_next] if has_next else []))
    return (res[0], res[1]) if has_next else (res[0], None)


def _rms_bwd_math(x, g, dy):
    r = _rstd(x)
    dyg = dy * g
    dx = r * dyg - x * (r * r * r) * jnp.mean(x * dyg, axis=-1, keepdims=True)
    dg = jnp.sum(dy * x * r, axis=0, keepdims=True)
    return dx, dg


def _rms_bwd(x, g, dy, dres, out_dtype, name):
    m, d = x.shape
    tr = _row_tile(m)
    grid = (m // tr,)
    has_res = dres is not None

    def body(*refs):
        x_ref, g_ref, dy_ref = refs[:3]
        dx_ref, dg_ref = refs[-2:]
        dx, dg = _rms_bwd_math(x_ref[...], g_ref[...], dy_ref[...].astype(F32))
        if has_res:
            dx = dx + refs[3][...]
        dx_ref[...] = dx.astype(out_dtype)

        @pl.when(pl.program_id(0) == 0)
        def _():
            dg_ref[...] = jnp.zeros_like(dg_ref)

        dg_ref[...] += dg

    row = pl.BlockSpec((tr, d), lambda i: (i, 0))
    vec = pl.BlockSpec((1, d), lambda i: (0, 0))
    ops = [x, g, dy] + ([dres] if has_res else [])
    return pl.pallas_call(
        body, name=name, grid=grid,
        out_shape=(jax.ShapeDtypeStruct((m, d), out_dtype), jax.ShapeDtypeStruct((1, d), F32)),
        in_specs=[row, vec, row] + ([row] if has_res else []), out_specs=(row, vec),
        compiler_params=_params(grid))(*ops)


def _merge_fwd(oa, op, oc, g):
    m = oa.shape[0]
    da, dp, dc = oa.shape[1], op.shape[1], oc.shape[1]
    d = da + dp + dc
    tr = _row_tile(m)
    grid = (m // tr,)

    def body(a_ref, p_ref, c_ref, g_ref, o_ref):
        off = 0
        for ref, w in ((a_ref, da), (p_ref, dp), (c_ref, dc)):
            xx = ref[...]
            o_ref[:, off:off + w] = (xx * _rstd(xx) * g_ref[:, off:off + w]).astype(BF16)
            off += w

    specs = [pl.BlockSpec((tr, w), lambda i: (i, 0)) for w in (da, dp, dc)]
    return pl.pallas_call(body, name="merge_fwd", grid=grid, out_shape=jax.ShapeDtypeStruct((m, d), BF16),
                          in_specs=specs + [pl.BlockSpec((1, d), lambda i: (0, 0))],
                          out_specs=pl.BlockSpec((tr, d), lambda i: (i, 0)), compiler_params=_params(grid))(oa, op, oc, g)


def _merge_bwd(oa, op, oc, g, dmerged):
    m = oa.shape[0]
    da, dp, dc = oa.shape[1], op.shape[1], oc.shape[1]
    d = da + dp + dc
    tr = _row_tile(m)
    grid = (m // tr,)

    def body(a_ref, p_ref, c_ref, g_ref, dm_ref, da_ref, dp_ref, dc_ref, dg_ref):
        @pl.when(pl.program_id(0) == 0)
        def _():
            dg_ref[...] = jnp.zeros_like(dg_ref)

        off = 0
        for ref, oref, w in ((a_ref, da_ref, da), (p_ref, dp_ref, dp), (c_ref, dc_ref, dc)):
            dx, dg = _rms_bwd_math(ref[...], g_ref[:, off:off + w], dm_ref[:, off:off + w])
            oref[...] = dx.astype(oref.dtype)
            dg_ref[:, off:off + w] += dg
            off += w

    specs = [pl.BlockSpec((tr, w), lambda i: (i, 0)) for w in (da, dp, dc)]
    vec = pl.BlockSpec((1, d), lambda i: (0, 0))
    return pl.pallas_call(
        body, name="merge_bwd", grid=grid,
        out_shape=(jax.ShapeDtypeStruct((m, da), F32), jax.ShapeDtypeStruct((m, dp), F32),
                   jax.ShapeDtypeStruct((m, dc), BF16), jax.ShapeDtypeStruct((1, d), F32)),
        in_specs=specs + [vec, pl.BlockSpec((tr, d), lambda i: (i, 0))], out_specs=tuple(specs) + (vec,),
        compiler_params=_params(grid))(oa, op, oc, g, dmerged)


def _loss_grad(h, target, n_meta, n_tok):
    m, d = h.shape
    tr = _row_tile(m)
    grid = (m // tr,)

    def body(h_ref, t_ref, loss_ref, dh_ref):
        i = pl.program_id(0)
        row = i * tr + lax.broadcasted_iota(jnp.int32, (tr, 1), 0)
        live = jnp.logical_and(row >= n_meta, row < n_meta + n_tok)
        diff = jnp.where(live, h_ref[...] - t_ref[...], 0.0)
        dh_ref[...] = diff * (1.0 / d)

        @pl.when(i == 0)
        def _():
            loss_ref[...] = jnp.zeros_like(loss_ref)

        loss_ref[...] += jnp.sum(jnp.sum(diff * diff, axis=1, keepdims=True), axis=0, keepdims=True) * (0.5 / d)

    row = pl.BlockSpec((tr, d), lambda i: (i, 0))
    return pl.pallas_call(
        body, name="loss_grad", grid=grid,
        out_shape=(jax.ShapeDtypeStruct((1, 1), F32), jax.ShapeDtypeStruct((m, d), F32)),
        in_specs=[row, row], out_specs=(pl.BlockSpec((1, 1), lambda i: (0, 0)), row),
        compiler_params=_params(grid))(h, target)


def _split_dot(x, tmat):
    hi = x.astype(BF16)
    lo = (x - hi.astype(F32)).astype(BF16)
    return _dot(hi, tmat) + _dot(lo, tmat)


def _softplus(z):
    return jnp.maximum(z, 0.0) + jnp.log(1.0 + jnp.exp(-jnp.abs(z)))


def _attn_tiles(lp):
    t = ATT_BLOCK
    nb = lp // t
    u = 3 if nb % 3 == 0 else (2 if nb % 2 == 0 else 1)
    return t, nb, u


def _attn_masks(tq, t, u):
    row = lax.broadcasted_iota(jnp.int32, (tq, t), 0)
    col = lax.broadcasted_iota(jnp.int32, (tq, t), 1)
    masks = [col + r * t < row for r in range(u)]
    a = lax.broadcasted_iota(jnp.int32, (t, 2 * t), 0)
    s = lax.broadcasted_iota(jnp.int32, (t, 2 * t), 1)
    after = jnp.logical_or(a > s, s >= t).astype(BF16)
    before = jnp.logical_or(a < s, s >= t).astype(BF16)
    return masks, after, before


def _service_heads(arrs, nh):
    heads, done = [], 0
    for a in arrs:
        done += a.size * a.dtype.itemsize
        heads.append(min(nh - 1, done // GATHER_BYTES_PER_HEAD + 1))
    return heads


def _attn_fwd(q, k, v, gather=()):
    nh, lp, dh = q.shape
    t, nb, u = _attn_tiles(lp)
    tq = u * t
    nq = nb // u
    grid = (nh, nq)
    ng = len(gather)
    service = _service_heads(gather, nh)

    def body(q_ref, k_ref, v_ref, *rest):
        o_ref, tot_ref, cnt_ref = rest[ng:ng + 3]
        h, i = pl.program_id(0), pl.program_id(1)
        if ng:
            plan = _GatherPlan(rest[:ng], rest[ng + 3:2 * ng + 3], *rest[2 * ng + 3:])
            pl.when(jnp.logical_and(h == 0, i == 0))(plan.start)
        qb = q_ref[...]
        masks, after, _ = _attn_masks(tq, t, u)
        o_ref[...] = jnp.zeros_like(o_ref)
        tot_ref[...] = jnp.zeros_like(tot_ref)

        def blocks(base, masked):
            outs = []
            for r in reversed(range(u)):
                lo = r * t if masked else 0
                start = pl.multiple_of((base + r) * t, t)
                kb = k_ref[pl.ds(start, t), :]
                z = _dot(qb[lo:], kb, NT)
                sp = _softplus(z)
                lnb = jnp.where(masks[r][lo:], -sp, 0.0) if masked else -sp
                res = _split_dot(lnb, after)
                outs.append((start, z - sp + res[:, :t], res[:, t:], r, lo))
            for start, logw, rs, r, lo in outs:
                cs = tot_ref[lo:, :]
                w = jnp.exp(logw + cs)
                if masked:
                    w = jnp.where(masks[r][lo:], w, 0.0)
                o_ref[lo:, :] += _dot(w.astype(BF16), v_ref[pl.ds(start, t), :])
                tot_ref[lo:, :] = cs + rs

        blocks(i * u, True)

        def live():
            return jnp.max(tot_ref[...]) > -DEAD_LOG_WEIGHT

        def step(state):
            n, _ = state
            blocks((i - 1 - n) * u, False)
            return n + 1, live()

        n_done, _ = lax.while_loop(lambda s: jnp.logical_and(s[0] < i, s[1]), step, (jnp.int32(0), live()))
        cnt_ref[h, i] = n_done.astype(F32)

        if ng:
            for a, head in enumerate(service):
                pl.when(jnp.logical_and(h == head, i == nq - 1))(functools.partial(plan.forward, a))
            pl.when(jnp.logical_and(h == nh - 1, i == nq - 1))(plan.finish)

    blk = pl.BlockSpec((None, tq, dh), lambda h, i: (h, i, 0))
    full = pl.BlockSpec((None, lp, dh), lambda h, i: (h, 0, 0))
    return pl.pallas_call(
        body, name="attn_fwd_gather" if ng else "attn_fwd", grid=grid,
        out_shape=(jax.ShapeDtypeStruct((nh, lp, dh), F32), jax.ShapeDtypeStruct((nh, lp, t), F32),
                   jax.ShapeDtypeStruct((nh, nq), F32)) + _gather_shapes(gather),
        in_specs=[blk, full, full] + [ANY] * ng,
        out_specs=(blk, pl.BlockSpec((None, tq, t), lambda h, i: (h, i, 0)), pl.BlockSpec(memory_space=pltpu.SMEM))
        + (ANY,) * ng,
        scratch_shapes=_comm_sems(ng, 7) if ng else [],
        compiler_params=_params(grid))(q, k, v, *gather)


def _attn_bwd(q, qt, k, v, do, dot_, tot, cnt, scale, exchange=()):
    nh, lp, dh = q.shape
    t, nb, u = _attn_tiles(lp)
    tq = u * t
    nq = nb // u
    grid = (nh, nq)
    ne = len(exchange)

    def body(q_ref, qt_ref, k_ref, v_ref, do_ref, dot_ref, tot_ref, cnt_ref, *rest):
        dq_ref, dk_ref, dv_ref = rest[ne:ne + 3]
        p_ref, g_ref = rest[2 * ne + 3:2 * ne + 5]
        h, i = pl.program_id(0), pl.program_id(1)
        if ne:
            plan = _ChipExchangePlan(rest[:ne], rest[ne + 3:2 * ne + 3], *rest[2 * ne + 5:])
            pl.when(jnp.logical_and(h == 0, i == 0))(plan.start)

        @pl.when(i == 0)
        def _():
            dk_ref[...] = jnp.zeros_like(dk_ref)
            dv_ref[...] = jnp.zeros_like(dv_ref)

        dq_ref[...] = jnp.zeros_like(dq_ref)
        p_ref[...] = jnp.zeros_like(p_ref)
        g_ref[...] = jnp.zeros_like(g_ref)
        qb, qtb, dob, dotb = q_ref[...], qt_ref[...], do_ref[...], dot_ref[...]
        total = tot_ref[...]
        masks, after, before = _attn_masks(tq, t, u)

        def blocks(base, masked):
            part = []
            for r in range(u):
                lo = r * t if masked else 0
                j = base + r
                start = pl.multiple_of(j * t, t)
                kb = k_ref[pl.ds(start, t), :]
                vb = v_ref[pl.ds(start, t), :]
                z = _dot(qb[lo:], kb, NT)
                sp = _softplus(z)
                lnb = jnp.where(masks[r][lo:], -sp, 0.0) if masked else -sp
                res = _split_dot(lnb, after)
                part.append((j, kb, z - sp, res[:, :t], res[:, t:], _dot(dob[lo:], vb, NT), r, lo))
            for j, kb, logsig, later_in, rs, dw, r, lo in part:
                pfx = p_ref[lo:, :] + rs
                w = jnp.exp(logsig + later_in + (total[lo:] - pfx))
                if masked:
                    w = jnp.where(masks[r][lo:], w, 0.0)
                dlogw = w * dw
                res2 = _dot(dlogw.astype(BF16), before)
                gp = g_ref[lo:, :]
                dz = dlogw - jnp.exp(logsig) * (dlogw + res2[:, :t] + gp)
                if masked:
                    dz = jnp.where(masks[r][lo:], dz, 0.0)
                p_ref[lo:, :] = pfx
                g_ref[lo:, :] = gp + res2[:, t:]
                dzb = dz.astype(BF16)
                dq_ref[lo:, :] += _dot(dzb, kb)
                dk_ref[j] += _dot(qtb[:, lo:], dzb)
                dv_ref[j] += _dot(dotb[:, lo:], w.astype(BF16))

        def step(n, carry):
            blocks(n * u, False)
            return carry

        walked = jnp.clip(cnt_ref[h, i].astype(jnp.int32), 0, i)
        lax.fori_loop(i - walked, i, step, 0)
        blocks(i * u, True)
        dq_ref[...] = dq_ref[...] * scale
        if ne:
            pl.when(jnp.logical_and(h == nh - 1, i == nq - 1))(plan.finish)

    blk = pl.BlockSpec((None, tq, dh), lambda h, i: (h, i, 0))
    blk_t = pl.BlockSpec((None, dh, tq), lambda h, i: (h, 0, i))
    full = pl.BlockSpec((None, lp, dh), lambda h, i: (h, 0, 0))
    acc = pl.BlockSpec((None, nb, dh, t), lambda h, i: (h, 0, 0, 0))
    acc_shape = jax.ShapeDtypeStruct((nh, nb, dh, t), F32)
    return pl.pallas_call(
        body, name="attn_bwd_exchange" if ne else "attn_bwd", grid=grid,
        out_shape=(jax.ShapeDtypeStruct((nh, lp, dh), F32), acc_shape, acc_shape) + _exchange_shapes(exchange),
        in_specs=[blk, blk_t, full, full, blk, blk_t, pl.BlockSpec((None, tq, t), lambda h, i: (h, i, 0)),
                  pl.BlockSpec(memory_space=pltpu.SMEM)] + [ANY] * ne,
        out_specs=(blk, acc, acc) + (ANY,) * ne,
        scratch_shapes=[pltpu.VMEM((tq, t), F32), pltpu.VMEM((tq, t), F32)] + (_comm_sems(ne, 3) if ne else []),
        compiler_params=_params(grid))(q, qt, k, v, do, dot_, tot, cnt, *exchange)


def _halo_specs(tm, width, nt):
    per = tm // HALO
    prev = pl.BlockSpec((HALO, width), lambda i: (jnp.maximum(i * per - 1, 0), 0))
    nxt = pl.BlockSpec((HALO, width), lambda i: (jnp.minimum((i + 1) * per, nt * per - 1), 0))
    return prev, nxt


def _shift_down(x, k):
    return x if k == 0 else pltpu.roll(x, k, axis=0)


def _shift_up(x, k):
    return x if k == 0 else pltpu.roll(x, x.shape[0] - k, axis=0)


def _pool_fwd(u, wp, scale, layer):
    m, dp = u.shape
    g = dp // len(POOL_WINDOWS)
    tm = _row_tile(m)
    nt = m // tm
    grid = (nt,)
    prev, _ = _halo_specs(tm, dp, nt)

    def body(u_ref, h_ref, w_ref, s_ref, o_ref):
        i = pl.program_id(0)
        halo = jnp.where(i > 0, h_ref[...], 0.0)
        s = jnp.concatenate([halo, u_ref[...]], axis=0)
        tpos = i * tm + lax.broadcasted_iota(jnp.int32, (tm, 1), 0)
        for gi, win in enumerate(POOL_WINDOWS):
            s = s + _shift_down(s, win // 2)
            cols = slice(gi * g, (gi + 1) * g)
            cnt = jnp.minimum(tpos + 1, win).astype(F32)
            pooled = s[HALO:, cols] / cnt - u_ref[:, cols]
            o_ref[:, cols] = _dot(pooled.astype(BF16), w_ref[gi]) * s_ref[:, cols]

    row = pl.BlockSpec((tm, dp), lambda i: (i, 0))
    return pl.pallas_call(
        body, name="pool_fwd", grid=grid, out_shape=jax.ShapeDtypeStruct((m, dp), F32),
        in_specs=[row, prev, pl.BlockSpec((None, len(POOL_WINDOWS), g, g), lambda i: (layer, 0, 0, 0)),
                  pl.BlockSpec((1, dp), lambda i: (0, 0))],
        out_specs=row, compiler_params=_params(grid))(u, u, wp, scale)


def _pool_bwd(u, dmix, wp, scale, layer):
    m, dp = u.shape
    ng = len(POOL_WINDOWS)
    g = dp // ng
    tm = _row_tile(m)
    nt = m // tm
    grid = (nt,)
    prev, nxt = _halo_specs(tm, dp, nt)

    def body(u_ref, h_ref, dm_ref, dmn_ref, w_ref, s_ref, du_ref, dw_ref, ds_ref):
        i = pl.program_id(0)

        @pl.when(i == 0)
        def _():
            dw_ref[...] = jnp.zeros_like(dw_ref)
            ds_ref[...] = jnp.zeros_like(ds_ref)

        halo = jnp.where(i > 0, h_ref[...], 0.0)
        s = jnp.concatenate([halo, u_ref[...]], axis=0)
        dmn = jnp.where(i < nt - 1, dmn_ref[...], 0.0)
        dmc = jnp.concatenate([dm_ref[...], dmn], axis=0)
        tpos = i * tm + lax.broadcasted_iota(jnp.int32, (tm, 1), 0)
        tpos_x = i * tm + lax.broadcasted_iota(jnp.int32, (tm + HALO, 1), 0)
        for gi, win in enumerate(POOL_WINDOWS):
            s = s + _shift_down(s, win // 2)
            cols = slice(gi * g, (gi + 1) * g)
            cnt = jnp.minimum(tpos + 1, win).astype(F32)
            pooled = (s[HALO:, cols] / cnt - u_ref[:, cols]).astype(BF16)
            wg = w_ref[gi]
            pm = _dot(pooled, wg)
            ds_ref[:, cols] += jnp.sum(dm_ref[:, cols] * pm, axis=0, keepdims=True)
            dpm = (dmc[:, cols] * s_ref[:, cols]).astype(BF16)
            dw_ref[gi] += _dot(pooled, dpm[:tm], TN)
            dpool = _dot(dpm, wg, NT)
            f = dpool / jnp.minimum(tpos_x + 1, win).astype(F32)
            step = 1
            while step < win:
                f = f + _shift_up(f, step)
                step *= 2
            du_ref[:, cols] = f[:tm] - dpool[:tm]

    row = pl.BlockSpec((tm, dp), lambda i: (i, 0))
    return pl.pallas_call(
        body, name="pool_bwd", grid=grid,
        out_shape=(jax.ShapeDtypeStruct((m, dp), F32), jax.ShapeDtypeStruct((ng, g, g), F32),
                   jax.ShapeDtypeStruct((1, dp), F32)),
        in_specs=[row, prev, row, nxt, pl.BlockSpec((None, ng, g, g), lambda i: (layer, 0, 0, 0)),
                  pl.BlockSpec((1, dp), lambda i: (0, 0))],
        out_specs=(row, pl.BlockSpec((ng, g, g), lambda i: (0, 0, 0)), pl.BlockSpec((1, dp), lambda i: (0, 0))),
        compiler_params=_params(grid))(u, u, dmix, dmix, wp, scale)


def _conv_taps(u, w_ref, rows):
    y = None
    for k in range(CONV_WIDTH):
        term = _shift_down(u, CONV_WIDTH - 1 - k)[HALO:HALO + rows] * w_ref[k:k + 1, :]
        y = term if y is None else y + term
    return y


def _layernorm_stats(y):
    mu = jnp.mean(y, axis=-1, keepdims=True)
    yc = y - mu
    rstd = lax.rsqrt(jnp.mean(yc * yc, axis=-1, keepdims=True) + EPS)
    return yc * rstd, rstd


def _conv_fwd(uc, wdw, b, lg, lb):
    m, c2 = uc.shape
    c = c2 // 2
    tm = _row_tile(m)
    nt = m // tm
    grid = (nt,)
    prev, _ = _halo_specs(tm, c2, nt)

    def body(x_ref, h_ref, w_ref, b_ref, g_ref, bb_ref, o_ref):
        i = pl.program_id(0)
        halo = jnp.where(i > 0, h_ref[...], 0.0)
        xc = jnp.concatenate([halo, x_ref[...]], axis=0)
        u = xc[:, :c] * jax.nn.sigmoid(xc[:, c:])
        y = _conv_taps(u, w_ref, tm) + b_ref[...]
        xhat, _ = _layernorm_stats(y)
        ln = xhat * g_ref[...] + bb_ref[...]
        o_ref[...] = (ln * jax.nn.sigmoid(ln)).astype(BF16)

    vec = pl.BlockSpec((1, c), lambda i: (0, 0))
    return pl.pallas_call(
        body, name="conv_fwd", grid=grid, out_shape=jax.ShapeDtypeStruct((m, c), BF16),
        in_specs=[pl.BlockSpec((tm, c2), lambda i: (i, 0)), prev, pl.BlockSpec((HALO, c), lambda i: (0, 0)), vec, vec, vec],
        out_specs=pl.BlockSpec((tm, c), lambda i: (i, 0)), compiler_params=_params(grid))(uc, uc, wdw, b, lg, lb)


def _conv_bwd(uc, ds, wdw, b, lg, lb):
    m, c2 = uc.shape
    c = c2 // 2
    tm = _row_tile(m)
    nt = m // tm
    grid = (nt,)
    prev, nxt = _halo_specs(tm, c2, nt)
    _, nxt_c = _halo_specs(tm, c, nt)

    def body(x_ref, hp_ref, hn_ref, ds_ref, dsn_ref, w_ref, b_ref, g_ref, bb_ref,
             dx_ref, dw_ref, db_ref, dg_ref, dbb_ref):
        i = pl.program_id(0)

        @pl.when(i == 0)
        def _():
            for ref in (dw_ref, db_ref, dg_ref, dbb_ref):
                ref[...] = jnp.zeros_like(ref)

        hp = jnp.where(i > 0, hp_ref[...], 0.0)
        xc = jnp.concatenate([hp, x_ref[...], hn_ref[...]], axis=0)
        sg = jax.nn.sigmoid(xc[:, c:])
        u = xc[:, :c] * sg
        rows = tm + HALO
        y = _conv_taps(u, w_ref, rows) + b_ref[...]
        xhat, rstd = _layernorm_stats(y)
        ln = xhat * g_ref[...] + bb_ref[...]
        sl = jax.nn.sigmoid(ln)
        dsn = jnp.where(i < nt - 1, dsn_ref[...], 0.0)
        dsx = jnp.concatenate([ds_ref[...], dsn], axis=0)
        dln = dsx * sl * (1.0 + ln * (1.0 - sl))
        dxh = dln * g_ref[...]
        dy = rstd * (dxh - jnp.mean(dxh, axis=-1, keepdims=True)
                     - xhat * jnp.mean(dxh * xhat, axis=-1, keepdims=True))
        dyt = dy[:tm]
        dg_ref[...] += jnp.sum(dln[:tm] * xhat[:tm], axis=0, keepdims=True)
        dbb_ref[...] += jnp.sum(dln[:tm], axis=0, keepdims=True)
        db_ref[...] += jnp.sum(dyt, axis=0, keepdims=True)
        du = None
        for k in range(CONV_WIDTH):
            lag = CONV_WIDTH - 1 - k
            dw_ref[k:k + 1, :] += jnp.sum(dyt * _shift_down(u, lag)[HALO:HALO + tm], axis=0, keepdims=True)
            term = _shift_up(dy, lag)[:tm] * w_ref[k:k + 1, :]
            du = term if du is None else du + term
        a_t = xc[HALO:HALO + tm, :c]
        sg_t = sg[HALO:HALO + tm]
        dx_ref[:, :c] = du * sg_t
        dx_ref[:, c:] = du * a_t * sg_t * (1.0 - sg_t)

    vec = pl.BlockSpec((1, c), lambda i: (0, 0))
    wsp = pl.BlockSpec((HALO, c), lambda i: (0, 0))
    vshape = jax.ShapeDtypeStruct((1, c), F32)
    return pl.pallas_call(
        body, name="conv_bwd", grid=grid,
        out_shape=(jax.ShapeDtypeStruct((m, c2), F32), jax.ShapeDtypeStruct((HALO, c), F32), vshape, vshape, vshape),
        in_specs=[pl.BlockSpec((tm, c2), lambda i: (i, 0)), prev, nxt, pl.BlockSpec((tm, c), lambda i: (i, 0)), nxt_c,
                  wsp, vec, vec, vec],
        out_specs=(pl.BlockSpec((tm, c2), lambda i: (i, 0)), wsp, vec, vec, vec),
        compiler_params=_params(grid))(uc, uc, uc, ds, ds, wdw, b, lg, lb)


def _mesh_pos():
    return lax.axis_index("x"), lax.axis_index("y"), lax.axis_index("c")


def _other_chips(x, y):
    return [(1 - x, y), (x, 1 - y), (1 - x, 1 - y)]


def _comm_sems(n, per):
    return [pltpu.SemaphoreType.DMA((n, per)), pltpu.SemaphoreType.DMA((n, per)), pltpu.SemaphoreType.DMA((n,))]


def _comm_call(body, name, arrs, out_shapes, n_sems):
    n = len(arrs)
    return pl.pallas_call(
        body, name=name, out_shape=out_shapes, in_specs=[ANY] * n, out_specs=tuple([ANY] * len(out_shapes)),
        scratch_shapes=_comm_sems(n, n_sems),
    )(*arrs)


class _GatherPlan:
    def __init__(self, ins, outs, send_sems, recv_sems, local_sems):
        self.ins, self.outs = ins, outs
        self.send_sems, self.recv_sems, self.local_sems = send_sems, recv_sems, local_sems
        x, y, c = _mesh_pos()
        self.c, self.me, self.sib = c, (x, y, c), (x, y, 1 - c)
        self.chips = _other_chips(x, y)

    def _slot(self, a, p):
        return self.outs[a].at[4 * p[0] + 2 * p[1] + p[2]]

    def _copy(self, a, k, block, to, own=False):
        return pltpu.make_async_remote_copy(
            src_ref=self.ins[a] if own else self._slot(a, block), dst_ref=self._slot(a, block),
            send_sem=self.send_sems.at[a, k], recv_sem=self.recv_sems.at[a, k], device_id=to, device_id_type=MESH)

    def _local(self, a):
        return pltpu.make_async_copy(self.ins[a], self._slot(a, self.me), self.local_sems.at[a])

    def start(self):
        for a in range(len(self.ins)):
            for j, chip in enumerate(self.chips):
                self._copy(a, 1 + j, self.me, (*chip, self.c), own=True).start()
            self._copy(a, 0, self.me, self.sib, own=True).start()
            self._local(a).start()

    def forward(self, a):
        for j, chip in enumerate(self.chips):
            self._copy(a, 1 + j, (*chip, self.c), self.me).wait_recv()
            self._copy(a, 4 + j, (*chip, self.c), self.sib).start()

    def finish(self):
        for a in range(len(self.ins)):
            self._copy(a, 0, self.sib, self.me).wait_recv()
            for j, chip in enumerate(self.chips):
                self._copy(a, 4 + j, (*chip, 1 - self.c), self.me).wait_recv()
        for a in range(len(self.ins)):
            self._copy(a, 0, self.me, self.sib, own=True).wait_send()
            for j, chip in enumerate(self.chips):
                self._copy(a, 1 + j, self.me, (*chip, self.c), own=True).wait_send()
                self._copy(a, 4 + j, (*chip, self.c), self.sib).wait_send()
            self._local(a).wait()


def _gather_shapes(arrs):
    return tuple(jax.ShapeDtypeStruct((N_DEV,) + a.shape, a.dtype) for a in arrs)


def _all_gather(arrs, name):
    n = len(arrs)

    def body(*refs):
        plan = _GatherPlan(refs[:n], refs[n:2 * n], *refs[2 * n:])
        plan.start()
        for a in range(n):
            plan.forward(a)
        plan.finish()

    return _comm_call(body, name, arrs, _gather_shapes(arrs), 7)


class _ChipExchangePlan:
    def __init__(self, ins, outs, send_sems, recv_sems, local_sems):
        self.ins, self.outs = ins, outs
        self.send_sems, self.recv_sems, self.local_sems = send_sems, recv_sems, local_sems
        x, y, c = _mesh_pos()
        self.pos, self.mine = (x, y, c), 2 * x + y
        self.chips = _other_chips(x, y)

    def _send(self, a, j):
        chip = self.chips[j]
        return pltpu.make_async_remote_copy(
            src_ref=self.ins[a].at[2 * chip[0] + chip[1]], dst_ref=self.outs[a].at[self.mine],
            send_sem=self.send_sems.at[a, j], recv_sem=self.recv_sems.at[a, j], device_id=(*chip, self.pos[2]),
            device_id_type=MESH)

    def _landing(self, a, j):
        chip = self.chips[j]
        ref = self.outs[a].at[2 * chip[0] + chip[1]]
        return pltpu.make_async_remote_copy(src_ref=ref, dst_ref=ref, send_sem=self.send_sems.at[a, j],
                                            recv_sem=self.recv_sems.at[a, j], device_id=self.pos, device_id_type=MESH)

    def _local(self, a):
        return pltpu.make_async_copy(self.ins[a].at[self.mine], self.outs[a].at[self.mine], self.local_sems.at[a])

    def start(self):
        for a in range(len(self.ins)):
            for j in range(3):
                self._send(a, j).start()
            self._local(a).start()

    def finish(self):
        for a in range(len(self.ins)):
            for j in range(3):
                self._landing(a, j).wait_recv()
        for a in range(len(self.ins)):
            for j in range(3):
                self._send(a, j).wait_send()
            self._local(a).wait()


def _pair_tile(kk):
    return _pick(kk, (1024, 704, 576, 512, 256, 128, 64))


def _pair_exchange(parts, core, name):
    _, kk, n = parts.shape
    tr = _pair_tile(kk)
    nr = kk // tr
    grid = (4, nr)

    def body(c_ref, src_ref, recv_ref, send_sems, recv_sems):
        q, r = pl.program_id(0), pl.program_id(1)
        x, y, c = _mesh_pos()
        step = q * nr + r
        slot = step % 2
        dst = recv_ref.at[pl.ds(pl.multiple_of(step * tr, tr), tr), :]
        cp = pltpu.make_async_remote_copy(src_ref=src_ref, dst_ref=dst, send_sem=send_sems.at[slot],
                                          recv_sem=recv_sems.at[slot], device_id=(x, y, 1 - c), device_id_type=MESH)
        cp.start()
        cp.wait_send()
        cp.wait_recv()

    spec = pltpu.PrefetchScalarGridSpec(
        num_scalar_prefetch=1, grid=grid,
        in_specs=[pl.BlockSpec((tr, n), lambda q, r, c: ((2 * q + 1 - c[0]) * nr + r, 0))],
        out_specs=ANY, scratch_shapes=[pltpu.SemaphoreType.DMA((2,)), pltpu.SemaphoreType.DMA((2,))])
    recv = pl.pallas_call(body, name=name, grid_spec=spec, out_shape=jax.ShapeDtypeStruct((4 * kk, n), parts.dtype),
                          compiler_params=_params(grid))(core, parts.reshape(N_DEV * kk, n))
    return recv.reshape(4, kk, n)


def _exchange_shapes(parts):
    return tuple(jax.ShapeDtypeStruct(p.shape, p.dtype) for p in parts)


def _pair_add(parts, recv, core, name):
    _, kk, n = recv.shape
    tr = _pair_tile(kk)
    grid = (4, kk // tr)

    def body(c_ref, a_ref, b_ref, o_ref):
        o_ref[...] = (a_ref[...].astype(F32) + b_ref[...].astype(F32)).astype(o_ref.dtype)

    out = pl.BlockSpec((None, tr, n), lambda q, r, c: (q, r, 0))
    spec = pltpu.PrefetchScalarGridSpec(
        num_scalar_prefetch=1, grid=grid,
        in_specs=[pl.BlockSpec((None, None, tr, n), lambda q, r, c: (q, c[0], r, 0)), out], out_specs=out)
    return pl.pallas_call(body, name=name, grid_spec=spec, out_shape=jax.ShapeDtypeStruct(recv.shape, recv.dtype),
                          compiler_params=_params(grid))(core, parts.reshape(4, 2, kk, n), recv)


def _adamw_math(w, g, m, v):
    m2 = ADAM_B1 * m + (1.0 - ADAM_B1) * g
    v2 = ADAM_B2 * v + (1.0 - ADAM_B2) * (g * g)
    m_hat = m2 / (1.0 - ADAM_B1 ** ADAM_STEP)
    v_hat = v2 / (1.0 - ADAM_B2 ** ADAM_STEP)
    delta = -ADAM_LR * (m_hat / (jnp.sqrt(v_hat) + ADAM_EPS) + ADAM_WD * w)
    return delta, m2, v2


def _adamw_sharded(parts, w, m, v, name):
    nl, kk, n = w.shape
    tr = _pick(kk, (256, 128, 64)) if n <= 1024 else _pick(kk, (64, 32))
    grid = (nl, kk // tr)

    def body(*refs):
        p_refs = refs[:nl]
        w_ref, m_ref, v_ref, g_ref, d_ref, m2_ref, v2_ref = refs[nl:]
        layer = pl.program_id(0)
        g = None
        for l, p_ref in enumerate(p_refs):
            s = p_ref[0].astype(F32)
            for q in range(1, 4):
                s = s + p_ref[q].astype(F32)
            g = s if g is None else jnp.where(layer == l, s, g)
        d, m2, v2 = _adamw_math(w_ref[...], g, m_ref[...], v_ref[...])
        g_ref[...] = g
        d_ref[...] = d
        m2_ref[...] = m2
        v2_ref[...] = v2

    spec = pl.BlockSpec((None, tr, n), lambda l, r: (l, r, 0))
    shp = jax.ShapeDtypeStruct(w.shape, F32)
    p_specs = [pl.BlockSpec((4, tr, n), lambda l, r, k=k: (0, jnp.where(l == k, r, 0), 0)) for k in range(nl)]
    return pl.pallas_call(
        body, name=name, grid=grid, out_shape=(shp, shp, shp, shp), in_specs=p_specs + [spec, spec, spec],
        out_specs=(spec, spec, spec, spec), compiler_params=_params(grid))(*parts, w, m, v)


def _sum_slots(g8):
    _, r, _ = g8.shape

    def body(g_ref, o_ref):
        s = g_ref[0]
        for d in range(1, N_DEV):
            s = s + g_ref[d]
        o_ref[...] = s

    return pl.pallas_call(body, name="sum_slots", out_shape=jax.ShapeDtypeStruct((r, 128), F32))(g8)


def _adamw_flat(w, g, m, v):
    def body(w_ref, g_ref, m_ref, v_ref, d_ref, m2_ref, v2_ref):
        d, m2, v2 = _adamw_math(w_ref[...], g_ref[...], m_ref[...], v_ref[...])
        d_ref[...] = d
        m2_ref[...] = m2
        v2_ref[...] = v2

    shp = jax.ShapeDtypeStruct(w.shape, F32)
    return pl.pallas_call(body, name="adamw_flat", out_shape=(shp, shp, shp))(w, g, m, v)


def _pack(arrs):
    flat = jnp.concatenate([a.reshape(-1) for a in arrs])
    pad = (-flat.shape[0]) % 1024
    return jnp.pad(flat, (0, pad)).reshape(-1, 128)


def _unpack(buf, like):
    flat = buf.reshape(-1)
    out, off = [], 0
    for a in like:
        out.append(flat[off:off + a.size].reshape(a.shape))
        off += a.size
    return out


def _heads(x2d, scale=None):
    lp, da = x2d.shape
    xh = x2d.reshape(lp, N_HEADS, da // N_HEADS).transpose(1, 0, 2)
    if scale is not None:
        xh = xh * scale
    return xh


def _unheads(xh):
    nh, lp, dh = xh.shape
    return xh.transpose(1, 0, 2).reshape(lp, nh * dh)


def _hosted(hosts, site, kind, fn):
    if hosts and site in hosts:
        arrays, done = hosts[site]
        res, got = fn(_Riders(kind, arrays))
        done(got)
        return res
    return fn(None)


def _layer_fwd(h, l, W, P, hosts=None, u1=None):
    d = h.shape[1]
    da, dp = d // 2, d // 4
    dh = da // N_HEADS
    sv = {"h": h}
    if u1 is None:
        u1 = _rms_fwd(h, P["pre_mix_g"][l:l + 1], "rms_pre_mix")
    proj = _hosted(hosts, "mm_in", "gather", lambda r: _mm_nt(u1, W["in"], "mm_in", riders=r))
    qh = _heads(proj[:, :da], dh ** -0.5).astype(BF16)
    kh = _heads(proj[:, da:2 * da]).astype(BF16)
    vh = _heads(proj[:, 2 * da:3 * da]).astype(BF16)
    u_pool = proj[:, 3 * da:3 * da + dp]
    u_conv = proj[:, 3 * da + dp:]
    if hosts and "attn" in hosts:
        arrs, done = hosts["attn"]
        oh, tot, cnt, *gathered = _attn_fwd(qh, kh, vh, arrs)
        done(gathered)
    else:
        oh, tot, cnt = _attn_fwd(qh, kh, vh)
    o_attn = _unheads(oh)
    o_pool = _pool_fwd(u_pool, W["pool"], P["pool_scale"][l:l + 1], l)
    s_conv = _conv_fwd(u_conv, W["dw"][l], P["b_dw"][l:l + 1], P["conv_ln_g"][l:l + 1], P["conv_ln_b"][l:l + 1])
    o_conv = _mm_nn([(s_conv, W["pw"])], "mm_pw")
    merged = _merge_fwd(o_attn, o_pool, o_conv, P["mix_out_g"][l:l + 1])
    z1 = _mm_nn([(merged, W["out"])], "mm_out")
    h1, u2 = _resid_rms_fwd(h, z1, P["post_mix_g"][l:l + 1], P["pre_ffn_g"][l:l + 1], "resid_post_mix")
    a_s, b_s, act_s = _hosted(hosts, "ffn_up", "gather", lambda r: _ffn_up(u2, W["gate"], W["up"], r))
    ff = _hosted(hosts, "mm_down", "gather", lambda r: _mm_nn([(act_s, W["down"])], "mm_down", riders=r))
    g_next = P["pre_mix_g"][l + 1:l + 2] if l + 1 < P["pre_mix_g"].shape[0] else None
    h2, u1_next = _resid_rms_fwd(h1, ff, P["post_ffn_g"][l:l + 1], g_next, "resid_post_ffn")
    sv.update(u1=u1, qh=qh, kh=kh, vh=vh, tot=tot, cnt=cnt, u_pool=u_pool, u_conv=u_conv, o_attn=o_attn, o_pool=o_pool,
              o_conv=o_conv, s_conv=s_conv, merged=merged, z1=z1, h1=h1, u2=u2, a_s=a_s, b_s=b_s, act_s=act_s, ff=ff)
    return h2, sv, u1_next


EARLY = ("w_down", "w_gate", "w_up", "w_out", "w_pw")


def _layer_bwd(dh_out, l, W, P, sv, exchange_with_attn=None, hosts=None, exchange_w_in=None):
    d = dh_out.shape[1]
    da = d // 2
    dhd = da // N_HEADS
    lp = dh_out.shape[0]
    g = {}
    dff, g["post_ffn_g"] = _rms_bwd(sv["ff"], P["post_ffn_g"][l:l + 1], dh_out, None, BF16, "rms_bwd_post_ffn")
    da_s, db_s = _hosted(hosts, "ffn_bwd_act", "exchange",
                         lambda r: _ffn_bwd_act(dff, W["down"], sv["a_s"], sv["b_s"], r))
    g["w_down"] = _mm_tn(sv["act_s"], dff, "mm_dw_down")
    du2 = _hosted(hosts, "mm_d_u2", "exchange",
                  lambda r: _mm_nn([(da_s, W["gate"]), (db_s, W["up"])], "mm_d_u2", riders=r))
    g["w_gate"] = _hosted(hosts, "mm_dw_gate", "exchange", lambda r: _mm_tn(da_s, sv["u2"], "mm_dw_gate", riders=r))
    g["w_up"] = _hosted(hosts, "mm_dw_up", "exchange", lambda r: _mm_tn(db_s, sv["u2"], "mm_dw_up", riders=r))
    dh1, g["pre_ffn_g"] = _rms_bwd(sv["h1"], P["pre_ffn_g"][l:l + 1], du2, dh_out, F32, "rms_bwd_pre_ffn")
    dz1, g["post_mix_g"] = _rms_bwd(sv["z1"], P["post_mix_g"][l:l + 1], dh1, None, BF16, "rms_bwd_post_mix")
    dmerged = _mm_nt(dz1, W["out"], "mm_d_merged")
    g["w_out"] = _mm_tn(sv["merged"], dz1, "mm_dw_out")
    d_oa, d_op, d_oc, g["mix_out_g"] = _merge_bwd(sv["o_attn"], sv["o_pool"], sv["o_conv"], P["mix_out_g"][l:l + 1], dmerged)
    d_s = _mm_nt(d_oc, W["pw"], "mm_d_sconv")
    g["w_pw"] = _mm_tn(sv["s_conv"], d_oc, "mm_dw_pw")
    d_uc, dwdw, g["b_dw"], g["conv_ln_g"], g["conv_ln_b"] = _conv_bwd(
        sv["u_conv"], d_s, W["dw"][l], P["b_dw"][l:l + 1], P["conv_ln_g"][l:l + 1], P["conv_ln_b"][l:l + 1])
    g["w_dw"] = dwdw[:CONV_WIDTH]
    d_up, g["w_pool"], g["pool_scale"] = _pool_bwd(sv["u_pool"], d_op, W["pool"], P["pool_scale"][l:l + 1], l)
    doh = _heads(d_oa).astype(BF16)
    riders = exchange_with_attn(g) if exchange_with_attn is not None else []
    dqh, dkt, dvt, *exchanged = _attn_bwd(sv["qh"], sv["qh"].transpose(0, 2, 1), sv["kh"], sv["vh"], doh,
                                          doh.transpose(0, 2, 1), sv["tot"], sv["cnt"], dhd ** -0.5, riders)
    dk2, dv2 = (a.transpose(1, 3, 0, 2).reshape(lp, da) for a in (dkt, dvt))
    dproj = jnp.concatenate([_unheads(dqh), dk2, dv2, d_up, d_uc], axis=1).astype(BF16)
    g["w_in"] = _mm_tn(dproj, sv["u1"], "mm_dw_in")
    late = {"mm_d_u1": exchange_w_in(g)} if exchange_w_in is not None else None
    du1 = _hosted(late, "mm_d_u1", "exchange", lambda r: _mm_nn([(dproj, W["in"])], "mm_d_u1", riders=r))
    dh0, g["pre_mix_g"] = _rms_bwd(sv["h"], P["pre_mix_g"][l:l + 1], du1, dh1, F32, "rms_bwd_pre_mix")
    return dh0, g, exchanged


SHARDED = ("w_in", "w_gate", "w_up", "w_down", "w_out", "w_pw")
SMALL = ("pre_mix_g", "w_pool", "pool_scale", "b_dw", "conv_ln_g", "conv_ln_b", "mix_out_g", "post_mix_g",
         "pre_ffn_g", "post_ffn_g", "w_dw", "meta_tokens")
ORDER = ("meta_tokens", "pre_mix_g", "w_in", "w_pool", "pool_scale", "w_dw", "b_dw", "conv_ln_g", "conv_ln_b", "w_pw",
         "mix_out_g", "w_out", "post_mix_g", "pre_ffn_g", "w_gate", "w_up", "w_down", "post_ffn_g")


def kernel(x, meta_tokens, pre_mix_g, w_in, w_pool, pool_scale, w_dw, b_dw, conv_ln_g, conv_ln_b, w_pw, mix_out_g, w_out, post_mix_g, pre_ffn_g, w_gate, w_up, w_down, post_ffn_g, loss_target, m_meta_tokens, m_pre_mix_g, m_w_in, m_w_pool, m_pool_scale, m_w_dw, m_b_dw, m_conv_ln_g, m_conv_ln_b, m_w_pw, m_mix_out_g, m_w_out, m_post_mix_g, m_pre_ffn_g, m_w_gate, m_w_up, m_w_down, m_post_ffn_g, v_meta_tokens, v_pre_mix_g, v_w_in, v_w_pool, v_pool_scale, v_w_dw, v_b_dw, v_conv_ln_g, v_conv_ln_b, v_w_pw, v_mix_out_g, v_w_out, v_post_mix_g, v_pre_ffn_g, v_w_gate, v_w_up, v_w_down, v_post_ffn_g):
    P = dict(meta_tokens=meta_tokens, pre_mix_g=pre_mix_g, w_in=w_in, w_pool=w_pool, pool_scale=pool_scale, w_dw=w_dw,
             b_dw=b_dw, conv_ln_g=conv_ln_g, conv_ln_b=conv_ln_b, w_pw=w_pw, mix_out_g=mix_out_g, w_out=w_out,
             post_mix_g=post_mix_g, pre_ffn_g=pre_ffn_g, w_gate=w_gate, w_up=w_up, w_down=w_down, post_ffn_g=post_ffn_g)
    M = dict(meta_tokens=m_meta_tokens, pre_mix_g=m_pre_mix_g, w_in=m_w_in, w_pool=m_w_pool, pool_scale=m_pool_scale,
             w_dw=m_w_dw, b_dw=m_b_dw, conv_ln_g=m_conv_ln_g, conv_ln_b=m_conv_ln_b, w_pw=m_w_pw, mix_out_g=m_mix_out_g,
             w_out=m_w_out, post_mix_g=m_post_mix_g, pre_ffn_g=m_pre_ffn_g, w_gate=m_w_gate, w_up=m_w_up,
             w_down=m_w_down, post_ffn_g=m_post_ffn_g)
    V = dict(meta_tokens=v_meta_tokens, pre_mix_g=v_pre_mix_g, w_in=v_w_in, w_pool=v_w_pool, pool_scale=v_pool_scale,
             w_dw=v_w_dw, b_dw=v_b_dw, conv_ln_g=v_conv_ln_g, conv_ln_b=v_conv_ln_b, w_pw=v_w_pw, mix_out_g=v_mix_out_g,
             w_out=v_w_out, post_mix_g=v_post_mix_g, pre_ffn_g=v_pre_ffn_g, w_gate=v_w_gate, w_up=v_w_up,
             w_down=v_w_down, post_ffn_g=v_post_ffn_g)
    xi, yi, ci = _mesh_pos()
    dev = 4 * xi + 2 * yi + ci
    n_tok, d = x.shape[1], x.shape[2]
    n_meta = meta_tokens.shape[0]
    n_layers = w_in.shape[0]
    c = d // 4
    l_real = n_meta + n_tok
    lp = -(-l_real // ATT_BLOCK) * ATT_BLOCK

    TRANSPOSED = ("w_in", "w_gate", "w_up")
    shard = {k: (jnp.swapaxes(P[k], 1, 2) if k in TRANSPOSED else P[k]) for k in SHARDED}

    def bf16_shard(k, l):
        return shard[k][l].astype(BF16)

    def whole(gathered):
        return gathered.reshape(-1, gathered.shape[-1])

    g_in0, g_dw, g_meta = _all_gather([bf16_shard("w_in", 0), w_dw, meta_tokens], "gather_first")
    wdw_full = g_dw.transpose(1, 2, 0, 3).reshape(n_layers, CONV_WIDTH, c)
    common = {"pool": w_pool.astype(BF16), "dw": jnp.pad(wdw_full, ((0, 0), (0, HALO - CONV_WIDTH), (0, 0)))}
    W = [dict(common) for _ in range(n_layers)]
    W[0]["in"] = whole(g_in0)
    meta_full = g_meta.transpose(1, 0, 2).reshape(n_meta, d)
    def gather_site(keys):
        def done(gathered):
            for (k, l), g in zip(keys, gathered):
                W[l][k[2:]] = whole(g)
        return [bf16_shard(k, l) for k, l in keys], done

    def fwd_hosts(l):
        first, then = ((), ("w_up", "w_down")) if l > 0 else (("w_pw", "w_out"), ("w_gate", "w_up", "w_down"))
        hosts = {"attn": gather_site([(k, l) for k in then])}
        if first:
            hosts["mm_in"] = gather_site([(k, l) for k in first])
        if l + 1 < n_layers:
            hosts["ffn_up"] = gather_site([(k, l + 1) for k in ("w_in", "w_pw", "w_out")])
            hosts["mm_down"] = gather_site([("w_gate", l + 1)])
        return hosts

    h = jnp.concatenate([meta_full, x[0], jnp.zeros((lp - l_real, d), F32)], axis=0)
    saved, u1 = [], None
    for l in range(n_layers):
        h, sv, u1 = _layer_fwd(h, l, W[l], P, fwd_hosts(l), u1)
        saved.append(sv)
    target = jnp.pad(loss_target[0], ((n_meta, lp - l_real), (0, 0)))
    loss_part, dh = _loss_grad(h, target, n_meta, n_tok)
    loss = lax.psum(loss_part[0, 0], ("x", "y", "c"))

    core = jnp.reshape(ci, (1,)).astype(jnp.int32)
    by_chip = {}

    def pair_reduce(g, keys, l):
        res = {}
        for k in keys:
            p = g[k].reshape(N_DEV, g[k].shape[0] // N_DEV, g[k].shape[1])
            recv = _pair_exchange(p, core, f"pair_exchange_{k}_{l}")
            res[k] = _pair_add(p, recv, core, f"pair_add_{k}_{l}")
        return res

    def exchange_site(sums, keys, l):
        def done(exchanged):
            by_chip.update({(l, k): e for k, e in zip(keys, exchanged)})
        return [sums[k] for k in keys], done

    def ride(g):
        sums = pair_reduce(g, EARLY, 0)
        return [sums[k] for k in EARLY]

    grads = [None] * n_layers
    hosts = None
    for l in reversed(range(n_layers)):
        last = (lambda g: exchange_site(pair_reduce(g, ("w_in",), 0), ("w_in",), 0)) if l == 0 else None
        dh, grads[l], exchanged = _layer_bwd(dh, l, W[l], P, saved[l], ride if l == 0 else None, hosts, last)
        if l == 0:
            by_chip.update({(0, k): e for k, e in zip(EARLY, exchanged)})
        else:
            sums = pair_reduce(grads[l], SHARDED, l)
            hosts = {"ffn_bwd_act": exchange_site(sums, ("w_down",), l),
                     "mm_d_u2": exchange_site(sums, ("w_gate", "w_up"), l),
                     "mm_dw_gate": exchange_site(sums, ("w_in",), l),
                     "mm_dw_up": exchange_site(sums, ("w_out", "w_pw"), l)}
    grad_x = dh[n_meta:l_real][None]
    out = {}
    for k in SHARDED:
        view = (lambda a: jnp.swapaxes(a, 1, 2)) if k in TRANSPOSED else (lambda a: a)
        res = _adamw_sharded([by_chip[(l, k)] for l in range(n_layers)], view(P[k]), view(M[k]), view(V[k]), "adamw_" + k)
        out[k] = tuple(view(r) for r in res)

    small_parts = []
    for k in SMALL:
        if k == "meta_tokens":
            small_parts.append(dh[:n_meta])
        else:
            small_parts.append(jnp.stack([grads[l][k].reshape(P[k].shape[1:] if k != "w_dw" else (CONV_WIDTH, c))
                                          for l in range(n_layers)], axis=0))
    (g8,) = _all_gather([_pack(small_parts)], "gather_small_grads")
    g_small = dict(zip(SMALL, _unpack(_sum_slots(g8), small_parts)))
    g_small["w_dw"] = lax.dynamic_slice_in_dim(g_small["w_dw"], dev * w_dw.shape[2], w_dw.shape[2], axis=2)
    g_small["meta_tokens"] = lax.dynamic_slice_in_dim(g_small["meta_tokens"], dev * meta_tokens.shape[1],
                                                      meta_tokens.shape[1], axis=1)
    like = [P[k] for k in SMALL]
    res = _adamw_flat(_pack(like), _pack([g_small[k] for k in SMALL]), _pack([M[k] for k in SMALL]),
                      _pack([V[k] for k in SMALL]))
    res = [_unpack(r, like) for r in res]
    for i, k in enumerate(SMALL):
        out[k] = (g_small[k], res[0][i], res[1][i], res[2][i])

    return (loss, grad_x, *[out[k][0] for k in ORDER], *[out[k][1] for k in ORDER],
            *[out[k][2] for k in ORDER], *[out[k][3] for k in ORDER])
```
